```python
import math
import jax, jax.numpy as jnp
from jax import lax
import numpy as np

D_MODEL = 1024
BATCH = 8
SEQ = 2048
DEPTH = 1

D_MIX = D_MODEL
HG_WIDTH = D_MIX // 2
HG_HEAD_DIM = 128
HG_HEADS = HG_WIDTH // HG_HEAD_DIM
HG_CHUNK = 64
SW_WIDTH = D_MIX - HG_WIDTH
SW_HEAD_DIM = 64
SW_Q_HEADS = SW_WIDTH // SW_HEAD_DIM
SW_KV_HEADS = SW_Q_HEADS // 4
SW_KV_WIDTH = SW_KV_HEADS * SW_HEAD_DIM
WINDOW = 128
ROPE_THETA = 500000.0
ROPE_DIM = SW_HEAD_DIM // 4
DN_ALPHA = (2.0 * DEPTH) ** 0.25
DN_BETA = (8.0 * DEPTH) ** -0.25
LN_EPS = 1e-5
RMS_EPS = 1e-6
IN_SIZES = (HG_WIDTH, HG_WIDTH, HG_WIDTH, HG_WIDTH, SW_WIDTH, SW_KV_WIDTH, SW_KV_WIDTH, SW_WIDTH)
IN_WIDTH = sum(IN_SIZES)
IN_SPLITS = tuple(int(s) for s in np.cumsum(IN_SIZES)[:-1])

kernel_name = "hybrid_hgrn2_swa_sink_deepnorm"


def hgrn2_chunk(q, k, v, log_f):
    B, T, H, K = q.shape
    V = v.shape[-1]
    C = HG_CHUNK
    N = T // C
    q = q.reshape(B, N, C, H, K)
    k = k.reshape(B, N, C, H, K)
    v = v.reshape(B, N, C, H, V)
    log_f = log_f.reshape(B, N, C, H, K)
    G = jnp.cumsum(log_f, axis=2)
    G_last = G[:, :, -1]
    q_dec = q * jnp.exp(G)
    k_dec = k * jnp.exp(-G)
    causal = jnp.tril(jnp.ones((C, C), dtype=bool))
    A = jnp.einsum('bnthk,bnshk->bnhts', q_dec, k_dec)
    A = jnp.where(causal, A, 0.0)
    o_intra = jnp.einsum('bnhts,bnshv->bnthv', A, v)
    inc = jnp.einsum('bnshk,bnshv->bnhkv', k * jnp.exp(G_last[:, :, None] - G), v)
    decay = jnp.exp(G_last)

    def step(S, xs):
        d, u = xs
        return d[..., None] * S + u, S

    S0 = jnp.zeros((B, H, K, V), q.dtype)
    _, S_prev = lax.scan(step, S0, (jnp.moveaxis(decay, 1, 0), jnp.moveaxis(inc, 1, 0)))
    S_prev = jnp.moveaxis(S_prev, 0, 1)
    o_inter = jnp.einsum('bnthk,bnhkv->bnthv', q_dec, S_prev)
    return (o_intra + o_inter).reshape(B, T, H, V)


def partial_rope(x, cos, sin):
    half = ROPE_DIM // 2
    x1 = x[..., :half]
    x2 = x[..., half:ROPE_DIM]
    rot = jnp.concatenate([x1 * cos - x2 * sin, x2 * cos + x1 * sin], axis=-1)
    return jnp.concatenate([rot, x[..., ROPE_DIM:]], axis=-1)


def swa_with_sinks(q, k, v, sinks):
    B, T, Hq, D = q.shape
    Hkv = k.shape[2]
    G = Hq // Hkv
    W = WINDOW
    nb = T // W
    qb = q.reshape(B, nb, W, Hkv, G, D)
    pad = jnp.zeros((B, W, Hkv, D), k.dtype)
    kb = jnp.concatenate([pad, k], axis=1).reshape(B, nb + 1, W, Hkv, D)
    vb = jnp.concatenate([pad, v], axis=1).reshape(B, nb + 1, W, Hkv, D)
    kw = jnp.concatenate([kb[:, :-1], kb[:, 1:]], axis=2)
    vw = jnp.concatenate([vb[:, :-1], vb[:, 1:]], axis=2)
    s = jnp.einsum('bnqhgd,bnkhd->bnhgqk', qb, kw).astype(jnp.float32) * (D ** -0.5)
    t = jnp.arange(W)[:, None]
    j = jnp.arange(2 * W)[None, :]
    band = (j > t) & (j <= t + W)
    valid = (jnp.arange(nb)[:, None, None] > 0) | (j[None] >= W)
    mask = band[None] & valid
    s = jnp.where(mask[None, :, None, None], s, -jnp.inf)
    sink = sinks.astype(jnp.float32).reshape(1, 1, Hkv, G, 1, 1)
    m = jnp.maximum(jnp.max(s, axis=-1, keepdims=True), sink)
    p = jnp.exp(s - m)
    denom = jnp.sum(p, axis=-1, keepdims=True) + jnp.exp(sink - m)
    p = (p / denom).astype(v.dtype)
    o = jnp.einsum('bnhgqk,bnkhd->bnqhgd', p, vw)
    return o.reshape(B, T, Hq * D)


def setup_inputs(seed: int = 0) -> dict:
    key = jax.random.key(seed)
    ks = jax.random.split(key, 8)
    x = jax.random.normal(ks[0], (BATCH, SEQ, D_MODEL), jnp.float32)
    col_scale = jnp.concatenate([
        jnp.ones((HG_WIDTH,)), jnp.ones((HG_WIDTH,)), jnp.full((HG_WIDTH,), DN_BETA), jnp.ones((HG_WIDTH,)),
        jnp.ones((SW_WIDTH,)), jnp.ones((SW_KV_WIDTH,)), jnp.full((SW_KV_WIDTH,), DN_BETA), jnp.ones((SW_WIDTH,)),
    ]).astype(jnp.float32)
    w_in = jax.random.normal(ks[1], (DEPTH, D_MODEL, IN_WIDTH), jnp.float32) * (D_MODEL ** -0.5) * col_scale
    lb_logits = 0.1 * jax.random.normal(ks[2], (DEPTH + 1, HG_WIDTH), jnp.float32)
    hg_norm_w = 1.0 + 0.02 * jax.random.normal(ks[3], (DEPTH, HG_WIDTH), jnp.float32)
    sinks = 0.5 * jax.random.normal(ks[4], (DEPTH, SW_Q_HEADS), jnp.float32)
    w_out = jax.random.normal(ks[5], (DEPTH, D_MIX, D_MODEL), jnp.float32) * (D_MIX ** -0.5) * DN_BETA
    ln_g = 1.0 + 0.02 * jax.random.normal(ks[6], (DEPTH, D_MODEL), jnp.float32)
    ln_b = 0.02 * jax.random.normal(ks[7], (DEPTH, D_MODEL), jnp.float32)
    return {"x": x, "w_in": w_in, "lb_logits": lb_logits, "hg_norm_w": hg_norm_w,
            "sinks": sinks, "w_out": w_out, "ln_g": ln_g, "ln_b": ln_b}


def reference(x, w_in, lb_logits, hg_norm_w, sinks, w_out, ln_g, ln_b):
    B, T, _ = x.shape
    f32 = jnp.float32
    dt = x.dtype
    pos = jnp.arange(T, dtype=f32)
    inv_freq = ROPE_THETA ** (-jnp.arange(0, ROPE_DIM, 2, dtype=f32) / ROPE_DIM)
    ang = pos[:, None] * inv_freq[None, :]
    cos = jnp.cos(ang)[:, None, :].astype(dt)
    sin = jnp.sin(ang)[:, None, :].astype(dt)
    lower_bounds = jnp.cumsum(jax.nn.softmax(lb_logits.astype(f32), axis=0), axis=0)
    h_res = x
    for layer in range(DEPTH):
        h = jnp.einsum('btd,de->bte', h_res, w_in[layer])
        hq, hf, hi, hg, aq, ak, av, ag = jnp.split(h, IN_SPLITS, axis=-1)
        lb = lower_bounds[layer]
        f = lb + (1.0 - lb) * jax.nn.sigmoid(hf.astype(f32))
        log_f = jnp.log(f)
        k_in = 1.0 - f
        q_h = jax.nn.silu(hq.astype(f32))
        rs = lambda a: a.reshape(B, T, HG_HEADS, HG_HEAD_DIM)
        o_h = hgrn2_chunk(rs(q_h), rs(k_in), rs(hi.astype(f32)), rs(log_f))
        o_h = o_h * lax.rsqrt(jnp.mean(o_h * o_h, axis=-1, keepdims=True) + RMS_EPS)
        o_h = o_h * hg_norm_w[layer].astype(f32).reshape(HG_HEADS, HG_HEAD_DIM)
        o_h = o_h.reshape(B, T, HG_WIDTH).astype(dt) * jax.nn.silu(hg)
        q_a = partial_rope(aq.reshape(B, T, SW_Q_HEADS, SW_HEAD_DIM), cos, sin)
        k_a = partial_rope(ak.reshape(B, T, SW_KV_HEADS, SW_HEAD_DIM), cos, sin)
        v_a = av.reshape(B, T, SW_KV_HEADS, SW_HEAD_DIM)
        o_a = swa_with_sinks(q_a, k_a, v_a, sinks[layer]) * jax.nn.silu(ag)
        mix = jnp.concatenate([o_h, o_a], axis=-1)
        out = jnp.einsum('bte,ed->btd', mix, w_out[layer])
        z = DN_ALPHA * h_res.astype(f32) + out.astype(f32)
        mu = jnp.mean(z, axis=-1, keepdims=True)
        var = jnp.mean(jnp.square(z - mu), axis=-1, keepdims=True)
        z = (z - mu) * lax.rsqrt(var + LN_EPS)
        h_res = (z * ln_g[layer].astype(f32) + ln_b[layer].astype(f32)).astype(dt)
    return h_res
```

```python
import functools

import jax
import jax.numpy as jnp
from jax import lax
from jax.experimental import pallas as pl
from jax.experimental.pallas import tpu as pltpu

D_MODEL = 1024
DEPTH = 1
HG_WIDTH = 512
HG_HEAD_DIM = 128
HG_HEADS = HG_WIDTH // HG_HEAD_DIM
HG_CHUNK = 64
SW_WIDTH = 512
SW_HEAD_DIM = 64
SW_Q_HEADS = SW_WIDTH // SW_HEAD_DIM
SW_KV_HEADS = SW_Q_HEADS // 4
SW_KV_WIDTH = SW_KV_HEADS * SW_HEAD_DIM
SW_GROUP = SW_Q_HEADS // SW_KV_HEADS
WINDOW = 128
ROPE_THETA = 500000.0
ROPE_DIM = SW_HEAD_DIM // 4
DN_ALPHA = (2.0 * DEPTH) ** 0.25
LN_EPS = 1e-5
RMS_EPS = 1e-6
IN_WIDTH = 4 * HG_WIDTH + SW_WIDTH + 2 * SW_KV_WIDTH + SW_WIDTH

OFF_HQ = 0
OFF_HF = OFF_HQ + HG_WIDTH
OFF_HI = OFF_HF + HG_WIDTH
OFF_HG = OFF_HI + HG_WIDTH
OFF_AQ = OFF_HG + HG_WIDTH
OFF_AK = OFF_AQ + SW_WIDTH
OFF_AV = OFF_AK + SW_KV_WIDTH
OFF_AG = OFF_AV + SW_KV_WIDTH

LANES = 128
HEADS_PER_TILE = LANES // SW_HEAD_DIM
TIME_TILE = 256
VMEM_LIMIT_BYTES = 48 * 1024 * 1024

F32 = jnp.float32
BF16 = jnp.bfloat16
NT_DIMS = (((1,), (1,)), ((), ()))
TN_DIMS = (((0,), (0,)), ((), ()))


def _sigmoid(v):
    return 1.0 / (1.0 + jnp.exp(-v))


def _layer_kernel(sinks_ref, x_ref, win_ref, wout_ref, lbl_ref, nw_ref, lng_ref, lnb_ref,
                  cos_ref, sin_ref, o_ref,
                  h_ref, mix_ref, st_ref, kwin_ref, vwin_ref):
    t = pl.program_id(1)
    tt = x_ref.shape[1]
    n_chunks = tt // HG_CHUNK
    n_blocks = tt // WINDOW

    @pl.when(t == 0)
    def _reset_carries():
        st_ref[...] = jnp.zeros_like(st_ref)
        kwin_ref[:, 0:WINDOW, :] = jnp.zeros((2 * SW_KV_HEADS, WINDOW, LANES), BF16)
        vwin_ref[:, 0:WINDOW, :] = jnp.zeros((2 * SW_KV_HEADS, WINDOW, LANES), BF16)

    x = x_ref[0]
    h_ref[...] = jnp.dot(x.astype(BF16), win_ref[...], preferred_element_type=F32)

    lbl = lbl_ref[...]
    lbl_e = jnp.exp(lbl - jnp.max(lbl, axis=0, keepdims=True))
    lb = lbl_e[0:1] / jnp.sum(lbl_e, axis=0, keepdims=True)
    nw = nw_ref[...]
    row = lax.broadcasted_iota(jnp.int32, (HG_CHUNK, HG_CHUNK), 0)
    col = lax.broadcasted_iota(jnp.int32, (HG_CHUNK, HG_CHUNK), 1)
    tril = col <= row
    cum_mat = tril.astype(F32)

    for c in range(n_chunks):
        r0 = c * HG_CHUNK
        rows = slice(r0, r0 + HG_CHUNK)
        hq = h_ref[rows, OFF_HQ:OFF_HQ + HG_WIDTH]
        hf = h_ref[rows, OFF_HF:OFF_HF + HG_WIDTH]
        f = lb + (1.0 - lb) * _sigmoid(hf)
        log_f = jnp.log(f)
        k_in = 1.0 - f
        g_cum = jnp.dot(cum_mat, log_f, precision=lax.Precision.HIGHEST,
                        preferred_element_type=F32)
        g_last = g_cum[HG_CHUNK - 1:HG_CHUNK, :]
        q_dec = (hq * _sigmoid(hq)) * jnp.exp(g_cum)
        k_dec = k_in * jnp.exp(-g_cum)
        k_tail = k_in * jnp.exp(g_last - g_cum)
        decay = jnp.exp(g_last)
        for hd in range(HG_HEADS):
            sl = slice(hd * HG_HEAD_DIM, (hd + 1) * HG_HEAD_DIM)
            qd = q_dec[:, sl].astype(BF16)
            kd = k_dec[:, sl].astype(BF16)
            v = h_ref[rows, OFF_HI + hd * HG_HEAD_DIM:OFF_HI + (hd + 1) * HG_HEAD_DIM].astype(BF16)
            a = lax.dot_general(qd, kd, NT_DIMS, preferred_element_type=F32)
            a = jnp.where(tril, a, 0.0)
            s_t = st_ref[hd]
            o = jnp.dot(a.astype(BF16), v, preferred_element_type=F32)
            o = o + lax.dot_general(qd, s_t.astype(BF16), NT_DIMS, preferred_element_type=F32)
            inc_t = lax.dot_general(v, k_tail[:, sl].astype(BF16), TN_DIMS,
                                    preferred_element_type=F32)
            st_ref[hd] = s_t * decay[:, sl] + inc_t
            o = o * lax.rsqrt(jnp.mean(o * o, axis=-1, keepdims=True) + RMS_EPS)
            o = o * nw[:, sl]
            hg = h_ref[rows, OFF_HG + hd * HG_HEAD_DIM:OFF_HG + (hd + 1) * HG_HEAD_DIM]
            mix_ref[rows, sl] = (o * (hg * _sigmoid(hg))).astype(BF16)

    lane = lax.broadcasted_iota(jnp.int32, (1, LANES), 1)
    rope_first = (lane % SW_HEAD_DIM) < (ROPE_DIM // 2)
    lane_lo = lane < SW_HEAD_DIM
    q_idx = lax.broadcasted_iota(jnp.int32, (WINDOW, 2 * WINDOW), 0)
    k_idx = lax.broadcasted_iota(jnp.int32, (WINDOW, 2 * WINDOW), 1)
    band = (k_idx > q_idx) & (k_idx <= q_idx + WINDOW)
    scale = SW_HEAD_DIM ** -0.5

    def rope(v, cos, sin):
        partner = jnp.where(rope_first,
                            pltpu.roll(v, LANES - ROPE_DIM // 2, 1),
                            pltpu.roll(v, ROPE_DIM // 2, 1))
        return v * cos + partner * sin

    def head_variants(v):
        sw = pltpu.roll(v, SW_HEAD_DIM, 1)
        zero = jnp.zeros_like(v)
        return (jnp.where(lane_lo, v, zero), jnp.where(lane_lo, zero, sw),
                jnp.where(lane_lo, sw, zero), jnp.where(lane_lo, zero, v))

    for blk in range(n_blocks):
        r0 = blk * WINDOW
        rows = slice(r0, r0 + WINDOW)
        cos = cos_ref[rows, :]
        sin = sin_ref[rows, :]
        k_rot = rope(h_ref[rows, OFF_AK:OFF_AK + SW_KV_WIDTH], cos, sin)
        for idx, kv in enumerate(head_variants(k_rot)):
            kwin_ref[idx, r0 + WINDOW:r0 + 2 * WINDOW, :] = kv.astype(BF16)
        for idx, vv in enumerate(head_variants(h_ref[rows, OFF_AV:OFF_AV + SW_KV_WIDTH])):
            vwin_ref[idx, r0 + WINDOW:r0 + 2 * WINDOW, :] = vv.astype(BF16)
        if blk == 0:
            first_valid = jnp.where(t > 0, 0, WINDOW)
            mask = band & (k_idx >= first_valid)
        else:
            mask = band
        for tile in range(SW_WIDTH // LANES):
            g = (tile * HEADS_PER_TILE) // SW_GROUP
            q_rot = rope(h_ref[rows, OFF_AQ + tile * LANES:OFF_AQ + (tile + 1) * LANES], cos, sin)
            q_bf = (q_rot * scale).astype(BF16)
            probs = []
            for j in range(HEADS_PER_TILE):
                sink = sinks_ref[tile * HEADS_PER_TILE + j]
                k_win = kwin_ref[2 * g + j, r0:r0 + 2 * WINDOW, :]
                s = lax.dot_general(q_bf, k_win, NT_DIMS, preferred_element_type=F32)
                s = jnp.where(mask, s, -jnp.inf)
                m = jnp.maximum(jnp.max(s, axis=-1, keepdims=True), sink)
                p = jnp.exp(s - m)
                denom = jnp.sum(p, axis=-1, keepdims=True) + jnp.exp(sink - m)
                probs.append((p / denom).astype(BF16))
            p_cat = jnp.concatenate(probs, axis=1)
            v_cat = jnp.concatenate(
                [vwin_ref[2 * g + j, r0:r0 + 2 * WINDOW, :] for j in range(HEADS_PER_TILE)], axis=0)
            o_a = jnp.dot(p_cat, v_cat, preferred_element_type=F32)
            ag = h_ref[rows, OFF_AG + tile * LANES:OFF_AG + (tile + 1) * LANES]
            mix_ref[rows, HG_WIDTH + tile * LANES:HG_WIDTH + (tile + 1) * LANES] = (
                o_a * (ag * _sigmoid(ag))).astype(BF16)

    kwin_ref[:, 0:WINDOW, :] = kwin_ref[:, tt:tt + WINDOW, :]
    vwin_ref[:, 0:WINDOW, :] = vwin_ref[:, tt:tt + WINDOW, :]

    out = jnp.dot(mix_ref[...], wout_ref[...], preferred_element_type=F32)
    z = DN_ALPHA * x + out
    mu = jnp.mean(z, axis=-1, keepdims=True)
    zc = z - mu
    var = jnp.mean(zc * zc, axis=-1, keepdims=True)
    o_ref[0] = (zc * lax.rsqrt(var + LN_EPS)) * lng_ref[...] + lnb_ref[...]


def _rope_tables(seq_len):
    pos = jnp.arange(seq_len, dtype=F32)
    inv_freq = ROPE_THETA ** (-jnp.arange(0, ROPE_DIM, 2, dtype=F32) / ROPE_DIM)
    ang = pos[:, None] * inv_freq[None, :]
    cos = jnp.cos(ang)
    sin = jnp.sin(ang)
    ones = jnp.ones((seq_len, SW_HEAD_DIM - ROPE_DIM), F32)
    cos_head = jnp.concatenate([cos, cos, ones], axis=1)
    sin_head = jnp.concatenate([-sin, sin, jnp.zeros_like(ones)], axis=1)
    return (jnp.tile(cos_head, (1, HEADS_PER_TILE)), jnp.tile(sin_head, (1, HEADS_PER_TILE)))


def kernel(x, w_in, lb_logits, hg_norm_w, sinks, w_out, ln_g, ln_b):
    batch, seq_len, d_model = x.shape
    assert d_model == D_MODEL and w_in.shape == (DEPTH, D_MODEL, IN_WIDTH)
    assert seq_len % TIME_TILE == 0
    cos_tab, sin_tab = _rope_tables(seq_len)
    n_tiles = seq_len // TIME_TILE

    const = lambda b, t, *_: (0, 0)
    grid_spec = pltpu.PrefetchScalarGridSpec(
        num_scalar_prefetch=1,
        grid=(batch, n_tiles),
        in_specs=[
            pl.BlockSpec((1, TIME_TILE, D_MODEL), lambda b, t, *_: (b, t, 0)),
            pl.BlockSpec((D_MODEL, IN_WIDTH), const),
            pl.BlockSpec((D_MODEL, D_MODEL), const),
            pl.BlockSpec((DEPTH + 1, HG_WIDTH), const),
            pl.BlockSpec((1, HG_WIDTH), const),
            pl.BlockSpec((1, D_MODEL), const),
            pl.BlockSpec((1, D_MODEL), const),
            pl.BlockSpec((TIME_TILE, LANES), lambda b, t, *_: (t, 0)),
            pl.BlockSpec((TIME_TILE, LANES), lambda b, t, *_: (t, 0)),
        ],
        out_specs=pl.BlockSpec((1, TIME_TILE, D_MODEL), lambda b, t, *_: (b, t, 0)),
        scratch_shapes=[
            pltpu.VMEM((TIME_TILE, IN_WIDTH), F32),
            pltpu.VMEM((TIME_TILE, D_MODEL), BF16),
            pltpu.VMEM((HG_HEADS, HG_HEAD_DIM, HG_HEAD_DIM), F32),
            pltpu.VMEM((2 * SW_KV_HEADS, TIME_TILE + WINDOW, LANES), BF16),
            pltpu.VMEM((2 * SW_KV_HEADS, TIME_TILE + WINDOW, LANES), BF16),
        ],
    )
    return pl.pallas_call(
        _layer_kernel,
        grid_spec=grid_spec,
        out_shape=jax.ShapeDtypeStruct(x.shape, x.dtype),
        compiler_params=pltpu.CompilerParams(
            dimension_semantics=("arbitrary", "arbitrary"),
            vmem_limit_bytes=VMEM_LIMIT_BYTES),
        name="hybrid_hgrn2_swa_layer",
    )(sinks[0], x, w_in[0].astype(BF16), w_out[0].astype(BF16), lb_logits, hg_norm_w,
      ln_g, ln_b, cos_tab, sin_tab)
```

```python
import jax
import jax.numpy as jnp
from jax import lax
from jax.experimental import pallas as pl
from jax.experimental.pallas import tpu as pltpu

D_MODEL = 1024
DEPTH = 1
HG_WIDTH = 512
HG_HEAD_DIM = 128
HG_HEADS = HG_WIDTH // HG_HEAD_DIM
HG_CHUNK = 64
SW_WIDTH = 512
SW_HEAD_DIM = 64
SW_Q_HEADS = SW_WIDTH // SW_HEAD_DIM
SW_KV_HEADS = SW_Q_HEADS // 4
SW_KV_WIDTH = SW_KV_HEADS * SW_HEAD_DIM
SW_GROUP = SW_Q_HEADS // SW_KV_HEADS
WINDOW = 128
ROPE_THETA = 500000.0
ROPE_DIM = SW_HEAD_DIM // 4
DN_ALPHA = (2.0 * DEPTH) ** 0.25
LN_EPS = 1e-5
RMS_EPS = 1e-6
IN_WIDTH = 4 * HG_WIDTH + SW_WIDTH + 2 * SW_KV_WIDTH + SW_WIDTH

OFF_HQ = 0
OFF_HF = OFF_HQ + HG_WIDTH
OFF_HI = OFF_HF + HG_WIDTH
OFF_HG = OFF_HI + HG_WIDTH
OFF_AQ = OFF_HG + HG_WIDTH
OFF_AK = OFF_AQ + SW_WIDTH
OFF_AV = OFF_AK + SW_KV_WIDTH
OFF_AG = OFF_AV + SW_KV_WIDTH

LANES = 128
BF16_SUBLANES = 16
HEADS_PER_TILE = LANES // SW_HEAD_DIM
assert HEADS_PER_TILE == 2 and SW_GROUP == 2 * HEADS_PER_TILE and SW_KV_WIDTH == LANES
VT_ROWS = SW_HEAD_DIM + BF16_SUBLANES
TIME_TILE = 256
VMEM_LIMIT_BYTES = 48 * 1024 * 1024

F32 = jnp.float32
BF16 = jnp.bfloat16
NT_DIMS = (((1,), (1,)), ((), ()))
TN_DIMS = (((0,), (0,)), ((), ()))


def _sigmoid(v):
    return 1.0 / (1.0 + jnp.exp(-v))


def _layer_kernel(sinks_ref, x_ref, win_ref, wout_ref, lbl_ref, nw_ref, lng_ref, lnb_ref,
                  cos_ref, sin_ref, o_ref,
                  h_ref, mix_ref, st_ref, kwin_ref, vt_ref):
    t = pl.program_id(1)
    tt = x_ref.shape[1]
    n_chunks = tt // HG_CHUNK
    n_blocks = tt // WINDOW

    @pl.when(t == 0)
    def _reset_carries():
        st_ref[...] = jnp.zeros_like(st_ref)
        kwin_ref[:, 0:WINDOW, :] = jnp.zeros((2 * SW_KV_HEADS, WINDOW, LANES), BF16)
        vt_ref[:, 0:SW_HEAD_DIM, 0:WINDOW] = jnp.zeros((SW_KV_HEADS, SW_HEAD_DIM, WINDOW), BF16)
        vt_ref[:, SW_HEAD_DIM:VT_ROWS, :] = jnp.ones(
            (SW_KV_HEADS, BF16_SUBLANES, tt + WINDOW), BF16)

    x = x_ref[0]
    h_ref[...] = jnp.dot(x.astype(BF16), win_ref[...], preferred_element_type=F32)

    lbl = lbl_ref[...]
    lbl_e = jnp.exp(lbl - jnp.max(lbl, axis=0, keepdims=True))
    lb = lbl_e[0:1] / jnp.sum(lbl_e, axis=0, keepdims=True)
    nw = nw_ref[...]
    row = lax.broadcasted_iota(jnp.int32, (HG_CHUNK, HG_CHUNK), 0)
    col = lax.broadcasted_iota(jnp.int32, (HG_CHUNK, HG_CHUNK), 1)
    tril = col <= row
    cum_mat = tril.astype(F32)

    for c in range(n_chunks):
        r0 = c * HG_CHUNK
        rows = slice(r0, r0 + HG_CHUNK)
        hq = h_ref[rows, OFF_HQ:OFF_HQ + HG_WIDTH]
        hf = h_ref[rows, OFF_HF:OFF_HF + HG_WIDTH]
        f = lb + (1.0 - lb) * _sigmoid(hf)
        log_f = jnp.log(f)
        k_in = 1.0 - f
        g_cum = jnp.dot(cum_mat, log_f, precision=lax.Precision.HIGHEST,
                        preferred_element_type=F32)
        g_last = g_cum[HG_CHUNK - 1:HG_CHUNK, :]
        q_dec = (hq * _sigmoid(hq)) * jnp.exp(g_cum)
        k_dec = k_in * jnp.exp(-g_cum)
        k_tail = k_in * jnp.exp(g_last - g_cum)
        decay = jnp.exp(g_last)
        for hd in range(HG_HEADS):
            sl = slice(hd * HG_HEAD_DIM, (hd + 1) * HG_HEAD_DIM)
            qd = q_dec[:, sl].astype(BF16)
            kd = k_dec[:, sl].astype(BF16)
            v = h_ref[rows, OFF_HI + hd * HG_HEAD_DIM:OFF_HI + (hd + 1) * HG_HEAD_DIM].astype(BF16)
            a = lax.dot_general(qd, kd, NT_DIMS, preferred_element_type=F32)
            a = jnp.where(tril, a, 0.0)
            s_t = st_ref[hd]
            o = jnp.dot(a.astype(BF16), v, preferred_element_type=F32)
            o = o + lax.dot_general(qd, s_t.astype(BF16), NT_DIMS, preferred_element_type=F32)
            inc_t = lax.dot_general(v, k_tail[:, sl].astype(BF16), TN_DIMS,
                                    preferred_element_type=F32)
            st_ref[hd] = s_t * decay[:, sl] + inc_t
            o = o * lax.rsqrt(jnp.mean(o * o, axis=-1, keepdims=True) + RMS_EPS)
            o = o * nw[:, sl]
            hg = h_ref[rows, OFF_HG + hd * HG_HEAD_DIM:OFF_HG + (hd + 1) * HG_HEAD_DIM]
            mix_ref[rows, sl] = (o * (hg * _sigmoid(hg))).astype(BF16)

    lane = lax.broadcasted_iota(jnp.int32, (1, LANES), 1)
    rope_first = (lane % SW_HEAD_DIM) < (ROPE_DIM // 2)
    lane_lo = lane < SW_HEAD_DIM
    key_idx = lax.broadcasted_iota(jnp.int32, (2 * WINDOW, 2 * WINDOW), 0)
    qry_idx = lax.broadcasted_iota(jnp.int32, (2 * WINDOW, 2 * WINDOW), 1) % WINDOW
    band = (key_idx > qry_idx) & (key_idx <= qry_idx + WINDOW)
    pair_lo = lax.broadcasted_iota(jnp.int32, (1, 2 * WINDOW), 1) < WINDOW
    scale = SW_HEAD_DIM ** -0.5

    def rope(v, cos, sin):
        partner = jnp.where(rope_first,
                            pltpu.roll(v, LANES - ROPE_DIM // 2, 1),
                            pltpu.roll(v, ROPE_DIM // 2, 1))
        return v * cos + partner * sin

    def head_variants(v):
        sw = pltpu.roll(v, SW_HEAD_DIM, 1)
        zero = jnp.zeros_like(v)
        return (jnp.where(lane_lo, v, zero), jnp.where(lane_lo, zero, sw),
                jnp.where(lane_lo, sw, zero), jnp.where(lane_lo, zero, v))

    for blk in range(n_blocks):
        r0 = blk * WINDOW
        rows = slice(r0, r0 + WINDOW)
        cur = slice(r0 + WINDOW, r0 + 2 * WINDOW)
        win = slice(r0, r0 + 2 * WINDOW)
        cos = cos_ref[rows, :]
        sin = sin_ref[rows, :]
        k_rot = rope(h_ref[rows, OFF_AK:OFF_AK + SW_KV_WIDTH], cos, sin)
        for idx, kv in enumerate(head_variants(k_rot)):
            kwin_ref[idx, cur, :] = kv.astype(BF16)
        v_t = h_ref[rows, OFF_AV:OFF_AV + SW_KV_WIDTH].T
        for g in range(SW_KV_HEADS):
            vt_ref[g, 0:SW_HEAD_DIM, cur] = v_t[g * SW_HEAD_DIM:(g + 1) * SW_HEAD_DIM].astype(BF16)
        if blk == 0:
            first_valid = jnp.where(t > 0, 0, WINDOW)
            mask = band & (key_idx >= first_valid)
        else:
            mask = band
        q_bf = []
        for tile in range(SW_WIDTH // LANES):
            q_rot = rope(h_ref[rows, OFF_AQ + tile * LANES:OFF_AQ + (tile + 1) * LANES], cos, sin)
            q_bf.append((q_rot * scale).astype(BF16))
        for g in range(SW_KV_HEADS):
            q_pair = jnp.concatenate([q_bf[2 * g], q_bf[2 * g + 1]], axis=0)
            v_ext = vt_ref[g, :, win]
            o_t = []
            for j in range(HEADS_PER_TILE):
                sink_row = jnp.where(pair_lo, sinks_ref[SW_GROUP * g + j],
                                     sinks_ref[SW_GROUP * g + HEADS_PER_TILE + j])
                s_t = lax.dot_general(kwin_ref[2 * g + j, win, :], q_pair, NT_DIMS,
                                      preferred_element_type=F32)
                s_t = jnp.where(mask, s_t, -jnp.inf)
                m = jnp.maximum(jnp.max(s_t, axis=0, keepdims=True), sink_row)
                p_t = jnp.exp(s_t - m).astype(BF16)
                acc = jnp.dot(v_ext, p_t, preferred_element_type=F32)
                denom = acc[SW_HEAD_DIM:SW_HEAD_DIM + 1, :] + jnp.exp(sink_row - m)
                o_t.append(acc[0:SW_HEAD_DIM, :] * (1.0 / denom))
            for tl in range(2):
                tile = 2 * g + tl
                lanes = slice(tl * LANES, (tl + 1) * LANES)
                o_a = jnp.concatenate([o_t[0][:, lanes], o_t[1][:, lanes]], axis=0).T
                ag = h_ref[rows, OFF_AG + tile * LANES:OFF_AG + (tile + 1) * LANES]
                mix_ref[rows, HG_WIDTH + tile * LANES:HG_WIDTH + (tile + 1) * LANES] = (
                    o_a * (ag * _sigmoid(ag))).astype(BF16)

    kwin_ref[:, 0:WINDOW, :] = kwin_ref[:, tt:tt + WINDOW, :]
    vt_ref[:, 0:SW_HEAD_DIM, 0:WINDOW] = vt_ref[:, 0:SW_HEAD_DIM, tt:tt + WINDOW]

    out = jnp.dot(mix_ref[...], wout_ref[...], preferred_element_type=F32)
    z = DN_ALPHA * x + out
    mu = jnp.mean(z, axis=-1, keepdims=True)
    zc = z - mu
    var = jnp.mean(zc * zc, axis=-1, keepdims=True)
    o_ref[0] = (zc * lax.rsqrt(var + LN_EPS)) * lng_ref[...] + lnb_ref[...]


def _rope_tables(seq_len):
    pos = jnp.arange(seq_len, dtype=F32)
    inv_freq = ROPE_THETA ** (-jnp.arange(0, ROPE_DIM, 2, dtype=F32) / ROPE_DIM)
    ang = pos[:, None] * inv_freq[None, :]
    cos = jnp.cos(ang)
    sin = jnp.sin(ang)
    ones = jnp.ones((seq_len, SW_HEAD_DIM - ROPE_DIM), F32)
    cos_head = jnp.concatenate([cos, cos, ones], axis=1)
    sin_head = jnp.concatenate([-sin, sin, jnp.zeros_like(ones)], axis=1)
    return (jnp.tile(cos_head, (1, HEADS_PER_TILE)), jnp.tile(sin_head, (1, HEADS_PER_TILE)))


def kernel(x, w_in, lb_logits, hg_norm_w, sinks, w_out, ln_g, ln_b):
    batch, seq_len, d_model = x.shape
    assert d_model == D_MODEL and w_in.shape == (DEPTH, D_MODEL, IN_WIDTH)
    assert seq_len % TIME_TILE == 0
    cos_tab, sin_tab = _rope_tables(seq_len)
    n_tiles = seq_len // TIME_TILE

    const = lambda b, t, *_: (0, 0)
    grid_spec = pltpu.PrefetchScalarGridSpec(
        num_scalar_prefetch=1,
        grid=(batch, n_tiles),
        in_specs=[
            pl.BlockSpec((1, TIME_TILE, D_MODEL), lambda b, t, *_: (b, t, 0)),
            pl.BlockSpec((D_MODEL, IN_WIDTH), const),
            pl.BlockSpec((D_MODEL, D_MODEL), const),
            pl.BlockSpec((DEPTH + 1, HG_WIDTH), const),
            pl.BlockSpec((1, HG_WIDTH), const),
            pl.BlockSpec((1, D_MODEL), const),
            pl.BlockSpec((1, D_MODEL), const),
            pl.BlockSpec((TIME_TILE, LANES), lambda b, t, *_: (t, 0)),
            pl.BlockSpec((TIME_TILE, LANES), lambda b, t, *_: (t, 0)),
        ],
        out_specs=pl.BlockSpec((1, TIME_TILE, D_MODEL), lambda b, t, *_: (b, t, 0)),
        scratch_shapes=[
            pltpu.VMEM((TIME_TILE, IN_WIDTH), F32),
            pltpu.VMEM((TIME_TILE, D_MODEL), BF16),
            pltpu.VMEM((HG_HEADS, HG_HEAD_DIM, HG_HEAD_DIM), F32),
            pltpu.VMEM((2 * SW_KV_HEADS, TIME_TILE + WINDOW, LANES), BF16),
            pltpu.VMEM((SW_KV_HEADS, VT_ROWS, TIME_TILE + WINDOW), BF16),
        ],
    )
    return pl.pallas_call(
        _layer_kernel,
        grid_spec=grid_spec,
        out_shape=jax.ShapeDtypeStruct(x.shape, x.dtype),
        compiler_params=pltpu.CompilerParams(
            dimension_semantics=("arbitrary", "arbitrary"),
            vmem_limit_bytes=VMEM_LIMIT_BYTES),
        name="hybrid_hgrn2_swa_layer",
    )(sinks[0], x, w_in[0].astype(BF16), w_out[0].astype(BF16), lb_logits, hg_norm_w,
      ln_g, ln_b, cos_tab, sin_tab)
```

```python
import jax
import jax.numpy as jnp
from jax import lax
from jax.experimental import pallas as pl
from jax.experimental.pallas import tpu as pltpu

D_MODEL = 1024
DEPTH = 1
HG_WIDTH = 512
HG_HEAD_DIM = 128
HG_HEADS = HG_WIDTH // HG_HEAD_DIM
HG_CHUNK = 64
SW_WIDTH = 512
SW_HEAD_DIM = 64
SW_Q_HEADS = SW_WIDTH // SW_HEAD_DIM
SW_KV_HEADS = SW_Q_HEADS // 4
SW_KV_WIDTH = SW_KV_HEADS * SW_HEAD_DIM
SW_GROUP = SW_Q_HEADS // SW_KV_HEADS
WINDOW = 128
ROPE_THETA = 500000.0
ROPE_DIM = SW_HEAD_DIM // 4
DN_ALPHA = (2.0 * DEPTH) ** 0.25
LN_EPS = 1e-5
RMS_EPS = 1e-6
IN_WIDTH = 4 * HG_WIDTH + SW_WIDTH + 2 * SW_KV_WIDTH + SW_WIDTH

OFF_HQ = 0
OFF_HF = OFF_HQ + HG_WIDTH
OFF_HI = OFF_HF + HG_WIDTH
OFF_HG = OFF_HI + HG_WIDTH
OFF_AQ = OFF_HG + HG_WIDTH
OFF_AK = OFF_AQ + SW_WIDTH
OFF_AV = OFF_AK + SW_KV_WIDTH
OFF_AG = OFF_AV + SW_KV_WIDTH

LANES = 128
BF16_SUBLANES = 16
MXU_WIDTH = 256
HEADS_PER_TILE = LANES // SW_HEAD_DIM
assert HEADS_PER_TILE == 2 and SW_GROUP == 2 * HEADS_PER_TILE and SW_KV_WIDTH == LANES
VT_ROWS = SW_HEAD_DIM + BF16_SUBLANES
TIME_TILE = 256
TILES_PER_STEP = 2
PROJ_PIECE = MXU_WIDTH
N_PROJ_PIECES = IN_WIDTH // PROJ_PIECE
assert IN_WIDTH % PROJ_PIECE == 0
N_CHUNKS = TIME_TILE // HG_CHUNK
N_BLOCKS = TIME_TILE // WINDOW
VMEM_LIMIT_BYTES = 56 * 1024 * 1024

F32 = jnp.float32
BF16 = jnp.bfloat16
NT_DIMS = (((1,), (1,)), ((), ()))
TN_DIMS = (((0,), (0,)), ((), ()))


def _sigmoid(v):
    return 1.0 / (1.0 + jnp.exp(-v))


def _layer_kernel(sinks_ref, xc_ref, xn_ref, win_ref, wout_ref, lbl_ref, nw_ref, lng_ref, lnb_ref,
                  cos_ref, sin_ref, o_ref,
                  ha_ref, hb_ref, xb_ref, mix_ref, st_ref, kwin_ref, vt_ref):
    b = pl.program_id(0)
    u = pl.program_id(1)
    tt = TIME_TILE

    @pl.when(jnp.logical_and(b == 0, u == 0))
    def _first_projection():
        ha_ref[...] = jnp.dot(xc_ref[0, 0:tt, :].astype(BF16), win_ref[...],
                              preferred_element_type=F32)

    @pl.when(u == 0)
    def _reset_carries():
        st_ref[...] = jnp.zeros_like(st_ref)
        kwin_ref[:, 0:WINDOW, :] = jnp.zeros((2 * SW_KV_HEADS, WINDOW, LANES), BF16)
        vt_ref[:, 0:SW_HEAD_DIM, 0:WINDOW] = jnp.zeros((SW_KV_HEADS, SW_HEAD_DIM, WINDOW), BF16)
        vt_ref[:, SW_HEAD_DIM:VT_ROWS, :] = jnp.ones(
            (SW_KV_HEADS, BF16_SUBLANES, tt + WINDOW), BF16)

    lbl = lbl_ref[...]
    lbl_e = jnp.exp(lbl - jnp.max(lbl, axis=0, keepdims=True))
    lb = lbl_e[0:1] / jnp.sum(lbl_e, axis=0, keepdims=True)
    nw = nw_ref[...]
    row = lax.broadcasted_iota(jnp.int32, (HG_CHUNK, HG_CHUNK), 0)
    col = lax.broadcasted_iota(jnp.int32, (HG_CHUNK, HG_CHUNK), 1)
    tril = col <= row
    cum_mat = tril.astype(BF16)
    cum3 = jnp.concatenate([cum_mat, cum_mat, cum_mat], axis=1)

    lane = lax.broadcasted_iota(jnp.int32, (1, LANES), 1)
    rope_first = (lane % SW_HEAD_DIM) < (ROPE_DIM // 2)
    lane_lo = lane < SW_HEAD_DIM
    key_idx = lax.broadcasted_iota(jnp.int32, (2 * WINDOW, 2 * WINDOW), 0)
    qry_idx = lax.broadcasted_iota(jnp.int32, (2 * WINDOW, 2 * WINDOW), 1) % WINDOW
    band = (key_idx > qry_idx) & (key_idx <= qry_idx + WINDOW)
    pair_lo = lax.broadcasted_iota(jnp.int32, (1, 2 * WINDOW), 1) < WINDOW
    scale = SW_HEAD_DIM ** -0.5

    def rope(v, cos, sin):
        partner = jnp.where(rope_first,
                            pltpu.roll(v, LANES - ROPE_DIM // 2, 1),
                            pltpu.roll(v, ROPE_DIM // 2, 1))
        return v * cos + partner * sin

    def head_variants(v):
        sw = pltpu.roll(v, SW_HEAD_DIM, 1)
        zero = jnp.zeros_like(v)
        return (jnp.where(lane_lo, v, zero), jnp.where(lane_lo, zero, sw),
                jnp.where(lane_lo, sw, zero), jnp.where(lane_lo, zero, v))

    def process_tile(sub, h_ref, h_next_ref, x_next_rows):
        base = sub * tt
        mix = mix_ref.at[sub]
        xb = xb_ref.at[sub]
        xb[...] = x_next_rows.astype(BF16)

        pieces = iter(range(N_PROJ_PIECES))

        def project_next_piece():
            k = next(pieces, None)
            if k is not None:
                cols = slice(k * PROJ_PIECE, (k + 1) * PROJ_PIECE)
                h_next_ref[:, cols] = jnp.dot(xb[...], win_ref[:, cols],
                                              preferred_element_type=F32)

        def hg_pre(c):
            rows = slice(c * HG_CHUNK, (c + 1) * HG_CHUNK)
            f = lb + (1.0 - lb) * _sigmoid(h_ref[rows, OFF_HF:OFF_HF + HG_WIDTH])
            log_f = jnp.log(f)
            hi = log_f.astype(BF16)
            rem = log_f - hi.astype(F32)
            mid = rem.astype(BF16)
            lo = (rem - mid.astype(F32)).astype(BF16)
            return 1.0 - f, jnp.concatenate([hi, mid, lo], axis=0)

        def hg_cum(parts):
            return jnp.dot(cum3, parts, preferred_element_type=F32)

        def hg_decays(c, k_in, g_cum):
            rows = slice(c * HG_CHUNK, (c + 1) * HG_CHUNK)
            hq = h_ref[rows, OFF_HQ:OFF_HQ + HG_WIDTH]
            g_last = g_cum[HG_CHUNK - 1:HG_CHUNK, :]
            q_dec = ((hq * _sigmoid(hq)) * jnp.exp(g_cum)).astype(BF16)
            k_dec = (k_in * jnp.exp(-g_cum)).astype(BF16)
            k_tail = (k_in * jnp.exp(g_last - g_cum)).astype(BF16)
            return q_dec, k_dec, k_tail, jnp.exp(g_last)

        def hg_values(c, hd):
            rows = slice(c * HG_CHUNK, (c + 1) * HG_CHUNK)
            return h_ref[rows, OFF_HI + hd * HG_HEAD_DIM:OFF_HI + (hd + 1) * HG_HEAD_DIM].astype(BF16)

        def hg_scores(q_dec, k_dec, hd):
            sl = slice(hd * HG_HEAD_DIM, (hd + 1) * HG_HEAD_DIM)
            return lax.dot_general(q_dec[:, sl], k_dec[:, sl], NT_DIMS, preferred_element_type=F32)

        def hg_output(c, hd, a, q_dec, k_tail, decay):
            rows = slice(c * HG_CHUNK, (c + 1) * HG_CHUNK)
            sl = slice(hd * HG_HEAD_DIM, (hd + 1) * HG_HEAD_DIM)
            v = hg_values(c, hd)
            s_t = st_ref[hd]
            o = jnp.dot(jnp.where(tril, a, 0.0).astype(BF16), v, preferred_element_type=F32)
            o = o + lax.dot_general(q_dec[:, sl], s_t.astype(BF16), NT_DIMS,
                                    preferred_element_type=F32)
            inc_t = lax.dot_general(v, k_tail[:, sl], TN_DIMS, preferred_element_type=F32)
            st_ref[hd] = s_t * decay[:, sl] + inc_t
            o = o * lax.rsqrt(jnp.mean(o * o, axis=-1, keepdims=True) + RMS_EPS)
            o = o * nw[:, sl]
            hg = h_ref[rows, OFF_HG + hd * HG_HEAD_DIM:OFF_HG + (hd + 1) * HG_HEAD_DIM]
            mix[rows, sl] = (o * (hg * _sigmoid(hg))).astype(BF16)

        pre = {}
        cum = {}
        for c in range(min(2, N_CHUNKS)):
            pre[c] = hg_pre(c)
            cum[c] = hg_cum(pre[c][1])
            project_next_piece()
        for c in range(N_CHUNKS):
            q_dec, k_dec, k_tail, decay = hg_decays(c, pre[c][0], cum[c])
            scores = [hg_scores(q_dec, k_dec, hd) for hd in range(HG_HEADS)]
            if c + 2 < N_CHUNKS:
                pre[c + 2] = hg_pre(c + 2)
                cum[c + 2] = hg_cum(pre[c + 2][1])
            project_next_piece()
            for hd in range(HG_HEADS):
                hg_output(c, hd, scores[hd], q_dec, k_tail, decay)
            project_next_piece()

        def sw_prepare(blk):
            r0 = blk * WINDOW
            rows = slice(r0, r0 + WINDOW)
            cur = slice(r0 + WINDOW, r0 + 2 * WINDOW)
            cos = cos_ref[base + r0:base + r0 + WINDOW, :]
            sin = sin_ref[base + r0:base + r0 + WINDOW, :]
            k_rot = rope(h_ref[rows, OFF_AK:OFF_AK + SW_KV_WIDTH], cos, sin)
            for idx, kv in enumerate(head_variants(k_rot)):
                kwin_ref[idx, cur, :] = kv.astype(BF16)
            v_t = h_ref[rows, OFF_AV:OFF_AV + SW_KV_WIDTH].T
            for g in range(SW_KV_HEADS):
                vt_ref[g, 0:SW_HEAD_DIM, cur] = (
                    v_t[g * SW_HEAD_DIM:(g + 1) * SW_HEAD_DIM].astype(BF16))
            q_bf = []
            for tile in range(SW_WIDTH // LANES):
                q_rot = rope(h_ref[rows, OFF_AQ + tile * LANES:OFF_AQ + (tile + 1) * LANES], cos, sin)
                q_bf.append((q_rot * scale).astype(BF16))
            return [jnp.concatenate([q_bf[2 * g], q_bf[2 * g + 1]], axis=0)
                    for g in range(SW_KV_HEADS)]

        def sw_scores(blk, g, j, q_pair):
            win = slice(blk * WINDOW, blk * WINDOW + 2 * WINDOW)
            return lax.dot_general(kwin_ref[2 * g + j, win, :], q_pair, NT_DIMS,
                                   preferred_element_type=F32)

        def sw_values(blk, g, j, s_t, mask):
            win = slice(blk * WINDOW, blk * WINDOW + 2 * WINDOW)
            sink_row = jnp.where(pair_lo, sinks_ref[SW_GROUP * g + j],
                                 sinks_ref[SW_GROUP * g + HEADS_PER_TILE + j])
            s_t = jnp.where(mask, s_t, -jnp.inf)
            m = jnp.maximum(jnp.max(s_t, axis=0, keepdims=True), sink_row)
            p_t = jnp.exp(s_t - m).astype(BF16)
            acc = jnp.dot(vt_ref[g, :, win], p_t, preferred_element_type=F32)
            denom = acc[SW_HEAD_DIM:SW_HEAD_DIM + 1, :] + jnp.exp(sink_row - m)
            return acc[0:SW_HEAD_DIM, :] * (1.0 / denom)

        def sw_finish(blk, g, o_t):
            rows = slice(blk * WINDOW, (blk + 1) * WINDOW)
            for tl in range(2):
                tile = 2 * g + tl
                lanes = slice(tl * LANES, (tl + 1) * LANES)
                o_a = jnp.concatenate([o_t[0][:, lanes], o_t[1][:, lanes]], axis=0).T
                ag = h_ref[rows, OFF_AG + tile * LANES:OFF_AG + (tile + 1) * LANES]
                mix[rows, HG_WIDTH + tile * LANES:HG_WIDTH + (tile + 1) * LANES] = (
                    o_a * (ag * _sigmoid(ag))).astype(BF16)

        q_pairs = [sw_prepare(blk) for blk in range(N_BLOCKS)]
        masks = []
        for blk in range(N_BLOCKS):
            if sub == 0 and blk == 0:
                first_valid = jnp.where(u > 0, 0, WINDOW)
                masks.append(band & (key_idx >= first_valid))
            else:
                masks.append(band)
        chains = [(blk, g, j) for blk in range(N_BLOCKS) for g in range(SW_KV_HEADS)
                  for j in range(HEADS_PER_TILE)]
        scores = {}
        for i in range(min(2, len(chains))):
            blk, g, j = chains[i]
            scores[i] = sw_scores(blk, g, j, q_pairs[blk][g])
        project_next_piece()
        outs = {}
        for i, (blk, g, j) in enumerate(chains):
            outs[(blk, g, j)] = sw_values(blk, g, j, scores.pop(i), masks[blk])
            if i + 2 < len(chains):
                nb, ng, nj = chains[i + 2]
                scores[i + 2] = sw_scores(nb, ng, nj, q_pairs[nb][ng])
            if j == HEADS_PER_TILE - 1:
                sw_finish(blk, g, [outs.pop((blk, g, jj)) for jj in range(HEADS_PER_TILE)])
            if i % 2 == 1:
                project_next_piece()
        for _ in range(N_PROJ_PIECES):
            project_next_piece()

        kwin_ref[:, 0:WINDOW, :] = kwin_ref[:, tt:tt + WINDOW, :]
        vt_ref[:, 0:SW_HEAD_DIM, 0:WINDOW] = vt_ref[:, 0:SW_HEAD_DIM, tt:tt + WINDOW]

        out = jnp.dot(mix[...], wout_ref[...], preferred_element_type=F32)
        z = DN_ALPHA * xc_ref[0, base:base + tt, :] + out
        mu = jnp.mean(z, axis=-1, keepdims=True)
        zc = z - mu
        var = jnp.mean(zc * zc, axis=-1, keepdims=True)
        o_ref[0, base:base + tt, :] = (zc * lax.rsqrt(var + LN_EPS)) * lng_ref[...] + lnb_ref[...]

    process_tile(0, ha_ref, hb_ref, xc_ref[0, tt:2 * tt, :])
    process_tile(1, hb_ref, ha_ref, xn_ref[0])


def _rope_tables(seq_len):
    pos = jnp.arange(seq_len, dtype=F32)
    inv_freq = ROPE_THETA ** (-jnp.arange(0, ROPE_DIM, 2, dtype=F32) / ROPE_DIM)
    ang = pos[:, None] * inv_freq[None, :]
    cos = jnp.cos(ang)
    sin = jnp.sin(ang)
    ones = jnp.ones((seq_len, SW_HEAD_DIM - ROPE_DIM), F32)
    cos_head = jnp.concatenate([cos, cos, ones], axis=1)
    sin_head = jnp.concatenate([-sin, sin, jnp.zeros_like(ones)], axis=1)
    return (jnp.tile(cos_head, (1, HEADS_PER_TILE)), jnp.tile(sin_head, (1, HEADS_PER_TILE)))


def kernel(x, w_in, lb_logits, hg_norm_w, sinks, w_out, ln_g, ln_b):
    batch, seq_len, d_model = x.shape
    assert d_model == D_MODEL and w_in.shape == (DEPTH, D_MODEL, IN_WIDTH)
    step_rows = TILES_PER_STEP * TIME_TILE
    assert seq_len % step_rows == 0
    cos_tab, sin_tab = _rope_tables(seq_len)
    n_steps = seq_len // step_rows
    last_step = batch * n_steps - 1

    def next_tile_index(b, u, *_):
        nxt = jnp.minimum(b * n_steps + u + 1, last_step)
        return (nxt // n_steps, TILES_PER_STEP * (nxt % n_steps), 0)

    const = lambda b, u, *_: (0, 0)
    grid_spec = pltpu.PrefetchScalarGridSpec(
        num_scalar_prefetch=1,
        grid=(batch, n_steps),
        in_specs=[
            pl.BlockSpec((1, step_rows, D_MODEL), lambda b, u, *_: (b, u, 0)),
            pl.BlockSpec((1, TIME_TILE, D_MODEL), next_tile_index),
            pl.BlockSpec((D_MODEL, IN_WIDTH), const),
            pl.BlockSpec((D_MODEL, D_MODEL), const),
            pl.BlockSpec((DEPTH + 1, HG_WIDTH), const),
            pl.BlockSpec((1, HG_WIDTH), const),
            pl.BlockSpec((1, D_MODEL), const),
            pl.BlockSpec((1, D_MODEL), const),
            pl.BlockSpec((step_rows, LANES), lambda b, u, *_: (u, 0)),
            pl.BlockSpec((step_rows, LANES), lambda b, u, *_: (u, 0)),
        ],
        out_specs=pl.BlockSpec((1, step_rows, D_MODEL), lambda b, u, *_: (b, u, 0)),
        scratch_shapes=[
            pltpu.VMEM((TIME_TILE, IN_WIDTH), F32),
            pltpu.VMEM((TIME_TILE, IN_WIDTH), F32),
            pltpu.VMEM((TILES_PER_STEP, TIME_TILE, D_MODEL), BF16),
            pltpu.VMEM((TILES_PER_STEP, TIME_TILE, D_MODEL), BF16),
            pltpu.VMEM((HG_HEADS, HG_HEAD_DIM, HG_HEAD_DIM), F32),
            pltpu.VMEM((2 * SW_KV_HEADS, TIME_TILE + WINDOW, LANES), BF16),
            pltpu.VMEM((SW_KV_HEADS, VT_ROWS, TIME_TILE + WINDOW), BF16),
        ],
    )
    return pl.pallas_call(
        _layer_kernel,
        grid_spec=grid_spec,
        out_shape=jax.ShapeDtypeStruct(x.shape, x.dtype),
        compiler_params=pltpu.CompilerParams(
            dimension_semantics=("arbitrary", "arbitrary"),
            vmem_limit_bytes=VMEM_LIMIT_BYTES),
        name="hybrid_hgrn2_swa_layer",
    )(sinks[0], x, x, w_in[0].astype(BF16), w_out[0].astype(BF16), lb_logits, hg_norm_w,
      ln_g, ln_b, cos_tab, sin_tab)
```

```python
import jax
import jax.numpy as jnp
from jax import lax
from jax.experimental import pallas as pl
from jax.experimental.pallas import tpu as pltpu

D_MODEL = 1024
DEPTH = 1
HG_WIDTH = 512
HG_HEAD_DIM = 128
HG_HEADS = HG_WIDTH // HG_HEAD_DIM
HG_CHUNK = 64
SW_WIDTH = 512
SW_HEAD_DIM = 64
SW_Q_HEADS = SW_WIDTH // SW_HEAD_DIM
SW_KV_HEADS = SW_Q_HEADS // 4
SW_KV_WIDTH = SW_KV_HEADS * SW_HEAD_DIM
SW_GROUP = SW_Q_HEADS // SW_KV_HEADS
WINDOW = 128
ROPE_THETA = 500000.0
ROPE_DIM = SW_HEAD_DIM // 4
DN_ALPHA = (2.0 * DEPTH) ** 0.25
LN_EPS = 1e-5
RMS_EPS = 1e-6
IN_WIDTH = 4 * HG_WIDTH + SW_WIDTH + 2 * SW_KV_WIDTH + SW_WIDTH

OFF_HQ = 0
OFF_HF = OFF_HQ + HG_WIDTH
OFF_HI = OFF_HF + HG_WIDTH
OFF_HG = OFF_HI + HG_WIDTH
OFF_AQ = OFF_HG + HG_WIDTH
OFF_AK = OFF_AQ + SW_WIDTH
OFF_AV = OFF_AK + SW_KV_WIDTH
OFF_AG = OFF_AV + SW_KV_WIDTH

LANES = 128
BF16_SUBLANES = 16
MXU_WIDTH = 256
HEADS_PER_TILE = LANES // SW_HEAD_DIM
assert HEADS_PER_TILE == 2 and SW_GROUP == 2 * HEADS_PER_TILE and SW_KV_WIDTH == LANES
VT_ROWS = SW_HEAD_DIM + BF16_SUBLANES
TIME_TILE = 256
TILES_PER_STEP = 4
PIECE = MXU_WIDTH
N_IN_PIECES = IN_WIDTH // PIECE
N_OUT_PIECES = D_MODEL // PIECE
assert IN_WIDTH % PIECE == 0 and D_MODEL % PIECE == 0
N_CHUNKS = TIME_TILE // HG_CHUNK
N_BLOCKS = TIME_TILE // WINDOW
VMEM_LIMIT_BYTES = 58 * 1024 * 1024

F32 = jnp.float32
BF16 = jnp.bfloat16
NT_DIMS = (((1,), (1,)), ((), ()))
TN_DIMS = (((0,), (0,)), ((), ()))


def _sigmoid(v):
    return 1.0 / (1.0 + jnp.exp(-v))


def _layer_kernel(sinks_ref, xc_ref, xn_ref, win_ref, wout_ref, lbl_ref, nw_ref, lng_ref, lnb_ref,
                  cos_ref, sin_ref, o_ref,
                  ha_ref, hb_ref, xb_ref, mix_ref, out_ref, st_ref, kwin_ref, vt_ref):
    b = pl.program_id(0)
    u = pl.program_id(1)
    tt = TIME_TILE
    h_bufs = (ha_ref, hb_ref)

    @pl.when(jnp.logical_and(b == 0, u == 0))
    def _first_projection():
        ha_ref[...] = jnp.dot(xc_ref[0, 0:tt, :].astype(BF16), win_ref[...],
                              preferred_element_type=F32)

    @pl.when(u == 0)
    def _reset_carries():
        st_ref[...] = jnp.zeros_like(st_ref)
        kwin_ref[:, 0:WINDOW, :] = jnp.zeros((2 * SW_KV_HEADS, WINDOW, LANES), BF16)
        vt_ref[:, 0:SW_HEAD_DIM, 0:WINDOW] = jnp.zeros((SW_KV_HEADS, SW_HEAD_DIM, WINDOW), BF16)
        vt_ref[:, SW_HEAD_DIM:VT_ROWS, :] = jnp.ones(
            (SW_KV_HEADS, BF16_SUBLANES, tt + WINDOW), BF16)

    lbl = lbl_ref[...]
    lbl_e = jnp.exp(lbl - jnp.max(lbl, axis=0, keepdims=True))
    lb = lbl_e[0:1] / jnp.sum(lbl_e, axis=0, keepdims=True)
    nw = nw_ref[...]
    row = lax.broadcasted_iota(jnp.int32, (HG_CHUNK, HG_CHUNK), 0)
    col = lax.broadcasted_iota(jnp.int32, (HG_CHUNK, HG_CHUNK), 1)
    tril = col <= row
    cum_mat = tril.astype(BF16)
    cum3 = jnp.concatenate([cum_mat, cum_mat, cum_mat], axis=1)

    lane = lax.broadcasted_iota(jnp.int32, (1, LANES), 1)
    rope_first = (lane % SW_HEAD_DIM) < (ROPE_DIM // 2)
    lane_lo = lane < SW_HEAD_DIM
    key_idx = lax.broadcasted_iota(jnp.int32, (2 * WINDOW, 2 * WINDOW), 0)
    qry_idx = lax.broadcasted_iota(jnp.int32, (2 * WINDOW, 2 * WINDOW), 1) % WINDOW
    band = (key_idx > qry_idx) & (key_idx <= qry_idx + WINDOW)
    pair_lo = lax.broadcasted_iota(jnp.int32, (1, 2 * WINDOW), 1) < WINDOW
    scale = SW_HEAD_DIM ** -0.5

    def rope(v, cos, sin):
        partner = jnp.where(rope_first,
                            pltpu.roll(v, LANES - ROPE_DIM // 2, 1),
                            pltpu.roll(v, ROPE_DIM // 2, 1))
        return v * cos + partner * sin

    def head_variants(v):
        sw = pltpu.roll(v, SW_HEAD_DIM, 1)
        zero = jnp.zeros_like(v)
        return (jnp.where(lane_lo, v, zero), jnp.where(lane_lo, zero, sw),
                jnp.where(lane_lo, sw, zero), jnp.where(lane_lo, zero, v))

    def in_projection_piece(k, piece):
        xb = xb_ref.at[k % 2]
        cols = slice(piece * PIECE, (piece + 1) * PIECE)
        h_bufs[k % 2][:, cols] = jnp.dot(xb[...], win_ref[:, cols], preferred_element_type=F32)

    def out_projection_piece(k, piece):
        cols = slice(piece * PIECE, (piece + 1) * PIECE)
        out_ref[:, cols] = jnp.dot(mix_ref[k], wout_ref[:, cols], preferred_element_type=F32)

    def post_norm(k):
        rows = slice(k * tt, (k + 1) * tt)
        z = DN_ALPHA * xc_ref[0, rows, :] + out_ref[...]
        mu = jnp.mean(z, axis=-1, keepdims=True)
        zc = z - mu
        var = jnp.mean(zc * zc, axis=-1, keepdims=True)
        o_ref[0, rows, :] = (zc * lax.rsqrt(var + LN_EPS)) * lng_ref[...] + lnb_ref[...]

    def make_fillers(k):
        stages = []
        if k > 0:
            for piece in range(N_OUT_PIECES):
                stages.append(lambda piece=piece: out_projection_piece(k - 1, piece))
            stages.append(lambda: post_norm(k - 1))
        for piece in range(N_IN_PIECES):
            stages.append(lambda piece=piece: in_projection_piece(k + 1, piece))
        return iter(stages)

    def process_tile(k):
        h_ref = h_bufs[k % 2]
        base = k * tt
        mix = mix_ref.at[k]
        if k + 1 < TILES_PER_STEP:
            x_next_rows = xc_ref[0, (k + 1) * tt:(k + 2) * tt, :]
        else:
            x_next_rows = xn_ref[0]
        xb_ref[(k + 1) % 2] = x_next_rows.astype(BF16)
        fillers = make_fillers(k)

        def fill(n=1):
            for _ in range(n):
                stage = next(fillers, None)
                if stage is not None:
                    stage()

        def hg_pre(c):
            rows = slice(c * HG_CHUNK, (c + 1) * HG_CHUNK)
            f = lb + (1.0 - lb) * _sigmoid(h_ref[rows, OFF_HF:OFF_HF + HG_WIDTH])
            log_f = jnp.log(f)
            hi = log_f.astype(BF16)
            rem = log_f - hi.astype(F32)
            mid = rem.astype(BF16)
            lo = (rem - mid.astype(F32)).astype(BF16)
            return 1.0 - f, jnp.concatenate([hi, mid, lo], axis=0)

        def hg_cum(parts):
            return jnp.dot(cum3, parts, preferred_element_type=F32)

        def hg_decays(c, k_in, g_cum):
            rows = slice(c * HG_CHUNK, (c + 1) * HG_CHUNK)
            hq = h_ref[rows, OFF_HQ:OFF_HQ + HG_WIDTH]
            g_last = g_cum[HG_CHUNK - 1:HG_CHUNK, :]
            q_dec = ((hq * _sigmoid(hq)) * jnp.exp(g_cum)).astype(BF16)
            k_dec = (k_in * jnp.exp(-g_cum)).astype(BF16)
            k_tail = (k_in * jnp.exp(g_last - g_cum)).astype(BF16)
            return q_dec, k_dec, k_tail, jnp.exp(g_last)

        def hg_scores(q_dec, k_dec, hd):
            sl = slice(hd * HG_HEAD_DIM, (hd + 1) * HG_HEAD_DIM)
            return lax.dot_general(q_dec[:, sl], k_dec[:, sl], NT_DIMS, preferred_element_type=F32)

        def hg_output(c, hd, a, q_dec, k_tail, decay):
            rows = slice(c * HG_CHUNK, (c + 1) * HG_CHUNK)
            sl = slice(hd * HG_HEAD_DIM, (hd + 1) * HG_HEAD_DIM)
            v = h_ref[rows, OFF_HI + hd * HG_HEAD_DIM:OFF_HI + (hd + 1) * HG_HEAD_DIM].astype(BF16)
            s_t = st_ref[hd]
            o = jnp.dot(jnp.where(tril, a, 0.0).astype(BF16), v, preferred_element_type=F32)
            o = o + lax.dot_general(q_dec[:, sl], s_t.astype(BF16), NT_DIMS,
                                    preferred_element_type=F32)
            inc_t = lax.dot_general(v, k_tail[:, sl], TN_DIMS, preferred_element_type=F32)
            st_ref[hd] = s_t * decay[:, sl] + inc_t
            o = o * lax.rsqrt(jnp.mean(o * o, axis=-1, keepdims=True) + RMS_EPS)
            o = o * nw[:, sl]
            hg = h_ref[rows, OFF_HG + hd * HG_HEAD_DIM:OFF_HG + (hd + 1) * HG_HEAD_DIM]
            mix[rows, sl] = (o * (hg * _sigmoid(hg))).astype(BF16)

        pre = {}
        cum = {}
        fill(2)
        for c in range(min(2, N_CHUNKS)):
            pre[c] = hg_pre(c)
            cum[c] = hg_cum(pre[c][1])
            fill()
        for c in range(N_CHUNKS):
            q_dec, k_dec, k_tail, decay = hg_decays(c, pre[c][0], cum[c])
            scores = [hg_scores(q_dec, k_dec, hd) for hd in range(HG_HEADS)]
            if c + 2 < N_CHUNKS:
                pre[c + 2] = hg_pre(c + 2)
                cum[c + 2] = hg_cum(pre[c + 2][1])
            fill()
            for hd in range(HG_HEADS):
                hg_output(c, hd, scores[hd], q_dec, k_tail, decay)
            fill()

        def sw_prepare(blk):
            r0 = blk * WINDOW
            rows = slice(r0, r0 + WINDOW)
            cur = slice(r0 + WINDOW, r0 + 2 * WINDOW)
            cos = cos_ref[base + r0:base + r0 + WINDOW, :]
            sin = sin_ref[base + r0:base + r0 + WINDOW, :]
            k_rot = rope(h_ref[rows, OFF_AK:OFF_AK + SW_KV_WIDTH], cos, sin)
            for idx, kv in enumerate(head_variants(k_rot)):
                kwin_ref[idx, cur, :] = kv.astype(BF16)
            v_t = h_ref[rows, OFF_AV:OFF_AV + SW_KV_WIDTH].T
            for g in range(SW_KV_HEADS):
                vt_ref[g, 0:SW_HEAD_DIM, cur] = (
                    v_t[g * SW_HEAD_DIM:(g + 1) * SW_HEAD_DIM].astype(BF16))
            q_bf = []
            for tile in range(SW_WIDTH // LANES):
                q_rot = rope(h_ref[rows, OFF_AQ + tile * LANES:OFF_AQ + (tile + 1) * LANES], cos, sin)
                q_bf.append((q_rot * scale).astype(BF16))
            return [jnp.concatenate([q_bf[2 * g], q_bf[2 * g + 1]], axis=0)
                    for g in range(SW_KV_HEADS)]

        def sw_scores(blk, g, j, q_pair):
            win = slice(blk * WINDOW, blk * WINDOW + 2 * WINDOW)
            return lax.dot_general(kwin_ref[2 * g + j, win, :], q_pair, NT_DIMS,
                                   preferred_element_type=F32)

        def sw_values(blk, g, j, s_t, mask):
            win = slice(blk * WINDOW, blk * WINDOW + 2 * WINDOW)
            sink_row = jnp.where(pair_lo, sinks_ref[SW_GROUP * g + j],
                                 sinks_ref[SW_GROUP * g + HEADS_PER_TILE + j])
            s_t = jnp.where(mask, s_t, -jnp.inf)
            m = jnp.maximum(jnp.max(s_t, axis=0, keepdims=True), sink_row)
            p_t = jnp.exp(s_t - m).astype(BF16)
            acc = jnp.dot(vt_ref[g, :, win], p_t, preferred_element_type=F32)
            denom = acc[SW_HEAD_DIM:SW_HEAD_DIM + 1, :] + jnp.exp(sink_row - m)
            return acc[0:SW_HEAD_DIM, :] * (1.0 / denom)

        def sw_finish(blk, g, o_t):
            rows = slice(blk * WINDOW, (blk + 1) * WINDOW)
            for tl in range(2):
                tile = 2 * g + tl
                lanes = slice(tl * LANES, (tl + 1) * LANES)
                o_a = jnp.concatenate([o_t[0][:, lanes], o_t[1][:, lanes]], axis=0).T
                ag = h_ref[rows, OFF_AG + tile * LANES:OFF_AG + (tile + 1) * LANES]
                mix[rows, HG_WIDTH + tile * LANES:HG_WIDTH + (tile + 1) * LANES] = (
                    o_a * (ag * _sigmoid(ag))).astype(BF16)

        q_pairs = [sw_prepare(blk) for blk in range(N_BLOCKS)]
        masks = []
        for blk in range(N_BLOCKS):
            if k == 0 and blk == 0:
                first_valid = jnp.where(u > 0, 0, WINDOW)
                masks.append(band & (key_idx >= first_valid))
            else:
                masks.append(band)
        chains = [(blk, g, j) for blk in range(N_BLOCKS) for g in range(SW_KV_HEADS)
                  for j in range(HEADS_PER_TILE)]
        scores = {}
        for i in range(min(2, len(chains))):
            blk, g, j = chains[i]
            scores[i] = sw_scores(blk, g, j, q_pairs[blk][g])
        fill()
        outs = {}
        for i, (blk, g, j) in enumerate(chains):
            outs[(blk, g, j)] = sw_values(blk, g, j, scores.pop(i), masks[blk])
            if i + 2 < len(chains):
                nb, ng, nj = chains[i + 2]
                scores[i + 2] = sw_scores(nb, ng, nj, q_pairs[nb][ng])
            if j == HEADS_PER_TILE - 1:
                sw_finish(blk, g, [outs.pop((blk, g, jj)) for jj in range(HEADS_PER_TILE)])
            fill()
        fill(N_OUT_PIECES + 1 + N_IN_PIECES)

        kwin_ref[:, 0:WINDOW, :] = kwin_ref[:, tt:tt + WINDOW, :]
        vt_ref[:, 0:SW_HEAD_DIM, 0:WINDOW] = vt_ref[:, 0:SW_HEAD_DIM, tt:tt + WINDOW]

    for k in range(TILES_PER_STEP):
        process_tile(k)
    for piece in range(N_OUT_PIECES):
        out_projection_piece(TILES_PER_STEP - 1, piece)
    post_norm(TILES_PER_STEP - 1)


def _rope_tables(seq_len):
    pos = jnp.arange(seq_len, dtype=F32)
    inv_freq = ROPE_THETA ** (-jnp.arange(0, ROPE_DIM, 2, dtype=F32) / ROPE_DIM)
    ang = pos[:, None] * inv_freq[None, :]
    cos = jnp.cos(ang)
    sin = jnp.sin(ang)
    ones = jnp.ones((seq_len, SW_HEAD_DIM - ROPE_DIM), F32)
    cos_head = jnp.concatenate([cos, cos, ones], axis=1)
    sin_head = jnp.concatenate([-sin, sin, jnp.zeros_like(ones)], axis=1)
    return (jnp.tile(cos_head, (1, HEADS_PER_TILE)), jnp.tile(sin_head, (1, HEADS_PER_TILE)))


def kernel(x, w_in, lb_logits, hg_norm_w, sinks, w_out, ln_g, ln_b):
    batch, seq_len, d_model = x.shape
    assert d_model == D_MODEL and w_in.shape == (DEPTH, D_MODEL, IN_WIDTH)
    step_rows = TILES_PER_STEP * TIME_TILE
    assert seq_len % step_rows == 0
    cos_tab, sin_tab = _rope_tables(seq_len)
    n_steps = seq_len // step_rows
    last_step = batch * n_steps - 1

    def next_tile_index(b, u, *_):
        nxt = jnp.minimum(b * n_steps + u + 1, last_step)
        return (nxt // n_steps, TILES_PER_STEP * (nxt % n_steps), 0)

    const = lambda b, u, *_: (0, 0)
    resident = dict(pipeline_mode=pl.Buffered(1))
    grid_spec = pltpu.PrefetchScalarGridSpec(
        num_scalar_prefetch=1,
        grid=(batch, n_steps),
        in_specs=[
            pl.BlockSpec((1, step_rows, D_MODEL), lambda b, u, *_: (b, u, 0)),
            pl.BlockSpec((1, TIME_TILE, D_MODEL), next_tile_index),
            pl.BlockSpec((D_MODEL, IN_WIDTH), const, **resident),
            pl.BlockSpec((D_MODEL, D_MODEL), const, **resident),
            pl.BlockSpec((DEPTH + 1, HG_WIDTH), const),
            pl.BlockSpec((1, HG_WIDTH), const),
            pl.BlockSpec((1, D_MODEL), const),
            pl.BlockSpec((1, D_MODEL), const),
            pl.BlockSpec((step_rows, LANES), lambda b, u, *_: (u, 0)),
            pl.BlockSpec((step_rows, LANES), lambda b, u, *_: (u, 0)),
        ],
        out_specs=pl.BlockSpec((1, step_rows, D_MODEL), lambda b, u, *_: (b, u, 0)),
        scratch_shapes=[
            pltpu.VMEM((TIME_TILE, IN_WIDTH), F32),
            pltpu.VMEM((TIME_TILE, IN_WIDTH), F32),
            pltpu.VMEM((2, TIME_TILE, D_MODEL), BF16),
            pltpu.VMEM((TILES_PER_STEP, TIME_TILE, D_MODEL), BF16),
            pltpu.VMEM((TIME_TILE, D_MODEL), F32),
            pltpu.VMEM((HG_HEADS, HG_HEAD_DIM, HG_HEAD_DIM), F32),
            pltpu.VMEM((2 * SW_KV_HEADS, TIME_TILE + WINDOW, LANES), BF16),
            pltpu.VMEM((SW_KV_HEADS, VT_ROWS, TIME_TILE + WINDOW), BF16),
        ],
    )
    return pl.pallas_call(
        _layer_kernel,
        grid_spec=grid_spec,
        out_shape=jax.ShapeDtypeStruct(x.shape, x.dtype),
        compiler_params=pltpu.CompilerParams(
            dimension_semantics=("arbitrary", "arbitrary"),
            vmem_limit_bytes=VMEM_LIMIT_BYTES),
        name="hybrid_hgrn2_swa_layer",
    )(sinks[0], x, x, w_in[0].astype(BF16), w_out[0].astype(BF16), lb_logits, hg_norm_w,
      ln_g, ln_b, cos_tab, sin_tab)
```

```python
import jax
import jax.numpy as jnp
from jax import lax
from jax.experimental import pallas as pl
from jax.experimental.pallas import tpu as pltpu

D_MODEL = 1024
DEPTH = 1
HG_WIDTH = 512
HG_HEAD_DIM = 128
HG_HEADS = HG_WIDTH // HG_HEAD_DIM
HG_CHUNK = 64
SW_WIDTH = 512
SW_HEAD_DIM = 64
SW_Q_HEADS = SW_WIDTH // SW_HEAD_DIM
SW_KV_HEADS = SW_Q_HEADS // 4
SW_KV_WIDTH = SW_KV_HEADS * SW_HEAD_DIM
SW_GROUP = SW_Q_HEADS // SW_KV_HEADS
WINDOW = 128
ROPE_THETA = 500000.0
ROPE_DIM = SW_HEAD_DIM // 4
DN_ALPHA = (2.0 * DEPTH) ** 0.25
LN_EPS = 1e-5
RMS_EPS = 1e-6
IN_WIDTH = 4 * HG_WIDTH + SW_WIDTH + 2 * SW_KV_WIDTH + SW_WIDTH

OFF_HQ = 0
OFF_HF = OFF_HQ + HG_WIDTH
OFF_HI = OFF_HF + HG_WIDTH
OFF_HG = OFF_HI + HG_WIDTH
OFF_AQ = OFF_HG + HG_WIDTH
OFF_AK = OFF_AQ + SW_WIDTH
OFF_AV = OFF_AK + SW_KV_WIDTH
OFF_AG = OFF_AV + SW_KV_WIDTH

LANES = 128
BF16_SUBLANES = 16
MXU_WIDTH = 256
HEADS_PER_TILE = LANES // SW_HEAD_DIM
assert HEADS_PER_TILE == 2 and SW_GROUP == 2 * HEADS_PER_TILE and SW_KV_WIDTH == LANES
VT_ROWS = SW_HEAD_DIM + BF16_SUBLANES
TIME_TILE = 256
TILES_PER_STEP = 4
PIECE = MXU_WIDTH
N_IN_PIECES = IN_WIDTH // PIECE
N_OUT_PIECES = D_MODEL // PIECE
assert IN_WIDTH % PIECE == 0 and D_MODEL % PIECE == 0
N_CHUNKS = TIME_TILE // HG_CHUNK
N_BLOCKS = TIME_TILE // WINDOW
STEP_BLOCKS = TILES_PER_STEP * N_BLOCKS
VMEM_LIMIT_BYTES = 58 * 1024 * 1024

F32 = jnp.float32
BF16 = jnp.bfloat16
NT_DIMS = (((1,), (1,)), ((), ()))
TN_DIMS = (((0,), (0,)), ((), ()))


def _sigmoid(v):
    return 1.0 / (1.0 + jnp.exp(-v))


def _layer_kernel(sinks_ref, xc_ref, xn_ref, win_ref, wout_ref, lbl_ref, nw_ref, lng_ref, lnb_ref,
                  cos_ref, sin_ref, o_ref,
                  ha_ref, hb_ref, xb_ref, mix_ref, out_ref, st_ref, kwin_ref, vt_ref,
                  kcarry_ref, vcarry_ref):
    b = pl.program_id(0)
    u = pl.program_id(1)
    tt = TIME_TILE
    h_bufs = (ha_ref, hb_ref)

    @pl.when(jnp.logical_and(b == 0, u == 0))
    def _first_projection():
        ha_ref[...] = jnp.dot(xc_ref[0, 0:tt, :].astype(BF16), win_ref[...],
                              preferred_element_type=F32)

    @pl.when(u == 0)
    def _reset_carries():
        st_ref[...] = jnp.zeros_like(st_ref)
        kcarry_ref[...] = jnp.zeros_like(kcarry_ref)
        vcarry_ref[:, 0:SW_HEAD_DIM, :] = jnp.zeros((SW_KV_HEADS, SW_HEAD_DIM, WINDOW), BF16)
        vcarry_ref[:, SW_HEAD_DIM:VT_ROWS, :] = jnp.ones((SW_KV_HEADS, BF16_SUBLANES, WINDOW), BF16)
        vt_ref[:, SW_HEAD_DIM:VT_ROWS, :] = jnp.ones(
            (SW_KV_HEADS, BF16_SUBLANES, STEP_BLOCKS * WINDOW), BF16)

    lbl = lbl_ref[...]
    lbl_e = jnp.exp(lbl - jnp.max(lbl, axis=0, keepdims=True))
    lb = lbl_e[0:1] / jnp.sum(lbl_e, axis=0, keepdims=True)
    nw = nw_ref[...]
    row = lax.broadcasted_iota(jnp.int32, (HG_CHUNK, HG_CHUNK), 0)
    col = lax.broadcasted_iota(jnp.int32, (HG_CHUNK, HG_CHUNK), 1)
    tril = col <= row
    cum_mat = tril.astype(BF16)
    cum3 = jnp.concatenate([cum_mat, cum_mat, cum_mat], axis=1)

    lane = lax.broadcasted_iota(jnp.int32, (1, LANES), 1)
    rope_first = (lane % SW_HEAD_DIM) < (ROPE_DIM // 2)
    lane_lo = lane < SW_HEAD_DIM
    key_idx = lax.broadcasted_iota(jnp.int32, (2 * WINDOW, 2 * WINDOW), 0)
    qry_idx = lax.broadcasted_iota(jnp.int32, (2 * WINDOW, 2 * WINDOW), 1) % WINDOW
    band = (key_idx > qry_idx) & (key_idx <= qry_idx + WINDOW)
    pair_lo = lax.broadcasted_iota(jnp.int32, (1, 2 * WINDOW), 1) < WINDOW
    scale = SW_HEAD_DIM ** -0.5

    def rope(v, cos, sin):
        partner = jnp.where(rope_first,
                            pltpu.roll(v, LANES - ROPE_DIM // 2, 1),
                            pltpu.roll(v, ROPE_DIM // 2, 1))
        return v * cos + partner * sin

    def head_variants(v):
        sw = pltpu.roll(v, SW_HEAD_DIM, 1)
        zero = jnp.zeros_like(v)
        return (jnp.where(lane_lo, v, zero), jnp.where(lane_lo, zero, sw),
                jnp.where(lane_lo, sw, zero), jnp.where(lane_lo, zero, v))

    def in_projection_piece(k, piece):
        xb = xb_ref.at[k % 2]
        cols = slice(piece * PIECE, (piece + 1) * PIECE)
        h_bufs[k % 2][:, cols] = jnp.dot(xb[...], win_ref[:, cols], preferred_element_type=F32)

    def out_projection_piece(k, piece):
        cols = slice(piece * PIECE, (piece + 1) * PIECE)
        out_ref[:, cols] = jnp.dot(mix_ref[k], wout_ref[:, cols], preferred_element_type=F32)

    def post_norm(k):
        rows = slice(k * tt, (k + 1) * tt)
        z = DN_ALPHA * xc_ref[0, rows, :] + out_ref[...]
        mu = jnp.mean(z, axis=-1, keepdims=True)
        zc = z - mu
        var = jnp.mean(zc * zc, axis=-1, keepdims=True)
        o_ref[0, rows, :] = (zc * lax.rsqrt(var + LN_EPS)) * lng_ref[...] + lnb_ref[...]

    def make_fillers(k):
        stages = []
        if k > 0:
            for piece in range(N_OUT_PIECES):
                stages.append(lambda piece=piece: out_projection_piece(k - 1, piece))
            stages.append(lambda: post_norm(k - 1))
        for piece in range(N_IN_PIECES):
            stages.append(lambda piece=piece: in_projection_piece(k + 1, piece))
        return iter(stages)

    def process_tile(k):
        h_ref = h_bufs[k % 2]
        base = k * tt
        mix = mix_ref.at[k]
        if k + 1 < TILES_PER_STEP:
            x_next_rows = xc_ref[0, (k + 1) * tt:(k + 2) * tt, :]
        else:
            x_next_rows = xn_ref[0]
        xb_ref[(k + 1) % 2] = x_next_rows.astype(BF16)
        fillers = make_fillers(k)

        def fill(n=1):
            for _ in range(n):
                stage = next(fillers, None)
                if stage is not None:
                    stage()

        def hg_pre(c):
            rows = slice(c * HG_CHUNK, (c + 1) * HG_CHUNK)
            f = lb + (1.0 - lb) * _sigmoid(h_ref[rows, OFF_HF:OFF_HF + HG_WIDTH])
            log_f = jnp.log(f)
            hi = log_f.astype(BF16)
            rem = log_f - hi.astype(F32)
            mid = rem.astype(BF16)
            lo = (rem - mid.astype(F32)).astype(BF16)
            return 1.0 - f, jnp.concatenate([hi, mid, lo], axis=0)

        def hg_cum(parts):
            return jnp.dot(cum3, parts, preferred_element_type=F32)

        def hg_decays(c, k_in, g_cum):
            rows = slice(c * HG_CHUNK, (c + 1) * HG_CHUNK)
            hq = h_ref[rows, OFF_HQ:OFF_HQ + HG_WIDTH]
            g_last = g_cum[HG_CHUNK - 1:HG_CHUNK, :]
            q_dec = ((hq * _sigmoid(hq)) * jnp.exp(g_cum)).astype(BF16)
            k_dec = (k_in * jnp.exp(-g_cum)).astype(BF16)
            k_tail = (k_in * jnp.exp(g_last - g_cum)).astype(BF16)
            return q_dec, k_dec, k_tail, jnp.exp(g_last)

        def hg_scores(q_dec, k_dec, hd):
            sl = slice(hd * HG_HEAD_DIM, (hd + 1) * HG_HEAD_DIM)
            return lax.dot_general(q_dec[:, sl], k_dec[:, sl], NT_DIMS, preferred_element_type=F32)

        def hg_output(c, hd, a, q_dec, k_tail, decay):
            rows = slice(c * HG_CHUNK, (c + 1) * HG_CHUNK)
            sl = slice(hd * HG_HEAD_DIM, (hd + 1) * HG_HEAD_DIM)
            v = h_ref[rows, OFF_HI + hd * HG_HEAD_DIM:OFF_HI + (hd + 1) * HG_HEAD_DIM].astype(BF16)
            s_t = st_ref[hd]
            o = jnp.dot(jnp.where(tril, a, 0.0).astype(BF16), v, preferred_element_type=F32)
            o = o + lax.dot_general(q_dec[:, sl], s_t.astype(BF16), NT_DIMS,
                                    preferred_element_type=F32)
            inc_t = lax.dot_general(v, k_tail[:, sl], TN_DIMS, preferred_element_type=F32)
            st_ref[hd] = s_t * decay[:, sl] + inc_t
            o = o * lax.rsqrt(jnp.mean(o * o, axis=-1, keepdims=True) + RMS_EPS)
            o = o * nw[:, sl]
            hg = h_ref[rows, OFF_HG + hd * HG_HEAD_DIM:OFF_HG + (hd + 1) * HG_HEAD_DIM]
            mix[rows, sl] = (o * (hg * _sigmoid(hg))).astype(BF16)

        pre = {}
        cum = {}
        fill(2)
        for c in range(min(2, N_CHUNKS)):
            pre[c] = hg_pre(c)
            cum[c] = hg_cum(pre[c][1])
            fill()
        for c in range(N_CHUNKS):
            q_dec, k_dec, k_tail, decay = hg_decays(c, pre[c][0], cum[c])
            scores = [hg_scores(q_dec, k_dec, hd) for hd in range(HG_HEADS)]
            if c + 2 < N_CHUNKS:
                pre[c + 2] = hg_pre(c + 2)
                cum[c + 2] = hg_cum(pre[c + 2][1])
            fill()
            for hd in range(HG_HEADS):
                hg_output(c, hd, scores[hd], q_dec, k_tail, decay)
            fill()

        def sw_prepare(blk):
            r0 = blk * WINDOW
            rows = slice(r0, r0 + WINDOW)
            cur = slice(base + r0, base + r0 + WINDOW)
            cos = cos_ref[base + r0:base + r0 + WINDOW, :]
            sin = sin_ref[base + r0:base + r0 + WINDOW, :]
            k_rot = rope(h_ref[rows, OFF_AK:OFF_AK + SW_KV_WIDTH], cos, sin)
            for idx, kv in enumerate(head_variants(k_rot)):
                kwin_ref[idx, cur, :] = kv.astype(BF16)
            v_t = h_ref[rows, OFF_AV:OFF_AV + SW_KV_WIDTH].T
            for g in range(SW_KV_HEADS):
                vt_ref[g, 0:SW_HEAD_DIM, cur] = (
                    v_t[g * SW_HEAD_DIM:(g + 1) * SW_HEAD_DIM].astype(BF16))
            q_bf = []
            for tile in range(SW_WIDTH // LANES):
                q_rot = rope(h_ref[rows, OFF_AQ + tile * LANES:OFF_AQ + (tile + 1) * LANES], cos, sin)
                q_bf.append((q_rot * scale).astype(BF16))
            return [jnp.concatenate([q_bf[2 * g], q_bf[2 * g + 1]], axis=0)
                    for g in range(SW_KV_HEADS)]

        def key_window(blk, idx):
            r0 = base + blk * WINDOW
            if r0 == 0:
                return jnp.concatenate([kcarry_ref[idx], kwin_ref[idx, 0:WINDOW, :]], axis=0)
            return kwin_ref[idx, r0 - WINDOW:r0 + WINDOW, :]

        def value_window(blk, g):
            r0 = base + blk * WINDOW
            if r0 == 0:
                return jnp.concatenate([vcarry_ref[g], vt_ref[g, :, 0:WINDOW]], axis=1)
            return vt_ref[g, :, r0 - WINDOW:r0 + WINDOW]

        def sw_scores(blk, g, j, q_pair):
            return lax.dot_general(key_window(blk, 2 * g + j), q_pair, NT_DIMS,
                                   preferred_element_type=F32)

        def sw_values(blk, g, j, s_t, mask):
            sink_row = jnp.where(pair_lo, sinks_ref[SW_GROUP * g + j],
                                 sinks_ref[SW_GROUP * g + HEADS_PER_TILE + j])
            s_t = jnp.where(mask, s_t, -jnp.inf)
            m = jnp.maximum(jnp.max(s_t, axis=0, keepdims=True), sink_row)
            p_t = jnp.exp(s_t - m).astype(BF16)
            acc = jnp.dot(value_window(blk, g), p_t, preferred_element_type=F32)
            denom = acc[SW_HEAD_DIM:SW_HEAD_DIM + 1, :] + jnp.exp(sink_row - m)
            return acc[0:SW_HEAD_DIM, :] * (1.0 / denom)

        def sw_finish(blk, g, o_t):
            rows = slice(blk * WINDOW, (blk + 1) * WINDOW)
            for tl in range(2):
                tile = 2 * g + tl
                lanes = slice(tl * LANES, (tl + 1) * LANES)
                o_a = jnp.concatenate([o_t[0][:, lanes], o_t[1][:, lanes]], axis=0).T
                ag = h_ref[rows, OFF_AG + tile * LANES:OFF_AG + (tile + 1) * LANES]
                mix[rows, HG_WIDTH + tile * LANES:HG_WIDTH + (tile + 1) * LANES] = (
                    o_a * (ag * _sigmoid(ag))).astype(BF16)

        q_pairs = [sw_prepare(blk) for blk in range(N_BLOCKS)]
        masks = []
        for blk in range(N_BLOCKS):
            if k == 0 and blk == 0:
                first_valid = jnp.where(u > 0, 0, WINDOW)
                masks.append(band & (key_idx >= first_valid))
            else:
                masks.append(band)
        chains = [(blk, g, j) for blk in range(N_BLOCKS) for g in range(SW_KV_HEADS)
                  for j in range(HEADS_PER_TILE)]
        scores = {}
        for i in range(min(2, len(chains))):
            blk, g, j = chains[i]
            scores[i] = sw_scores(blk, g, j, q_pairs[blk][g])
        fill()
        outs = {}
        for i, (blk, g, j) in enumerate(chains):
            outs[(blk, g, j)] = sw_values(blk, g, j, scores.pop(i), masks[blk])
            if i + 2 < len(chains):
                nb, ng, nj = chains[i + 2]
                scores[i + 2] = sw_scores(nb, ng, nj, q_pairs[nb][ng])
            if j == HEADS_PER_TILE - 1:
                sw_finish(blk, g, [outs.pop((blk, g, jj)) for jj in range(HEADS_PER_TILE)])
            fill()
        fill(N_OUT_PIECES + 1 + N_IN_PIECES)

    for k in range(TILES_PER_STEP):
        process_tile(k)
    for piece in range(N_OUT_PIECES):
        out_projection_piece(TILES_PER_STEP - 1, piece)
    post_norm(TILES_PER_STEP - 1)

    last = slice((STEP_BLOCKS - 1) * WINDOW, STEP_BLOCKS * WINDOW)
    for idx in range(2 * SW_KV_HEADS):
        kcarry_ref[idx] = kwin_ref[idx, last, :]
    for g in range(SW_KV_HEADS):
        vcarry_ref[g, 0:SW_HEAD_DIM, :] = vt_ref[g, 0:SW_HEAD_DIM, last]


def _rope_tables(seq_len):
    pos = jnp.arange(seq_len, dtype=F32)
    inv_freq = ROPE_THETA ** (-jnp.arange(0, ROPE_DIM, 2, dtype=F32) / ROPE_DIM)
    ang = pos[:, None] * inv_freq[None, :]
    cos = jnp.cos(ang)
    sin = jnp.sin(ang)
    ones = jnp.ones((seq_len, SW_HEAD_DIM - ROPE_DIM), F32)
    cos_head = jnp.concatenate([cos, cos, ones], axis=1)
    sin_head = jnp.concatenate([-sin, sin, jnp.zeros_like(ones)], axis=1)
    return (jnp.tile(cos_head, (1, HEADS_PER_TILE)), jnp.tile(sin_head, (1, HEADS_PER_TILE)))


def kernel(x, w_in, lb_logits, hg_norm_w, sinks, w_out, ln_g, ln_b):
    batch, seq_len, d_model = x.shape
    assert d_model == D_MODEL and w_in.shape == (DEPTH, D_MODEL, IN_WIDTH)
    step_rows = TILES_PER_STEP * TIME_TILE
    assert seq_len % step_rows == 0
    cos_tab, sin_tab = _rope_tables(seq_len)
    n_steps = seq_len // step_rows
    last_step = batch * n_steps - 1

    def next_tile_index(b, u, *_):
        nxt = jnp.minimum(b * n_steps + u + 1, last_step)
        return (nxt // n_steps, TILES_PER_STEP * (nxt % n_steps), 0)

    const = lambda b, u, *_: (0, 0)
    resident = dict(pipeline_mode=pl.Buffered(1))
    grid_spec = pltpu.PrefetchScalarGridSpec(
        num_scalar_prefetch=1,
        grid=(batch, n_steps),
        in_specs=[
            pl.BlockSpec((1, step_rows, D_MODEL), lambda b, u, *_: (b, u, 0)),
            pl.BlockSpec((1, TIME_TILE, D_MODEL), next_tile_index),
            pl.BlockSpec((D_MODEL, IN_WIDTH), const, **resident),
            pl.BlockSpec((D_MODEL, D_MODEL), const, **resident),
            pl.BlockSpec((DEPTH + 1, HG_WIDTH), const),
            pl.BlockSpec((1, HG_WIDTH), const),
            pl.BlockSpec((1, D_MODEL), const),
            pl.BlockSpec((1, D_MODEL), const),
            pl.BlockSpec((step_rows, LANES), lambda b, u, *_: (u, 0)),
            pl.BlockSpec((step_rows, LANES), lambda b, u, *_: (u, 0)),
        ],
        out_specs=pl.BlockSpec((1, step_rows, D_MODEL), lambda b, u, *_: (b, u, 0)),
        scratch_shapes=[
            pltpu.VMEM((TIME_TILE, IN_WIDTH), F32),
            pltpu.VMEM((TIME_TILE, IN_WIDTH), F32),
            pltpu.VMEM((2, TIME_TILE, D_MODEL), BF16),
            pltpu.VMEM((TILES_PER_STEP, TIME_TILE, D_MODEL), BF16),
            pltpu.VMEM((TIME_TILE, D_MODEL), F32),
            pltpu.VMEM((HG_HEADS, HG_HEAD_DIM, HG_HEAD_DIM), F32),
            pltpu.VMEM((2 * SW_KV_HEADS, STEP_BLOCKS * WINDOW, LANES), BF16),
            pltpu.VMEM((SW_KV_HEADS, VT_ROWS, STEP_BLOCKS * WINDOW), BF16),
            pltpu.VMEM((2 * SW_KV_HEADS, WINDOW, LANES), BF16),
            pltpu.VMEM((SW_KV_HEADS, VT_ROWS, WINDOW), BF16),
        ],
    )
    return pl.pallas_call(
        _layer_kernel,
        grid_spec=grid_spec,
        out_shape=jax.ShapeDtypeStruct(x.shape, x.dtype),
        compiler_params=pltpu.CompilerParams(
            dimension_semantics=("arbitrary", "arbitrary"),
            vmem_limit_bytes=VMEM_LIMIT_BYTES),
        name="hybrid_hgrn2_swa_layer",
    )(sinks[0], x, x, w_in[0].astype(BF16), w_out[0].astype(BF16), lb_logits, hg_norm_w,
      ln_g, ln_b, cos_tab, sin_tab)
```

```python
import jax
import jax.numpy as jnp
from jax import lax
from jax.experimental import pallas as pl
from jax.experimental.pallas import tpu as pltpu

D_MODEL = 1024
DEPTH = 1
HG_WIDTH = 512
HG_HEAD_DIM = 128
HG_HEADS = HG_WIDTH // HG_HEAD_DIM
HG_CHUNK = 64
SW_WIDTH = 512
SW_HEAD_DIM = 64
SW_Q_HEADS = SW_WIDTH // SW_HEAD_DIM
SW_KV_HEADS = SW_Q_HEADS // 4
SW_KV_WIDTH = SW_KV_HEADS * SW_HEAD_DIM
SW_GROUP = SW_Q_HEADS // SW_KV_HEADS
WINDOW = 128
ROPE_THETA = 500000.0
ROPE_DIM = SW_HEAD_DIM // 4
DN_ALPHA = (2.0 * DEPTH) ** 0.25
LN_EPS = 1e-5
RMS_EPS = 1e-6
IN_WIDTH = 4 * HG_WIDTH + SW_WIDTH + 2 * SW_KV_WIDTH + SW_WIDTH

OFF_HQ = 0
OFF_HF = OFF_HQ + HG_WIDTH
OFF_HI = OFF_HF + HG_WIDTH
OFF_HG = OFF_HI + HG_WIDTH
OFF_AQ = OFF_HG + HG_WIDTH
OFF_AK = OFF_AQ + SW_WIDTH
OFF_AV = OFF_AK + SW_KV_WIDTH
OFF_AG = OFF_AV + SW_KV_WIDTH

LANES = 128
BF16_SUBLANES = 16
MXU_WIDTH = 256
HEADS_PER_TILE = LANES // SW_HEAD_DIM
assert HEADS_PER_TILE == 2 and SW_GROUP == 2 * HEADS_PER_TILE and SW_KV_WIDTH == LANES
VT_ROWS = SW_HEAD_DIM + BF16_SUBLANES
TIME_TILE = 256
TILES_PER_STEP = 4
PIECE = MXU_WIDTH
N_IN_PIECES = IN_WIDTH // PIECE
N_OUT_PIECES = D_MODEL // PIECE
assert IN_WIDTH % PIECE == 0 and D_MODEL % PIECE == 0
N_CHUNKS = TIME_TILE // HG_CHUNK
N_BLOCKS = TIME_TILE // WINDOW
STEP_BLOCKS = TILES_PER_STEP * N_BLOCKS
VMEM_LIMIT_BYTES = 58 * 1024 * 1024

F32 = jnp.float32
BF16 = jnp.bfloat16
NT_DIMS = (((1,), (1,)), ((), ()))
TN_DIMS = (((0,), (0,)), ((), ()))


LOG2_E = 1.4426950408889634


def _sigmoid(v):
    return 1.0 / (1.0 + jnp.exp2(v * (-LOG2_E)))


def _layer_kernel(sinks_ref, xc_ref, xn_ref, win_ref, wout_ref, lbl_ref, nw_ref, lng_ref, lnb_ref,
                  cos_ref, sin_ref, o_ref,
                  ha_ref, hb_ref, xb_ref, mix_ref, out_ref, st_ref, kwin_ref, vt_ref,
                  kcarry_ref, vcarry_ref):
    b = pl.program_id(0)
    u = pl.program_id(1)
    tt = TIME_TILE
    h_bufs = (ha_ref, hb_ref)

    @pl.when(jnp.logical_and(b == 0, u == 0))
    def _first_projection():
        ha_ref[...] = jnp.dot(xc_ref[0, 0:tt, :].astype(BF16), win_ref[...],
                              preferred_element_type=F32)

    @pl.when(u == 0)
    def _reset_carries():
        st_ref[...] = jnp.zeros_like(st_ref)
        kcarry_ref[...] = jnp.zeros_like(kcarry_ref)
        vcarry_ref[:, 0:SW_HEAD_DIM, :] = jnp.zeros((SW_KV_HEADS, SW_HEAD_DIM, WINDOW), BF16)
        vcarry_ref[:, SW_HEAD_DIM:VT_ROWS, :] = jnp.ones((SW_KV_HEADS, BF16_SUBLANES, WINDOW), BF16)
        vt_ref[:, SW_HEAD_DIM:VT_ROWS, :] = jnp.ones(
            (SW_KV_HEADS, BF16_SUBLANES, STEP_BLOCKS * WINDOW), BF16)

    lbl = lbl_ref[...]
    lbl_e = jnp.exp(lbl - jnp.max(lbl, axis=0, keepdims=True))
    lb = lbl_e[0:1] / jnp.sum(lbl_e, axis=0, keepdims=True)
    nw = nw_ref[...]
    row = lax.broadcasted_iota(jnp.int32, (HG_CHUNK, HG_CHUNK), 0)
    col = lax.broadcasted_iota(jnp.int32, (HG_CHUNK, HG_CHUNK), 1)
    tril = col <= row
    cum_mat = tril.astype(BF16)
    cum3 = jnp.concatenate([cum_mat, cum_mat, cum_mat], axis=1)

    lane = lax.broadcasted_iota(jnp.int32, (1, LANES), 1)
    rope_first = (lane % SW_HEAD_DIM) < (ROPE_DIM // 2)
    lane_lo = lane < SW_HEAD_DIM
    key_idx = lax.broadcasted_iota(jnp.int32, (2 * WINDOW, 2 * WINDOW), 0)
    qry_idx = lax.broadcasted_iota(jnp.int32, (2 * WINDOW, 2 * WINDOW), 1) % WINDOW
    band = (key_idx > qry_idx) & (key_idx <= qry_idx + WINDOW)
    pair_lo = lax.broadcasted_iota(jnp.int32, (1, 2 * WINDOW), 1) < WINDOW
    scale = SW_HEAD_DIM ** -0.5 * LOG2_E
    first_valid = jnp.where(u > 0, 0, WINDOW)
    first_band = band & (key_idx >= first_valid)

    def rope(v, cos, sin):
        partner = jnp.where(rope_first,
                            pltpu.roll(v, LANES - ROPE_DIM // 2, 1),
                            pltpu.roll(v, ROPE_DIM // 2, 1))
        return v * cos + partner * sin

    def head_variants(v):
        sw = pltpu.roll(v, SW_HEAD_DIM, 1)
        zero = jnp.zeros_like(v)
        return (jnp.where(lane_lo, v, zero), jnp.where(lane_lo, zero, sw),
                jnp.where(lane_lo, sw, zero), jnp.where(lane_lo, zero, v))

    def in_projection_piece(k, piece):
        xb = xb_ref.at[k % 2]
        cols = slice(piece * PIECE, (piece + 1) * PIECE)
        h_bufs[k % 2][:, cols] = jnp.dot(xb[...], win_ref[:, cols], preferred_element_type=F32)

    def out_projection_piece(k, piece):
        cols = slice(piece * PIECE, (piece + 1) * PIECE)
        out_ref[:, cols] = jnp.dot(mix_ref[k], wout_ref[:, cols], preferred_element_type=F32)

    def post_norm(k):
        rows = slice(k * tt, (k + 1) * tt)
        z = DN_ALPHA * xc_ref[0, rows, :] + out_ref[...]
        mu = jnp.mean(z, axis=-1, keepdims=True)
        zc = z - mu
        var = jnp.mean(zc * zc, axis=-1, keepdims=True)
        o_ref[0, rows, :] = (zc * lax.rsqrt(var + LN_EPS)) * lng_ref[...] + lnb_ref[...]

    def make_fillers(k):
        stages = []
        if k > 0:
            for piece in range(N_OUT_PIECES):
                stages.append(lambda piece=piece: out_projection_piece(k - 1, piece))
            stages.append(lambda: post_norm(k - 1))
        for piece in range(N_IN_PIECES):
            stages.append(lambda piece=piece: in_projection_piece(k + 1, piece))
        return iter(stages)

    def process_tile(k):
        h_ref = h_bufs[k % 2]
        base = k * tt
        mix = mix_ref.at[k]
        if k + 1 < TILES_PER_STEP:
            x_next_rows = xc_ref[0, (k + 1) * tt:(k + 2) * tt, :]
        else:
            x_next_rows = xn_ref[0]
        xb_ref[(k + 1) % 2] = x_next_rows.astype(BF16)
        fillers = make_fillers(k)

        def fill(n=1):
            for _ in range(n):
                stage = next(fillers, None)
                if stage is not None:
                    stage()

        def hg_pre(c):
            rows = slice(c * HG_CHUNK, (c + 1) * HG_CHUNK)
            f = lb + (1.0 - lb) * _sigmoid(h_ref[rows, OFF_HF:OFF_HF + HG_WIDTH])
            log_f = jnp.log2(f)
            hi = log_f.astype(BF16)
            rem = log_f - hi.astype(F32)
            mid = rem.astype(BF16)
            lo = (rem - mid.astype(F32)).astype(BF16)
            return 1.0 - f, jnp.concatenate([hi, mid, lo], axis=0)

        def hg_cum(parts):
            return jnp.dot(cum3, parts, preferred_element_type=F32)

        def hg_decays(c, k_in, g_cum):
            rows = slice(c * HG_CHUNK, (c + 1) * HG_CHUNK)
            hq = h_ref[rows, OFF_HQ:OFF_HQ + HG_WIDTH]
            g_last = g_cum[HG_CHUNK - 1:HG_CHUNK, :]
            q_dec = ((hq * _sigmoid(hq)) * jnp.exp2(g_cum)).astype(BF16)
            k_dec = (k_in * jnp.exp2(-g_cum)).astype(BF16)
            k_tail = (k_in * jnp.exp2(g_last - g_cum)).astype(BF16)
            return q_dec, k_dec, k_tail, jnp.exp2(g_last)

        def hg_scores(q_dec, k_dec, hd):
            sl = slice(hd * HG_HEAD_DIM, (hd + 1) * HG_HEAD_DIM)
            return lax.dot_general(q_dec[:, sl], k_dec[:, sl], NT_DIMS, preferred_element_type=F32)

        def hg_output(c, hd, a, q_dec, k_tail, decay):
            rows = slice(c * HG_CHUNK, (c + 1) * HG_CHUNK)
            sl = slice(hd * HG_HEAD_DIM, (hd + 1) * HG_HEAD_DIM)
            v = h_ref[rows, OFF_HI + hd * HG_HEAD_DIM:OFF_HI + (hd + 1) * HG_HEAD_DIM].astype(BF16)
            s_t = st_ref[hd]
            o = jnp.dot(jnp.where(tril, a, 0.0).astype(BF16), v, preferred_element_type=F32)
            o = o + lax.dot_general(q_dec[:, sl], s_t.astype(BF16), NT_DIMS,
                                    preferred_element_type=F32)
            inc_t = lax.dot_general(v, k_tail[:, sl], TN_DIMS, preferred_element_type=F32)
            st_ref[hd] = s_t * decay[:, sl] + inc_t
            o = o * lax.rsqrt(jnp.mean(o * o, axis=-1, keepdims=True) + RMS_EPS)
            o = o * nw[:, sl]
            hg = h_ref[rows, OFF_HG + hd * HG_HEAD_DIM:OFF_HG + (hd + 1) * HG_HEAD_DIM]
            mix[rows, sl] = (o * (hg * _sigmoid(hg))).astype(BF16)

        many = k > 0
        pre = {}
        cum = {}
        fill(2 if many else 1)
        for c in range(min(2, N_CHUNKS)):
            pre[c] = hg_pre(c)
            cum[c] = hg_cum(pre[c][1])
            fill(1 if many else 0)
        for c in range(N_CHUNKS):
            q_dec, k_dec, k_tail, decay = hg_decays(c, pre[c][0], cum[c])
            scores = [hg_scores(q_dec, k_dec, hd) for hd in range(HG_HEADS)]
            if c + 2 < N_CHUNKS:
                pre[c + 2] = hg_pre(c + 2)
                cum[c + 2] = hg_cum(pre[c + 2][1])
            fill()
            for hd in range(HG_HEADS):
                hg_output(c, hd, scores[hd], q_dec, k_tail, decay)
            fill(1 if many else 0)

        def sw_prepare(blk):
            r0 = blk * WINDOW
            rows = slice(r0, r0 + WINDOW)
            cur = slice(base + r0, base + r0 + WINDOW)
            cos = cos_ref[base + r0:base + r0 + WINDOW, :]
            sin = sin_ref[base + r0:base + r0 + WINDOW, :]
            k_rot = rope(h_ref[rows, OFF_AK:OFF_AK + SW_KV_WIDTH], cos, sin)
            for idx, kv in enumerate(head_variants(k_rot)):
                kwin_ref[idx, cur, :] = kv.astype(BF16)
            v_t = h_ref[rows, OFF_AV:OFF_AV + SW_KV_WIDTH].T
            for g in range(SW_KV_HEADS):
                vt_ref[g, 0:SW_HEAD_DIM, cur] = (
                    v_t[g * SW_HEAD_DIM:(g + 1) * SW_HEAD_DIM].astype(BF16))
            q_bf = []
            for tile in range(SW_WIDTH // LANES):
                q_rot = rope(h_ref[rows, OFF_AQ + tile * LANES:OFF_AQ + (tile + 1) * LANES], cos, sin)
                q_bf.append((q_rot * scale).astype(BF16))
            return [jnp.concatenate([q_bf[2 * g], q_bf[2 * g + 1]], axis=0)
                    for g in range(SW_KV_HEADS)]

        def key_window(blk, idx):
            r0 = base + blk * WINDOW
            if r0 == 0:
                return jnp.concatenate([kcarry_ref[idx], kwin_ref[idx, 0:WINDOW, :]], axis=0)
            return kwin_ref[idx, r0 - WINDOW:r0 + WINDOW, :]

        def value_window(blk, g):
            r0 = base + blk * WINDOW
            if r0 == 0:
                return jnp.concatenate([vcarry_ref[g], vt_ref[g, :, 0:WINDOW]], axis=1)
            return vt_ref[g, :, r0 - WINDOW:r0 + WINDOW]

        def sw_scores(blk, g, j, q_pair):
            return lax.dot_general(key_window(blk, 2 * g + j), q_pair, NT_DIMS,
                                   preferred_element_type=F32)

        def sw_values(blk, g, j, s_t, mask):
            sink_row = jnp.where(pair_lo, sinks_ref[SW_GROUP * g + j],
                                 sinks_ref[SW_GROUP * g + HEADS_PER_TILE + j]) * LOG2_E
            s_t = jnp.where(mask, s_t, -jnp.inf)
            m = jnp.maximum(jnp.max(s_t, axis=0, keepdims=True), sink_row)
            p_t = jnp.exp2(s_t - m).astype(BF16)
            acc = jnp.dot(value_window(blk, g), p_t, preferred_element_type=F32)
            denom = acc[SW_HEAD_DIM:SW_HEAD_DIM + 1, :] + jnp.exp2(sink_row - m)
            return acc[0:SW_HEAD_DIM, :] * (1.0 / denom)

        def sw_finish(blk, g, o_t):
            rows = slice(blk * WINDOW, (blk + 1) * WINDOW)
            for tl in range(2):
                tile = 2 * g + tl
                lanes = slice(tl * LANES, (tl + 1) * LANES)
                o_a = jnp.concatenate([o_t[0][:, lanes], o_t[1][:, lanes]], axis=0).T
                ag = h_ref[rows, OFF_AG + tile * LANES:OFF_AG + (tile + 1) * LANES]
                mix[rows, HG_WIDTH + tile * LANES:HG_WIDTH + (tile + 1) * LANES] = (
                    o_a * (ag * _sigmoid(ag))).astype(BF16)

        q_pairs = [sw_prepare(blk) for blk in range(N_BLOCKS)]
        masks = [first_band if (k == 0 and blk == 0) else band for blk in range(N_BLOCKS)]
        chains = [(blk, g, j) for blk in range(N_BLOCKS) for g in range(SW_KV_HEADS)
                  for j in range(HEADS_PER_TILE)]
        scores = {}
        for i in range(min(2, len(chains))):
            blk, g, j = chains[i]
            scores[i] = sw_scores(blk, g, j, q_pairs[blk][g])
        fill()
        outs = {}
        for i, (blk, g, j) in enumerate(chains):
            outs[(blk, g, j)] = sw_values(blk, g, j, scores.pop(i), masks[blk])
            if i + 2 < len(chains):
                nb, ng, nj = chains[i + 2]
                scores[i + 2] = sw_scores(nb, ng, nj, q_pairs[nb][ng])
            if j == HEADS_PER_TILE - 1:
                sw_finish(blk, g, [outs.pop((blk, g, jj)) for jj in range(HEADS_PER_TILE)])
            fill()
        fill(N_OUT_PIECES + 1 + N_IN_PIECES)

    for k in range(TILES_PER_STEP):
        process_tile(k)
    for piece in range(N_OUT_PIECES):
        out_projection_piece(TILES_PER_STEP - 1, piece)
    post_norm(TILES_PER_STEP - 1)

    last = slice((STEP_BLOCKS - 1) * WINDOW, STEP_BLOCKS * WINDOW)
    for idx in range(2 * SW_KV_HEADS):
        kcarry_ref[idx] = kwin_ref[idx, last, :]
    for g in range(SW_KV_HEADS):
        vcarry_ref[g, 0:SW_HEAD_DIM, :] = vt_ref[g, 0:SW_HEAD_DIM, last]


def _rope_tables(seq_len):
    pos = jnp.arange(seq_len, dtype=F32)
    inv_freq = ROPE_THETA ** (-jnp.arange(0, ROPE_DIM, 2, dtype=F32) / ROPE_DIM)
    ang = pos[:, None] * inv_freq[None, :]
    cos = jnp.cos(ang)
    sin = jnp.sin(ang)
    ones = jnp.ones((seq_len, SW_HEAD_DIM - ROPE_DIM), F32)
    cos_head = jnp.concatenate([cos, cos, ones], axis=1)
    sin_head = jnp.concatenate([-sin, sin, jnp.zeros_like(ones)], axis=1)
    return (jnp.tile(cos_head, (1, HEADS_PER_TILE)), jnp.tile(sin_head, (1, HEADS_PER_TILE)))


def kernel(x, w_in, lb_logits, hg_norm_w, sinks, w_out, ln_g, ln_b):
    batch, seq_len, d_model = x.shape
    assert d_model == D_MODEL and w_in.shape == (DEPTH, D_MODEL, IN_WIDTH)
    step_rows = TILES_PER_STEP * TIME_TILE
    assert seq_len % step_rows == 0
    cos_tab, sin_tab = _rope_tables(seq_len)
    n_steps = seq_len // step_rows
    last_step = batch * n_steps - 1

    def next_tile_index(b, u, *_):
        nxt = jnp.minimum(b * n_steps + u + 1, last_step)
        return (nxt // n_steps, TILES_PER_STEP * (nxt % n_steps), 0)

    const = lambda b, u, *_: (0, 0)
    resident = dict(pipeline_mode=pl.Buffered(1))
    grid_spec = pltpu.PrefetchScalarGridSpec(
        num_scalar_prefetch=1,
        grid=(batch, n_steps),
        in_specs=[
            pl.BlockSpec((1, step_rows, D_MODEL), lambda b, u, *_: (b, u, 0)),
            pl.BlockSpec((1, TIME_TILE, D_MODEL), next_tile_index),
            pl.BlockSpec((D_MODEL, IN_WIDTH), const, **resident),
            pl.BlockSpec((D_MODEL, D_MODEL), const, **resident),
            pl.BlockSpec((DEPTH + 1, HG_WIDTH), const),
            pl.BlockSpec((1, HG_WIDTH), const),
            pl.BlockSpec((1, D_MODEL), const),
            pl.BlockSpec((1, D_MODEL), const),
            pl.BlockSpec((step_rows, LANES), lambda b, u, *_: (u, 0)),
            pl.BlockSpec((step_rows, LANES), lambda b, u, *_: (u, 0)),
        ],
        out_specs=pl.BlockSpec((1, step_rows, D_MODEL), lambda b, u, *_: (b, u, 0)),
        scratch_shapes=[
            pltpu.VMEM((TIME_TILE, IN_WIDTH), F32),
            pltpu.VMEM((TIME_TILE, IN_WIDTH), F32),
            pltpu.VMEM((2, TIME_TILE, D_MODEL), BF16),
            pltpu.VMEM((TILES_PER_STEP, TIME_TILE, D_MODEL), BF16),
            pltpu.VMEM((TIME_TILE, D_MODEL), F32),
            pltpu.VMEM((HG_HEADS, HG_HEAD_DIM, HG_HEAD_DIM), F32),
            pltpu.VMEM((2 * SW_KV_HEADS, STEP_BLOCKS * WINDOW, LANES), BF16),
            pltpu.VMEM((SW_KV_HEADS, VT_ROWS, STEP_BLOCKS * WINDOW), BF16),
            pltpu.VMEM((2 * SW_KV_HEADS, WINDOW, LANES), BF16),
            pltpu.VMEM((SW_KV_HEADS, VT_ROWS, WINDOW), BF16),
        ],
    )
    return pl.pallas_call(
        _layer_kernel,
        grid_spec=grid_spec,
        out_shape=jax.ShapeDtypeStruct(x.shape, x.dtype),
        compiler_params=pltpu.CompilerParams(
            dimension_semantics=("arbitrary", "arbitrary"),
            vmem_limit_bytes=VMEM_LIMIT_BYTES),
        name="hybrid_hgrn2_swa_layer",
    )(sinks[0], x, x, w_in[0].astype(BF16), w_out[0].astype(BF16), lb_logits, hg_norm_w,
      ln_g, ln_b, cos_tab, sin_tab)
```

```python
import jax
import jax.numpy as jnp
from jax import lax
from jax.experimental import pallas as pl
from jax.experimental.pallas import tpu as pltpu

D_MODEL = 1024
DEPTH = 1
HG_WIDTH = 512
HG_HEAD_DIM = 128
HG_HEADS = HG_WIDTH // HG_HEAD_DIM
HG_CHUNK = 64
SW_WIDTH = 512
SW_HEAD_DIM = 64
SW_Q_HEADS = SW_WIDTH // SW_HEAD_DIM
SW_KV_HEADS = SW_Q_HEADS // 4
SW_KV_WIDTH = SW_KV_HEADS * SW_HEAD_DIM
SW_GROUP = SW_Q_HEADS // SW_KV_HEADS
WINDOW = 128
ROPE_THETA = 500000.0
ROPE_DIM = SW_HEAD_DIM // 4
DN_ALPHA = (2.0 * DEPTH) ** 0.25
LN_EPS = 1e-5
RMS_EPS = 1e-6
IN_WIDTH = 4 * HG_WIDTH + SW_WIDTH + 2 * SW_KV_WIDTH + SW_WIDTH

OFF_HQ = 0
OFF_HF = OFF_HQ + HG_WIDTH
OFF_HI = OFF_HF + HG_WIDTH
OFF_HG = OFF_HI + HG_WIDTH
OFF_AQ = OFF_HG + HG_WIDTH
OFF_AK = OFF_AQ + SW_WIDTH
OFF_AV = OFF_AK + SW_KV_WIDTH
OFF_AG = OFF_AV + SW_KV_WIDTH

LANES = 128
BF16_SUBLANES = 16
MXU_WIDTH = 256
HEADS_PER_TILE = LANES // SW_HEAD_DIM
assert HEADS_PER_TILE == 2 and SW_GROUP == 2 * HEADS_PER_TILE and SW_KV_WIDTH == LANES
VT_ROWS = SW_HEAD_DIM + BF16_SUBLANES
TIME_TILE = 256
TILES_PER_STEP = 4
PIECE = MXU_WIDTH
N_IN_PIECES = IN_WIDTH // PIECE
N_OUT_PIECES = D_MODEL // PIECE
assert IN_WIDTH % PIECE == 0 and D_MODEL % PIECE == 0
N_CHUNKS = TIME_TILE // HG_CHUNK
N_BLOCKS = TIME_TILE // WINDOW
STEP_BLOCKS = TILES_PER_STEP * N_BLOCKS
WEIGHT_CHUNK_ROWS = 128
VMEM_LIMIT_BYTES = 58 * 1024 * 1024

F32 = jnp.float32
BF16 = jnp.bfloat16
NT_DIMS = (((1,), (1,)), ((), ()))
TN_DIMS = (((0,), (0,)), ((), ()))


LOG2_E = 1.4426950408889634


def _sigmoid(v):
    return 1.0 / (1.0 + jnp.exp2(v * (-LOG2_E)))


def _load_weight_as_bf16(w_hbm, stage_ref, sem_ref, dst_ref):
    n_rows = dst_ref.shape[0]
    chunk = stage_ref.shape[1]
    n_chunks = n_rows // chunk
    assert n_chunks * chunk == n_rows

    def chunk_copy(i):
        return pltpu.make_async_copy(w_hbm.at[0, pl.ds(i * chunk, chunk), :],
                                     stage_ref.at[i % 2], sem_ref.at[i % 2])

    chunk_copy(0).start()
    for i in range(n_chunks):
        if i + 1 < n_chunks:
            chunk_copy(i + 1).start()
        chunk_copy(i).wait()
        dst_ref[i * chunk:(i + 1) * chunk, :] = stage_ref[i % 2].astype(BF16)


def _layer_kernel(sinks_ref, xc_ref, xn_ref, win_hbm, wout_hbm, lbl_ref, nw_ref, lng_ref, lnb_ref,
                  cos_ref, sin_ref, o_ref,
                  ha_ref, hb_ref, xb_ref, mix_ref, out_ref, st_ref, kwin_ref, vt_ref,
                  kcarry_ref, vcarry_ref, win_ref, wout_ref, win_stage, wout_stage,
                  win_sems, wout_sems):
    b = pl.program_id(0)
    u = pl.program_id(1)
    tt = TIME_TILE
    h_bufs = (ha_ref, hb_ref)

    @pl.when(jnp.logical_and(b == 0, u == 0))
    def _first_step():
        _load_weight_as_bf16(win_hbm, win_stage, win_sems, win_ref)
        _load_weight_as_bf16(wout_hbm, wout_stage, wout_sems, wout_ref)
        ha_ref[...] = jnp.dot(xc_ref[0, 0:tt, :].astype(BF16), win_ref[...],
                              preferred_element_type=F32)

    @pl.when(u == 0)
    def _reset_carries():
        st_ref[...] = jnp.zeros_like(st_ref)
        kcarry_ref[...] = jnp.zeros_like(kcarry_ref)
        vcarry_ref[:, 0:SW_HEAD_DIM, :] = jnp.zeros((SW_KV_HEADS, SW_HEAD_DIM, WINDOW), BF16)
        vcarry_ref[:, SW_HEAD_DIM:VT_ROWS, :] = jnp.ones((SW_KV_HEADS, BF16_SUBLANES, WINDOW), BF16)
        vt_ref[:, SW_HEAD_DIM:VT_ROWS, :] = jnp.ones(
            (SW_KV_HEADS, BF16_SUBLANES, STEP_BLOCKS * WINDOW), BF16)

    lbl = lbl_ref[...]
    lbl_e = jnp.exp(lbl - jnp.max(lbl, axis=0, keepdims=True))
    lb = lbl_e[0:1] / jnp.sum(lbl_e, axis=0, keepdims=True)
    nw = nw_ref[...]
    row = lax.broadcasted_iota(jnp.int32, (HG_CHUNK, HG_CHUNK), 0)
    col = lax.broadcasted_iota(jnp.int32, (HG_CHUNK, HG_CHUNK), 1)
    tril = col <= row
    cum_mat = tril.astype(BF16)
    cum3 = jnp.concatenate([cum_mat, cum_mat, cum_mat], axis=1)

    lane = lax.broadcasted_iota(jnp.int32, (1, LANES), 1)
    rope_first = (lane % SW_HEAD_DIM) < (ROPE_DIM // 2)
    lane_lo = lane < SW_HEAD_DIM
    key_idx = lax.broadcasted_iota(jnp.int32, (2 * WINDOW, 2 * WINDOW), 0)
    qry_idx = lax.broadcasted_iota(jnp.int32, (2 * WINDOW, 2 * WINDOW), 1) % WINDOW
    band = (key_idx > qry_idx) & (key_idx <= qry_idx + WINDOW)
    pair_lo = lax.broadcasted_iota(jnp.int32, (1, 2 * WINDOW), 1) < WINDOW
    scale = SW_HEAD_DIM ** -0.5 * LOG2_E
    first_valid = jnp.where(u > 0, 0, WINDOW)
    first_band = band & (key_idx >= first_valid)

    def rope(v, cos, sin):
        partner = jnp.where(rope_first,
                            pltpu.roll(v, LANES - ROPE_DIM // 2, 1),
                            pltpu.roll(v, ROPE_DIM // 2, 1))
        return v * cos + partner * sin

    def head_variants(v):
        sw = pltpu.roll(v, SW_HEAD_DIM, 1)
        zero = jnp.zeros_like(v)
        return (jnp.where(lane_lo, v, zero), jnp.where(lane_lo, zero, sw),
                jnp.where(lane_lo, sw, zero), jnp.where(lane_lo, zero, v))

    def in_projection_piece(k, piece):
        xb = xb_ref.at[k % 2]
        cols = slice(piece * PIECE, (piece + 1) * PIECE)
        h_bufs[k % 2][:, cols] = jnp.dot(xb[...], win_ref[:, cols], preferred_element_type=F32)

    def out_projection_piece(k, piece):
        cols = slice(piece * PIECE, (piece + 1) * PIECE)
        out_ref[:, cols] = jnp.dot(mix_ref[k], wout_ref[:, cols], preferred_element_type=F32)

    def post_norm(k):
        rows = slice(k * tt, (k + 1) * tt)
        z = DN_ALPHA * xc_ref[0, rows, :] + out_ref[...]
        mu = jnp.mean(z, axis=-1, keepdims=True)
        zc = z - mu
        var = jnp.mean(zc * zc, axis=-1, keepdims=True)
        o_ref[0, rows, :] = (zc * lax.rsqrt(var + LN_EPS)) * lng_ref[...] + lnb_ref[...]

    def make_fillers(k):
        stages = []
        if k > 0:
            for piece in range(N_OUT_PIECES):
                stages.append(lambda piece=piece: out_projection_piece(k - 1, piece))
            stages.append(lambda: post_norm(k - 1))
        for piece in range(N_IN_PIECES):
            stages.append(lambda piece=piece: in_projection_piece(k + 1, piece))
        return iter(stages)

    def process_tile(k):
        h_ref = h_bufs[k % 2]
        base = k * tt
        mix = mix_ref.at[k]
        if k + 1 < TILES_PER_STEP:
            x_next_rows = xc_ref[0, (k + 1) * tt:(k + 2) * tt, :]
        else:
            x_next_rows = xn_ref[0]
        xb_ref[(k + 1) % 2] = x_next_rows.astype(BF16)
        fillers = make_fillers(k)

        def fill(n=1):
            for _ in range(n):
                stage = next(fillers, None)
                if stage is not None:
                    stage()

        def hg_pre(c):
            rows = slice(c * HG_CHUNK, (c + 1) * HG_CHUNK)
            f = lb + (1.0 - lb) * _sigmoid(h_ref[rows, OFF_HF:OFF_HF + HG_WIDTH])
            log_f = jnp.log2(f)
            hi = log_f.astype(BF16)
            rem = log_f - hi.astype(F32)
            mid = rem.astype(BF16)
            lo = (rem - mid.astype(F32)).astype(BF16)
            return 1.0 - f, jnp.concatenate([hi, mid, lo], axis=0)

        def hg_cum(parts):
            return jnp.dot(cum3, parts, preferred_element_type=F32)

        def hg_decays(c, k_in, g_cum):
            rows = slice(c * HG_CHUNK, (c + 1) * HG_CHUNK)
            hq = h_ref[rows, OFF_HQ:OFF_HQ + HG_WIDTH]
            g_last = g_cum[HG_CHUNK - 1:HG_CHUNK, :]
            q_dec = ((hq * _sigmoid(hq)) * jnp.exp2(g_cum)).astype(BF16)
            k_dec = (k_in * jnp.exp2(-g_cum)).astype(BF16)
            k_tail = (k_in * jnp.exp2(g_last - g_cum)).astype(BF16)
            return q_dec, k_dec, k_tail, jnp.exp2(g_last)

        def hg_scores(q_dec, k_dec, hd):
            sl = slice(hd * HG_HEAD_DIM, (hd + 1) * HG_HEAD_DIM)
            return lax.dot_general(q_dec[:, sl], k_dec[:, sl], NT_DIMS, preferred_element_type=F32)

        def hg_output(c, hd, a, q_dec, k_tail, decay):
            rows = slice(c * HG_CHUNK, (c + 1) * HG_CHUNK)
            sl = slice(hd * HG_HEAD_DIM, (hd + 1) * HG_HEAD_DIM)
            v = h_ref[rows, OFF_HI + hd * HG_HEAD_DIM:OFF_HI + (hd + 1) * HG_HEAD_DIM].astype(BF16)
            s_t = st_ref[hd]
            o = jnp.dot(jnp.where(tril, a, 0.0).astype(BF16), v, preferred_element_type=F32)
            o = o + lax.dot_general(q_dec[:, sl], s_t.astype(BF16), NT_DIMS,
                                    preferred_element_type=F32)
            inc_t = lax.dot_general(v, k_tail[:, sl], TN_DIMS, preferred_element_type=F32)
            st_ref[hd] = s_t * decay[:, sl] + inc_t
            o = o * lax.rsqrt(jnp.mean(o * o, axis=-1, keepdims=True) + RMS_EPS)
            o = o * nw[:, sl]
            hg = h_ref[rows, OFF_HG + hd * HG_HEAD_DIM:OFF_HG + (hd + 1) * HG_HEAD_DIM]
            mix[rows, sl] = (o * (hg * _sigmoid(hg))).astype(BF16)

        def sw_prepare(blk):
            r0 = blk * WINDOW
            rows = slice(r0, r0 + WINDOW)
            cur = slice(base + r0, base + r0 + WINDOW)
            cos = cos_ref[base + r0:base + r0 + WINDOW, :]
            sin = sin_ref[base + r0:base + r0 + WINDOW, :]
            k_rot = rope(h_ref[rows, OFF_AK:OFF_AK + SW_KV_WIDTH], cos, sin)
            for idx, kv in enumerate(head_variants(k_rot)):
                kwin_ref[idx, cur, :] = kv.astype(BF16)
            v_t = h_ref[rows, OFF_AV:OFF_AV + SW_KV_WIDTH].T
            for g in range(SW_KV_HEADS):
                vt_ref[g, 0:SW_HEAD_DIM, cur] = (
                    v_t[g * SW_HEAD_DIM:(g + 1) * SW_HEAD_DIM].astype(BF16))
            q_bf = []
            for tile in range(SW_WIDTH // LANES):
                q_rot = rope(h_ref[rows, OFF_AQ + tile * LANES:OFF_AQ + (tile + 1) * LANES], cos, sin)
                q_bf.append((q_rot * scale).astype(BF16))
            return [jnp.concatenate([q_bf[2 * g], q_bf[2 * g + 1]], axis=0)
                    for g in range(SW_KV_HEADS)]

        def key_window(blk, idx):
            r0 = base + blk * WINDOW
            if r0 == 0:
                return jnp.concatenate([kcarry_ref[idx], kwin_ref[idx, 0:WINDOW, :]], axis=0)
            return kwin_ref[idx, r0 - WINDOW:r0 + WINDOW, :]

        def value_window(blk, g):
            r0 = base + blk * WINDOW
            if r0 == 0:
                return jnp.concatenate([vcarry_ref[g], vt_ref[g, :, 0:WINDOW]], axis=1)
            return vt_ref[g, :, r0 - WINDOW:r0 + WINDOW]

        def sw_scores(blk, g, j, q_pair):
            return lax.dot_general(key_window(blk, 2 * g + j), q_pair, NT_DIMS,
                                   preferred_element_type=F32)

        def sw_values(blk, g, j, s_t, mask):
            sink_row = jnp.where(pair_lo, sinks_ref[SW_GROUP * g + j],
                                 sinks_ref[SW_GROUP * g + HEADS_PER_TILE + j]) * LOG2_E
            s_t = jnp.where(mask, s_t, -jnp.inf)
            m = jnp.maximum(jnp.max(s_t, axis=0, keepdims=True), sink_row)
            p_t = jnp.exp2(s_t - m).astype(BF16)
            acc = jnp.dot(value_window(blk, g), p_t, preferred_element_type=F32)
            denom = acc[SW_HEAD_DIM:SW_HEAD_DIM + 1, :] + jnp.exp2(sink_row - m)
            return acc[0:SW_HEAD_DIM, :] * (1.0 / denom)

        def sw_finish(blk, g, o_t):
            rows = slice(blk * WINDOW, (blk + 1) * WINDOW)
            for tl in range(2):
                tile = 2 * g + tl
                lanes = slice(tl * LANES, (tl + 1) * LANES)
                o_a = jnp.concatenate([o_t[0][:, lanes], o_t[1][:, lanes]], axis=0).T
                ag = h_ref[rows, OFF_AG + tile * LANES:OFF_AG + (tile + 1) * LANES]
                mix[rows, HG_WIDTH + tile * LANES:HG_WIDTH + (tile + 1) * LANES] = (
                    o_a * (ag * _sigmoid(ag))).astype(BF16)

        many = k > 0
        pre = {}
        cum = {}
        fill(2 if many else 1)
        for c in range(min(2, N_CHUNKS)):
            pre[c] = hg_pre(c)
            cum[c] = hg_cum(pre[c][1])
            fill(1 if many else 0)
        for c in range(N_CHUNKS):
            q_dec, k_dec, k_tail, decay = hg_decays(c, pre[c][0], cum[c])
            hg_s = [hg_scores(q_dec, k_dec, hd) for hd in range(HG_HEADS)]
            if c + 2 < N_CHUNKS:
                pre[c + 2] = hg_pre(c + 2)
                cum[c + 2] = hg_cum(pre[c + 2][1])
            fill()
            for hd in range(HG_HEADS):
                hg_output(c, hd, hg_s[hd], q_dec, k_tail, decay)
            fill(1 if many else 0)

        q_pairs = [sw_prepare(blk) for blk in range(N_BLOCKS)]
        masks = [first_band if (k == 0 and blk == 0) else band for blk in range(N_BLOCKS)]
        chains = [(blk, g, j) for blk in range(N_BLOCKS) for g in range(SW_KV_HEADS)
                  for j in range(HEADS_PER_TILE)]
        scores = {}
        for i in range(min(2, len(chains))):
            blk, g, j = chains[i]
            scores[i] = sw_scores(blk, g, j, q_pairs[blk][g])
        fill()
        outs = {}
        for i, (blk, g, j) in enumerate(chains):
            outs[(blk, g, j)] = sw_values(blk, g, j, scores.pop(i), masks[blk])
            if i + 2 < len(chains):
                nb, ng, nj = chains[i + 2]
                scores[i + 2] = sw_scores(nb, ng, nj, q_pairs[nb][ng])
            if j == HEADS_PER_TILE - 1:
                sw_finish(blk, g, [outs.pop((blk, g, jj)) for jj in range(HEADS_PER_TILE)])
            fill()
        fill(N_OUT_PIECES + 1 + N_IN_PIECES)

    for k in range(TILES_PER_STEP):
        process_tile(k)
    for piece in range(N_OUT_PIECES):
        out_projection_piece(TILES_PER_STEP - 1, piece)
    post_norm(TILES_PER_STEP - 1)

    last = slice((STEP_BLOCKS - 1) * WINDOW, STEP_BLOCKS * WINDOW)
    for idx in range(2 * SW_KV_HEADS):
        kcarry_ref[idx] = kwin_ref[idx, last, :]
    for g in range(SW_KV_HEADS):
        vcarry_ref[g, 0:SW_HEAD_DIM, :] = vt_ref[g, 0:SW_HEAD_DIM, last]


def _rope_tables(seq_len):
    pos = jnp.arange(seq_len, dtype=F32)
    inv_freq = ROPE_THETA ** (-jnp.arange(0, ROPE_DIM, 2, dtype=F32) / ROPE_DIM)
    ang = pos[:, None] * inv_freq[None, :]
    cos = jnp.cos(ang)
    sin = jnp.sin(ang)
    ones = jnp.ones((seq_len, SW_HEAD_DIM - ROPE_DIM), F32)
    cos_head = jnp.concatenate([cos, cos, ones], axis=1)
    sin_head = jnp.concatenate([-sin, sin, jnp.zeros_like(ones)], axis=1)
    return (jnp.tile(cos_head, (1, HEADS_PER_TILE)), jnp.tile(sin_head, (1, HEADS_PER_TILE)))


def kernel(x, w_in, lb_logits, hg_norm_w, sinks, w_out, ln_g, ln_b):
    batch, seq_len, d_model = x.shape
    assert d_model == D_MODEL and w_in.shape == (DEPTH, D_MODEL, IN_WIDTH)
    step_rows = TILES_PER_STEP * TIME_TILE
    assert seq_len % step_rows == 0
    cos_tab, sin_tab = _rope_tables(seq_len)
    n_steps = seq_len // step_rows
    last_step = batch * n_steps - 1

    def next_tile_index(b, u, *_):
        nxt = jnp.minimum(b * n_steps + u + 1, last_step)
        return (nxt // n_steps, TILES_PER_STEP * (nxt % n_steps), 0)

    const = lambda b, u, *_: (0, 0)
    grid_spec = pltpu.PrefetchScalarGridSpec(
        num_scalar_prefetch=1,
        grid=(batch, n_steps),
        in_specs=[
            pl.BlockSpec((1, step_rows, D_MODEL), lambda b, u, *_: (b, u, 0)),
            pl.BlockSpec((1, TIME_TILE, D_MODEL), next_tile_index),
            pl.BlockSpec(memory_space=pl.ANY),
            pl.BlockSpec(memory_space=pl.ANY),
            pl.BlockSpec((DEPTH + 1, HG_WIDTH), const),
            pl.BlockSpec((1, HG_WIDTH), const),
            pl.BlockSpec((1, D_MODEL), const),
            pl.BlockSpec((1, D_MODEL), const),
            pl.BlockSpec((step_rows, LANES), lambda b, u, *_: (u, 0)),
            pl.BlockSpec((step_rows, LANES), lambda b, u, *_: (u, 0)),
        ],
        out_specs=pl.BlockSpec((1, step_rows, D_MODEL), lambda b, u, *_: (b, u, 0)),
        scratch_shapes=[
            pltpu.VMEM((TIME_TILE, IN_WIDTH), F32),
            pltpu.VMEM((TIME_TILE, IN_WIDTH), F32),
            pltpu.VMEM((2, TIME_TILE, D_MODEL), BF16),
            pltpu.VMEM((TILES_PER_STEP, TIME_TILE, D_MODEL), BF16),
            pltpu.VMEM((TIME_TILE, D_MODEL), F32),
            pltpu.VMEM((HG_HEADS, HG_HEAD_DIM, HG_HEAD_DIM), F32),
            pltpu.VMEM((2 * SW_KV_HEADS, STEP_BLOCKS * WINDOW, LANES), BF16),
            pltpu.VMEM((SW_KV_HEADS, VT_ROWS, STEP_BLOCKS * WINDOW), BF16),
            pltpu.VMEM((2 * SW_KV_HEADS, WINDOW, LANES), BF16),
            pltpu.VMEM((SW_KV_HEADS, VT_ROWS, WINDOW), BF16),
            pltpu.VMEM((D_MODEL, IN_WIDTH), BF16),
            pltpu.VMEM((D_MODEL, D_MODEL), BF16),
            pltpu.VMEM((2, WEIGHT_CHUNK_ROWS, IN_WIDTH), F32),
            pltpu.VMEM((2, WEIGHT_CHUNK_ROWS, D_MODEL), F32),
            pltpu.SemaphoreType.DMA((2,)),
            pltpu.SemaphoreType.DMA((2,)),
        ],
    )
    return pl.pallas_call(
        _layer_kernel,
        grid_spec=grid_spec,
        out_shape=jax.ShapeDtypeStruct(x.shape, x.dtype),
        compiler_params=pltpu.CompilerParams(
            dimension_semantics=("arbitrary", "arbitrary"),
            vmem_limit_bytes=VMEM_LIMIT_BYTES),
        name="hybrid_hgrn2_swa_layer",
    )(sinks[0], x, x, w_in, w_out, lb_logits, hg_norm_w, ln_g, ln_b, cos_tab, sin_tab)
```

```python
import jax
import jax.numpy as jnp
from jax import lax
from jax.experimental import pallas as pl
from jax.experimental.pallas import tpu as pltpu

D_MODEL = 1024
DEPTH = 1
HG_WIDTH = 512
HG_HEAD_DIM = 128
HG_HEADS = HG_WIDTH // HG_HEAD_DIM
HG_CHUNK = 64
SW_WIDTH = 512
SW_HEAD_DIM = 64
SW_Q_HEADS = SW_WIDTH // SW_HEAD_DIM
SW_KV_HEADS = SW_Q_HEADS // 4
SW_KV_WIDTH = SW_KV_HEADS * SW_HEAD_DIM
SW_GROUP = SW_Q_HEADS // SW_KV_HEADS
WINDOW = 128
ROPE_THETA = 500000.0
ROPE_DIM = SW_HEAD_DIM // 4
DN_ALPHA = (2.0 * DEPTH) ** 0.25
LN_EPS = 1e-5
RMS_EPS = 1e-6
IN_WIDTH = 4 * HG_WIDTH + SW_WIDTH + 2 * SW_KV_WIDTH + SW_WIDTH

OFF_HQ = 0
OFF_HF = OFF_HQ + HG_WIDTH
OFF_HI = OFF_HF + HG_WIDTH
OFF_HG = OFF_HI + HG_WIDTH
OFF_AQ = OFF_HG + HG_WIDTH
OFF_AK = OFF_AQ + SW_WIDTH
OFF_AV = OFF_AK + SW_KV_WIDTH
OFF_AG = OFF_AV + SW_KV_WIDTH

LANES = 128
BF16_SUBLANES = 16
MXU_WIDTH = 256
HEADS_PER_TILE = LANES // SW_HEAD_DIM
assert HEADS_PER_TILE == 2 and SW_GROUP == 2 * HEADS_PER_TILE and SW_KV_WIDTH == LANES
VT_ROWS = SW_HEAD_DIM + BF16_SUBLANES
TIME_TILE = 256
TILES_PER_STEP = 4
PIECE = MXU_WIDTH
N_IN_PIECES = IN_WIDTH // PIECE
N_OUT_PIECES = D_MODEL // PIECE
assert IN_WIDTH % PIECE == 0 and D_MODEL % PIECE == 0
N_CHUNKS = TIME_TILE // HG_CHUNK
N_BLOCKS = TIME_TILE // WINDOW
STEP_BLOCKS = TILES_PER_STEP * N_BLOCKS
WEIGHT_CHUNK_ROWS = 128
TAIL_IN_PIECES = 3
FILLER_PLAN = ((1, 0, 0), (2, 1, 1), (2, 1, 1), (2, 1, 0))
assert len(FILLER_PLAN) == TILES_PER_STEP
VMEM_LIMIT_BYTES = 58 * 1024 * 1024

F32 = jnp.float32
BF16 = jnp.bfloat16
NT_DIMS = (((1,), (1,)), ((), ()))
TN_DIMS = (((0,), (0,)), ((), ()))


LOG2_E = 1.4426950408889634


def _sigmoid(v):
    return 1.0 / (1.0 + jnp.exp2(v * (-LOG2_E)))


def _load_weight_as_bf16(w_hbm, stage_ref, sem_ref, dst_ref):
    n_rows = dst_ref.shape[0]
    chunk = stage_ref.shape[1]
    n_chunks = n_rows // chunk
    assert n_chunks * chunk == n_rows

    def chunk_copy(i):
        return pltpu.make_async_copy(w_hbm.at[0, pl.ds(i * chunk, chunk), :],
                                     stage_ref.at[i % 2], sem_ref.at[i % 2])

    chunk_copy(0).start()
    for i in range(n_chunks):
        if i + 1 < n_chunks:
            chunk_copy(i + 1).start()
        chunk_copy(i).wait()
        dst_ref[i * chunk:(i + 1) * chunk, :] = stage_ref[i % 2].astype(BF16)


def _layer_kernel(sinks_ref, xc_ref, xn_ref, win_hbm, wout_hbm, lbl_ref, nw_ref, lng_ref, lnb_ref,
                  cos_ref, sin_ref, o_ref,
                  ha_ref, hb_ref, xb_ref, mix_ref, out_ref, st_ref, kwin_ref, vt_ref,
                  kcarry_ref, vcarry_ref, win_ref, wout_ref, win_stage, wout_stage,
                  win_sems, wout_sems):
    b = pl.program_id(0)
    u = pl.program_id(1)
    tt = TIME_TILE
    h_bufs = (ha_ref, hb_ref)

    @pl.when(jnp.logical_and(b == 0, u == 0))
    def _first_step():
        _load_weight_as_bf16(win_hbm, win_stage, win_sems, win_ref)
        _load_weight_as_bf16(wout_hbm, wout_stage, wout_sems, wout_ref)
        ha_ref[...] = jnp.dot(xc_ref[0, 0:tt, :].astype(BF16), win_ref[...],
                              preferred_element_type=F32)

    @pl.when(u == 0)
    def _reset_carries():
        st_ref[...] = jnp.zeros_like(st_ref)
        kcarry_ref[...] = jnp.zeros_like(kcarry_ref)
        vcarry_ref[:, 0:SW_HEAD_DIM, :] = jnp.zeros((SW_KV_HEADS, SW_HEAD_DIM, WINDOW), BF16)
        vcarry_ref[:, SW_HEAD_DIM:VT_ROWS, :] = jnp.ones((SW_KV_HEADS, BF16_SUBLANES, WINDOW), BF16)
        vt_ref[:, SW_HEAD_DIM:VT_ROWS, :] = jnp.ones(
            (SW_KV_HEADS, BF16_SUBLANES, STEP_BLOCKS * WINDOW), BF16)

    lbl = lbl_ref[...]
    lbl_e = jnp.exp(lbl - jnp.max(lbl, axis=0, keepdims=True))
    lb = lbl_e[0:1] / jnp.sum(lbl_e, axis=0, keepdims=True)
    nw = nw_ref[...]
    row = lax.broadcasted_iota(jnp.int32, (HG_CHUNK, HG_CHUNK), 0)
    col = lax.broadcasted_iota(jnp.int32, (HG_CHUNK, HG_CHUNK), 1)
    tril = col <= row
    cum_mat = tril.astype(BF16)
    cum3 = jnp.concatenate([cum_mat, cum_mat, cum_mat], axis=1)

    lane = lax.broadcasted_iota(jnp.int32, (1, LANES), 1)
    rope_first = (lane % SW_HEAD_DIM) < (ROPE_DIM // 2)
    lane_lo = lane < SW_HEAD_DIM
    key_idx = lax.broadcasted_iota(jnp.int32, (2 * WINDOW, 2 * WINDOW), 0)
    qry_idx = lax.broadcasted_iota(jnp.int32, (2 * WINDOW, 2 * WINDOW), 1) % WINDOW
    band = (key_idx > qry_idx) & (key_idx <= qry_idx + WINDOW)
    pair_lo = lax.broadcasted_iota(jnp.int32, (1, 2 * WINDOW), 1) < WINDOW
    scale = SW_HEAD_DIM ** -0.5 * LOG2_E
    first_valid = jnp.where(u > 0, 0, WINDOW)
    first_band = band & (key_idx >= first_valid)

    def rope(v, cos, sin):
        partner = jnp.where(rope_first,
                            pltpu.roll(v, LANES - ROPE_DIM // 2, 1),
                            pltpu.roll(v, ROPE_DIM // 2, 1))
        return v * cos + partner * sin

    def head_variants(v):
        sw = pltpu.roll(v, SW_HEAD_DIM, 1)
        zero = jnp.zeros_like(v)
        return (jnp.where(lane_lo, v, zero), jnp.where(lane_lo, zero, sw),
                jnp.where(lane_lo, sw, zero), jnp.where(lane_lo, zero, v))

    def in_projection_piece(k, piece):
        xb = xb_ref.at[k % 2]
        cols = slice(piece * PIECE, (piece + 1) * PIECE)
        h_bufs[k % 2][:, cols] = jnp.dot(xb[...], win_ref[:, cols], preferred_element_type=F32)

    def out_projection_piece(k, piece):
        cols = slice(piece * PIECE, (piece + 1) * PIECE)
        out_ref[:, cols] = jnp.dot(mix_ref[k], wout_ref[:, cols], preferred_element_type=F32)

    def post_norm(k):
        rows = slice(k * tt, (k + 1) * tt)
        z = DN_ALPHA * xc_ref[0, rows, :] + out_ref[...]
        mu = jnp.mean(z, axis=-1, keepdims=True)
        zc = z - mu
        var = jnp.mean(zc * zc, axis=-1, keepdims=True)
        o_ref[0, rows, :] = (zc * lax.rsqrt(var + LN_EPS)) * lng_ref[...] + lnb_ref[...]

    def make_fillers(k):
        stages = []
        if k > 0:
            for piece in range(N_OUT_PIECES):
                stages.append(lambda piece=piece: out_projection_piece(k - 1, piece))
            stages.append(lambda: post_norm(k - 1))
        held_back = TAIL_IN_PIECES if k == TILES_PER_STEP - 1 else 0
        for piece in range(N_IN_PIECES - held_back):
            stages.append(lambda piece=piece: in_projection_piece(k + 1, piece))
        return iter(stages)

    def process_tile(k):
        h_ref = h_bufs[k % 2]
        base = k * tt
        mix = mix_ref.at[k]
        if k + 1 < TILES_PER_STEP:
            x_next_rows = xc_ref[0, (k + 1) * tt:(k + 2) * tt, :]
        else:
            x_next_rows = xn_ref[0]
        xb_ref[(k + 1) % 2] = x_next_rows.astype(BF16)
        fillers = make_fillers(k)

        def fill(n=1):
            for _ in range(n):
                stage = next(fillers, None)
                if stage is not None:
                    stage()

        def hg_pre(c):
            rows = slice(c * HG_CHUNK, (c + 1) * HG_CHUNK)
            f = lb + (1.0 - lb) * _sigmoid(h_ref[rows, OFF_HF:OFF_HF + HG_WIDTH])
            log_f = jnp.log2(f)
            hi = log_f.astype(BF16)
            rem = log_f - hi.astype(F32)
            mid = rem.astype(BF16)
            lo = (rem - mid.astype(F32)).astype(BF16)
            return 1.0 - f, jnp.concatenate([hi, mid, lo], axis=0)

        def hg_cum(parts):
            return jnp.dot(cum3, parts, preferred_element_type=F32)

        def hg_decays(c, k_in, g_cum):
            rows = slice(c * HG_CHUNK, (c + 1) * HG_CHUNK)
            hq = h_ref[rows, OFF_HQ:OFF_HQ + HG_WIDTH]
            g_last = g_cum[HG_CHUNK - 1:HG_CHUNK, :]
            q_dec = ((hq * _sigmoid(hq)) * jnp.exp2(g_cum)).astype(BF16)
            k_dec = (k_in * jnp.exp2(-g_cum)).astype(BF16)
            k_tail = (k_in * jnp.exp2(g_last - g_cum)).astype(BF16)
            return q_dec, k_dec, k_tail, jnp.exp2(g_last)

        def hg_scores(q_dec, k_dec, hd):
            sl = slice(hd * HG_HEAD_DIM, (hd + 1) * HG_HEAD_DIM)
            return lax.dot_general(q_dec[:, sl], k_dec[:, sl], NT_DIMS, preferred_element_type=F32)

        def hg_output(c, hd, a, q_dec, k_tail, decay):
            rows = slice(c * HG_CHUNK, (c + 1) * HG_CHUNK)
            sl = slice(hd * HG_HEAD_DIM, (hd + 1) * HG_HEAD_DIM)
            v = h_ref[rows, OFF_HI + hd * HG_HEAD_DIM:OFF_HI + (hd + 1) * HG_HEAD_DIM].astype(BF16)
            s_t = st_ref[hd]
            o = jnp.dot(jnp.where(tril, a, 0.0).astype(BF16), v, preferred_element_type=F32)
            o = o + lax.dot_general(q_dec[:, sl], s_t.astype(BF16), NT_DIMS,
                                    preferred_element_type=F32)
            inc_t = lax.dot_general(v, k_tail[:, sl], TN_DIMS, preferred_element_type=F32)
            st_ref[hd] = s_t * decay[:, sl] + inc_t
            o = o * lax.rsqrt(jnp.mean(o * o, axis=-1, keepdims=True) + RMS_EPS)
            o = o * nw[:, sl]
            hg = h_ref[rows, OFF_HG + hd * HG_HEAD_DIM:OFF_HG + (hd + 1) * HG_HEAD_DIM]
            mix[rows, sl] = (o * (hg * _sigmoid(hg))).astype(BF16)

        def sw_prepare(blk):
            r0 = blk * WINDOW
            rows = slice(r0, r0 + WINDOW)
            cur = slice(base + r0, base + r0 + WINDOW)
            cos = cos_ref[base + r0:base + r0 + WINDOW, :]
            sin = sin_ref[base + r0:base + r0 + WINDOW, :]
            k_rot = rope(h_ref[rows, OFF_AK:OFF_AK + SW_KV_WIDTH], cos, sin)
            for idx, kv in enumerate(head_variants(k_rot)):
                kwin_ref[idx, cur, :] = kv.astype(BF16)
            v_t = h_ref[rows, OFF_AV:OFF_AV + SW_KV_WIDTH].T
            for g in range(SW_KV_HEADS):
                vt_ref[g, 0:SW_HEAD_DIM, cur] = (
                    v_t[g * SW_HEAD_DIM:(g + 1) * SW_HEAD_DIM].astype(BF16))
            q_bf = []
            for tile in range(SW_WIDTH // LANES):
                q_rot = rope(h_ref[rows, OFF_AQ + tile * LANES:OFF_AQ + (tile + 1) * LANES], cos, sin)
                q_bf.append((q_rot * scale).astype(BF16))
            return [jnp.concatenate([q_bf[2 * g], q_bf[2 * g + 1]], axis=0)
                    for g in range(SW_KV_HEADS)]

        def key_window(blk, idx):
            r0 = base + blk * WINDOW
            if r0 == 0:
                return jnp.concatenate([kcarry_ref[idx], kwin_ref[idx, 0:WINDOW, :]], axis=0)
            return kwin_ref[idx, r0 - WINDOW:r0 + WINDOW, :]

        def value_window(blk, g):
            r0 = base + blk * WINDOW
            if r0 == 0:
                return jnp.concatenate([vcarry_ref[g], vt_ref[g, :, 0:WINDOW]], axis=1)
            return vt_ref[g, :, r0 - WINDOW:r0 + WINDOW]

        def sw_scores(blk, g, j, q_pair):
            return lax.dot_general(key_window(blk, 2 * g + j), q_pair, NT_DIMS,
                                   preferred_element_type=F32)

        def sw_values(blk, g, j, s_t, mask):
            sink_row = jnp.where(pair_lo, sinks_ref[SW_GROUP * g + j],
                                 sinks_ref[SW_GROUP * g + HEADS_PER_TILE + j]) * LOG2_E
            s_t = jnp.where(mask, s_t, -jnp.inf)
            m = jnp.maximum(jnp.max(s_t, axis=0, keepdims=True), sink_row)
            p_t = jnp.exp2(s_t - m).astype(BF16)
            acc = jnp.dot(value_window(blk, g), p_t, preferred_element_type=F32)
            denom = acc[SW_HEAD_DIM:SW_HEAD_DIM + 1, :] + jnp.exp2(sink_row - m)
            return acc[0:SW_HEAD_DIM, :] * (1.0 / denom)

        def sw_finish(blk, g, o_t):
            rows = slice(blk * WINDOW, (blk + 1) * WINDOW)
            for tl in range(2):
                tile = 2 * g + tl
                lanes = slice(tl * LANES, (tl + 1) * LANES)
                o_a = jnp.concatenate([o_t[0][:, lanes], o_t[1][:, lanes]], axis=0).T
                ag = h_ref[rows, OFF_AG + tile * LANES:OFF_AG + (tile + 1) * LANES]
                mix[rows, HG_WIDTH + tile * LANES:HG_WIDTH + (tile + 1) * LANES] = (
                    o_a * (ag * _sigmoid(ag))).astype(BF16)

        n_start, n_cum, n_out = FILLER_PLAN[k]
        pre = {}
        cum = {}
        fill(n_start)
        for c in range(min(2, N_CHUNKS)):
            pre[c] = hg_pre(c)
            cum[c] = hg_cum(pre[c][1])
            fill(n_cum)
        for c in range(N_CHUNKS):
            q_dec, k_dec, k_tail, decay = hg_decays(c, pre[c][0], cum[c])
            hg_s = [hg_scores(q_dec, k_dec, hd) for hd in range(HG_HEADS)]
            if c + 2 < N_CHUNKS:
                pre[c + 2] = hg_pre(c + 2)
                cum[c + 2] = hg_cum(pre[c + 2][1])
            fill()
            for hd in range(HG_HEADS):
                hg_output(c, hd, hg_s[hd], q_dec, k_tail, decay)
            fill(n_out)

        q_pairs = [sw_prepare(blk) for blk in range(N_BLOCKS)]
        masks = [first_band if (k == 0 and blk == 0) else band for blk in range(N_BLOCKS)]
        chains = [(blk, g, j) for blk in range(N_BLOCKS) for g in range(SW_KV_HEADS)
                  for j in range(HEADS_PER_TILE)]
        scores = {}
        for i in range(min(2, len(chains))):
            blk, g, j = chains[i]
            scores[i] = sw_scores(blk, g, j, q_pairs[blk][g])
        fill()
        outs = {}
        for i, (blk, g, j) in enumerate(chains):
            outs[(blk, g, j)] = sw_values(blk, g, j, scores.pop(i), masks[blk])
            if i + 2 < len(chains):
                nb, ng, nj = chains[i + 2]
                scores[i + 2] = sw_scores(nb, ng, nj, q_pairs[nb][ng])
            if j == HEADS_PER_TILE - 1:
                sw_finish(blk, g, [outs.pop((blk, g, jj)) for jj in range(HEADS_PER_TILE)])
            fill()
        fill(N_OUT_PIECES + 1 + N_IN_PIECES)

    for k in range(TILES_PER_STEP):
        process_tile(k)
    for piece in range(N_OUT_PIECES):
        out_projection_piece(TILES_PER_STEP - 1, piece)
    post_norm(TILES_PER_STEP - 1)
    for piece in range(N_IN_PIECES - TAIL_IN_PIECES, N_IN_PIECES):
        in_projection_piece(TILES_PER_STEP, piece)

    last = slice((STEP_BLOCKS - 1) * WINDOW, STEP_BLOCKS * WINDOW)
    for idx in range(2 * SW_KV_HEADS):
        kcarry_ref[idx] = kwin_ref[idx, last, :]
    for g in range(SW_KV_HEADS):
        vcarry_ref[g, 0:SW_HEAD_DIM, :] = vt_ref[g, 0:SW_HEAD_DIM, last]


def _rope_tables(seq_len):
    pos = jnp.arange(seq_len, dtype=F32)
    inv_freq = ROPE_THETA ** (-jnp.arange(0, ROPE_DIM, 2, dtype=F32) / ROPE_DIM)
    ang = pos[:, None] * inv_freq[None, :]
    cos = jnp.cos(ang)
    sin = jnp.sin(ang)
    ones = jnp.ones((seq_len, SW_HEAD_DIM - ROPE_DIM), F32)
    cos_head = jnp.concatenate([cos, cos, ones], axis=1)
    sin_head = jnp.concatenate([-sin, sin, jnp.zeros_like(ones)], axis=1)
    return (jnp.tile(cos_head, (1, HEADS_PER_TILE)), jnp.tile(sin_head, (1, HEADS_PER_TILE)))


def kernel(x, w_in, lb_logits, hg_norm_w, sinks, w_out, ln_g, ln_b):
    batch, seq_len, d_model = x.shape
    assert d_model == D_MODEL and w_in.shape == (DEPTH, D_MODEL, IN_WIDTH)
    step_rows = TILES_PER_STEP * TIME_TILE
    assert seq_len % step_rows == 0
    cos_tab, sin_tab = _rope_tables(seq_len)
    n_steps = seq_len // step_rows
    last_step = batch * n_steps - 1

    def next_tile_index(b, u, *_):
        nxt = jnp.minimum(b * n_steps + u + 1, last_step)
        return (nxt // n_steps, TILES_PER_STEP * (nxt % n_steps), 0)

    const = lambda b, u, *_: (0, 0)
    grid_spec = pltpu.PrefetchScalarGridSpec(
        num_scalar_prefetch=1,
        grid=(batch, n_steps),
        in_specs=[
            pl.BlockSpec((1, step_rows, D_MODEL), lambda b, u, *_: (b, u, 0)),
            pl.BlockSpec((1, TIME_TILE, D_MODEL), next_tile_index),
            pl.BlockSpec(memory_space=pl.ANY),
            pl.BlockSpec(memory_space=pl.ANY),
            pl.BlockSpec((DEPTH + 1, HG_WIDTH), const),
            pl.BlockSpec((1, HG_WIDTH), const),
            pl.BlockSpec((1, D_MODEL), const),
            pl.BlockSpec((1, D_MODEL), const),
            pl.BlockSpec((step_rows, LANES), lambda b, u, *_: (u, 0)),
            pl.BlockSpec((step_rows, LANES), lambda b, u, *_: (u, 0)),
        ],
        out_specs=pl.BlockSpec((1, step_rows, D_MODEL), lambda b, u, *_: (b, u, 0)),
        scratch_shapes=[
            pltpu.VMEM((TIME_TILE, IN_WIDTH), F32),
            pltpu.VMEM((TIME_TILE, IN_WIDTH), F32),
            pltpu.VMEM((2, TIME_TILE, D_MODEL), BF16),
            pltpu.VMEM((TILES_PER_STEP, TIME_TILE, D_MODEL), BF16),
            pltpu.VMEM((TIME_TILE, D_MODEL), F32),
            pltpu.VMEM((HG_HEADS, HG_HEAD_DIM, HG_HEAD_DIM), F32),
            pltpu.VMEM((2 * SW_KV_HEADS, STEP_BLOCKS * WINDOW, LANES), BF16),
            pltpu.VMEM((SW_KV_HEADS, VT_ROWS, STEP_BLOCKS * WINDOW), BF16),
            pltpu.VMEM((2 * SW_KV_HEADS, WINDOW, LANES), BF16),
            pltpu.VMEM((SW_KV_HEADS, VT_ROWS, WINDOW), BF16),
            pltpu.VMEM((D_MODEL, IN_WIDTH), BF16),
            pltpu.VMEM((D_MODEL, D_MODEL), BF16),
            pltpu.VMEM((2, WEIGHT_CHUNK_ROWS, IN_WIDTH), F32),
            pltpu.VMEM((2, WEIGHT_CHUNK_ROWS, D_MODEL), F32),
            pltpu.SemaphoreType.DMA((2,)),
            pltpu.SemaphoreType.DMA((2,)),
        ],
    )
    return pl.pallas_call(
        _layer_kernel,
        grid_spec=grid_spec,
        out_shape=jax.ShapeDtypeStruct(x.shape, x.dtype),
        compiler_params=pltpu.CompilerParams(
            dimension_semantics=("arbitrary", "arbitrary"),
            vmem_limit_bytes=VMEM_LIMIT_BYTES),
        name="hybrid_hgrn2_swa_layer",
    )(sinks[0], x, x, w_in, w_out, lb_logits, hg_norm_w, ln_g, ln_b, cos_tab, sin_tab)
```

```python
import jax
import jax.numpy as jnp
from jax import lax
from jax.experimental import pallas as pl
from jax.experimental.pallas import tpu as pltpu

D_MODEL = 1024
DEPTH = 1
HG_WIDTH = 512
HG_HEAD_DIM = 128
HG_HEADS = HG_WIDTH // HG_HEAD_DIM
HG_CHUNK = 64
SW_WIDTH = 512
SW_HEAD_DIM = 64
SW_Q_HEADS = SW_WIDTH // SW_HEAD_DIM
SW_KV_HEADS = SW_Q_HEADS // 4
SW_KV_WIDTH = SW_KV_HEADS * SW_HEAD_DIM
SW_GROUP = SW_Q_HEADS // SW_KV_HEADS
WINDOW = 128
ROPE_THETA = 500000.0
ROPE_DIM = SW_HEAD_DIM // 4
DN_ALPHA = (2.0 * DEPTH) ** 0.25
LN_EPS = 1e-5
RMS_EPS = 1e-6
IN_WIDTH = 4 * HG_WIDTH + SW_WIDTH + 2 * SW_KV_WIDTH + SW_WIDTH

OFF_HQ = 0
OFF_HF = OFF_HQ + HG_WIDTH
OFF_HI = OFF_HF + HG_WIDTH
OFF_HG = OFF_HI + HG_WIDTH
OFF_AQ = OFF_HG + HG_WIDTH
OFF_AK = OFF_AQ + SW_WIDTH
OFF_AV = OFF_AK + SW_KV_WIDTH
OFF_AG = OFF_AV + SW_KV_WIDTH

LANES = 128
BF16_SUBLANES = 16
MXU_WIDTH = 256
HEADS_PER_TILE = LANES // SW_HEAD_DIM
assert HEADS_PER_TILE == 2 and SW_GROUP == 2 * HEADS_PER_TILE and SW_KV_WIDTH == LANES
VT_ROWS = SW_HEAD_DIM + BF16_SUBLANES
TIME_TILE = 256
TILES_PER_STEP = 4
PIECE = MXU_WIDTH
N_IN_PIECES = IN_WIDTH // PIECE
N_OUT_PIECES = D_MODEL // PIECE
assert IN_WIDTH % PIECE == 0 and D_MODEL % PIECE == 0
N_CHUNKS = TIME_TILE // HG_CHUNK
N_BLOCKS = TIME_TILE // WINDOW
STEP_BLOCKS = TILES_PER_STEP * N_BLOCKS
WEIGHT_CHUNK_ROWS = 64
WEIGHT_SLOTS = 4
VMEM_LIMIT_BYTES = 58 * 1024 * 1024

F32 = jnp.float32
BF16 = jnp.bfloat16
NT_DIMS = (((1,), (1,)), ((), ()))
TN_DIMS = (((0,), (0,)), ((), ()))


LOG2_E = 1.4426950408889634


def _sigmoid(v):
    return 1.0 / (1.0 + jnp.exp2(v * (-LOG2_E)))


def _load_weight_as_bf16(w_hbm, stage_ref, sem_ref, dst_ref):
    n_rows = dst_ref.shape[0]
    n_slots, chunk = stage_ref.shape[0], stage_ref.shape[1]
    n_chunks = n_rows // chunk
    assert n_chunks * chunk == n_rows and n_slots >= 2

    def chunk_copy(i):
        return pltpu.make_async_copy(w_hbm.at[0, pl.ds(i * chunk, chunk), :],
                                     stage_ref.at[i % n_slots], sem_ref.at[i % n_slots])

    for i in range(min(n_slots - 1, n_chunks)):
        chunk_copy(i).start()
    for i in range(n_chunks):
        ahead = i + n_slots - 1
        if ahead < n_chunks:
            chunk_copy(ahead).start()
        chunk_copy(i).wait()
        dst_ref[i * chunk:(i + 1) * chunk, :] = stage_ref[i % n_slots].astype(BF16)


def _layer_kernel(sinks_ref, xc_ref, xn_ref, win_hbm, wout_hbm, lbl_ref, nw_ref, lng_ref, lnb_ref,
                  cos_ref, sin_ref, o_ref,
                  ha_ref, hb_ref, xb_ref, mix_ref, out_ref, st_ref, kwin_ref, vt_ref,
                  kcarry_ref, vcarry_ref, win_ref, wout_ref, win_stage, wout_stage,
                  win_sems, wout_sems):
    b = pl.program_id(0)
    u = pl.program_id(1)
    tt = TIME_TILE
    h_bufs = (ha_ref, hb_ref)

    @pl.when(jnp.logical_and(b == 0, u == 0))
    def _first_step():
        _load_weight_as_bf16(win_hbm, win_stage, win_sems, win_ref)
        _load_weight_as_bf16(wout_hbm, wout_stage, wout_sems, wout_ref)
        ha_ref[...] = jnp.dot(xc_ref[0, 0:tt, :].astype(BF16), win_ref[...],
                              preferred_element_type=F32)

    @pl.when(u == 0)
    def _reset_carries():
        st_ref[...] = jnp.zeros_like(st_ref)
        kcarry_ref[...] = jnp.zeros_like(kcarry_ref)
        vcarry_ref[:, 0:SW_HEAD_DIM, :] = jnp.zeros((SW_KV_HEADS, SW_HEAD_DIM, WINDOW), BF16)
        vcarry_ref[:, SW_HEAD_DIM:VT_ROWS, :] = jnp.ones((SW_KV_HEADS, BF16_SUBLANES, WINDOW), BF16)
        vt_ref[:, SW_HEAD_DIM:VT_ROWS, :] = jnp.ones(
            (SW_KV_HEADS, BF16_SUBLANES, STEP_BLOCKS * WINDOW), BF16)

    lbl = lbl_ref[...]
    lbl_e = jnp.exp(lbl - jnp.max(lbl, axis=0, keepdims=True))
    lb = lbl_e[0:1] / jnp.sum(lbl_e, axis=0, keepdims=True)
    nw = nw_ref[...]
    row = lax.broadcasted_iota(jnp.int32, (HG_CHUNK, HG_CHUNK), 0)
    col = lax.broadcasted_iota(jnp.int32, (HG_CHUNK, HG_CHUNK), 1)
    tril = col <= row
    cum_mat = tril.astype(BF16)
    cum3 = jnp.concatenate([cum_mat, cum_mat, cum_mat], axis=1)

    lane = lax.broadcasted_iota(jnp.int32, (1, LANES), 1)
    rope_first = (lane % SW_HEAD_DIM) < (ROPE_DIM // 2)
    lane_lo = lane < SW_HEAD_DIM
    key_idx = lax.broadcasted_iota(jnp.int32, (2 * WINDOW, 2 * WINDOW), 0)
    qry_idx = lax.broadcasted_iota(jnp.int32, (2 * WINDOW, 2 * WINDOW), 1) % WINDOW
    band = (key_idx > qry_idx) & (key_idx <= qry_idx + WINDOW)
    pair_lo = lax.broadcasted_iota(jnp.int32, (1, 2 * WINDOW), 1) < WINDOW
    scale = SW_HEAD_DIM ** -0.5 * LOG2_E
    first_valid = jnp.where(u > 0, 0, WINDOW)
    first_band = band & (key_idx >= first_valid)

    def rope(v, cos, sin):
        partner = jnp.where(rope_first,
                            pltpu.roll(v, LANES - ROPE_DIM // 2, 1),
                            pltpu.roll(v, ROPE_DIM // 2, 1))
        return v * cos + partner * sin

    def head_variants(v):
        sw = pltpu.roll(v, SW_HEAD_DIM, 1)
        zero = jnp.zeros_like(v)
        return (jnp.where(lane_lo, v, zero), jnp.where(lane_lo, zero, sw),
                jnp.where(lane_lo, sw, zero), jnp.where(lane_lo, zero, v))

    def in_projection_piece(k, piece):
        xb = xb_ref.at[k % 2]
        cols = slice(piece * PIECE, (piece + 1) * PIECE)
        h_bufs[k % 2][:, cols] = jnp.dot(xb[...], win_ref[:, cols], preferred_element_type=F32)

    def out_projection_piece(k, piece):
        cols = slice(piece * PIECE, (piece + 1) * PIECE)
        out_ref[:, cols] = jnp.dot(mix_ref[k], wout_ref[:, cols], preferred_element_type=F32)

    def post_norm(k):
        rows = slice(k * tt, (k + 1) * tt)
        z = DN_ALPHA * xc_ref[0, rows, :] + out_ref[...]
        mu = jnp.mean(z, axis=-1, keepdims=True)
        zc = z - mu
        var = jnp.mean(zc * zc, axis=-1, keepdims=True)
        o_ref[0, rows, :] = (zc * lax.rsqrt(var + LN_EPS)) * lng_ref[...] + lnb_ref[...]

    def make_fillers(k):
        stages = []
        if k > 0:
            for piece in range(N_OUT_PIECES):
                stages.append(lambda piece=piece: out_projection_piece(k - 1, piece))
            stages.append(lambda: post_norm(k - 1))
        for piece in range(N_IN_PIECES):
            stages.append(lambda piece=piece: in_projection_piece(k + 1, piece))
        return iter(stages)

    def process_tile(k):
        h_ref = h_bufs[k % 2]
        base = k * tt
        mix = mix_ref.at[k]
        if k + 1 < TILES_PER_STEP:
            x_next_rows = xc_ref[0, (k + 1) * tt:(k + 2) * tt, :]
        else:
            x_next_rows = xn_ref[0]
        xb_ref[(k + 1) % 2] = x_next_rows.astype(BF16)
        fillers = make_fillers(k)

        def fill(n=1):
            for _ in range(n):
                stage = next(fillers, None)
                if stage is not None:
                    stage()

        def hg_pre(c):
            rows = slice(c * HG_CHUNK, (c + 1) * HG_CHUNK)
            f = lb + (1.0 - lb) * _sigmoid(h_ref[rows, OFF_HF:OFF_HF + HG_WIDTH])
            log_f = jnp.log2(f)
            hi = log_f.astype(BF16)
            rem = log_f - hi.astype(F32)
            mid = rem.astype(BF16)
            lo = (rem - mid.astype(F32)).astype(BF16)
            return 1.0 - f, jnp.concatenate([hi, mid, lo], axis=0)

        def hg_cum(parts):
            return jnp.dot(cum3, parts, preferred_element_type=F32)

        def hg_decays(c, k_in, g_cum):
            rows = slice(c * HG_CHUNK, (c + 1) * HG_CHUNK)
            hq = h_ref[rows, OFF_HQ:OFF_HQ + HG_WIDTH]
            g_last = g_cum[HG_CHUNK - 1:HG_CHUNK, :]
            q_dec = ((hq * _sigmoid(hq)) * jnp.exp2(g_cum)).astype(BF16)
            k_dec = (k_in * jnp.exp2(-g_cum)).astype(BF16)
            k_tail = (k_in * jnp.exp2(g_last - g_cum)).astype(BF16)
            return q_dec, k_dec, k_tail, jnp.exp2(g_last)

        def hg_scores(q_dec, k_dec, hd):
            sl = slice(hd * HG_HEAD_DIM, (hd + 1) * HG_HEAD_DIM)
            return lax.dot_general(q_dec[:, sl], k_dec[:, sl], NT_DIMS, preferred_element_type=F32)

        def hg_output(c, hd, a, q_dec, k_tail, decay):
            rows = slice(c * HG_CHUNK, (c + 1) * HG_CHUNK)
            sl = slice(hd * HG_HEAD_DIM, (hd + 1) * HG_HEAD_DIM)
            v = h_ref[rows, OFF_HI + hd * HG_HEAD_DIM:OFF_HI + (hd + 1) * HG_HEAD_DIM].astype(BF16)
            s_t = st_ref[hd]
            o = jnp.dot(jnp.where(tril, a, 0.0).astype(BF16), v, preferred_element_type=F32)
            o = o + lax.dot_general(q_dec[:, sl], s_t.astype(BF16), NT_DIMS,
                                    preferred_element_type=F32)
            inc_t = lax.dot_general(v, k_tail[:, sl], TN_DIMS, preferred_element_type=F32)
            st_ref[hd] = s_t * decay[:, sl] + inc_t
            o = o * lax.rsqrt(jnp.mean(o * o, axis=-1, keepdims=True) + RMS_EPS)
            o = o * nw[:, sl]
            hg = h_ref[rows, OFF_HG + hd * HG_HEAD_DIM:OFF_HG + (hd + 1) * HG_HEAD_DIM]
            mix[rows, sl] = (o * (hg * _sigmoid(hg))).astype(BF16)

        def sw_prepare(blk):
            r0 = blk * WINDOW
            rows = slice(r0, r0 + WINDOW)
            cur = slice(base + r0, base + r0 + WINDOW)
            cos = cos_ref[base + r0:base + r0 + WINDOW, :]
            sin = sin_ref[base + r0:base + r0 + WINDOW, :]
            k_rot = rope(h_ref[rows, OFF_AK:OFF_AK + SW_KV_WIDTH], cos, sin)
            for idx, kv in enumerate(head_variants(k_rot)):
                kwin_ref[idx, cur, :] = kv.astype(BF16)
            v_t = h_ref[rows, OFF_AV:OFF_AV + SW_KV_WIDTH].T
            for g in range(SW_KV_HEADS):
                vt_ref[g, 0:SW_HEAD_DIM, cur] = (
                    v_t[g * SW_HEAD_DIM:(g + 1) * SW_HEAD_DIM].astype(BF16))
            q_bf = []
            for tile in range(SW_WIDTH // LANES):
                q_rot = rope(h_ref[rows, OFF_AQ + tile * LANES:OFF_AQ + (tile + 1) * LANES], cos, sin)
                q_bf.append((q_rot * scale).astype(BF16))
            return [jnp.concatenate([q_bf[2 * g], q_bf[2 * g + 1]], axis=0)
                    for g in range(SW_KV_HEADS)]

        def key_window(blk, idx):
            r0 = base + blk * WINDOW
            if r0 == 0:
                return jnp.concatenate([kcarry_ref[idx], kwin_ref[idx, 0:WINDOW, :]], axis=0)
            return kwin_ref[idx, r0 - WINDOW:r0 + WINDOW, :]

        def value_window(blk, g):
            r0 = base + blk * WINDOW
            if r0 == 0:
                return jnp.concatenate([vcarry_ref[g], vt_ref[g, :, 0:WINDOW]], axis=1)
            return vt_ref[g, :, r0 - WINDOW:r0 + WINDOW]

        def sw_scores(blk, g, j, q_pair):
            return lax.dot_general(key_window(blk, 2 * g + j), q_pair, NT_DIMS,
                                   preferred_element_type=F32)

        def sw_values(blk, g, j, s_t, mask):
            sink_row = jnp.where(pair_lo, sinks_ref[SW_GROUP * g + j],
                                 sinks_ref[SW_GROUP * g + HEADS_PER_TILE + j]) * LOG2_E
            s_t = jnp.where(mask, s_t, -jnp.inf)
            m = jnp.maximum(jnp.max(s_t, axis=0, keepdims=True), sink_row)
            p_t = jnp.exp2(s_t - m).astype(BF16)
            acc = jnp.dot(value_window(blk, g), p_t, preferred_element_type=F32)
            denom = acc[SW_HEAD_DIM:SW_HEAD_DIM + 1, :] + jnp.exp2(sink_row - m)
            return acc[0:SW_HEAD_DIM, :] * (1.0 / denom)

        def sw_finish(blk, g, o_t):
            rows = slice(blk * WINDOW, (blk + 1) * WINDOW)
            for tl in range(2):
                tile = 2 * g + tl
                lanes = slice(tl * LANES, (tl + 1) * LANES)
                o_a = jnp.concatenate([o_t[0][:, lanes], o_t[1][:, lanes]], axis=0).T
                ag = h_ref[rows, OFF_AG + tile * LANES:OFF_AG + (tile + 1) * LANES]
                mix[rows, HG_WIDTH + tile * LANES:HG_WIDTH + (tile + 1) * LANES] = (
                    o_a * (ag * _sigmoid(ag))).astype(BF16)

        many = k > 0
        pre = {}
        cum = {}
        fill(2 if many else 1)
        for c in range(min(2, N_CHUNKS)):
            pre[c] = hg_pre(c)
            cum[c] = hg_cum(pre[c][1])
            fill(1 if many else 0)
        for c in range(N_CHUNKS):
            q_dec, k_dec, k_tail, decay = hg_decays(c, pre[c][0], cum[c])
            hg_s = [hg_scores(q_dec, k_dec, hd) for hd in range(HG_HEADS)]
            if c + 2 < N_CHUNKS:
                pre[c + 2] = hg_pre(c + 2)
                cum[c + 2] = hg_cum(pre[c + 2][1])
            fill()
            for hd in range(HG_HEADS):
                hg_output(c, hd, hg_s[hd], q_dec, k_tail, decay)
            fill(1 if many else 0)

        q_pairs = [sw_prepare(blk) for blk in range(N_BLOCKS)]
        masks = [first_band if (k == 0 and blk == 0) else band for blk in range(N_BLOCKS)]
        chains = [(blk, g, j) for blk in range(N_BLOCKS) for g in range(SW_KV_HEADS)
                  for j in range(HEADS_PER_TILE)]
        scores = {}
        for i in range(min(2, len(chains))):
            blk, g, j = chains[i]
            scores[i] = sw_scores(blk, g, j, q_pairs[blk][g])
        fill()
        outs = {}
        for i, (blk, g, j) in enumerate(chains):
            outs[(blk, g, j)] = sw_values(blk, g, j, scores.pop(i), masks[blk])
            if i + 2 < len(chains):
                nb, ng, nj = chains[i + 2]
                scores[i + 2] = sw_scores(nb, ng, nj, q_pairs[nb][ng])
            if j == HEADS_PER_TILE - 1:
                sw_finish(blk, g, [outs.pop((blk, g, jj)) for jj in range(HEADS_PER_TILE)])
            fill()
        fill(N_OUT_PIECES + 1 + N_IN_PIECES)

    for k in range(TILES_PER_STEP):
        process_tile(k)
    for piece in range(N_OUT_PIECES):
        out_projection_piece(TILES_PER_STEP - 1, piece)
    post_norm(TILES_PER_STEP - 1)

    last = slice((STEP_BLOCKS - 1) * WINDOW, STEP_BLOCKS * WINDOW)
    for idx in range(2 * SW_KV_HEADS):
        kcarry_ref[idx] = kwin_ref[idx, last, :]
    for g in range(SW_KV_HEADS):
        vcarry_ref[g, 0:SW_HEAD_DIM, :] = vt_ref[g, 0:SW_HEAD_DIM, last]


def _rope_tables(seq_len):
    pos = jnp.arange(seq_len, dtype=F32)
    inv_freq = ROPE_THETA ** (-jnp.arange(0, ROPE_DIM, 2, dtype=F32) / ROPE_DIM)
    ang = pos[:, None] * inv_freq[None, :]
    cos = jnp.cos(ang)
    sin = jnp.sin(ang)
    ones = jnp.ones((seq_len, SW_HEAD_DIM - ROPE_DIM), F32)
    cos_head = jnp.concatenate([cos, cos, ones], axis=1)
    sin_head = jnp.concatenate([-sin, sin, jnp.zeros_like(ones)], axis=1)
    return (jnp.tile(cos_head, (1, HEADS_PER_TILE)), jnp.tile(sin_head, (1, HEADS_PER_TILE)))


def kernel(x, w_in, lb_logits, hg_norm_w, sinks, w_out, ln_g, ln_b):
    batch, seq_len, d_model = x.shape
    assert d_model == D_MODEL and w_in.shape == (DEPTH, D_MODEL, IN_WIDTH)
    step_rows = TILES_PER_STEP * TIME_TILE
    assert seq_len % step_rows == 0
    cos_tab, sin_tab = _rope_tables(seq_len)
    n_steps = seq_len // step_rows
    last_step = batch * n_steps - 1

    def next_tile_index(b, u, *_):
        nxt = jnp.minimum(b * n_steps + u + 1, last_step)
        return (nxt // n_steps, TILES_PER_STEP * (nxt % n_steps), 0)

    const = lambda b, u, *_: (0, 0)
    grid_spec = pltpu.PrefetchScalarGridSpec(
        num_scalar_prefetch=1,
        grid=(batch, n_steps),
        in_specs=[
            pl.BlockSpec((1, step_rows, D_MODEL), lambda b, u, *_: (b, u, 0)),
            pl.BlockSpec((1, TIME_TILE, D_MODEL), next_tile_index),
            pl.BlockSpec(memory_space=pl.ANY),
            pl.BlockSpec(memory_space=pl.ANY),
            pl.BlockSpec((DEPTH + 1, HG_WIDTH), const),
            pl.BlockSpec((1, HG_WIDTH), const),
            pl.BlockSpec((1, D_MODEL), const),
            pl.BlockSpec((1, D_MODEL), const),
            pl.BlockSpec((step_rows, LANES), lambda b, u, *_: (u, 0)),
            pl.BlockSpec((step_rows, LANES), lambda b, u, *_: (u, 0)),
        ],
        out_specs=pl.BlockSpec((1, step_rows, D_MODEL), lambda b, u, *_: (b, u, 0)),
        scratch_shapes=[
            pltpu.VMEM((TIME_TILE, IN_WIDTH), F32),
            pltpu.VMEM((TIME_TILE, IN_WIDTH), F32),
            pltpu.VMEM((2, TIME_TILE, D_MODEL), BF16),
            pltpu.VMEM((TILES_PER_STEP, TIME_TILE, D_MODEL), BF16),
            pltpu.VMEM((TIME_TILE, D_MODEL), F32),
            pltpu.VMEM((HG_HEADS, HG_HEAD_DIM, HG_HEAD_DIM), F32),
            pltpu.VMEM((2 * SW_KV_HEADS, STEP_BLOCKS * WINDOW, LANES), BF16),
            pltpu.VMEM((SW_KV_HEADS, VT_ROWS, STEP_BLOCKS * WINDOW), BF16),
            pltpu.VMEM((2 * SW_KV_HEADS, WINDOW, LANES), BF16),
            pltpu.VMEM((SW_KV_HEADS, VT_ROWS, WINDOW), BF16),
            pltpu.VMEM((D_MODEL, IN_WIDTH), BF16),
            pltpu.VMEM((D_MODEL, D_MODEL), BF16),
            pltpu.VMEM((WEIGHT_SLOTS, WEIGHT_CHUNK_ROWS, IN_WIDTH), F32),
            pltpu.VMEM((WEIGHT_SLOTS, WEIGHT_CHUNK_ROWS, D_MODEL), F32),
            pltpu.SemaphoreType.DMA((WEIGHT_SLOTS,)),
            pltpu.SemaphoreType.DMA((WEIGHT_SLOTS,)),
        ],
    )
    return pl.pallas_call(
        _layer_kernel,
        grid_spec=grid_spec,
        out_shape=jax.ShapeDtypeStruct(x.shape, x.dtype),
        compiler_params=pltpu.CompilerParams(
            dimension_semantics=("arbitrary", "arbitrary"),
            vmem_limit_bytes=VMEM_LIMIT_BYTES),
        name="hybrid_hgrn2_swa_layer",
    )(sinks[0], x, x, w_in, w_out, lb_logits, hg_norm_w, ln_g, ln_b, cos_tab, sin_tab)
```

```python
import jax
import jax.numpy as jnp
from jax import lax
from jax.experimental import pallas as pl
from jax.experimental.pallas import tpu as pltpu

D_MODEL = 1024
DEPTH = 1
HG_WIDTH = 512
HG_HEAD_DIM = 128
HG_HEADS = HG_WIDTH // HG_HEAD_DIM
HG_CHUNK = 64
SW_WIDTH = 512
SW_HEAD_DIM = 64
SW_Q_HEADS = SW_WIDTH // SW_HEAD_DIM
SW_KV_HEADS = SW_Q_HEADS // 4
SW_KV_WIDTH = SW_KV_HEADS * SW_HEAD_DIM
SW_GROUP = SW_Q_HEADS // SW_KV_HEADS
WINDOW = 128
ROPE_THETA = 500000.0
ROPE_DIM = SW_HEAD_DIM // 4
DN_ALPHA = (2.0 * DEPTH) ** 0.25
LN_EPS = 1e-5
RMS_EPS = 1e-6
IN_WIDTH = 4 * HG_WIDTH + SW_WIDTH + 2 * SW_KV_WIDTH + SW_WIDTH

OFF_HQ = 0
OFF_HF = OFF_HQ + HG_WIDTH
OFF_HI = OFF_HF + HG_WIDTH
OFF_HG = OFF_HI + HG_WIDTH
OFF_AQ = OFF_HG + HG_WIDTH
OFF_AK = OFF_AQ + SW_WIDTH
OFF_AV = OFF_AK + SW_KV_WIDTH
OFF_AG = OFF_AV + SW_KV_WIDTH

LANES = 128
BF16_SUBLANES = 16
MXU_WIDTH = 256
HEADS_PER_TILE = LANES // SW_HEAD_DIM
assert HEADS_PER_TILE == 2 and SW_GROUP == 2 * HEADS_PER_TILE and SW_KV_WIDTH == LANES
VT_ROWS = SW_HEAD_DIM + BF16_SUBLANES
TIME_TILE = 256
TILES_PER_STEP = 4
PIECE = MXU_WIDTH
N_IN_PIECES = IN_WIDTH // PIECE
N_OUT_PIECES = D_MODEL // PIECE
assert IN_WIDTH % PIECE == 0 and D_MODEL % PIECE == 0
N_CHUNKS = TIME_TILE // HG_CHUNK
N_BLOCKS = TIME_TILE // WINDOW
STEP_BLOCKS = TILES_PER_STEP * N_BLOCKS
WEIGHT_CHUNK_ROWS = 64
WEIGHT_SLOTS = 4
VMEM_LIMIT_BYTES = 58 * 1024 * 1024

F32 = jnp.float32
BF16 = jnp.bfloat16
NT_DIMS = (((1,), (1,)), ((), ()))
TN_DIMS = (((0,), (0,)), ((), ()))


LOG2_E = 1.4426950408889634


def _sigmoid(v):
    return 1.0 / (1.0 + jnp.exp2(v * (-LOG2_E)))


def _load_weight_as_bf16(w_hbm, stage_ref, sem_ref, dst_ref):
    n_rows = dst_ref.shape[0]
    n_slots, chunk = stage_ref.shape[0], stage_ref.shape[1]
    n_chunks = n_rows // chunk
    assert n_chunks * chunk == n_rows and n_slots >= 2

    def chunk_copy(i):
        return pltpu.make_async_copy(w_hbm.at[0, pl.ds(i * chunk, chunk), :],
                                     stage_ref.at[i % n_slots], sem_ref.at[i % n_slots])

    for i in range(min(n_slots - 1, n_chunks)):
        chunk_copy(i).start()
    for i in range(n_chunks):
        ahead = i + n_slots - 1
        if ahead < n_chunks:
            chunk_copy(ahead).start()
        chunk_copy(i).wait()
        dst_ref[i * chunk:(i + 1) * chunk, :] = stage_ref[i % n_slots].astype(BF16)


def _layer_kernel(sinks_ref, xc_ref, xn_ref, win_hbm, wout_hbm, lbl_ref, nw_ref, lng_ref, lnb_ref,
                  cos_ref, sin_ref, o_hbm,
                  ha_ref, hb_ref, xb_ref, mix_ref, out_ref, st_ref, kwin_ref, vt_ref,
                  kcarry_ref, vcarry_ref, win_ref, wout_ref, win_stage, wout_stage,
                  win_sems, wout_sems, ostage_ref, o_sems, xres_ref):
    b = pl.program_id(0)
    u = pl.program_id(1)
    n_steps = pl.num_programs(1)
    tt = TIME_TILE
    h_bufs = (ha_ref, hb_ref)
    last_tile = TILES_PER_STEP - 1

    step = b * n_steps + u
    bank = lax.rem(step, 2) * TILES_PER_STEP
    other_bank = TILES_PER_STEP - bank
    prev_step = jnp.maximum(step - 1, 0)
    step_row0 = u * (TILES_PER_STEP * tt)
    is_last_step = step == pl.num_programs(0) * n_steps - 1

    def out_copy(slot, dst_b, dst_row):
        return pltpu.make_async_copy(ostage_ref.at[slot], o_hbm.at[dst_b, pl.ds(dst_row, tt), :],
                                     o_sems.at[slot])

    @pl.when(step == 0)
    def _first_step():
        _load_weight_as_bf16(win_hbm, win_stage, win_sems, win_ref)
        _load_weight_as_bf16(wout_hbm, wout_stage, wout_sems, wout_ref)
        ha_ref[...] = jnp.dot(xc_ref[0, 0:tt, :].astype(BF16), win_ref[...],
                              preferred_element_type=F32)
        mix_ref[last_tile] = jnp.zeros((tt, D_MODEL), BF16)
        xres_ref[...] = jnp.zeros_like(xres_ref)
        for e in range(TILES_PER_STEP):
            ostage_ref[TILES_PER_STEP + e] = jnp.zeros((tt, D_MODEL), F32)
            out_copy(TILES_PER_STEP + e, 0, e * tt).start()

    @pl.when(u == 0)
    def _reset_carries():
        st_ref[...] = jnp.zeros_like(st_ref)
        kcarry_ref[...] = jnp.zeros_like(kcarry_ref)
        vcarry_ref[:, 0:SW_HEAD_DIM, :] = jnp.zeros((SW_KV_HEADS, SW_HEAD_DIM, WINDOW), BF16)
        vcarry_ref[:, SW_HEAD_DIM:VT_ROWS, :] = jnp.ones((SW_KV_HEADS, BF16_SUBLANES, WINDOW), BF16)
        vt_ref[:, SW_HEAD_DIM:VT_ROWS, :] = jnp.ones(
            (SW_KV_HEADS, BF16_SUBLANES, STEP_BLOCKS * WINDOW), BF16)

    lbl = lbl_ref[...]
    lbl_e = jnp.exp(lbl - jnp.max(lbl, axis=0, keepdims=True))
    lb = lbl_e[0:1] / jnp.sum(lbl_e, axis=0, keepdims=True)
    nw = nw_ref[...]
    row = lax.broadcasted_iota(jnp.int32, (HG_CHUNK, HG_CHUNK), 0)
    col = lax.broadcasted_iota(jnp.int32, (HG_CHUNK, HG_CHUNK), 1)
    tril = col <= row
    cum_mat = tril.astype(BF16)
    cum3 = jnp.concatenate([cum_mat, cum_mat, cum_mat], axis=1)

    lane = lax.broadcasted_iota(jnp.int32, (1, LANES), 1)
    rope_first = (lane % SW_HEAD_DIM) < (ROPE_DIM // 2)
    lane_lo = lane < SW_HEAD_DIM
    key_idx = lax.broadcasted_iota(jnp.int32, (2 * WINDOW, 2 * WINDOW), 0)
    qry_idx = lax.broadcasted_iota(jnp.int32, (2 * WINDOW, 2 * WINDOW), 1) % WINDOW
    band = (key_idx > qry_idx) & (key_idx <= qry_idx + WINDOW)
    pair_lo = lax.broadcasted_iota(jnp.int32, (1, 2 * WINDOW), 1) < WINDOW
    scale = SW_HEAD_DIM ** -0.5 * LOG2_E
    first_valid = jnp.where(u > 0, 0, WINDOW)
    first_band = band & (key_idx >= first_valid)

    def rope(v, cos, sin):
        partner = jnp.where(rope_first,
                            pltpu.roll(v, LANES - ROPE_DIM // 2, 1),
                            pltpu.roll(v, ROPE_DIM // 2, 1))
        return v * cos + partner * sin

    def head_variants(v):
        sw = pltpu.roll(v, SW_HEAD_DIM, 1)
        zero = jnp.zeros_like(v)
        return (jnp.where(lane_lo, v, zero), jnp.where(lane_lo, zero, sw),
                jnp.where(lane_lo, sw, zero), jnp.where(lane_lo, zero, v))

    def in_projection_piece(k, piece):
        xb = xb_ref.at[k % 2]
        cols = slice(piece * PIECE, (piece + 1) * PIECE)
        h_bufs[k % 2][:, cols] = jnp.dot(xb[...], win_ref[:, cols], preferred_element_type=F32)

    def out_projection_piece(k, piece):
        cols = slice(piece * PIECE, (piece + 1) * PIECE)
        out_ref[:, cols] = jnp.dot(mix_ref[k], wout_ref[:, cols], preferred_element_type=F32)

    def tile_input(k):
        return xc_ref[0, k * tt:(k + 1) * tt, :]

    def post_norm(x_rows, slot):
        z = DN_ALPHA * x_rows + out_ref[...]
        mu = jnp.mean(z, axis=-1, keepdims=True)
        zc = z - mu
        var = jnp.mean(zc * zc, axis=-1, keepdims=True)
        ostage_ref[slot] = (zc * lax.rsqrt(var + LN_EPS)) * lng_ref[...] + lnb_ref[...]

    def make_fillers(k):
        stages = []
        src = k - 1 if k > 0 else last_tile
        for piece in range(N_OUT_PIECES):
            stages.append(lambda piece=piece: out_projection_piece(src, piece))
        if k > 0:
            stages.append(lambda: post_norm(tile_input(k - 1), bank + k))
        else:
            stages.append(lambda: post_norm(xres_ref[...], bank))
        for piece in range(N_IN_PIECES):
            stages.append(lambda piece=piece: in_projection_piece(k + 1, piece))
        return iter(stages)

    def process_tile(k):
        h_ref = h_bufs[k % 2]
        base = k * tt
        mix = mix_ref.at[k]
        if k + 1 < TILES_PER_STEP:
            x_next_rows = xc_ref[0, (k + 1) * tt:(k + 2) * tt, :]
        else:
            x_next_rows = xn_ref[0]
        xb_ref[(k + 1) % 2] = x_next_rows.astype(BF16)
        fillers = make_fillers(k)

        def fill(n=1):
            for _ in range(n):
                stage = next(fillers, None)
                if stage is not None:
                    stage()

        def hg_pre(c):
            rows = slice(c * HG_CHUNK, (c + 1) * HG_CHUNK)
            f = lb + (1.0 - lb) * _sigmoid(h_ref[rows, OFF_HF:OFF_HF + HG_WIDTH])
            log_f = jnp.log2(f)
            hi = log_f.astype(BF16)
            rem = log_f - hi.astype(F32)
            mid = rem.astype(BF16)
            lo = (rem - mid.astype(F32)).astype(BF16)
            return 1.0 - f, jnp.concatenate([hi, mid, lo], axis=0)

        def hg_cum(parts):
            return jnp.dot(cum3, parts, preferred_element_type=F32)

        def hg_decays(c, k_in, g_cum):
            rows = slice(c * HG_CHUNK, (c + 1) * HG_CHUNK)
            hq = h_ref[rows, OFF_HQ:OFF_HQ + HG_WIDTH]
            g_last = g_cum[HG_CHUNK - 1:HG_CHUNK, :]
            q_dec = ((hq * _sigmoid(hq)) * jnp.exp2(g_cum)).astype(BF16)
            k_dec = (k_in * jnp.exp2(-g_cum)).astype(BF16)
            k_tail = (k_in * jnp.exp2(g_last - g_cum)).astype(BF16)
            return q_dec, k_dec, k_tail, jnp.exp2(g_last)

        def hg_scores(q_dec, k_dec, hd):
            sl = slice(hd * HG_HEAD_DIM, (hd + 1) * HG_HEAD_DIM)
            return lax.dot_general(q_dec[:, sl], k_dec[:, sl], NT_DIMS, preferred_element_type=F32)

        def hg_output(c, hd, a, q_dec, k_tail, decay):
            rows = slice(c * HG_CHUNK, (c + 1) * HG_CHUNK)
            sl = slice(hd * HG_HEAD_DIM, (hd + 1) * HG_HEAD_DIM)
            v = h_ref[rows, OFF_HI + hd * HG_HEAD_DIM:OFF_HI + (hd + 1) * HG_HEAD_DIM].astype(BF16)
            s_t = st_ref[hd]
            o = jnp.dot(jnp.where(tril, a, 0.0).astype(BF16), v, preferred_element_type=F32)
            o = o + lax.dot_general(q_dec[:, sl], s_t.astype(BF16), NT_DIMS,
                                    preferred_element_type=F32)
            inc_t = lax.dot_general(v, k_tail[:, sl], TN_DIMS, preferred_element_type=F32)
            st_ref[hd] = s_t * decay[:, sl] + inc_t
            o = o * lax.rsqrt(jnp.mean(o * o, axis=-1, keepdims=True) + RMS_EPS)
            o = o * nw[:, sl]
            hg = h_ref[rows, OFF_HG + hd * HG_HEAD_DIM:OFF_HG + (hd + 1) * HG_HEAD_DIM]
            mix[rows, sl] = (o * (hg * _sigmoid(hg))).astype(BF16)

        def sw_prepare(blk):
            r0 = blk * WINDOW
            rows = slice(r0, r0 + WINDOW)
            cur = slice(base + r0, base + r0 + WINDOW)
            cos = cos_ref[base + r0:base + r0 + WINDOW, :]
            sin = sin_ref[base + r0:base + r0 + WINDOW, :]
            k_rot = rope(h_ref[rows, OFF_AK:OFF_AK + SW_KV_WIDTH], cos, sin)
            for idx, kv in enumerate(head_variants(k_rot)):
                kwin_ref[idx, cur, :] = kv.astype(BF16)
            v_t = h_ref[rows, OFF_AV:OFF_AV + SW_KV_WIDTH].T
            for g in range(SW_KV_HEADS):
                vt_ref[g, 0:SW_HEAD_DIM, cur] = (
                    v_t[g * SW_HEAD_DIM:(g + 1) * SW_HEAD_DIM].astype(BF16))
            q_bf = []
            for tile in range(SW_WIDTH // LANES):
                q_rot = rope(h_ref[rows, OFF_AQ + tile * LANES:OFF_AQ + (tile + 1) * LANES], cos, sin)
                q_bf.append((q_rot * scale).astype(BF16))
            return [jnp.concatenate([q_bf[2 * g], q_bf[2 * g + 1]], axis=0)
                    for g in range(SW_KV_HEADS)]

        def key_window(blk, idx):
            r0 = base + blk * WINDOW
            if r0 == 0:
                return jnp.concatenate([kcarry_ref[idx], kwin_ref[idx, 0:WINDOW, :]], axis=0)
            return kwin_ref[idx, r0 - WINDOW:r0 + WINDOW, :]

        def value_window(blk, g):
            r0 = base + blk * WINDOW
            if r0 == 0:
                return jnp.concatenate([vcarry_ref[g], vt_ref[g, :, 0:WINDOW]], axis=1)
            return vt_ref[g, :, r0 - WINDOW:r0 + WINDOW]

        def sw_scores(blk, g, j, q_pair):
            return lax.dot_general(key_window(blk, 2 * g + j), q_pair, NT_DIMS,
                                   preferred_element_type=F32)

        def sw_values(blk, g, j, s_t, mask):
            sink_row = jnp.where(pair_lo, sinks_ref[SW_GROUP * g + j],
                                 sinks_ref[SW_GROUP * g + HEADS_PER_TILE + j]) * LOG2_E
            s_t = jnp.where(mask, s_t, -jnp.inf)
            m = jnp.maximum(jnp.max(s_t, axis=0, keepdims=True), sink_row)
            p_t = jnp.exp2(s_t - m).astype(BF16)
            acc = jnp.dot(value_window(blk, g), p_t, preferred_element_type=F32)
            denom = acc[SW_HEAD_DIM:SW_HEAD_DIM + 1, :] + jnp.exp2(sink_row - m)
            return acc[0:SW_HEAD_DIM, :] * (1.0 / denom)

        def sw_finish(blk, g, o_t):
            rows = slice(blk * WINDOW, (blk + 1) * WINDOW)
            for tl in range(2):
                tile = 2 * g + tl
                lanes = slice(tl * LANES, (tl + 1) * LANES)
                o_a = jnp.concatenate([o_t[0][:, lanes], o_t[1][:, lanes]], axis=0).T
                ag = h_ref[rows, OFF_AG + tile * LANES:OFF_AG + (tile + 1) * LANES]
                mix[rows, HG_WIDTH + tile * LANES:HG_WIDTH + (tile + 1) * LANES] = (
                    o_a * (ag * _sigmoid(ag))).astype(BF16)

        pre = {}
        cum = {}
        fill(2)
        for c in range(min(2, N_CHUNKS)):
            pre[c] = hg_pre(c)
            cum[c] = hg_cum(pre[c][1])
            fill()
        for c in range(N_CHUNKS):
            q_dec, k_dec, k_tail, decay = hg_decays(c, pre[c][0], cum[c])
            hg_s = [hg_scores(q_dec, k_dec, hd) for hd in range(HG_HEADS)]
            if c + 2 < N_CHUNKS:
                pre[c + 2] = hg_pre(c + 2)
                cum[c + 2] = hg_cum(pre[c + 2][1])
            fill()
            for hd in range(HG_HEADS):
                hg_output(c, hd, hg_s[hd], q_dec, k_tail, decay)
            fill()

        q_pairs = [sw_prepare(blk) for blk in range(N_BLOCKS)]
        masks = [first_band if (k == 0 and blk == 0) else band for blk in range(N_BLOCKS)]
        chains = [(blk, g, j) for blk in range(N_BLOCKS) for g in range(SW_KV_HEADS)
                  for j in range(HEADS_PER_TILE)]
        scores = {}
        for i in range(min(2, len(chains))):
            blk, g, j = chains[i]
            scores[i] = sw_scores(blk, g, j, q_pairs[blk][g])
        fill()
        outs = {}
        for i, (blk, g, j) in enumerate(chains):
            outs[(blk, g, j)] = sw_values(blk, g, j, scores.pop(i), masks[blk])
            if i + 2 < len(chains):
                nb, ng, nj = chains[i + 2]
                scores[i + 2] = sw_scores(nb, ng, nj, q_pairs[nb][ng])
            if j == HEADS_PER_TILE - 1:
                sw_finish(blk, g, [outs.pop((blk, g, jj)) for jj in range(HEADS_PER_TILE)])
            fill()
        fill(N_OUT_PIECES + 1 + N_IN_PIECES)

    for k in range(TILES_PER_STEP):
        process_tile(k)

    xres_ref[...] = tile_input(last_tile)
    for e in range(TILES_PER_STEP):
        out_copy(other_bank + e, 0, 0).wait()

    @pl.when(is_last_step)
    def _final_tile():
        for piece in range(N_OUT_PIECES):
            out_projection_piece(last_tile, piece)
        post_norm(tile_input(last_tile), other_bank)
        final = out_copy(other_bank, b, step_row0 + last_tile * tt)
        final.start()
        final.wait()

    out_copy(bank, prev_step // n_steps,
             lax.rem(prev_step, n_steps) * (TILES_PER_STEP * tt) + last_tile * tt).start()
    for e in range(1, TILES_PER_STEP):
        out_copy(bank + e, b, step_row0 + (e - 1) * tt).start()

    @pl.when(is_last_step)
    def _drain():
        for e in range(TILES_PER_STEP):
            out_copy(bank + e, 0, 0).wait()

    last = slice((STEP_BLOCKS - 1) * WINDOW, STEP_BLOCKS * WINDOW)
    for idx in range(2 * SW_KV_HEADS):
        kcarry_ref[idx] = kwin_ref[idx, last, :]
    for g in range(SW_KV_HEADS):
        vcarry_ref[g, 0:SW_HEAD_DIM, :] = vt_ref[g, 0:SW_HEAD_DIM, last]


def _rope_tables(seq_len):
    pos = jnp.arange(seq_len, dtype=F32)
    inv_freq = ROPE_THETA ** (-jnp.arange(0, ROPE_DIM, 2, dtype=F32) / ROPE_DIM)
    ang = pos[:, None] * inv_freq[None, :]
    cos = jnp.cos(ang)
    sin = jnp.sin(ang)
    ones = jnp.ones((seq_len, SW_HEAD_DIM - ROPE_DIM), F32)
    cos_head = jnp.concatenate([cos, cos, ones], axis=1)
    sin_head = jnp.concatenate([-sin, sin, jnp.zeros_like(ones)], axis=1)
    return (jnp.tile(cos_head, (1, HEADS_PER_TILE)), jnp.tile(sin_head, (1, HEADS_PER_TILE)))


def kernel(x, w_in, lb_logits, hg_norm_w, sinks, w_out, ln_g, ln_b):
    batch, seq_len, d_model = x.shape
    assert d_model == D_MODEL and w_in.shape == (DEPTH, D_MODEL, IN_WIDTH)
    step_rows = TILES_PER_STEP * TIME_TILE
    assert seq_len % step_rows == 0
    cos_tab, sin_tab = _rope_tables(seq_len)
    n_steps = seq_len // step_rows
    last_step = batch * n_steps - 1

    def next_tile_index(b, u, *_):
        nxt = jnp.minimum(b * n_steps + u + 1, last_step)
        return (nxt // n_steps, TILES_PER_STEP * (nxt % n_steps), 0)

    const = lambda b, u, *_: (0, 0)
    grid_spec = pltpu.PrefetchScalarGridSpec(
        num_scalar_prefetch=1,
        grid=(batch, n_steps),
        in_specs=[
            pl.BlockSpec((1, step_rows, D_MODEL), lambda b, u, *_: (b, u, 0)),
            pl.BlockSpec((1, TIME_TILE, D_MODEL), next_tile_index),
            pl.BlockSpec(memory_space=pl.ANY),
            pl.BlockSpec(memory_space=pl.ANY),
            pl.BlockSpec((DEPTH + 1, HG_WIDTH), const),
            pl.BlockSpec((1, HG_WIDTH), const),
            pl.BlockSpec((1, D_MODEL), const),
            pl.BlockSpec((1, D_MODEL), const),
            pl.BlockSpec((step_rows, LANES), lambda b, u, *_: (u, 0)),
            pl.BlockSpec((step_rows, LANES), lambda b, u, *_: (u, 0)),
        ],
        out_specs=pl.BlockSpec(memory_space=pl.ANY),
        scratch_shapes=[
            pltpu.VMEM((TIME_TILE, IN_WIDTH), F32),
            pltpu.VMEM((TIME_TILE, IN_WIDTH), F32),
            pltpu.VMEM((2, TIME_TILE, D_MODEL), BF16),
            pltpu.VMEM((TILES_PER_STEP, TIME_TILE, D_MODEL), BF16),
            pltpu.VMEM((TIME_TILE, D_MODEL), F32),
            pltpu.VMEM((HG_HEADS, HG_HEAD_DIM, HG_HEAD_DIM), F32),
            pltpu.VMEM((2 * SW_KV_HEADS, STEP_BLOCKS * WINDOW, LANES), BF16),
            pltpu.VMEM((SW_KV_HEADS, VT_ROWS, STEP_BLOCKS * WINDOW), BF16),
            pltpu.VMEM((2 * SW_KV_HEADS, WINDOW, LANES), BF16),
            pltpu.VMEM((SW_KV_HEADS, VT_ROWS, WINDOW), BF16),
            pltpu.VMEM((D_MODEL, IN_WIDTH), BF16),
            pltpu.VMEM((D_MODEL, D_MODEL), BF16),
            pltpu.VMEM((WEIGHT_SLOTS, WEIGHT_CHUNK_ROWS, IN_WIDTH), F32),
            pltpu.VMEM((WEIGHT_SLOTS, WEIGHT_CHUNK_ROWS, D_MODEL), F32),
            pltpu.SemaphoreType.DMA((WEIGHT_SLOTS,)),
            pltpu.SemaphoreType.DMA((WEIGHT_SLOTS,)),
            pltpu.VMEM((2 * TILES_PER_STEP, TIME_TILE, D_MODEL), F32),
            pltpu.SemaphoreType.DMA((2 * TILES_PER_STEP,)),
            pltpu.VMEM((TIME_TILE, D_MODEL), F32),
        ],
    )
    return pl.pallas_call(
        _layer_kernel,
        grid_spec=grid_spec,
        out_shape=jax.ShapeDtypeStruct(x.shape, x.dtype),
        compiler_params=pltpu.CompilerParams(
            dimension_semantics=("arbitrary", "arbitrary"),
            vmem_limit_bytes=VMEM_LIMIT_BYTES),
        name="hybrid_hgrn2_swa_layer",
    )(sinks[0], x, x, w_in, w_out, lb_logits, hg_norm_w, ln_g, ln_b, cos_tab, sin_tab)
```

```python
import jax
import jax.numpy as jnp
from jax import lax
from jax.experimental import pallas as pl
from jax.experimental.pallas import tpu as pltpu

D_MODEL = 1024
DEPTH = 1
HG_WIDTH = 512
HG_HEAD_DIM = 128
HG_HEADS = HG_WIDTH // HG_HEAD_DIM
HG_CHUNK = 64
SW_WIDTH = 512
SW_HEAD_DIM = 64
SW_Q_HEADS = SW_WIDTH // SW_HEAD_DIM
SW_KV_HEADS = SW_Q_HEADS // 4
SW_KV_WIDTH = SW_KV_HEADS * SW_HEAD_DIM
SW_GROUP = SW_Q_HEADS // SW_KV_HEADS
WINDOW = 128
ROPE_THETA = 500000.0
ROPE_DIM = SW_HEAD_DIM // 4
DN_ALPHA = (2.0 * DEPTH) ** 0.25
LN_EPS = 1e-5
RMS_EPS = 1e-6
IN_WIDTH = 4 * HG_WIDTH + SW_WIDTH + 2 * SW_KV_WIDTH + SW_WIDTH

OFF_HQ = 0
OFF_HF = OFF_HQ + HG_WIDTH
OFF_HI = OFF_HF + HG_WIDTH
OFF_HG = OFF_HI + HG_WIDTH
OFF_AQ = OFF_HG + HG_WIDTH
OFF_AK = OFF_AQ + SW_WIDTH
OFF_AV = OFF_AK + SW_KV_WIDTH
OFF_AG = OFF_AV + SW_KV_WIDTH

LANES = 128
BF16_SUBLANES = 16
MXU_WIDTH = 256
HEADS_PER_TILE = LANES // SW_HEAD_DIM
assert HEADS_PER_TILE == 2 and SW_GROUP == 2 * HEADS_PER_TILE and SW_KV_WIDTH == LANES
VT_ROWS = SW_HEAD_DIM + BF16_SUBLANES
TIME_TILE = 256
TILES_PER_STEP = 4
PIECE = MXU_WIDTH
N_IN_PIECES = IN_WIDTH // PIECE
N_OUT_PIECES = D_MODEL // PIECE
assert IN_WIDTH % PIECE == 0 and D_MODEL % PIECE == 0
N_CHUNKS = TIME_TILE // HG_CHUNK
N_BLOCKS = TIME_TILE // WINDOW
STEP_BLOCKS = TILES_PER_STEP * N_BLOCKS
WEIGHT_CHUNK_ROWS = 64
WEIGHT_SLOTS = 4
VMEM_LIMIT_BYTES = 58 * 1024 * 1024

F32 = jnp.float32
BF16 = jnp.bfloat16
NT_DIMS = (((1,), (1,)), ((), ()))
TN_DIMS = (((0,), (0,)), ((), ()))


LOG2_E = 1.4426950408889634


def _sigmoid(v):
    return 1.0 / (1.0 + jnp.exp2(v * (-LOG2_E)))


def _load_weight_as_bf16(w_hbm, stage_ref, sem_ref, dst_ref, scale=None):
    n_rows = dst_ref.shape[0]
    n_slots, chunk = stage_ref.shape[0], stage_ref.shape[1]
    n_chunks = n_rows // chunk
    assert n_chunks * chunk == n_rows and n_slots >= 2

    def chunk_copy(i):
        return pltpu.make_async_copy(w_hbm.at[0, pl.ds(i * chunk, chunk), :],
                                     stage_ref.at[i % n_slots], sem_ref.at[i % n_slots])

    for i in range(min(n_slots - 1, n_chunks)):
        chunk_copy(i).start()
    for i in range(n_chunks):
        ahead = i + n_slots - 1
        if ahead < n_chunks:
            chunk_copy(ahead).start()
        chunk_copy(i).wait()
        rows = stage_ref[i % n_slots]
        if scale is not None:
            rows = rows * scale
        dst_ref[i * chunk:(i + 1) * chunk, :] = rows.astype(BF16)


def _layer_kernel(sinks_ref, xc_ref, xn_ref, win_hbm, wout_hbm, lbl_ref, nw_ref, lng_ref, lnb_ref,
                  cos_ref, sin_ref, o_ref,
                  ha_ref, hb_ref, xb_ref, mix_ref, out_ref, st_ref, kwin_ref, vt_ref,
                  kcarry_ref, vcarry_ref, win_ref, wout_ref, win_stage, wout_stage,
                  win_sems, wout_sems):
    b = pl.program_id(0)
    u = pl.program_id(1)
    tt = TIME_TILE
    h_bufs = (ha_ref, hb_ref)

    @pl.when(jnp.logical_and(b == 0, u == 0))
    def _first_step():
        _load_weight_as_bf16(win_hbm, win_stage, win_sems, win_ref)
        _load_weight_as_bf16(wout_hbm, wout_stage, wout_sems, wout_ref, scale=1.0 / DN_ALPHA)
        ha_ref[...] = jnp.dot(xc_ref[0, 0:tt, :].astype(BF16), win_ref[...],
                              preferred_element_type=F32)

    @pl.when(u == 0)
    def _reset_carries():
        st_ref[...] = jnp.zeros_like(st_ref)
        kcarry_ref[...] = jnp.zeros_like(kcarry_ref)
        vcarry_ref[:, 0:SW_HEAD_DIM, :] = jnp.zeros((SW_KV_HEADS, SW_HEAD_DIM, WINDOW), BF16)
        vcarry_ref[:, SW_HEAD_DIM:VT_ROWS, :] = jnp.ones((SW_KV_HEADS, BF16_SUBLANES, WINDOW), BF16)
        vt_ref[:, SW_HEAD_DIM:VT_ROWS, :] = jnp.ones(
            (SW_KV_HEADS, BF16_SUBLANES, STEP_BLOCKS * WINDOW), BF16)

    lbl = lbl_ref[...]
    lbl_e = jnp.exp(lbl - jnp.max(lbl, axis=0, keepdims=True))
    lb = lbl_e[0:1] / jnp.sum(lbl_e, axis=0, keepdims=True)
    nw = nw_ref[...] * (HG_HEAD_DIM ** 0.5)
    row = lax.broadcasted_iota(jnp.int32, (HG_CHUNK, HG_CHUNK), 0)
    col = lax.broadcasted_iota(jnp.int32, (HG_CHUNK, HG_CHUNK), 1)
    tril = col <= row
    cum_mat = tril.astype(BF16)
    cum3 = jnp.concatenate([cum_mat, cum_mat, cum_mat], axis=1)

    lane = lax.broadcasted_iota(jnp.int32, (1, LANES), 1)
    rope_first = (lane % SW_HEAD_DIM) < (ROPE_DIM // 2)
    lane_lo = lane < SW_HEAD_DIM
    key_idx = lax.broadcasted_iota(jnp.int32, (2 * WINDOW, 2 * WINDOW), 0)
    qry_idx = lax.broadcasted_iota(jnp.int32, (2 * WINDOW, 2 * WINDOW), 1) % WINDOW
    band = (key_idx > qry_idx) & (key_idx <= qry_idx + WINDOW)
    pair_lo = lax.broadcasted_iota(jnp.int32, (1, 2 * WINDOW), 1) < WINDOW
    scale = SW_HEAD_DIM ** -0.5 * LOG2_E
    first_valid = jnp.where(u > 0, 0, WINDOW)
    band_cap = jnp.where(band, jnp.inf, -jnp.inf)
    first_cap = jnp.where(band & (key_idx >= first_valid), jnp.inf, -jnp.inf)

    def rope(v, cos, sin):
        partner = jnp.where(rope_first,
                            pltpu.roll(v, LANES - ROPE_DIM // 2, 1),
                            pltpu.roll(v, ROPE_DIM // 2, 1))
        return v * cos + partner * sin

    def head_variants(v):
        sw = pltpu.roll(v, SW_HEAD_DIM, 1)
        zero = jnp.zeros_like(v)
        return (jnp.where(lane_lo, v, zero), jnp.where(lane_lo, zero, sw),
                jnp.where(lane_lo, sw, zero), jnp.where(lane_lo, zero, v))

    def in_projection_piece(k, piece):
        xb = xb_ref.at[k % 2]
        cols = slice(piece * PIECE, (piece + 1) * PIECE)
        h_bufs[k % 2][:, cols] = jnp.dot(xb[...], win_ref[:, cols], preferred_element_type=F32)

    def out_projection_piece(k, piece):
        cols = slice(piece * PIECE, (piece + 1) * PIECE)
        out_ref[:, cols] = jnp.dot(mix_ref[k], wout_ref[:, cols], preferred_element_type=F32)

    def post_norm(k):
        rows = slice(k * tt, (k + 1) * tt)
        z = xc_ref[0, rows, :] + out_ref[...]
        mu = jnp.mean(z, axis=-1, keepdims=True)
        zc = z - mu
        var = jnp.mean(zc * zc, axis=-1, keepdims=True)
        o_ref[0, rows, :] = ((zc * lax.rsqrt(var + LN_EPS / DN_ALPHA ** 2)) * lng_ref[...]
                             + lnb_ref[...])

    def make_fillers(k):
        stages = []
        if k > 0:
            for piece in range(N_OUT_PIECES):
                stages.append(lambda piece=piece: out_projection_piece(k - 1, piece))
            stages.append(lambda: post_norm(k - 1))
        for piece in range(N_IN_PIECES):
            stages.append(lambda piece=piece: in_projection_piece(k + 1, piece))
        return iter(stages)

    def process_tile(k):
        h_ref = h_bufs[k % 2]
        base = k * tt
        mix = mix_ref.at[k]
        if k + 1 < TILES_PER_STEP:
            x_next_rows = xc_ref[0, (k + 1) * tt:(k + 2) * tt, :]
        else:
            x_next_rows = xn_ref[0]
        xb_ref[(k + 1) % 2] = x_next_rows.astype(BF16)
        fillers = make_fillers(k)

        def fill(n=1):
            for _ in range(n):
                stage = next(fillers, None)
                if stage is not None:
                    stage()

        def hg_pre(c):
            rows = slice(c * HG_CHUNK, (c + 1) * HG_CHUNK)
            f = lb + (1.0 - lb) * _sigmoid(h_ref[rows, OFF_HF:OFF_HF + HG_WIDTH])
            log_f = jnp.log2(f)
            hi = log_f.astype(BF16)
            rem = log_f - hi.astype(F32)
            mid = rem.astype(BF16)
            lo = (rem - mid.astype(F32)).astype(BF16)
            return 1.0 - f, jnp.concatenate([hi, mid, lo], axis=0)

        def hg_cum(parts):
            return jnp.dot(cum3, parts, preferred_element_type=F32)

        def hg_decays(c, k_in, g_cum):
            rows = slice(c * HG_CHUNK, (c + 1) * HG_CHUNK)
            hq = h_ref[rows, OFF_HQ:OFF_HQ + HG_WIDTH]
            g_last = g_cum[HG_CHUNK - 1:HG_CHUNK, :]
            q_dec = ((hq * _sigmoid(hq)) * jnp.exp2(g_cum)).astype(BF16)
            k_dec = (k_in * jnp.exp2(-g_cum)).astype(BF16)
            k_tail = (k_in * jnp.exp2(g_last - g_cum)).astype(BF16)
            return q_dec, k_dec, k_tail, jnp.exp2(g_last)

        def hg_scores(q_dec, k_dec, hd):
            sl = slice(hd * HG_HEAD_DIM, (hd + 1) * HG_HEAD_DIM)
            return lax.dot_general(q_dec[:, sl], k_dec[:, sl], NT_DIMS, preferred_element_type=F32)

        def hg_output(c, hd, a, q_dec, k_tail, decay):
            rows = slice(c * HG_CHUNK, (c + 1) * HG_CHUNK)
            sl = slice(hd * HG_HEAD_DIM, (hd + 1) * HG_HEAD_DIM)
            v = h_ref[rows, OFF_HI + hd * HG_HEAD_DIM:OFF_HI + (hd + 1) * HG_HEAD_DIM].astype(BF16)
            s_t = st_ref[hd]
            o = jnp.dot(jnp.where(tril, a, 0.0).astype(BF16), v, preferred_element_type=F32)
            o = o + lax.dot_general(q_dec[:, sl], s_t.astype(BF16), NT_DIMS,
                                    preferred_element_type=F32)
            inc_t = lax.dot_general(v, k_tail[:, sl], TN_DIMS, preferred_element_type=F32)
            st_ref[hd] = s_t * decay[:, sl] + inc_t
            o = o * lax.rsqrt(jnp.sum(o * o, axis=-1, keepdims=True) + HG_HEAD_DIM * RMS_EPS)
            o = o * nw[:, sl]
            hg = h_ref[rows, OFF_HG + hd * HG_HEAD_DIM:OFF_HG + (hd + 1) * HG_HEAD_DIM]
            mix[rows, sl] = (o * (hg * _sigmoid(hg))).astype(BF16)

        def sw_prepare(blk):
            r0 = blk * WINDOW
            rows = slice(r0, r0 + WINDOW)
            cur = slice(base + r0, base + r0 + WINDOW)
            cos = cos_ref[base + r0:base + r0 + WINDOW, :]
            sin = sin_ref[base + r0:base + r0 + WINDOW, :]
            k_rot = rope(h_ref[rows, OFF_AK:OFF_AK + SW_KV_WIDTH], cos, sin)
            for idx, kv in enumerate(head_variants(k_rot)):
                kwin_ref[idx, cur, :] = kv.astype(BF16)
            v_t = h_ref[rows, OFF_AV:OFF_AV + SW_KV_WIDTH].T
            for g in range(SW_KV_HEADS):
                vt_ref[g, 0:SW_HEAD_DIM, cur] = (
                    v_t[g * SW_HEAD_DIM:(g + 1) * SW_HEAD_DIM].astype(BF16))
            q_bf = []
            for tile in range(SW_WIDTH // LANES):
                q_rot = rope(h_ref[rows, OFF_AQ + tile * LANES:OFF_AQ + (tile + 1) * LANES], cos, sin)
                q_bf.append((q_rot * scale).astype(BF16))
            return [jnp.concatenate([q_bf[2 * g], q_bf[2 * g + 1]], axis=0)
                    for g in range(SW_KV_HEADS)]

        def key_window(blk, idx):
            r0 = base + blk * WINDOW
            if r0 == 0:
                return jnp.concatenate([kcarry_ref[idx], kwin_ref[idx, 0:WINDOW, :]], axis=0)
            return kwin_ref[idx, r0 - WINDOW:r0 + WINDOW, :]

        def value_window(blk, g):
            r0 = base + blk * WINDOW
            if r0 == 0:
                return jnp.concatenate([vcarry_ref[g], vt_ref[g, :, 0:WINDOW]], axis=1)
            return vt_ref[g, :, r0 - WINDOW:r0 + WINDOW]

        def sw_scores(blk, g, j, q_pair):
            return lax.dot_general(key_window(blk, 2 * g + j), q_pair, NT_DIMS,
                                   preferred_element_type=F32)

        def sw_values(blk, g, j, s_t, cap):
            sink_row = jnp.where(pair_lo, sinks_ref[SW_GROUP * g + j],
                                 sinks_ref[SW_GROUP * g + HEADS_PER_TILE + j]) * LOG2_E
            s_t = jnp.minimum(s_t, cap)
            m = jnp.maximum(jnp.max(s_t, axis=0, keepdims=True), sink_row)
            p_t = jnp.exp2(s_t - m).astype(BF16)
            acc = jnp.dot(value_window(blk, g), p_t, preferred_element_type=F32)
            denom = acc[SW_HEAD_DIM:SW_HEAD_DIM + 1, :] + jnp.exp2(sink_row - m)
            return acc[0:SW_HEAD_DIM, :] * (1.0 / denom)

        def sw_finish(blk, g, o_t):
            rows = slice(blk * WINDOW, (blk + 1) * WINDOW)
            for tl in range(2):
                tile = 2 * g + tl
                lanes = slice(tl * LANES, (tl + 1) * LANES)
                o_a = jnp.concatenate([o_t[0][:, lanes], o_t[1][:, lanes]], axis=0).T
                ag = h_ref[rows, OFF_AG + tile * LANES:OFF_AG + (tile + 1) * LANES]
                mix[rows, HG_WIDTH + tile * LANES:HG_WIDTH + (tile + 1) * LANES] = (
                    o_a * (ag * _sigmoid(ag))).astype(BF16)

        many = k > 0
        pre = {}
        cum = {}
        fill(2 if many else 1)
        for c in range(min(2, N_CHUNKS)):
            pre[c] = hg_pre(c)
            cum[c] = hg_cum(pre[c][1])
            fill(1 if many else 0)
        for c in range(N_CHUNKS):
            q_dec, k_dec, k_tail, decay = hg_decays(c, pre[c][0], cum[c])
            hg_s = [hg_scores(q_dec, k_dec, hd) for hd in range(HG_HEADS)]
            if c + 2 < N_CHUNKS:
                pre[c + 2] = hg_pre(c + 2)
                cum[c + 2] = hg_cum(pre[c + 2][1])
            fill()
            for hd in range(HG_HEADS):
                hg_output(c, hd, hg_s[hd], q_dec, k_tail, decay)
            fill(1 if many else 0)

        q_pairs = [sw_prepare(blk) for blk in range(N_BLOCKS)]
        masks = [first_cap if (k == 0 and blk == 0) else band_cap for blk in range(N_BLOCKS)]
        chains = [(blk, g, j) for blk in range(N_BLOCKS) for g in range(SW_KV_HEADS)
                  for j in range(HEADS_PER_TILE)]
        scores = {}
        for i in range(min(2, len(chains))):
            blk, g, j = chains[i]
            scores[i] = sw_scores(blk, g, j, q_pairs[blk][g])
        fill()
        outs = {}
        for i, (blk, g, j) in enumerate(chains):
            outs[(blk, g, j)] = sw_values(blk, g, j, scores.pop(i), masks[blk])
            if i + 2 < len(chains):
                nb, ng, nj = chains[i + 2]
                scores[i + 2] = sw_scores(nb, ng, nj, q_pairs[nb][ng])
            if j == HEADS_PER_TILE - 1:
                sw_finish(blk, g, [outs.pop((blk, g, jj)) for jj in range(HEADS_PER_TILE)])
            fill()
        fill(N_OUT_PIECES + 1 + N_IN_PIECES)

    for k in range(TILES_PER_STEP):
        process_tile(k)
    for piece in range(N_OUT_PIECES):
        out_projection_piece(TILES_PER_STEP - 1, piece)
    post_norm(TILES_PER_STEP - 1)

    last = slice((STEP_BLOCKS - 1) * WINDOW, STEP_BLOCKS * WINDOW)
    for idx in range(2 * SW_KV_HEADS):
        kcarry_ref[idx] = kwin_ref[idx, last, :]
    for g in range(SW_KV_HEADS):
        vcarry_ref[g, 0:SW_HEAD_DIM, :] = vt_ref[g, 0:SW_HEAD_DIM, last]


def _rope_tables(seq_len):
    pos = jnp.arange(seq_len, dtype=F32)
    inv_freq = ROPE_THETA ** (-jnp.arange(0, ROPE_DIM, 2, dtype=F32) / ROPE_DIM)
    ang = pos[:, None] * inv_freq[None, :]
    cos = jnp.cos(ang)
    sin = jnp.sin(ang)
    ones = jnp.ones((seq_len, SW_HEAD_DIM - ROPE_DIM), F32)
    cos_head = jnp.concatenate([cos, cos, ones], axis=1)
    sin_head = jnp.concatenate([-sin, sin, jnp.zeros_like(ones)], axis=1)
    return (jnp.tile(cos_head, (1, HEADS_PER_TILE)), jnp.tile(sin_head, (1, HEADS_PER_TILE)))


def kernel(x, w_in, lb_logits, hg_norm_w, sinks, w_out, ln_g, ln_b):
    batch, seq_len, d_model = x.shape
    assert d_model == D_MODEL and w_in.shape == (DEPTH, D_MODEL, IN_WIDTH)
    step_rows = TILES_PER_STEP * TIME_TILE
    assert seq_len % step_rows == 0
    cos_tab, sin_tab = _rope_tables(seq_len)
    n_steps = seq_len // step_rows
    last_step = batch * n_steps - 1

    def next_tile_index(b, u, *_):
        nxt = jnp.minimum(b * n_steps + u + 1, last_step)
        return (nxt // n_steps, TILES_PER_STEP * (nxt % n_steps), 0)

    const = lambda b, u, *_: (0, 0)
    grid_spec = pltpu.PrefetchScalarGridSpec(
        num_scalar_prefetch=1,
        grid=(batch, n_steps),
        in_specs=[
            pl.BlockSpec((1, step_rows, D_MODEL), lambda b, u, *_: (b, u, 0)),
            pl.BlockSpec((1, TIME_TILE, D_MODEL), next_tile_index),
            pl.BlockSpec(memory_space=pl.ANY),
            pl.BlockSpec(memory_space=pl.ANY),
            pl.BlockSpec((DEPTH + 1, HG_WIDTH), const),
            pl.BlockSpec((1, HG_WIDTH), const),
            pl.BlockSpec((1, D_MODEL), const),
            pl.BlockSpec((1, D_MODEL), const),
            pl.BlockSpec((step_rows, LANES), lambda b, u, *_: (u, 0)),
            pl.BlockSpec((step_rows, LANES), lambda b, u, *_: (u, 0)),
        ],
        out_specs=pl.BlockSpec((1, step_rows, D_MODEL), lambda b, u, *_: (b, u, 0)),
        scratch_shapes=[
            pltpu.VMEM((TIME_TILE, IN_WIDTH), F32),
            pltpu.VMEM((TIME_TILE, IN_WIDTH), F32),
            pltpu.VMEM((2, TIME_TILE, D_MODEL), BF16),
            pltpu.VMEM((TILES_PER_STEP, TIME_TILE, D_MODEL), BF16),
            pltpu.VMEM((TIME_TILE, D_MODEL), F32),
            pltpu.VMEM((HG_HEADS, HG_HEAD_DIM, HG_HEAD_DIM), F32),
            pltpu.VMEM((2 * SW_KV_HEADS, STEP_BLOCKS * WINDOW, LANES), BF16),
            pltpu.VMEM((SW_KV_HEADS, VT_ROWS, STEP_BLOCKS * WINDOW), BF16),
            pltpu.VMEM((2 * SW_KV_HEADS, WINDOW, LANES), BF16),
            pltpu.VMEM((SW_KV_HEADS, VT_ROWS, WINDOW), BF16),
            pltpu.VMEM((D_MODEL, IN_WIDTH), BF16),
            pltpu.VMEM((D_MODEL, D_MODEL), BF16),
            pltpu.VMEM((WEIGHT_SLOTS, WEIGHT_CHUNK_ROWS, IN_WIDTH), F32),
            pltpu.VMEM((WEIGHT_SLOTS, WEIGHT_CHUNK_ROWS, D_MODEL), F32),
            pltpu.SemaphoreType.DMA((WEIGHT_SLOTS,)),
            pltpu.SemaphoreType.DMA((WEIGHT_SLOTS,)),
        ],
    )
    return pl.pallas_call(
        _layer_kernel,
        grid_spec=grid_spec,
        out_shape=jax.ShapeDtypeStruct(x.shape, x.dtype),
        compiler_params=pltpu.CompilerParams(
            dimension_semantics=("arbitrary", "arbitrary"),
            vmem_limit_bytes=VMEM_LIMIT_BYTES),
        name="hybrid_hgrn2_swa_layer",
    )(sinks[0], x, x, w_in, w_out, lb_logits, hg_norm_w, ln_g, ln_b, cos_tab, sin_tab)
```

```python
import jax
import jax.numpy as jnp
from jax import lax
from jax.experimental import pallas as pl
from jax.experimental.pallas import tpu as pltpu

D_MODEL = 1024
DEPTH = 1
HG_WIDTH = 512
HG_HEAD_DIM = 128
HG_HEADS = HG_WIDTH // HG_HEAD_DIM
HG_CHUNK = 64
SW_WIDTH = 512
SW_HEAD_DIM = 64
SW_Q_HEADS = SW_WIDTH // SW_HEAD_DIM
SW_KV_HEADS = SW_Q_HEADS // 4
SW_KV_WIDTH = SW_KV_HEADS * SW_HEAD_DIM
SW_GROUP = SW_Q_HEADS // SW_KV_HEADS
WINDOW = 128
ROPE_THETA = 500000.0
ROPE_DIM = SW_HEAD_DIM // 4
DN_ALPHA = (2.0 * DEPTH) ** 0.25
LN_EPS = 1e-5
RMS_EPS = 1e-6
IN_WIDTH = 4 * HG_WIDTH + SW_WIDTH + 2 * SW_KV_WIDTH + SW_WIDTH

OFF_HQ = 0
OFF_HF = OFF_HQ + HG_WIDTH
OFF_HI = OFF_HF + HG_WIDTH
OFF_HG = OFF_HI + HG_WIDTH
OFF_AQ = OFF_HG + HG_WIDTH
OFF_AK = OFF_AQ + SW_WIDTH
OFF_AV = OFF_AK + SW_KV_WIDTH
OFF_AG = OFF_AV + SW_KV_WIDTH

LANES = 128
BF16_SUBLANES = 16
MXU_WIDTH = 256
HEADS_PER_TILE = LANES // SW_HEAD_DIM
assert HEADS_PER_TILE == 2 and SW_GROUP == 2 * HEADS_PER_TILE and SW_KV_WIDTH == LANES
VT_ROWS = SW_HEAD_DIM + BF16_SUBLANES
TIME_TILE = 256
TILES_PER_STEP = 4
PIECE = MXU_WIDTH
N_IN_PIECES = IN_WIDTH // PIECE
N_OUT_PIECES = D_MODEL // PIECE
assert IN_WIDTH % PIECE == 0 and D_MODEL % PIECE == 0
N_CHUNKS = TIME_TILE // HG_CHUNK
N_BLOCKS = TIME_TILE // WINDOW
STEP_BLOCKS = TILES_PER_STEP * N_BLOCKS
SW_SCORES_AHEAD = 4
WEIGHT_CHUNK_ROWS = 64
WEIGHT_SLOTS = 4
VMEM_LIMIT_BYTES = 58 * 1024 * 1024

F32 = jnp.float32
BF16 = jnp.bfloat16
NT_DIMS = (((1,), (1,)), ((), ()))
TN_DIMS = (((0,), (0,)), ((), ()))


LOG2_E = 1.4426950408889634


def _sigmoid(v):
    return 1.0 / (1.0 + jnp.exp2(v * (-LOG2_E)))


def _load_weight_as_bf16(w_hbm, stage_ref, sem_ref, dst_ref, scale=None):
    n_rows = dst_ref.shape[0]
    n_slots, chunk = stage_ref.shape[0], stage_ref.shape[1]
    n_chunks = n_rows // chunk
    assert n_chunks * chunk == n_rows and n_slots >= 2

    def chunk_copy(i):
        return pltpu.make_async_copy(w_hbm.at[0, pl.ds(i * chunk, chunk), :],
                                     stage_ref.at[i % n_slots], sem_ref.at[i % n_slots])

    for i in range(min(n_slots - 1, n_chunks)):
        chunk_copy(i).start()
    for i in range(n_chunks):
        ahead = i + n_slots - 1
        if ahead < n_chunks:
            chunk_copy(ahead).start()
        chunk_copy(i).wait()
        rows = stage_ref[i % n_slots]
        if scale is not None:
            rows = rows * scale
        dst_ref[i * chunk:(i + 1) * chunk, :] = rows.astype(BF16)


def _layer_kernel(sinks_ref, xc_ref, xn_ref, win_hbm, wout_hbm, lbl_ref, nw_ref, lng_ref, lnb_ref,
                  cos_ref, sin_ref, o_ref,
                  ha_ref, hb_ref, xb_ref, mix_ref, out_ref, st_ref, kwin_ref, vt_ref,
                  kcarry_ref, vcarry_ref, win_ref, wout_ref, win_stage, wout_stage,
                  win_sems, wout_sems):
    b = pl.program_id(0)
    u = pl.program_id(1)
    tt = TIME_TILE
    h_bufs = (ha_ref, hb_ref)

    @pl.when(jnp.logical_and(b == 0, u == 0))
    def _first_step():
        _load_weight_as_bf16(win_hbm, win_stage, win_sems, win_ref)
        _load_weight_as_bf16(wout_hbm, wout_stage, wout_sems, wout_ref, scale=1.0 / DN_ALPHA)
        ha_ref[...] = jnp.dot(xc_ref[0, 0:tt, :].astype(BF16), win_ref[...],
                              preferred_element_type=F32)

    @pl.when(u == 0)
    def _reset_carries():
        st_ref[...] = jnp.zeros_like(st_ref)
        kcarry_ref[...] = jnp.zeros_like(kcarry_ref)
        vcarry_ref[:, 0:SW_HEAD_DIM, :] = jnp.zeros((SW_KV_HEADS, SW_HEAD_DIM, WINDOW), BF16)
        vcarry_ref[:, SW_HEAD_DIM:VT_ROWS, :] = jnp.ones((SW_KV_HEADS, BF16_SUBLANES, WINDOW), BF16)
        vt_ref[:, SW_HEAD_DIM:VT_ROWS, :] = jnp.ones(
            (SW_KV_HEADS, BF16_SUBLANES, STEP_BLOCKS * WINDOW), BF16)

    lbl = lbl_ref[...]
    lbl_e = jnp.exp(lbl - jnp.max(lbl, axis=0, keepdims=True))
    lb = lbl_e[0:1] / jnp.sum(lbl_e, axis=0, keepdims=True)
    nw = nw_ref[...] * (HG_HEAD_DIM ** 0.5)
    row = lax.broadcasted_iota(jnp.int32, (HG_CHUNK, HG_CHUNK), 0)
    col = lax.broadcasted_iota(jnp.int32, (HG_CHUNK, HG_CHUNK), 1)
    tril = col <= row
    cum_mat = tril.astype(BF16)
    cum3 = jnp.concatenate([cum_mat, cum_mat, cum_mat], axis=1)

    lane = lax.broadcasted_iota(jnp.int32, (1, LANES), 1)
    rope_first = (lane % SW_HEAD_DIM) < (ROPE_DIM // 2)
    lane_lo = lane < SW_HEAD_DIM
    key_idx = lax.broadcasted_iota(jnp.int32, (2 * WINDOW, 2 * WINDOW), 0)
    qry_idx = lax.broadcasted_iota(jnp.int32, (2 * WINDOW, 2 * WINDOW), 1) % WINDOW
    band = (key_idx > qry_idx) & (key_idx <= qry_idx + WINDOW)
    pair_lo = lax.broadcasted_iota(jnp.int32, (1, 2 * WINDOW), 1) < WINDOW
    scale = SW_HEAD_DIM ** -0.5 * LOG2_E
    first_valid = jnp.where(u > 0, 0, WINDOW)
    band_cap = jnp.where(band, jnp.inf, -jnp.inf)
    first_cap = jnp.where(band & (key_idx >= first_valid), jnp.inf, -jnp.inf)

    def rope(v, cos, sin):
        partner = jnp.where(rope_first,
                            pltpu.roll(v, LANES - ROPE_DIM // 2, 1),
                            pltpu.roll(v, ROPE_DIM // 2, 1))
        return v * cos + partner * sin

    def head_variants(v):
        sw = pltpu.roll(v, SW_HEAD_DIM, 1)
        zero = jnp.zeros_like(v)
        return (jnp.where(lane_lo, v, zero), jnp.where(lane_lo, zero, sw),
                jnp.where(lane_lo, sw, zero), jnp.where(lane_lo, zero, v))

    def in_projection_piece(k, piece):
        xb = xb_ref.at[k % 2]
        cols = slice(piece * PIECE, (piece + 1) * PIECE)
        h_bufs[k % 2][:, cols] = jnp.dot(xb[...], win_ref[:, cols], preferred_element_type=F32)

    def out_projection_piece(k, piece):
        cols = slice(piece * PIECE, (piece + 1) * PIECE)
        out_ref[:, cols] = jnp.dot(mix_ref[k], wout_ref[:, cols], preferred_element_type=F32)

    def post_norm(k):
        rows = slice(k * tt, (k + 1) * tt)
        z = xc_ref[0, rows, :] + out_ref[...]
        mu = jnp.mean(z, axis=-1, keepdims=True)
        zc = z - mu
        var = jnp.mean(zc * zc, axis=-1, keepdims=True)
        o_ref[0, rows, :] = ((zc * lax.rsqrt(var + LN_EPS / DN_ALPHA ** 2)) * lng_ref[...]
                             + lnb_ref[...])

    def make_fillers(k):
        stages = []
        if k > 0:
            for piece in range(N_OUT_PIECES):
                stages.append(lambda piece=piece: out_projection_piece(k - 1, piece))
            stages.append(lambda: post_norm(k - 1))
        for piece in range(N_IN_PIECES):
            stages.append(lambda piece=piece: in_projection_piece(k + 1, piece))
        return iter(stages)

    def process_tile(k):
        h_ref = h_bufs[k % 2]
        base = k * tt
        mix = mix_ref.at[k]
        if k + 1 < TILES_PER_STEP:
            x_next_rows = xc_ref[0, (k + 1) * tt:(k + 2) * tt, :]
        else:
            x_next_rows = xn_ref[0]
        xb_ref[(k + 1) % 2] = x_next_rows.astype(BF16)
        fillers = make_fillers(k)

        def fill(n=1):
            for _ in range(n):
                stage = next(fillers, None)
                if stage is not None:
                    stage()

        def hg_pre(c):
            rows = slice(c * HG_CHUNK, (c + 1) * HG_CHUNK)
            f = lb + (1.0 - lb) * _sigmoid(h_ref[rows, OFF_HF:OFF_HF + HG_WIDTH])
            log_f = jnp.log2(f)
            hi = log_f.astype(BF16)
            rem = log_f - hi.astype(F32)
            mid = rem.astype(BF16)
            lo = (rem - mid.astype(F32)).astype(BF16)
            return 1.0 - f, jnp.concatenate([hi, mid, lo], axis=0)

        def hg_cum(parts):
            return jnp.dot(cum3, parts, preferred_element_type=F32)

        def hg_decays(c, k_in, g_cum):
            rows = slice(c * HG_CHUNK, (c + 1) * HG_CHUNK)
            hq = h_ref[rows, OFF_HQ:OFF_HQ + HG_WIDTH]
            g_last = g_cum[HG_CHUNK - 1:HG_CHUNK, :]
            q_dec = ((hq * _sigmoid(hq)) * jnp.exp2(g_cum)).astype(BF16)
            k_dec = (k_in * jnp.exp2(-g_cum)).astype(BF16)
            k_tail = (k_in * jnp.exp2(g_last - g_cum)).astype(BF16)
            return q_dec, k_dec, k_tail, jnp.exp2(g_last)

        def hg_scores(q_dec, k_dec, hd):
            sl = slice(hd * HG_HEAD_DIM, (hd + 1) * HG_HEAD_DIM)
            return lax.dot_general(q_dec[:, sl], k_dec[:, sl], NT_DIMS, preferred_element_type=F32)

        def hg_output(c, hd, a, q_dec, k_tail, decay):
            rows = slice(c * HG_CHUNK, (c + 1) * HG_CHUNK)
            sl = slice(hd * HG_HEAD_DIM, (hd + 1) * HG_HEAD_DIM)
            v = h_ref[rows, OFF_HI + hd * HG_HEAD_DIM:OFF_HI + (hd + 1) * HG_HEAD_DIM].astype(BF16)
            s_t = st_ref[hd]
            o = jnp.dot(jnp.where(tril, a, 0.0).astype(BF16), v, preferred_element_type=F32)
            o = o + lax.dot_general(q_dec[:, sl], s_t.astype(BF16), NT_DIMS,
                                    preferred_element_type=F32)
            inc_t = lax.dot_general(v, k_tail[:, sl], TN_DIMS, preferred_element_type=F32)
            st_ref[hd] = s_t * decay[:, sl] + inc_t
            o = o * lax.rsqrt(jnp.sum(o * o, axis=-1, keepdims=True) + HG_HEAD_DIM * RMS_EPS)
            o = o * nw[:, sl]
            hg = h_ref[rows, OFF_HG + hd * HG_HEAD_DIM:OFF_HG + (hd + 1) * HG_HEAD_DIM]
            mix[rows, sl] = (o * (hg * _sigmoid(hg))).astype(BF16)

        def sw_prepare(blk):
            r0 = blk * WINDOW
            rows = slice(r0, r0 + WINDOW)
            cur = slice(base + r0, base + r0 + WINDOW)
            cos = cos_ref[base + r0:base + r0 + WINDOW, :]
            sin = sin_ref[base + r0:base + r0 + WINDOW, :]
            k_rot = rope(h_ref[rows, OFF_AK:OFF_AK + SW_KV_WIDTH], cos, sin)
            for idx, kv in enumerate(head_variants(k_rot)):
                kwin_ref[idx, cur, :] = kv.astype(BF16)
            v_t = h_ref[rows, OFF_AV:OFF_AV + SW_KV_WIDTH].T
            for g in range(SW_KV_HEADS):
                vt_ref[g, 0:SW_HEAD_DIM, cur] = (
                    v_t[g * SW_HEAD_DIM:(g + 1) * SW_HEAD_DIM].astype(BF16))
            q_bf = []
            for tile in range(SW_WIDTH // LANES):
                q_rot = rope(h_ref[rows, OFF_AQ + tile * LANES:OFF_AQ + (tile + 1) * LANES], cos, sin)
                q_bf.append((q_rot * scale).astype(BF16))
            return [jnp.concatenate([q_bf[2 * g], q_bf[2 * g + 1]], axis=0)
                    for g in range(SW_KV_HEADS)]

        def key_window(blk, idx):
            r0 = base + blk * WINDOW
            if r0 == 0:
                return jnp.concatenate([kcarry_ref[idx], kwin_ref[idx, 0:WINDOW, :]], axis=0)
            return kwin_ref[idx, r0 - WINDOW:r0 + WINDOW, :]

        def value_window(blk, g):
            r0 = base + blk * WINDOW
            if r0 == 0:
                return jnp.concatenate([vcarry_ref[g], vt_ref[g, :, 0:WINDOW]], axis=1)
            return vt_ref[g, :, r0 - WINDOW:r0 + WINDOW]

        def sw_scores(blk, g, j, q_pair):
            return lax.dot_general(key_window(blk, 2 * g + j), q_pair, NT_DIMS,
                                   preferred_element_type=F32)

        def sw_values(blk, g, j, s_t, cap):
            sink_row = jnp.where(pair_lo, sinks_ref[SW_GROUP * g + j],
                                 sinks_ref[SW_GROUP * g + HEADS_PER_TILE + j]) * LOG2_E
            s_t = jnp.minimum(s_t, cap)
            m = jnp.maximum(jnp.max(s_t, axis=0, keepdims=True), sink_row)
            p_t = jnp.exp2(s_t - m).astype(BF16)
            acc = jnp.dot(value_window(blk, g), p_t, preferred_element_type=F32)
            denom = acc[SW_HEAD_DIM:SW_HEAD_DIM + 1, :] + jnp.exp2(sink_row - m)
            return acc[0:SW_HEAD_DIM, :] * (1.0 / denom)

        def sw_finish(blk, g, o_t):
            rows = slice(blk * WINDOW, (blk + 1) * WINDOW)
            for tl in range(2):
                tile = 2 * g + tl
                lanes = slice(tl * LANES, (tl + 1) * LANES)
                o_a = jnp.concatenate([o_t[0][:, lanes], o_t[1][:, lanes]], axis=0).T
                ag = h_ref[rows, OFF_AG + tile * LANES:OFF_AG + (tile + 1) * LANES]
                mix[rows, HG_WIDTH + tile * LANES:HG_WIDTH + (tile + 1) * LANES] = (
                    o_a * (ag * _sigmoid(ag))).astype(BF16)

        many = k > 0
        pre = {}
        cum = {}
        fill(2 if many else 1)
        for c in range(min(2, N_CHUNKS)):
            pre[c] = hg_pre(c)
            cum[c] = hg_cum(pre[c][1])
            fill(1 if many else 0)
        for c in range(N_CHUNKS):
            q_dec, k_dec, k_tail, decay = hg_decays(c, pre[c][0], cum[c])
            hg_s = [hg_scores(q_dec, k_dec, hd) for hd in range(HG_HEADS)]
            if c + 2 < N_CHUNKS:
                pre[c + 2] = hg_pre(c + 2)
                cum[c + 2] = hg_cum(pre[c + 2][1])
            fill()
            for hd in range(HG_HEADS):
                hg_output(c, hd, hg_s[hd], q_dec, k_tail, decay)
            fill(1 if many else 0)

        q_pairs = [sw_prepare(blk) for blk in range(N_BLOCKS)]
        masks = [first_cap if (k == 0 and blk == 0) else band_cap for blk in range(N_BLOCKS)]
        chains = [(blk, g, j) for blk in range(N_BLOCKS) for g in range(SW_KV_HEADS)
                  for j in range(HEADS_PER_TILE)]
        scores = {}
        for i in range(min(SW_SCORES_AHEAD, len(chains))):
            blk, g, j = chains[i]
            scores[i] = sw_scores(blk, g, j, q_pairs[blk][g])
        fill()
        outs = {}
        for i, (blk, g, j) in enumerate(chains):
            outs[(blk, g, j)] = sw_values(blk, g, j, scores.pop(i), masks[blk])
            if i + SW_SCORES_AHEAD < len(chains):
                nb, ng, nj = chains[i + SW_SCORES_AHEAD]
                scores[i + SW_SCORES_AHEAD] = sw_scores(nb, ng, nj, q_pairs[nb][ng])
            if j == HEADS_PER_TILE - 1:
                sw_finish(blk, g, [outs.pop((blk, g, jj)) for jj in range(HEADS_PER_TILE)])
            fill()
        fill(N_OUT_PIECES + 1 + N_IN_PIECES)

    for k in range(TILES_PER_STEP):
        process_tile(k)
    for piece in range(N_OUT_PIECES):
        out_projection_piece(TILES_PER_STEP - 1, piece)
    post_norm(TILES_PER_STEP - 1)

    last = slice((STEP_BLOCKS - 1) * WINDOW, STEP_BLOCKS * WINDOW)
    for idx in range(2 * SW_KV_HEADS):
        kcarry_ref[idx] = kwin_ref[idx, last, :]
    for g in range(SW_KV_HEADS):
        vcarry_ref[g, 0:SW_HEAD_DIM, :] = vt_ref[g, 0:SW_HEAD_DIM, last]


def _rope_tables(seq_len):
    pos = jnp.arange(seq_len, dtype=F32)
    inv_freq = ROPE_THETA ** (-jnp.arange(0, ROPE_DIM, 2, dtype=F32) / ROPE_DIM)
    ang = pos[:, None] * inv_freq[None, :]
    cos = jnp.cos(ang)
    sin = jnp.sin(ang)
    ones = jnp.ones((seq_len, SW_HEAD_DIM - ROPE_DIM), F32)
    cos_head = jnp.concatenate([cos, cos, ones], axis=1)
    sin_head = jnp.concatenate([-sin, sin, jnp.zeros_like(ones)], axis=1)
    return (jnp.tile(cos_head, (1, HEADS_PER_TILE)), jnp.tile(sin_head, (1, HEADS_PER_TILE)))


def kernel(x, w_in, lb_logits, hg_norm_w, sinks, w_out, ln_g, ln_b):
    batch, seq_len, d_model = x.shape
    assert d_model == D_MODEL and w_in.shape == (DEPTH, D_MODEL, IN_WIDTH)
    step_rows = TILES_PER_STEP * TIME_TILE
    assert seq_len % step_rows == 0
    cos_tab, sin_tab = _rope_tables(seq_len)
    n_steps = seq_len // step_rows
    last_step = batch * n_steps - 1

    def next_tile_index(b, u, *_):
        nxt = jnp.minimum(b * n_steps + u + 1, last_step)
        return (nxt // n_steps, TILES_PER_STEP * (nxt % n_steps), 0)

    const = lambda b, u, *_: (0, 0)
    grid_spec = pltpu.PrefetchScalarGridSpec(
        num_scalar_prefetch=1,
        grid=(batch, n_steps),
        in_specs=[
            pl.BlockSpec((1, step_rows, D_MODEL), lambda b, u, *_: (b, u, 0)),
            pl.BlockSpec((1, TIME_TILE, D_MODEL), next_tile_index),
            pl.BlockSpec(memory_space=pl.ANY),
            pl.BlockSpec(memory_space=pl.ANY),
            pl.BlockSpec((DEPTH + 1, HG_WIDTH), const),
            pl.BlockSpec((1, HG_WIDTH), const),
            pl.BlockSpec((1, D_MODEL), const),
            pl.BlockSpec((1, D_MODEL), const),
            pl.BlockSpec((step_rows, LANES), lambda b, u, *_: (u, 0)),
            pl.BlockSpec((step_rows, LANES), lambda b, u, *_: (u, 0)),
        ],
        out_specs=pl.BlockSpec((1, step_rows, D_MODEL), lambda b, u, *_: (b, u, 0)),
        scratch_shapes=[
            pltpu.VMEM((TIME_TILE, IN_WIDTH), F32),
            pltpu.VMEM((TIME_TILE, IN_WIDTH), F32),
            pltpu.VMEM((2, TIME_TILE, D_MODEL), BF16),
            pltpu.VMEM((TILES_PER_STEP, TIME_TILE, D_MODEL), BF16),
            pltpu.VMEM((TIME_TILE, D_MODEL), F32),
            pltpu.VMEM((HG_HEADS, HG_HEAD_DIM, HG_HEAD_DIM), F32),
            pltpu.VMEM((2 * SW_KV_HEADS, STEP_BLOCKS * WINDOW, LANES), BF16),
            pltpu.VMEM((SW_KV_HEADS, VT_ROWS, STEP_BLOCKS * WINDOW), BF16),
            pltpu.VMEM((2 * SW_KV_HEADS, WINDOW, LANES), BF16),
            pltpu.VMEM((SW_KV_HEADS, VT_ROWS, WINDOW), BF16),
            pltpu.VMEM((D_MODEL, IN_WIDTH), BF16),
            pltpu.VMEM((D_MODEL, D_MODEL), BF16),
            pltpu.VMEM((WEIGHT_SLOTS, WEIGHT_CHUNK_ROWS, IN_WIDTH), F32),
            pltpu.VMEM((WEIGHT_SLOTS, WEIGHT_CHUNK_ROWS, D_MODEL), F32),
            pltpu.SemaphoreType.DMA((WEIGHT_SLOTS,)),
            pltpu.SemaphoreType.DMA((WEIGHT_SLOTS,)),
        ],
    )
    return pl.pallas_call(
        _layer_kernel,
        grid_spec=grid_spec,
        out_shape=jax.ShapeDtypeStruct(x.shape, x.dtype),
        compiler_params=pltpu.CompilerParams(
            dimension_semantics=("arbitrary", "arbitrary"),
            vmem_limit_bytes=VMEM_LIMIT_BYTES),
        name="hybrid_hgrn2_swa_layer",
    )(sinks[0], x, x, w_in, w_out, lb_logits, hg_norm_w, ln_g, ln_b, cos_tab, sin_tab)
```

```python
import jax
import jax.numpy as jnp
from jax import lax
from jax.experimental import pallas as pl
from jax.experimental.pallas import tpu as pltpu

D_MODEL = 1024
DEPTH = 1
HG_WIDTH = 512
HG_HEAD_DIM = 128
HG_HEADS = HG_WIDTH // HG_HEAD_DIM
HG_CHUNK = 64
SW_WIDTH = 512
SW_HEAD_DIM = 64
SW_Q_HEADS = SW_WIDTH // SW_HEAD_DIM
SW_KV_HEADS = SW_Q_HEADS // 4
SW_KV_WIDTH = SW_KV_HEADS * SW_HEAD_DIM
SW_GROUP = SW_Q_HEADS // SW_KV_HEADS
WINDOW = 128
ROPE_THETA = 500000.0
ROPE_DIM = SW_HEAD_DIM // 4
DN_ALPHA = (2.0 * DEPTH) ** 0.25
LN_EPS = 1e-5
RMS_EPS = 1e-6
IN_WIDTH = 4 * HG_WIDTH + SW_WIDTH + 2 * SW_KV_WIDTH + SW_WIDTH

OFF_HQ = 0
OFF_HF = OFF_HQ + HG_WIDTH
OFF_HI = OFF_HF + HG_WIDTH
OFF_HG = OFF_HI + HG_WIDTH
OFF_AQ = OFF_HG + HG_WIDTH
OFF_AK = OFF_AQ + SW_WIDTH
OFF_AV = OFF_AK + SW_KV_WIDTH
OFF_AG = OFF_AV + SW_KV_WIDTH

LANES = 128
BF16_SUBLANES = 16
MXU_WIDTH = 256
HEADS_PER_TILE = LANES // SW_HEAD_DIM
assert HEADS_PER_TILE == 2 and SW_GROUP == 2 * HEADS_PER_TILE and SW_KV_WIDTH == LANES
VT_ROWS = SW_HEAD_DIM + BF16_SUBLANES
TIME_TILE = 256
TILES_PER_STEP = 4
PIECE = MXU_WIDTH
N_IN_PIECES = IN_WIDTH // PIECE
N_OUT_PIECES = D_MODEL // PIECE
assert IN_WIDTH % PIECE == 0 and D_MODEL % PIECE == 0
N_CHUNKS = TIME_TILE // HG_CHUNK
N_BLOCKS = TIME_TILE // WINDOW
STEP_BLOCKS = TILES_PER_STEP * N_BLOCKS
SW_SCORES_AHEAD = 8
WEIGHT_CHUNK_ROWS = 64
WEIGHT_SLOTS = 4
VMEM_LIMIT_BYTES = 58 * 1024 * 1024

F32 = jnp.float32
BF16 = jnp.bfloat16
NT_DIMS = (((1,), (1,)), ((), ()))
TN_DIMS = (((0,), (0,)), ((), ()))


LOG2_E = 1.4426950408889634


def _sigmoid(v):
    return 1.0 / (1.0 + jnp.exp2(v * (-LOG2_E)))


def _load_weight_as_bf16(w_hbm, stage_ref, sem_ref, dst_ref, scale=None):
    n_rows = dst_ref.shape[0]
    n_slots, chunk = stage_ref.shape[0], stage_ref.shape[1]
    n_chunks = n_rows // chunk
    assert n_chunks * chunk == n_rows and n_slots >= 2

    def chunk_copy(i):
        return pltpu.make_async_copy(w_hbm.at[0, pl.ds(i * chunk, chunk), :],
                                     stage_ref.at[i % n_slots], sem_ref.at[i % n_slots])

    for i in range(min(n_slots - 1, n_chunks)):
        chunk_copy(i).start()
    for i in range(n_chunks):
        ahead = i + n_slots - 1
        if ahead < n_chunks:
            chunk_copy(ahead).start()
        chunk_copy(i).wait()
        rows = stage_ref[i % n_slots]
        if scale is not None:
            rows = rows * scale
        dst_ref[i * chunk:(i + 1) * chunk, :] = rows.astype(BF16)


def _layer_kernel(sinks_ref, xc_ref, xn_ref, win_hbm, wout_hbm, lbl_ref, nw_ref, lng_ref, lnb_ref,
                  cos_ref, sin_ref, o_ref,
                  ha_ref, hb_ref, xb_ref, mix_ref, out_ref, st_ref, kwin_ref, vt_ref,
                  kcarry_ref, vcarry_ref, win_ref, wout_ref, win_stage, wout_stage,
                  win_sems, wout_sems):
    b = pl.program_id(0)
    u = pl.program_id(1)
    tt = TIME_TILE
    h_bufs = (ha_ref, hb_ref)

    @pl.when(jnp.logical_and(b == 0, u == 0))
    def _first_step():
        _load_weight_as_bf16(win_hbm, win_stage, win_sems, win_ref)
        _load_weight_as_bf16(wout_hbm, wout_stage, wout_sems, wout_ref, scale=1.0 / DN_ALPHA)
        ha_ref[...] = jnp.dot(xc_ref[0, 0:tt, :].astype(BF16), win_ref[...],
                              preferred_element_type=F32)

    @pl.when(u == 0)
    def _reset_carries():
        st_ref[...] = jnp.zeros_like(st_ref)
        kcarry_ref[...] = jnp.zeros_like(kcarry_ref)
        vcarry_ref[:, 0:SW_HEAD_DIM, :] = jnp.zeros((SW_KV_HEADS, SW_HEAD_DIM, WINDOW), BF16)
        vcarry_ref[:, SW_HEAD_DIM:VT_ROWS, :] = jnp.ones((SW_KV_HEADS, BF16_SUBLANES, WINDOW), BF16)
        vt_ref[:, SW_HEAD_DIM:VT_ROWS, :] = jnp.ones(
            (SW_KV_HEADS, BF16_SUBLANES, STEP_BLOCKS * WINDOW), BF16)

    lbl = lbl_ref[...]
    lbl_e = jnp.exp(lbl - jnp.max(lbl, axis=0, keepdims=True))
    lb = lbl_e[0:1] / jnp.sum(lbl_e, axis=0, keepdims=True)
    nw = nw_ref[...] * (HG_HEAD_DIM ** 0.5)
    row = lax.broadcasted_iota(jnp.int32, (HG_CHUNK, HG_CHUNK), 0)
    col = lax.broadcasted_iota(jnp.int32, (HG_CHUNK, HG_CHUNK), 1)
    tril = col <= row
    cum_mat = tril.astype(BF16)
    cum3 = jnp.concatenate([cum_mat, cum_mat, cum_mat], axis=1)

    lane = lax.broadcasted_iota(jnp.int32, (1, LANES), 1)
    rope_first = (lane % SW_HEAD_DIM) < (ROPE_DIM // 2)
    lane_lo = lane < SW_HEAD_DIM
    key_idx = lax.broadcasted_iota(jnp.int32, (2 * WINDOW, 2 * WINDOW), 0)
    qry_idx = lax.broadcasted_iota(jnp.int32, (2 * WINDOW, 2 * WINDOW), 1) % WINDOW
    band = (key_idx > qry_idx) & (key_idx <= qry_idx + WINDOW)
    pair_lo = lax.broadcasted_iota(jnp.int32, (1, 2 * WINDOW), 1) < WINDOW
    scale = SW_HEAD_DIM ** -0.5 * LOG2_E
    first_valid = jnp.where(u > 0, 0, WINDOW)
    band_cap = jnp.where(band, jnp.inf, -jnp.inf)
    first_cap = jnp.where(band & (key_idx >= first_valid), jnp.inf, -jnp.inf)

    def rope(v, cos, sin):
        partner = jnp.where(rope_first,
                            pltpu.roll(v, LANES - ROPE_DIM // 2, 1),
                            pltpu.roll(v, ROPE_DIM // 2, 1))
        return v * cos + partner * sin

    def head_variants(v):
        sw = pltpu.roll(v, SW_HEAD_DIM, 1)
        zero = jnp.zeros_like(v)
        return (jnp.where(lane_lo, v, zero), jnp.where(lane_lo, zero, sw),
                jnp.where(lane_lo, sw, zero), jnp.where(lane_lo, zero, v))

    def in_projection_piece(k, piece):
        xb = xb_ref.at[k % 2]
        cols = slice(piece * PIECE, (piece + 1) * PIECE)
        h_bufs[k % 2][:, cols] = jnp.dot(xb[...], win_ref[:, cols], preferred_element_type=F32)

    def out_projection_piece(k, piece):
        cols = slice(piece * PIECE, (piece + 1) * PIECE)
        out_ref[:, cols] = jnp.dot(mix_ref[k], wout_ref[:, cols], preferred_element_type=F32)

    def post_norm(k):
        rows = slice(k * tt, (k + 1) * tt)
        z = xc_ref[0, rows, :] + out_ref[...]
        mu = jnp.mean(z, axis=-1, keepdims=True)
        zc = z - mu
        var = jnp.mean(zc * zc, axis=-1, keepdims=True)
        o_ref[0, rows, :] = ((zc * lax.rsqrt(var + LN_EPS / DN_ALPHA ** 2)) * lng_ref[...]
                             + lnb_ref[...])

    def make_fillers(k):
        stages = []
        if k > 0:
            for piece in range(N_OUT_PIECES):
                stages.append(lambda piece=piece: out_projection_piece(k - 1, piece))
            stages.append(lambda: post_norm(k - 1))
        for piece in range(N_IN_PIECES):
            stages.append(lambda piece=piece: in_projection_piece(k + 1, piece))
        return iter(stages)

    def process_tile(k):
        h_ref = h_bufs[k % 2]
        base = k * tt
        mix = mix_ref.at[k]
        if k + 1 < TILES_PER_STEP:
            x_next_rows = xc_ref[0, (k + 1) * tt:(k + 2) * tt, :]
        else:
            x_next_rows = xn_ref[0]
        xb_ref[(k + 1) % 2] = x_next_rows.astype(BF16)
        fillers = make_fillers(k)

        def fill(n=1):
            for _ in range(n):
                stage = next(fillers, None)
                if stage is not None:
                    stage()

        def hg_pre(c):
            rows = slice(c * HG_CHUNK, (c + 1) * HG_CHUNK)
            f = lb + (1.0 - lb) * _sigmoid(h_ref[rows, OFF_HF:OFF_HF + HG_WIDTH])
            log_f = jnp.log2(f)
            hi = log_f.astype(BF16)
            rem = log_f - hi.astype(F32)
            mid = rem.astype(BF16)
            lo = (rem - mid.astype(F32)).astype(BF16)
            return 1.0 - f, jnp.concatenate([hi, mid, lo], axis=0)

        def hg_cum(parts):
            return jnp.dot(cum3, parts, preferred_element_type=F32)

        def hg_decays(c, k_in, g_cum):
            rows = slice(c * HG_CHUNK, (c + 1) * HG_CHUNK)
            hq = h_ref[rows, OFF_HQ:OFF_HQ + HG_WIDTH]
            g_last = g_cum[HG_CHUNK - 1:HG_CHUNK, :]
            q_dec = ((hq * _sigmoid(hq)) * jnp.exp2(g_cum)).astype(BF16)
            k_dec = (k_in * jnp.exp2(-g_cum)).astype(BF16)
            k_tail = (k_in * jnp.exp2(g_last - g_cum)).astype(BF16)
            return q_dec, k_dec, k_tail, jnp.exp2(g_last)

        def hg_scores(q_dec, k_dec, hd):
            sl = slice(hd * HG_HEAD_DIM, (hd + 1) * HG_HEAD_DIM)
            return lax.dot_general(q_dec[:, sl], k_dec[:, sl], NT_DIMS, preferred_element_type=F32)

        def hg_output(c, hd, a, q_dec, k_tail, decay):
            rows = slice(c * HG_CHUNK, (c + 1) * HG_CHUNK)
            sl = slice(hd * HG_HEAD_DIM, (hd + 1) * HG_HEAD_DIM)
            v = h_ref[rows, OFF_HI + hd * HG_HEAD_DIM:OFF_HI + (hd + 1) * HG_HEAD_DIM].astype(BF16)
            s_t = st_ref[hd]
            o = jnp.dot(jnp.where(tril, a, 0.0).astype(BF16), v, preferred_element_type=F32)
            o = o + lax.dot_general(q_dec[:, sl], s_t.astype(BF16), NT_DIMS,
                                    preferred_element_type=F32)
            inc_t = lax.dot_general(v, k_tail[:, sl], TN_DIMS, preferred_element_type=F32)
            st_ref[hd] = s_t * decay[:, sl] + inc_t
            o = o * lax.rsqrt(jnp.sum(o * o, axis=-1, keepdims=True) + HG_HEAD_DIM * RMS_EPS)
            o = o * nw[:, sl]
            hg = h_ref[rows, OFF_HG + hd * HG_HEAD_DIM:OFF_HG + (hd + 1) * HG_HEAD_DIM]
            mix[rows, sl] = (o * (hg * _sigmoid(hg))).astype(BF16)

        def sw_prepare(blk):
            r0 = blk * WINDOW
            rows = slice(r0, r0 + WINDOW)
            cur = slice(base + r0, base + r0 + WINDOW)
            cos = cos_ref[base + r0:base + r0 + WINDOW, :]
            sin = sin_ref[base + r0:base + r0 + WINDOW, :]
            k_rot = rope(h_ref[rows, OFF_AK:OFF_AK + SW_KV_WIDTH], cos, sin)
            for idx, kv in enumerate(head_variants(k_rot)):
                kwin_ref[idx, cur, :] = kv.astype(BF16)
            v_t = h_ref[rows, OFF_AV:OFF_AV + SW_KV_WIDTH].T
            for g in range(SW_KV_HEADS):
                vt_ref[g, 0:SW_HEAD_DIM, cur] = (
                    v_t[g * SW_HEAD_DIM:(g + 1) * SW_HEAD_DIM].astype(BF16))
            q_bf = []
            for tile in range(SW_WIDTH // LANES):
                q_rot = rope(h_ref[rows, OFF_AQ + tile * LANES:OFF_AQ + (tile + 1) * LANES], cos, sin)
                q_bf.append((q_rot * scale).astype(BF16))
            return [jnp.concatenate([q_bf[2 * g], q_bf[2 * g + 1]], axis=0)
                    for g in range(SW_KV_HEADS)]

        def key_window(blk, idx):
            r0 = base + blk * WINDOW
            if r0 == 0:
                return jnp.concatenate([kcarry_ref[idx], kwin_ref[idx, 0:WINDOW, :]], axis=0)
            return kwin_ref[idx, r0 - WINDOW:r0 + WINDOW, :]

        def value_window(blk, g):
            r0 = base + blk * WINDOW
            if r0 == 0:
                return jnp.concatenate([vcarry_ref[g], vt_ref[g, :, 0:WINDOW]], axis=1)
            return vt_ref[g, :, r0 - WINDOW:r0 + WINDOW]

        def sw_scores(blk, g, j, q_pair):
            return lax.dot_general(key_window(blk, 2 * g + j), q_pair, NT_DIMS,
                                   preferred_element_type=F32)

        def sw_values(blk, g, j, s_t, cap):
            sink_row = jnp.where(pair_lo, sinks_ref[SW_GROUP * g + j],
                                 sinks_ref[SW_GROUP * g + HEADS_PER_TILE + j]) * LOG2_E
            s_t = jnp.minimum(s_t, cap)
            m = jnp.maximum(jnp.max(s_t, axis=0, keepdims=True), sink_row)
            p_t = jnp.exp2(s_t - m).astype(BF16)
            acc = jnp.dot(value_window(blk, g), p_t, preferred_element_type=F32)
            denom = acc[SW_HEAD_DIM:SW_HEAD_DIM + 1, :] + jnp.exp2(sink_row - m)
            return acc[0:SW_HEAD_DIM, :] * (1.0 / denom)

        def sw_finish(blk, g, o_t):
            rows = slice(blk * WINDOW, (blk + 1) * WINDOW)
            for tl in range(2):
                tile = 2 * g + tl
                lanes = slice(tl * LANES, (tl + 1) * LANES)
                o_a = jnp.concatenate([o_t[0][:, lanes], o_t[1][:, lanes]], axis=0).T
                ag = h_ref[rows, OFF_AG + tile * LANES:OFF_AG + (tile + 1) * LANES]
                mix[rows, HG_WIDTH + tile * LANES:HG_WIDTH + (tile + 1) * LANES] = (
                    o_a * (ag * _sigmoid(ag))).astype(BF16)

        many = k > 0
        masks = [first_cap if (k == 0 and blk == 0) else band_cap for blk in range(N_BLOCKS)]
        chains = [(blk, g, j) for blk in range(N_BLOCKS) for g in range(SW_KV_HEADS)
                  for j in range(HEADS_PER_TILE)]
        q_pairs = {}
        scores = {}

        def sw_issue_scores(i):
            if i < len(chains) and i not in scores:
                blk, g, j = chains[i]
                if blk not in q_pairs:
                    q_pairs[blk] = sw_prepare(blk)
                scores[i] = sw_scores(blk, g, j, q_pairs[blk][g])

        pre = {}
        cum = {}
        fill(2 if many else 1)
        for c in range(min(2, N_CHUNKS)):
            pre[c] = hg_pre(c)
            cum[c] = hg_cum(pre[c][1])
            fill(1 if many else 0)
        for c in range(N_CHUNKS):
            q_dec, k_dec, k_tail, decay = hg_decays(c, pre[c][0], cum[c])
            hg_s = [hg_scores(q_dec, k_dec, hd) for hd in range(HG_HEADS)]
            if c + 2 < N_CHUNKS:
                pre[c + 2] = hg_pre(c + 2)
                cum[c + 2] = hg_cum(pre[c + 2][1])
            fill()
            for hd in range(HG_HEADS):
                hg_output(c, hd, hg_s[hd], q_dec, k_tail, decay)
            fill(1 if many else 0)

        for i in range(min(SW_SCORES_AHEAD, len(chains))):
            sw_issue_scores(i)
        fill()
        outs = {}
        for i, (blk, g, j) in enumerate(chains):
            outs[(blk, g, j)] = sw_values(blk, g, j, scores.pop(i), masks[blk])
            sw_issue_scores(i + SW_SCORES_AHEAD)
            if j == HEADS_PER_TILE - 1:
                sw_finish(blk, g, [outs.pop((blk, g, jj)) for jj in range(HEADS_PER_TILE)])
            fill()
        fill(N_OUT_PIECES + 1 + N_IN_PIECES)

    for k in range(TILES_PER_STEP):
        process_tile(k)
    for piece in range(N_OUT_PIECES):
        out_projection_piece(TILES_PER_STEP - 1, piece)
    post_norm(TILES_PER_STEP - 1)

    last = slice((STEP_BLOCKS - 1) * WINDOW, STEP_BLOCKS * WINDOW)
    for idx in range(2 * SW_KV_HEADS):
        kcarry_ref[idx] = kwin_ref[idx, last, :]
    for g in range(SW_KV_HEADS):
        vcarry_ref[g, 0:SW_HEAD_DIM, :] = vt_ref[g, 0:SW_HEAD_DIM, last]


def _rope_tables(seq_len):
    pos = jnp.arange(seq_len, dtype=F32)
    inv_freq = ROPE_THETA ** (-jnp.arange(0, ROPE_DIM, 2, dtype=F32) / ROPE_DIM)
    ang = pos[:, None] * inv_freq[None, :]
    cos = jnp.cos(ang)
    sin = jnp.sin(ang)
    ones = jnp.ones((seq_len, SW_HEAD_DIM - ROPE_DIM), F32)
    cos_head = jnp.concatenate([cos, cos, ones], axis=1)
    sin_head = jnp.concatenate([-sin, sin, jnp.zeros_like(ones)], axis=1)
    return (jnp.tile(cos_head, (1, HEADS_PER_TILE)), jnp.tile(sin_head, (1, HEADS_PER_TILE)))


def kernel(x, w_in, lb_logits, hg_norm_w, sinks, w_out, ln_g, ln_b):
    batch, seq_len, d_model = x.shape
    assert d_model == D_MODEL and w_in.shape == (DEPTH, D_MODEL, IN_WIDTH)
    step_rows = TILES_PER_STEP * TIME_TILE
    assert seq_len % step_rows == 0
    cos_tab, sin_tab = _rope_tables(seq_len)
    n_steps = seq_len // step_rows
    last_step = batch * n_steps - 1

    def next_tile_index(b, u, *_):
        nxt = jnp.minimum(b * n_steps + u + 1, last_step)
        return (nxt // n_steps, TILES_PER_STEP * (nxt % n_steps), 0)

    const = lambda b, u, *_: (0, 0)
    grid_spec = pltpu.PrefetchScalarGridSpec(
        num_scalar_prefetch=1,
        grid=(batch, n_steps),
        in_specs=[
            pl.BlockSpec((1, step_rows, D_MODEL), lambda b, u, *_: (b, u, 0)),
            pl.BlockSpec((1, TIME_TILE, D_MODEL), next_tile_index),
            pl.BlockSpec(memory_space=pl.ANY),
            pl.BlockSpec(memory_space=pl.ANY),
            pl.BlockSpec((DEPTH + 1, HG_WIDTH), const),
            pl.BlockSpec((1, HG_WIDTH), const),
            pl.BlockSpec((1, D_MODEL), const),
            pl.BlockSpec((1, D_MODEL), const),
            pl.BlockSpec((step_rows, LANES), lambda b, u, *_: (u, 0)),
            pl.BlockSpec((step_rows, LANES), lambda b, u, *_: (u, 0)),
        ],
        out_specs=pl.BlockSpec((1, step_rows, D_MODEL), lambda b, u, *_: (b, u, 0)),
        scratch_shapes=[
            pltpu.VMEM((TIME_TILE, IN_WIDTH), F32),
            pltpu.VMEM((TIME_TILE, IN_WIDTH), F32),
            pltpu.VMEM((2, TIME_TILE, D_MODEL), BF16),
            pltpu.VMEM((TILES_PER_STEP, TIME_TILE, D_MODEL), BF16),
            pltpu.VMEM((TIME_TILE, D_MODEL), F32),
            pltpu.VMEM((HG_HEADS, HG_HEAD_DIM, HG_HEAD_DIM), F32),
            pltpu.VMEM((2 * SW_KV_HEADS, STEP_BLOCKS * WINDOW, LANES), BF16),
            pltpu.VMEM((SW_KV_HEADS, VT_ROWS, STEP_BLOCKS * WINDOW), BF16),
            pltpu.VMEM((2 * SW_KV_HEADS, WINDOW, LANES), BF16),
            pltpu.VMEM((SW_KV_HEADS, VT_ROWS, WINDOW), BF16),
            pltpu.VMEM((D_MODEL, IN_WIDTH), BF16),
            pltpu.VMEM((D_MODEL, D_MODEL), BF16),
            pltpu.VMEM((WEIGHT_SLOTS, WEIGHT_CHUNK_ROWS, IN_WIDTH), F32),
            pltpu.VMEM((WEIGHT_SLOTS, WEIGHT_CHUNK_ROWS, D_MODEL), F32),
            pltpu.SemaphoreType.DMA((WEIGHT_SLOTS,)),
            pltpu.SemaphoreType.DMA((WEIGHT_SLOTS,)),
        ],
    )
    return pl.pallas_call(
        _layer_kernel,
        grid_spec=grid_spec,
        out_shape=jax.ShapeDtypeStruct(x.shape, x.dtype),
        compiler_params=pltpu.CompilerParams(
            dimension_semantics=("arbitrary", "arbitrary"),
            vmem_limit_bytes=VMEM_LIMIT_BYTES),
        name="hybrid_hgrn2_swa_layer",
    )(sinks[0], x, x, w_in, w_out, lb_logits, hg_norm_w, ln_g, ln_b, cos_tab, sin_tab)
```

```python
import jax
import jax.numpy as jnp
from jax import lax
from jax.experimental import pallas as pl
from jax.experimental.pallas import tpu as pltpu

D_MODEL = 1024
DEPTH = 1
HG_WIDTH = 512
HG_HEAD_DIM = 128
HG_HEADS = HG_WIDTH // HG_HEAD_DIM
HG_CHUNK = 64
SW_WIDTH = 512
SW_HEAD_DIM = 64
SW_Q_HEADS = SW_WIDTH // SW_HEAD_DIM
SW_KV_HEADS = SW_Q_HEADS // 4
SW_KV_WIDTH = SW_KV_HEADS * SW_HEAD_DIM
SW_GROUP = SW_Q_HEADS // SW_KV_HEADS
WINDOW = 128
ROPE_THETA = 500000.0
ROPE_DIM = SW_HEAD_DIM // 4
DN_ALPHA = (2.0 * DEPTH) ** 0.25
LN_EPS = 1e-5
RMS_EPS = 1e-6
IN_WIDTH = 4 * HG_WIDTH + SW_WIDTH + 2 * SW_KV_WIDTH + SW_WIDTH

OFF_HQ = 0
OFF_HF = OFF_HQ + HG_WIDTH
OFF_HI = OFF_HF + HG_WIDTH
OFF_HG = OFF_HI + HG_WIDTH
OFF_AQ = OFF_HG + HG_WIDTH
OFF_AK = OFF_AQ + SW_WIDTH
OFF_AV = OFF_AK + SW_KV_WIDTH
OFF_AG = OFF_AV + SW_KV_WIDTH

LANES = 128
BF16_SUBLANES = 16
MXU_WIDTH = 256
HEADS_PER_TILE = LANES // SW_HEAD_DIM
assert HEADS_PER_TILE == 2 and SW_GROUP == 2 * HEADS_PER_TILE and SW_KV_WIDTH == LANES
VT_ROWS = SW_HEAD_DIM + BF16_SUBLANES
TIME_TILE = 256
TILES_PER_STEP = 4
PIECE = MXU_WIDTH
N_IN_PIECES = IN_WIDTH // PIECE
N_OUT_PIECES = D_MODEL // PIECE
assert IN_WIDTH % PIECE == 0 and D_MODEL % PIECE == 0
N_CHUNKS = TIME_TILE // HG_CHUNK
N_BLOCKS = TIME_TILE // WINDOW
STEP_BLOCKS = TILES_PER_STEP * N_BLOCKS
SW_SCORES_AHEAD = 8
WEIGHT_CHUNK_ROWS = 64
WEIGHT_SLOTS = 4
VMEM_LIMIT_BYTES = 58 * 1024 * 1024

F32 = jnp.float32
BF16 = jnp.bfloat16
NT_DIMS = (((1,), (1,)), ((), ()))
TN_DIMS = (((0,), (0,)), ((), ()))


LOG2_E = 1.4426950408889634


def _sigmoid(v):
    return 1.0 / (1.0 + jnp.exp2(v * (-LOG2_E)))


def _load_weight_as_bf16(w_hbm, stage_ref, sem_ref, dst_ref, scale=None):
    n_rows = dst_ref.shape[0]
    n_slots, chunk = stage_ref.shape[0], stage_ref.shape[1]
    n_chunks = n_rows // chunk
    assert n_chunks * chunk == n_rows and n_slots >= 2

    def chunk_copy(i):
        return pltpu.make_async_copy(w_hbm.at[0, pl.ds(i * chunk, chunk), :],
                                     stage_ref.at[i % n_slots], sem_ref.at[i % n_slots])

    for i in range(min(n_slots - 1, n_chunks)):
        chunk_copy(i).start()
    for i in range(n_chunks):
        ahead = i + n_slots - 1
        if ahead < n_chunks:
            chunk_copy(ahead).start()
        chunk_copy(i).wait()
        rows = stage_ref[i % n_slots]
        if scale is not None:
            rows = rows * scale
        dst_ref[i * chunk:(i + 1) * chunk, :] = rows.astype(BF16)


def _layer_kernel(sinks_ref, xc_ref, xn_ref, win_hbm, wout_hbm, lbl_ref, nw_ref, lng_ref, lnb_ref,
                  cos_ref, sin_ref, o_ref,
                  ha_ref, hb_ref, xb_ref, mix_ref, out_ref, st_ref, kwin_ref, vt_ref,
                  kcarry_ref, vcarry_ref, win_ref, wout_ref, win_stage, wout_stage,
                  win_sems, wout_sems):
    b = pl.program_id(0)
    u = pl.program_id(1)
    tt = TIME_TILE
    h_bufs = (ha_ref, hb_ref)

    @pl.when(jnp.logical_and(b == 0, u == 0))
    def _first_step():
        _load_weight_as_bf16(win_hbm, win_stage, win_sems, win_ref)
        _load_weight_as_bf16(wout_hbm, wout_stage, wout_sems, wout_ref, scale=1.0 / DN_ALPHA)
        ha_ref[...] = jnp.dot(xc_ref[0, 0:tt, :].astype(BF16), win_ref[...],
                              preferred_element_type=F32)

    @pl.when(u == 0)
    def _reset_carries():
        st_ref[...] = jnp.zeros_like(st_ref)
        kcarry_ref[...] = jnp.zeros_like(kcarry_ref)
        vcarry_ref[:, 0:SW_HEAD_DIM, :] = jnp.zeros((SW_KV_HEADS, SW_HEAD_DIM, WINDOW), BF16)
        vcarry_ref[:, SW_HEAD_DIM:VT_ROWS, :] = jnp.ones((SW_KV_HEADS, BF16_SUBLANES, WINDOW), BF16)
        vt_ref[:, SW_HEAD_DIM:VT_ROWS, :] = jnp.ones(
            (SW_KV_HEADS, BF16_SUBLANES, STEP_BLOCKS * WINDOW), BF16)

    lbl = lbl_ref[...]
    lbl_e = jnp.exp(lbl - jnp.max(lbl, axis=0, keepdims=True))
    lb = lbl_e[0:1] / jnp.sum(lbl_e, axis=0, keepdims=True)
    nw = nw_ref[...] * (HG_HEAD_DIM ** 0.5)
    row = lax.broadcasted_iota(jnp.int32, (HG_CHUNK, HG_CHUNK), 0)
    col = lax.broadcasted_iota(jnp.int32, (HG_CHUNK, HG_CHUNK), 1)
    tril = col <= row
    cum_mat = tril.astype(BF16)
    cum3 = jnp.concatenate([cum_mat, cum_mat, cum_mat], axis=1)

    lane = lax.broadcasted_iota(jnp.int32, (1, LANES), 1)
    rope_first = (lane % SW_HEAD_DIM) < (ROPE_DIM // 2)
    lane_lo = lane < SW_HEAD_DIM
    key_idx = lax.broadcasted_iota(jnp.int32, (2 * WINDOW, 2 * WINDOW), 0)
    qry_idx = lax.broadcasted_iota(jnp.int32, (2 * WINDOW, 2 * WINDOW), 1) % WINDOW
    band = (key_idx > qry_idx) & (key_idx <= qry_idx + WINDOW)
    pair_lo = lax.broadcasted_iota(jnp.int32, (1, 2 * WINDOW), 1) < WINDOW
    scale = SW_HEAD_DIM ** -0.5 * LOG2_E
    first_valid = jnp.where(u > 0, 0, WINDOW)
    band_cap = jnp.where(band, jnp.inf, -jnp.inf)
    first_cap = jnp.where(band & (key_idx >= first_valid), jnp.inf, -jnp.inf)

    def rope(v, cos, sin):
        partner = jnp.where(rope_first,
                            pltpu.roll(v, LANES - ROPE_DIM // 2, 1),
                            pltpu.roll(v, ROPE_DIM // 2, 1))
        return v * cos + partner * sin

    def head_variants(v):
        sw = pltpu.roll(v, SW_HEAD_DIM, 1)
        zero = jnp.zeros_like(v)
        return (jnp.where(lane_lo, v, zero), jnp.where(lane_lo, zero, sw),
                jnp.where(lane_lo, sw, zero), jnp.where(lane_lo, zero, v))

    def in_projection_piece(k, piece):
        xb = xb_ref.at[k % 2]
        cols = slice(piece * PIECE, (piece + 1) * PIECE)
        h_bufs[k % 2][:, cols] = jnp.dot(xb[...], win_ref[:, cols], preferred_element_type=F32)

    def out_projection_piece(k, piece):
        cols = slice(piece * PIECE, (piece + 1) * PIECE)
        out_ref[:, cols] = jnp.dot(mix_ref[k], wout_ref[:, cols], preferred_element_type=F32)

    def post_norm(k):
        rows = slice(k * tt, (k + 1) * tt)
        z = xc_ref[0, rows, :] + out_ref[...]
        mu = jnp.mean(z, axis=-1, keepdims=True)
        zc = z - mu
        var = jnp.mean(zc * zc, axis=-1, keepdims=True)
        o_ref[0, rows, :] = ((zc * lax.rsqrt(var + LN_EPS / DN_ALPHA ** 2)) * lng_ref[...]
                             + lnb_ref[...])

    def make_fillers(k):
        stages = []
        if k > 0:
            for piece in range(N_OUT_PIECES):
                stages.append(lambda piece=piece: out_projection_piece(k - 1, piece))
            stages.append(lambda: post_norm(k - 1))
        for piece in range(N_IN_PIECES):
            stages.append(lambda piece=piece: in_projection_piece(k + 1, piece))
        return iter(stages)

    def process_tile(k):
        h_ref = h_bufs[k % 2]
        base = k * tt
        mix = mix_ref.at[k]
        if k + 1 < TILES_PER_STEP:
            x_next_rows = xc_ref[0, (k + 1) * tt:(k + 2) * tt, :]
        else:
            x_next_rows = xn_ref[0]
        xb_ref[(k + 1) % 2] = x_next_rows.astype(BF16)
        fillers = make_fillers(k)

        def fill(n=1):
            for _ in range(n):
                stage = next(fillers, None)
                if stage is not None:
                    stage()

        def hg_pre(c):
            rows = slice(c * HG_CHUNK, (c + 1) * HG_CHUNK)
            f = lb + (1.0 - lb) * _sigmoid(h_ref[rows, OFF_HF:OFF_HF + HG_WIDTH])
            log_f = jnp.log2(f)
            hi = log_f.astype(BF16)
            rem = log_f - hi.astype(F32)
            mid = rem.astype(BF16)
            lo = (rem - mid.astype(F32)).astype(BF16)
            return 1.0 - f, jnp.concatenate([hi, mid, lo], axis=0)

        def hg_cum(parts):
            return jnp.dot(cum3, parts, preferred_element_type=F32)

        def hg_decays(c, k_in, g_cum):
            rows = slice(c * HG_CHUNK, (c + 1) * HG_CHUNK)
            hq = h_ref[rows, OFF_HQ:OFF_HQ + HG_WIDTH]
            g_last = g_cum[HG_CHUNK - 1:HG_CHUNK, :]
            q_dec = ((hq * _sigmoid(hq)) * jnp.exp2(g_cum)).astype(BF16)
            k_dec = (k_in * jnp.exp2(-g_cum)).astype(BF16)
            k_tail = (k_in * jnp.exp2(g_last - g_cum)).astype(BF16)
            return q_dec, k_dec, k_tail, jnp.exp2(g_last)

        def hg_scores(q_dec, k_dec, hd):
            sl = slice(hd * HG_HEAD_DIM, (hd + 1) * HG_HEAD_DIM)
            return lax.dot_general(q_dec[:, sl], k_dec[:, sl], NT_DIMS, preferred_element_type=F32)

        def hg_output(c, hd, a, q_dec, k_tail, decay):
            rows = slice(c * HG_CHUNK, (c + 1) * HG_CHUNK)
            sl = slice(hd * HG_HEAD_DIM, (hd + 1) * HG_HEAD_DIM)
            v = h_ref[rows, OFF_HI + hd * HG_HEAD_DIM:OFF_HI + (hd + 1) * HG_HEAD_DIM].astype(BF16)
            s_t = st_ref[hd]
            o = jnp.dot(jnp.where(tril, a, 0.0).astype(BF16), v, preferred_element_type=F32)
            o = o + jnp.dot(q_dec[:, sl], s_t.T.astype(BF16), preferred_element_type=F32)
            inc_t = lax.dot_general(v, k_tail[:, sl], TN_DIMS, preferred_element_type=F32)
            st_ref[hd] = s_t * decay[:, sl] + inc_t
            o = o * lax.rsqrt(jnp.sum(o * o, axis=-1, keepdims=True) + HG_HEAD_DIM * RMS_EPS)
            o = o * nw[:, sl]
            hg = h_ref[rows, OFF_HG + hd * HG_HEAD_DIM:OFF_HG + (hd + 1) * HG_HEAD_DIM]
            mix[rows, sl] = (o * (hg * _sigmoid(hg))).astype(BF16)

        def sw_prepare(blk):
            r0 = blk * WINDOW
            rows = slice(r0, r0 + WINDOW)
            cur = slice(base + r0, base + r0 + WINDOW)
            cos = cos_ref[base + r0:base + r0 + WINDOW, :]
            sin = sin_ref[base + r0:base + r0 + WINDOW, :]
            k_rot = rope(h_ref[rows, OFF_AK:OFF_AK + SW_KV_WIDTH], cos, sin)
            for idx, kv in enumerate(head_variants(k_rot)):
                kwin_ref[idx, cur, :] = kv.astype(BF16)
            v_t = h_ref[rows, OFF_AV:OFF_AV + SW_KV_WIDTH].T
            for g in range(SW_KV_HEADS):
                vt_ref[g, 0:SW_HEAD_DIM, cur] = (
                    v_t[g * SW_HEAD_DIM:(g + 1) * SW_HEAD_DIM].astype(BF16))
            q_bf = []
            for tile in range(SW_WIDTH // LANES):
                q_rot = rope(h_ref[rows, OFF_AQ + tile * LANES:OFF_AQ + (tile + 1) * LANES], cos, sin)
                q_bf.append((q_rot * scale).astype(BF16))
            return [jnp.concatenate([q_bf[2 * g], q_bf[2 * g + 1]], axis=0)
                    for g in range(SW_KV_HEADS)]

        def key_window(blk, idx):
            r0 = base + blk * WINDOW
            if r0 == 0:
                return jnp.concatenate([kcarry_ref[idx], kwin_ref[idx, 0:WINDOW, :]], axis=0)
            return kwin_ref[idx, r0 - WINDOW:r0 + WINDOW, :]

        def value_window(blk, g):
            r0 = base + blk * WINDOW
            if r0 == 0:
                return jnp.concatenate([vcarry_ref[g], vt_ref[g, :, 0:WINDOW]], axis=1)
            return vt_ref[g, :, r0 - WINDOW:r0 + WINDOW]

        def sw_scores(blk, g, j, q_pair):
            return lax.dot_general(key_window(blk, 2 * g + j), q_pair, NT_DIMS,
                                   preferred_element_type=F32)

        def sw_values(blk, g, j, s_t, cap):
            sink_row = jnp.where(pair_lo, sinks_ref[SW_GROUP * g + j],
                                 sinks_ref[SW_GROUP * g + HEADS_PER_TILE + j]) * LOG2_E
            s_t = jnp.minimum(s_t, cap)
            m = jnp.maximum(jnp.max(s_t, axis=0, keepdims=True), sink_row)
            p_t = jnp.exp2(s_t - m).astype(BF16)
            acc = jnp.dot(value_window(blk, g), p_t, preferred_element_type=F32)
            denom = acc[SW_HEAD_DIM:SW_HEAD_DIM + 1, :] + jnp.exp2(sink_row - m)
            return acc[0:SW_HEAD_DIM, :] * (1.0 / denom)

        def sw_finish(blk, g, o_t):
            rows = slice(blk * WINDOW, (blk + 1) * WINDOW)
            for tl in range(2):
                tile = 2 * g + tl
                lanes = slice(tl * LANES, (tl + 1) * LANES)
                o_a = jnp.concatenate([o_t[0][:, lanes], o_t[1][:, lanes]], axis=0).T
                ag = h_ref[rows, OFF_AG + tile * LANES:OFF_AG + (tile + 1) * LANES]
                mix[rows, HG_WIDTH + tile * LANES:HG_WIDTH + (tile + 1) * LANES] = (
                    o_a * (ag * _sigmoid(ag))).astype(BF16)

        many = k > 0
        masks = [first_cap if (k == 0 and blk == 0) else band_cap for blk in range(N_BLOCKS)]
        chains = [(blk, g, j) for blk in range(N_BLOCKS) for g in range(SW_KV_HEADS)
                  for j in range(HEADS_PER_TILE)]
        q_pairs = {}
        scores = {}

        def sw_issue_scores(i):
            if i < len(chains) and i not in scores:
                blk, g, j = chains[i]
                if blk not in q_pairs:
                    q_pairs[blk] = sw_prepare(blk)
                scores[i] = sw_scores(blk, g, j, q_pairs[blk][g])

        pre = {}
        cum = {}
        fill(2 if many else 1)
        for c in range(min(2, N_CHUNKS)):
            pre[c] = hg_pre(c)
            cum[c] = hg_cum(pre[c][1])
            fill(1 if many else 0)
        for c in range(N_CHUNKS):
            q_dec, k_dec, k_tail, decay = hg_decays(c, pre[c][0], cum[c])
            hg_s = [hg_scores(q_dec, k_dec, hd) for hd in range(HG_HEADS)]
            if c + 2 < N_CHUNKS:
                pre[c + 2] = hg_pre(c + 2)
                cum[c + 2] = hg_cum(pre[c + 2][1])
            fill()
            for hd in range(HG_HEADS):
                hg_output(c, hd, hg_s[hd], q_dec, k_tail, decay)
            fill(1 if many else 0)

        for i in range(min(SW_SCORES_AHEAD, len(chains))):
            sw_issue_scores(i)
        fill()
        outs = {}
        for i, (blk, g, j) in enumerate(chains):
            outs[(blk, g, j)] = sw_values(blk, g, j, scores.pop(i), masks[blk])
            sw_issue_scores(i + SW_SCORES_AHEAD)
            if j == HEADS_PER_TILE - 1:
                sw_finish(blk, g, [outs.pop((blk, g, jj)) for jj in range(HEADS_PER_TILE)])
            fill()
        fill(N_OUT_PIECES + 1 + N_IN_PIECES)

    for k in range(TILES_PER_STEP):
        process_tile(k)
    for piece in range(N_OUT_PIECES):
        out_projection_piece(TILES_PER_STEP - 1, piece)
    post_norm(TILES_PER_STEP - 1)

    last = slice((STEP_BLOCKS - 1) * WINDOW, STEP_BLOCKS * WINDOW)
    for idx in range(2 * SW_KV_HEADS):
        kcarry_ref[idx] = kwin_ref[idx, last, :]
    for g in range(SW_KV_HEADS):
        vcarry_ref[g, 0:SW_HEAD_DIM, :] = vt_ref[g, 0:SW_HEAD_DIM, last]


def _rope_tables(seq_len):
    pos = jnp.arange(seq_len, dtype=F32)
    inv_freq = ROPE_THETA ** (-jnp.arange(0, ROPE_DIM, 2, dtype=F32) / ROPE_DIM)
    ang = pos[:, None] * inv_freq[None, :]
    cos = jnp.cos(ang)
    sin = jnp.sin(ang)
    ones = jnp.ones((seq_len, SW_HEAD_DIM - ROPE_DIM), F32)
    cos_head = jnp.concatenate([cos, cos, ones], axis=1)
    sin_head = jnp.concatenate([-sin, sin, jnp.zeros_like(ones)], axis=1)
    return (jnp.tile(cos_head, (1, HEADS_PER_TILE)), jnp.tile(sin_head, (1, HEADS_PER_TILE)))


def kernel(x, w_in, lb_logits, hg_norm_w, sinks, w_out, ln_g, ln_b):
    batch, seq_len, d_model = x.shape
    assert d_model == D_MODEL and w_in.shape == (DEPTH, D_MODEL, IN_WIDTH)
    step_rows = TILES_PER_STEP * TIME_TILE
    assert seq_len % step_rows == 0
    cos_tab, sin_tab = _rope_tables(seq_len)
    n_steps = seq_len // step_rows
    last_step = batch * n_steps - 1

    def next_tile_index(b, u, *_):
        nxt = jnp.minimum(b * n_steps + u + 1, last_step)
        return (nxt // n_steps, TILES_PER_STEP * (nxt % n_steps), 0)

    const = lambda b, u, *_: (0, 0)
    grid_spec = pltpu.PrefetchScalarGridSpec(
        num_scalar_prefetch=1,
        grid=(batch, n_steps),
        in_specs=[
            pl.BlockSpec((1, step_rows, D_MODEL), lambda b, u, *_: (b, u, 0)),
            pl.BlockSpec((1, TIME_TILE, D_MODEL), next_tile_index),
            pl.BlockSpec(memory_space=pl.ANY),
            pl.BlockSpec(memory_space=pl.ANY),
            pl.BlockSpec((DEPTH + 1, HG_WIDTH), const),
            pl.BlockSpec((1, HG_WIDTH), const),
            pl.BlockSpec((1, D_MODEL), const),
            pl.BlockSpec((1, D_MODEL), const),
            pl.BlockSpec((step_rows, LANES), lambda b, u, *_: (u, 0)),
            pl.BlockSpec((step_rows, LANES), lambda b, u, *_: (u, 0)),
        ],
        out_specs=pl.BlockSpec((1, step_rows, D_MODEL), lambda b, u, *_: (b, u, 0)),
        scratch_shapes=[
            pltpu.VMEM((TIME_TILE, IN_WIDTH), F32),
            pltpu.VMEM((TIME_TILE, IN_WIDTH), F32),
            pltpu.VMEM((2, TIME_TILE, D_MODEL), BF16),
            pltpu.VMEM((TILES_PER_STEP, TIME_TILE, D_MODEL), BF16),
            pltpu.VMEM((TIME_TILE, D_MODEL), F32),
            pltpu.VMEM((HG_HEADS, HG_HEAD_DIM, HG_HEAD_DIM), F32),
            pltpu.VMEM((2 * SW_KV_HEADS, STEP_BLOCKS * WINDOW, LANES), BF16),
            pltpu.VMEM((SW_KV_HEADS, VT_ROWS, STEP_BLOCKS * WINDOW), BF16),
            pltpu.VMEM((2 * SW_KV_HEADS, WINDOW, LANES), BF16),
            pltpu.VMEM((SW_KV_HEADS, VT_ROWS, WINDOW), BF16),
            pltpu.VMEM((D_MODEL, IN_WIDTH), BF16),
            pltpu.VMEM((D_MODEL, D_MODEL), BF16),
            pltpu.VMEM((WEIGHT_SLOTS, WEIGHT_CHUNK_ROWS, IN_WIDTH), F32),
            pltpu.VMEM((WEIGHT_SLOTS, WEIGHT_CHUNK_ROWS, D_MODEL), F32),
            pltpu.SemaphoreType.DMA((WEIGHT_SLOTS,)),
            pltpu.SemaphoreType.DMA((WEIGHT_SLOTS,)),
        ],
    )
    return pl.pallas_call(
        _layer_kernel,
        grid_spec=grid_spec,
        out_shape=jax.ShapeDtypeStruct(x.shape, x.dtype),
        compiler_params=pltpu.CompilerParams(
            dimension_semantics=("arbitrary", "arbitrary"),
            vmem_limit_bytes=VMEM_LIMIT_BYTES),
        name="hybrid_hgrn2_swa_layer",
    )(sinks[0], x, x, w_in, w_out, lb_logits, hg_norm_w, ln_g, ln_b, cos_tab, sin_tab)
```

```python
import jax
import jax.numpy as jnp
from jax import lax
from jax.experimental import pallas as pl
from jax.experimental.pallas import tpu as pltpu

D_MODEL = 1024
DEPTH = 1
HG_WIDTH = 512
HG_HEAD_DIM = 128
HG_HEADS = HG_WIDTH // HG_HEAD_DIM
HG_CHUNK = 64
SW_WIDTH = 512
SW_HEAD_DIM = 64
SW_Q_HEADS = SW_WIDTH // SW_HEAD_DIM
SW_KV_HEADS = SW_Q_HEADS // 4
SW_KV_WIDTH = SW_KV_HEADS * SW_HEAD_DIM
SW_GROUP = SW_Q_HEADS // SW_KV_HEADS
WINDOW = 128
ROPE_THETA = 500000.0
ROPE_DIM = SW_HEAD_DIM // 4
DN_ALPHA = (2.0 * DEPTH) ** 0.25
LN_EPS = 1e-5
RMS_EPS = 1e-6
IN_WIDTH = 4 * HG_WIDTH + SW_WIDTH + 2 * SW_KV_WIDTH + SW_WIDTH

OFF_HQ = 0
OFF_HF = OFF_HQ + HG_WIDTH
OFF_HI = OFF_HF + HG_WIDTH
OFF_HG = OFF_HI + HG_WIDTH
OFF_AQ = OFF_HG + HG_WIDTH
OFF_AK = OFF_AQ + SW_WIDTH
OFF_AV = OFF_AK + SW_KV_WIDTH
OFF_AG = OFF_AV + SW_KV_WIDTH

LANES = 128
BF16_SUBLANES = 16
MXU_WIDTH = 256
HEADS_PER_TILE = LANES // SW_HEAD_DIM
assert HEADS_PER_TILE == 2 and SW_GROUP == 2 * HEADS_PER_TILE and SW_KV_WIDTH == LANES
VT_ROWS = SW_HEAD_DIM + BF16_SUBLANES
TIME_TILE = 256
TILES_PER_STEP = 4
PIECE = MXU_WIDTH
N_IN_PIECES = IN_WIDTH // PIECE
N_OUT_PIECES = D_MODEL // PIECE
assert IN_WIDTH % PIECE == 0 and D_MODEL % PIECE == 0
N_CHUNKS = TIME_TILE // HG_CHUNK
N_BLOCKS = TIME_TILE // WINDOW
STEP_BLOCKS = TILES_PER_STEP * N_BLOCKS
HEAD_CHUNKS = 2
SW_SCORES_AHEAD = 8
WEIGHT_CHUNK_ROWS = 64
WEIGHT_SLOTS = 4
VMEM_LIMIT_BYTES = 58 * 1024 * 1024

F32 = jnp.float32
BF16 = jnp.bfloat16
NT_DIMS = (((1,), (1,)), ((), ()))
TN_DIMS = (((0,), (0,)), ((), ()))


LOG2_E = 1.4426950408889634


def _sigmoid(v):
    return 1.0 / (1.0 + jnp.exp2(v * (-LOG2_E)))


def _load_weight_as_bf16(w_hbm, stage_ref, sem_ref, dst_ref, scale=None):
    n_rows = dst_ref.shape[0]
    n_slots, chunk = stage_ref.shape[0], stage_ref.shape[1]
    n_chunks = n_rows // chunk
    assert n_chunks * chunk == n_rows and n_slots >= 2

    def chunk_copy(i):
        return pltpu.make_async_copy(w_hbm.at[0, pl.ds(i * chunk, chunk), :],
                                     stage_ref.at[i % n_slots], sem_ref.at[i % n_slots])

    for i in range(min(n_slots - 1, n_chunks)):
        chunk_copy(i).start()
    for i in range(n_chunks):
        ahead = i + n_slots - 1
        if ahead < n_chunks:
            chunk_copy(ahead).start()
        chunk_copy(i).wait()
        rows = stage_ref[i % n_slots]
        if scale is not None:
            rows = rows * scale
        dst_ref[i * chunk:(i + 1) * chunk, :] = rows.astype(BF16)


def _layer_kernel(sinks_ref, xc_ref, xn_ref, win_hbm, wout_hbm, lbl_ref, nw_ref, lng_ref, lnb_ref,
                  cos_ref, sin_ref, o_ref,
                  ha_ref, hb_ref, xb_ref, mix_ref, out_ref, st_ref, kwin_ref, vt_ref,
                  kcarry_ref, vcarry_ref, win_ref, wout_ref, win_stage, wout_stage,
                  win_sems, wout_sems):
    b = pl.program_id(0)
    u = pl.program_id(1)
    tt = TIME_TILE
    h_bufs = (ha_ref, hb_ref)

    @pl.when(jnp.logical_and(b == 0, u == 0))
    def _first_step():
        _load_weight_as_bf16(win_hbm, win_stage, win_sems, win_ref)
        _load_weight_as_bf16(wout_hbm, wout_stage, wout_sems, wout_ref, scale=1.0 / DN_ALPHA)
        ha_ref[...] = jnp.dot(xc_ref[0, 0:tt, :].astype(BF16), win_ref[...],
                              preferred_element_type=F32)

    @pl.when(u == 0)
    def _reset_carries():
        st_ref[...] = jnp.zeros_like(st_ref)
        kcarry_ref[...] = jnp.zeros_like(kcarry_ref)
        vcarry_ref[:, 0:SW_HEAD_DIM, :] = jnp.zeros((SW_KV_HEADS, SW_HEAD_DIM, WINDOW), BF16)
        vcarry_ref[:, SW_HEAD_DIM:VT_ROWS, :] = jnp.ones((SW_KV_HEADS, BF16_SUBLANES, WINDOW), BF16)
        vt_ref[:, SW_HEAD_DIM:VT_ROWS, :] = jnp.ones(
            (SW_KV_HEADS, BF16_SUBLANES, STEP_BLOCKS * WINDOW), BF16)

    lbl = lbl_ref[...]
    lbl_e = jnp.exp(lbl - jnp.max(lbl, axis=0, keepdims=True))
    lb = lbl_e[0:1] / jnp.sum(lbl_e, axis=0, keepdims=True)
    nw = nw_ref[...] * (HG_HEAD_DIM ** 0.5)
    row = lax.broadcasted_iota(jnp.int32, (HG_CHUNK, HG_CHUNK), 0)
    col = lax.broadcasted_iota(jnp.int32, (HG_CHUNK, HG_CHUNK), 1)
    tril = col <= row
    cum_mat = tril.astype(BF16)
    cum3 = jnp.concatenate([cum_mat, cum_mat, cum_mat], axis=1)

    lane = lax.broadcasted_iota(jnp.int32, (1, LANES), 1)
    rope_first = (lane % SW_HEAD_DIM) < (ROPE_DIM // 2)
    lane_lo = lane < SW_HEAD_DIM
    key_idx = lax.broadcasted_iota(jnp.int32, (2 * WINDOW, 2 * WINDOW), 0)
    qry_idx = lax.broadcasted_iota(jnp.int32, (2 * WINDOW, 2 * WINDOW), 1) % WINDOW
    band = (key_idx > qry_idx) & (key_idx <= qry_idx + WINDOW)
    pair_lo = lax.broadcasted_iota(jnp.int32, (1, 2 * WINDOW), 1) < WINDOW
    scale = SW_HEAD_DIM ** -0.5 * LOG2_E
    first_valid = jnp.where(u > 0, 0, WINDOW)
    band_cap = jnp.where(band, jnp.inf, -jnp.inf)
    first_cap = jnp.where(band & (key_idx >= first_valid), jnp.inf, -jnp.inf)

    def rope(v, cos, sin):
        partner = jnp.where(rope_first,
                            pltpu.roll(v, LANES - ROPE_DIM // 2, 1),
                            pltpu.roll(v, ROPE_DIM // 2, 1))
        return v * cos + partner * sin

    def head_variants(v):
        sw = pltpu.roll(v, SW_HEAD_DIM, 1)
        zero = jnp.zeros_like(v)
        return (jnp.where(lane_lo, v, zero), jnp.where(lane_lo, zero, sw),
                jnp.where(lane_lo, sw, zero), jnp.where(lane_lo, zero, v))

    def in_projection_piece(k, piece):
        xb = xb_ref.at[k % 2]
        cols = slice(piece * PIECE, (piece + 1) * PIECE)
        h_bufs[k % 2][:, cols] = jnp.dot(xb[...], win_ref[:, cols], preferred_element_type=F32)

    def out_projection_piece(k, piece):
        cols = slice(piece * PIECE, (piece + 1) * PIECE)
        out_ref[:, cols] = jnp.dot(mix_ref[k], wout_ref[:, cols], preferred_element_type=F32)

    def post_norm(k):
        rows = slice(k * tt, (k + 1) * tt)
        z = xc_ref[0, rows, :] + out_ref[...]
        mu = jnp.mean(z, axis=-1, keepdims=True)
        zc = z - mu
        var = jnp.mean(zc * zc, axis=-1, keepdims=True)
        o_ref[0, rows, :] = ((zc * lax.rsqrt(var + LN_EPS / DN_ALPHA ** 2)) * lng_ref[...]
                             + lnb_ref[...])

    def make_fillers(k):
        stages = []
        if k > 0:
            for piece in range(N_OUT_PIECES):
                stages.append(lambda piece=piece: out_projection_piece(k - 1, piece))
            stages.append(lambda: post_norm(k - 1))
        for piece in range(N_IN_PIECES):
            def in_stage(piece=piece):
                in_projection_piece(k + 1, piece)
                projected[k + 1] = piece + 1
            stages.append(in_stage)
        return iter(stages)

    projected = {}

    def process_tile(k, head):
        h_ref = h_bufs[k % 2]
        base = k * tt
        mix = mix_ref.at[k]
        if k + 1 < TILES_PER_STEP:
            x_next_rows = xc_ref[0, (k + 1) * tt:(k + 2) * tt, :]
        else:
            x_next_rows = xn_ref[0]
        xb_ref[(k + 1) % 2] = x_next_rows.astype(BF16)
        fillers = make_fillers(k)

        def fill(n=1):
            for _ in range(n):
                stage = next(fillers, None)
                if stage is not None:
                    stage()

        def hg_pre(c, h=h_ref):
            rows = slice(c * HG_CHUNK, (c + 1) * HG_CHUNK)
            f = lb + (1.0 - lb) * _sigmoid(h[rows, OFF_HF:OFF_HF + HG_WIDTH])
            log_f = jnp.log2(f)
            hi = log_f.astype(BF16)
            rem = log_f - hi.astype(F32)
            mid = rem.astype(BF16)
            lo = (rem - mid.astype(F32)).astype(BF16)
            return 1.0 - f, jnp.concatenate([hi, mid, lo], axis=0)

        def hg_cum(parts):
            return jnp.dot(cum3, parts, preferred_element_type=F32)

        def hg_decays(c, k_in, g_cum, h=h_ref):
            rows = slice(c * HG_CHUNK, (c + 1) * HG_CHUNK)
            hq = h[rows, OFF_HQ:OFF_HQ + HG_WIDTH]
            g_last = g_cum[HG_CHUNK - 1:HG_CHUNK, :]
            q_dec = ((hq * _sigmoid(hq)) * jnp.exp2(g_cum)).astype(BF16)
            k_dec = (k_in * jnp.exp2(-g_cum)).astype(BF16)
            k_tail = (k_in * jnp.exp2(g_last - g_cum)).astype(BF16)
            return q_dec, k_dec, k_tail, jnp.exp2(g_last)

        def hg_scores(q_dec, k_dec, hd):
            sl = slice(hd * HG_HEAD_DIM, (hd + 1) * HG_HEAD_DIM)
            return lax.dot_general(q_dec[:, sl], k_dec[:, sl], NT_DIMS, preferred_element_type=F32)

        def hg_output(c, hd, a, q_dec, k_tail, decay):
            rows = slice(c * HG_CHUNK, (c + 1) * HG_CHUNK)
            sl = slice(hd * HG_HEAD_DIM, (hd + 1) * HG_HEAD_DIM)
            v = h_ref[rows, OFF_HI + hd * HG_HEAD_DIM:OFF_HI + (hd + 1) * HG_HEAD_DIM].astype(BF16)
            s_t = st_ref[hd]
            o = jnp.dot(jnp.where(tril, a, 0.0).astype(BF16), v, preferred_element_type=F32)
            o = o + lax.dot_general(q_dec[:, sl], s_t.astype(BF16), NT_DIMS,
                                    preferred_element_type=F32)
            inc_t = lax.dot_general(v, k_tail[:, sl], TN_DIMS, preferred_element_type=F32)
            st_ref[hd] = s_t * decay[:, sl] + inc_t
            o = o * lax.rsqrt(jnp.sum(o * o, axis=-1, keepdims=True) + HG_HEAD_DIM * RMS_EPS)
            o = o * nw[:, sl]
            hg = h_ref[rows, OFF_HG + hd * HG_HEAD_DIM:OFF_HG + (hd + 1) * HG_HEAD_DIM]
            mix[rows, sl] = (o * (hg * _sigmoid(hg))).astype(BF16)

        def sw_prepare(blk):
            r0 = blk * WINDOW
            rows = slice(r0, r0 + WINDOW)
            cur = slice(base + r0, base + r0 + WINDOW)
            cos = cos_ref[base + r0:base + r0 + WINDOW, :]
            sin = sin_ref[base + r0:base + r0 + WINDOW, :]
            k_rot = rope(h_ref[rows, OFF_AK:OFF_AK + SW_KV_WIDTH], cos, sin)
            for idx, kv in enumerate(head_variants(k_rot)):
                kwin_ref[idx, cur, :] = kv.astype(BF16)
            v_t = h_ref[rows, OFF_AV:OFF_AV + SW_KV_WIDTH].T
            for g in range(SW_KV_HEADS):
                vt_ref[g, 0:SW_HEAD_DIM, cur] = (
                    v_t[g * SW_HEAD_DIM:(g + 1) * SW_HEAD_DIM].astype(BF16))
            q_bf = []
            for tile in range(SW_WIDTH // LANES):
                q_rot = rope(h_ref[rows, OFF_AQ + tile * LANES:OFF_AQ + (tile + 1) * LANES], cos, sin)
                q_bf.append((q_rot * scale).astype(BF16))
            return [jnp.concatenate([q_bf[2 * g], q_bf[2 * g + 1]], axis=0)
                    for g in range(SW_KV_HEADS)]

        def key_window(blk, idx):
            r0 = base + blk * WINDOW
            if r0 == 0:
                return jnp.concatenate([kcarry_ref[idx], kwin_ref[idx, 0:WINDOW, :]], axis=0)
            return kwin_ref[idx, r0 - WINDOW:r0 + WINDOW, :]

        def value_window(blk, g):
            r0 = base + blk * WINDOW
            if r0 == 0:
                return jnp.concatenate([vcarry_ref[g], vt_ref[g, :, 0:WINDOW]], axis=1)
            return vt_ref[g, :, r0 - WINDOW:r0 + WINDOW]

        def sw_scores(blk, g, j, q_pair):
            return lax.dot_general(key_window(blk, 2 * g + j), q_pair, NT_DIMS,
                                   preferred_element_type=F32)

        def sw_values(blk, g, j, s_t, cap):
            sink_row = jnp.where(pair_lo, sinks_ref[SW_GROUP * g + j],
                                 sinks_ref[SW_GROUP * g + HEADS_PER_TILE + j]) * LOG2_E
            s_t = jnp.minimum(s_t, cap)
            m = jnp.maximum(jnp.max(s_t, axis=0, keepdims=True), sink_row)
            p_t = jnp.exp2(s_t - m).astype(BF16)
            acc = jnp.dot(value_window(blk, g), p_t, preferred_element_type=F32)
            denom = acc[SW_HEAD_DIM:SW_HEAD_DIM + 1, :] + jnp.exp2(sink_row - m)
            return acc[0:SW_HEAD_DIM, :] * (1.0 / denom)

        def sw_finish(blk, g, o_t):
            rows = slice(blk * WINDOW, (blk + 1) * WINDOW)
            for tl in range(2):
                tile = 2 * g + tl
                lanes = slice(tl * LANES, (tl + 1) * LANES)
                o_a = jnp.concatenate([o_t[0][:, lanes], o_t[1][:, lanes]], axis=0).T
                ag = h_ref[rows, OFF_AG + tile * LANES:OFF_AG + (tile + 1) * LANES]
                mix[rows, HG_WIDTH + tile * LANES:HG_WIDTH + (tile + 1) * LANES] = (
                    o_a * (ag * _sigmoid(ag))).astype(BF16)

        many = k > 0
        masks = [first_cap if (k == 0 and blk == 0) else band_cap for blk in range(N_BLOCKS)]
        chains = [(blk, g, j) for blk in range(N_BLOCKS) for g in range(SW_KV_HEADS)
                  for j in range(HEADS_PER_TILE)]
        q_pairs = {}
        scores = {}

        def sw_issue_scores(i):
            if i < len(chains) and i not in scores:
                blk, g, j = chains[i]
                if blk not in q_pairs:
                    q_pairs[blk] = sw_prepare(blk)
                scores[i] = sw_scores(blk, g, j, q_pairs[blk][g])

        pre = {}
        cum = {}
        fill(2 if many else 1)
        for c in range(min(HEAD_CHUNKS, N_CHUNKS)):
            if head is not None:
                pre[c], cum[c] = head[c]
            else:
                pre[c] = hg_pre(c)
                cum[c] = hg_cum(pre[c][1])
            fill(1 if many else 0)
        for c in range(N_CHUNKS):
            if c == 0 and head is not None:
                (q_dec, k_dec, k_tail, decay), hg_s = head["first"]
            else:
                q_dec, k_dec, k_tail, decay = hg_decays(c, pre[c][0], cum[c])
                hg_s = [hg_scores(q_dec, k_dec, hd) for hd in range(HG_HEADS)]
            if c + 2 < N_CHUNKS:
                pre[c + 2] = hg_pre(c + 2)
                cum[c + 2] = hg_cum(pre[c + 2][1])
            fill()
            for hd in range(HG_HEADS):
                hg_output(c, hd, hg_s[hd], q_dec, k_tail, decay)
            fill(1 if many else 0)

        for i in range(min(SW_SCORES_AHEAD, len(chains))):
            sw_issue_scores(i)
        fill()
        outs = {}
        next_head = None
        for i, (blk, g, j) in enumerate(chains):
            outs[(blk, g, j)] = sw_values(blk, g, j, scores.pop(i), masks[blk])
            sw_issue_scores(i + SW_SCORES_AHEAD)
            if j == HEADS_PER_TILE - 1:
                sw_finish(blk, g, [outs.pop((blk, g, jj)) for jj in range(HEADS_PER_TILE)])
            fill()
            if i == len(chains) // 2 - 1 and k + 1 < TILES_PER_STEP:
                assert projected.get(k + 1, 0) * PIECE >= OFF_HF + HG_WIDTH
                next_head = {}
                for c in range(min(HEAD_CHUNKS, N_CHUNKS)):
                    p = hg_pre(c, h_bufs[(k + 1) % 2])
                    next_head[c] = (p, hg_cum(p[1]))
                d0 = hg_decays(0, next_head[0][0][0], next_head[0][1], h_bufs[(k + 1) % 2])
                next_head["first"] = (d0, [hg_scores(d0[0], d0[1], hd) for hd in range(HG_HEADS)])
        fill(N_OUT_PIECES + 1 + N_IN_PIECES)
        return next_head

    head = None
    for k in range(TILES_PER_STEP):
        head = process_tile(k, head)
    for piece in range(N_OUT_PIECES):
        out_projection_piece(TILES_PER_STEP - 1, piece)
    post_norm(TILES_PER_STEP - 1)

    last = slice((STEP_BLOCKS - 1) * WINDOW, STEP_BLOCKS * WINDOW)
    for idx in range(2 * SW_KV_HEADS):
        kcarry_ref[idx] = kwin_ref[idx, last, :]
    for g in range(SW_KV_HEADS):
        vcarry_ref[g, 0:SW_HEAD_DIM, :] = vt_ref[g, 0:SW_HEAD_DIM, last]


def _rope_tables(seq_len):
    pos = jnp.arange(seq_len, dtype=F32)
    inv_freq = ROPE_THETA ** (-jnp.arange(0, ROPE_DIM, 2, dtype=F32) / ROPE_DIM)
    ang = pos[:, None] * inv_freq[None, :]
    cos = jnp.cos(ang)
    sin = jnp.sin(ang)
    ones = jnp.ones((seq_len, SW_HEAD_DIM - ROPE_DIM), F32)
    cos_head = jnp.concatenate([cos, cos, ones], axis=1)
    sin_head = jnp.concatenate([-sin, sin, jnp.zeros_like(ones)], axis=1)
    return (jnp.tile(cos_head, (1, HEADS_PER_TILE)), jnp.tile(sin_head, (1, HEADS_PER_TILE)))


def kernel(x, w_in, lb_logits, hg_norm_w, sinks, w_out, ln_g, ln_b):
    batch, seq_len, d_model = x.shape
    assert d_model == D_MODEL and w_in.shape == (DEPTH, D_MODEL, IN_WIDTH)
    step_rows = TILES_PER_STEP * TIME_TILE
    assert seq_len % step_rows == 0
    cos_tab, sin_tab = _rope_tables(seq_len)
    n_steps = seq_len // step_rows
    last_step = batch * n_steps - 1

    def next_tile_index(b, u, *_):
        nxt = jnp.minimum(b * n_steps + u + 1, last_step)
        return (nxt // n_steps, TILES_PER_STEP * (nxt % n_steps), 0)

    const = lambda b, u, *_: (0, 0)
    grid_spec = pltpu.PrefetchScalarGridSpec(
        num_scalar_prefetch=1,
        grid=(batch, n_steps),
        in_specs=[
            pl.BlockSpec((1, step_rows, D_MODEL), lambda b, u, *_: (b, u, 0)),
            pl.BlockSpec((1, TIME_TILE, D_MODEL), next_tile_index),
            pl.BlockSpec(memory_space=pl.ANY),
            pl.BlockSpec(memory_space=pl.ANY),
            pl.BlockSpec((DEPTH + 1, HG_WIDTH), const),
            pl.BlockSpec((1, HG_WIDTH), const),
            pl.BlockSpec((1, D_MODEL), const),
            pl.BlockSpec((1, D_MODEL), const),
            pl.BlockSpec((step_rows, LANES), lambda b, u, *_: (u, 0)),
            pl.BlockSpec((step_rows, LANES), lambda b, u, *_: (u, 0)),
        ],
        out_specs=pl.BlockSpec((1, step_rows, D_MODEL), lambda b, u, *_: (b, u, 0)),
        scratch_shapes=[
            pltpu.VMEM((TIME_TILE, IN_WIDTH), F32),
            pltpu.VMEM((TIME_TILE, IN_WIDTH), F32),
            pltpu.VMEM((2, TIME_TILE, D_MODEL), BF16),
            pltpu.VMEM((TILES_PER_STEP, TIME_TILE, D_MODEL), BF16),
            pltpu.VMEM((TIME_TILE, D_MODEL), F32),
            pltpu.VMEM((HG_HEADS, HG_HEAD_DIM, HG_HEAD_DIM), F32),
            pltpu.VMEM((2 * SW_KV_HEADS, STEP_BLOCKS * WINDOW, LANES), BF16),
            pltpu.VMEM((SW_KV_HEADS, VT_ROWS, STEP_BLOCKS * WINDOW), BF16),
            pltpu.VMEM((2 * SW_KV_HEADS, WINDOW, LANES), BF16),
            pltpu.VMEM((SW_KV_HEADS, VT_ROWS, WINDOW), BF16),
            pltpu.VMEM((D_MODEL, IN_WIDTH), BF16),
            pltpu.VMEM((D_MODEL, D_MODEL), BF16),
            pltpu.VMEM((WEIGHT_SLOTS, WEIGHT_CHUNK_ROWS, IN_WIDTH), F32),
            pltpu.VMEM((WEIGHT_SLOTS, WEIGHT_CHUNK_ROWS, D_MODEL), F32),
            pltpu.SemaphoreType.DMA((WEIGHT_SLOTS,)),
            pltpu.SemaphoreType.DMA((WEIGHT_SLOTS,)),
        ],
    )
    return pl.pallas_call(
        _layer_kernel,
        grid_spec=grid_spec,
        out_shape=jax.ShapeDtypeStruct(x.shape, x.dtype),
        compiler_params=pltpu.CompilerParams(
            dimension_semantics=("arbitrary", "arbitrary"),
            vmem_limit_bytes=VMEM_LIMIT_BYTES),
        name="hybrid_hgrn2_swa_layer",
    )(sinks[0], x, x, w_in, w_out, lb_logits, hg_norm_w, ln_g, ln_b, cos_tab, sin_tab)
```

```python
import jax
import jax.numpy as jnp
from jax import lax
from jax.experimental import pallas as pl
from jax.experimental.pallas import tpu as pltpu

D_MODEL = 1024
DEPTH = 1
HG_WIDTH = 512
HG_HEAD_DIM = 128
HG_HEADS = HG_WIDTH // HG_HEAD_DIM
HG_CHUNK = 64
SW_WIDTH = 512
SW_HEAD_DIM = 64
SW_Q_HEADS = SW_WIDTH // SW_HEAD_DIM
SW_KV_HEADS = SW_Q_HEADS // 4
SW_KV_WIDTH = SW_KV_HEADS * SW_HEAD_DIM
SW_GROUP = SW_Q_HEADS // SW_KV_HEADS
WINDOW = 128
ROPE_THETA = 500000.0
ROPE_DIM = SW_HEAD_DIM // 4
DN_ALPHA = (2.0 * DEPTH) ** 0.25
LN_EPS = 1e-5
RMS_EPS = 1e-6
IN_WIDTH = 4 * HG_WIDTH + SW_WIDTH + 2 * SW_KV_WIDTH + SW_WIDTH

OFF_HQ = 0
OFF_HF = OFF_HQ + HG_WIDTH
OFF_HI = OFF_HF + HG_WIDTH
OFF_HG = OFF_HI + HG_WIDTH
OFF_AQ = OFF_HG + HG_WIDTH
OFF_AK = OFF_AQ + SW_WIDTH
OFF_AV = OFF_AK + SW_KV_WIDTH
OFF_AG = OFF_AV + SW_KV_WIDTH

LANES = 128
BF16_SUBLANES = 16
MXU_WIDTH = 256
HEADS_PER_TILE = LANES // SW_HEAD_DIM
assert HEADS_PER_TILE == 2 and SW_GROUP == 2 * HEADS_PER_TILE and SW_KV_WIDTH == LANES
VT_ROWS = SW_HEAD_DIM + BF16_SUBLANES
TIME_TILE = 256
TILES_PER_STEP = 4
PIECE = MXU_WIDTH
N_IN_PIECES = IN_WIDTH // PIECE
N_OUT_PIECES = D_MODEL // PIECE
assert IN_WIDTH % PIECE == 0 and D_MODEL % PIECE == 0
N_CHUNKS = TIME_TILE // HG_CHUNK
N_BLOCKS = TIME_TILE // WINDOW
STEP_BLOCKS = TILES_PER_STEP * N_BLOCKS
HEAD_CUM_CHUNKS = 4
HEAD_DEC_CHUNKS = 4
SW_SCORES_AHEAD = 8
WEIGHT_CHUNK_ROWS = 64
WEIGHT_SLOTS = 4
VMEM_LIMIT_BYTES = 58 * 1024 * 1024

F32 = jnp.float32
BF16 = jnp.bfloat16
NT_DIMS = (((1,), (1,)), ((), ()))
TN_DIMS = (((0,), (0,)), ((), ()))


LOG2_E = 1.4426950408889634


def _sigmoid(v):
    return 1.0 / (1.0 + jnp.exp2(v * (-LOG2_E)))


def _load_weight_as_bf16(w_hbm, stage_ref, sem_ref, dst_ref, scale=None):
    n_rows = dst_ref.shape[0]
    n_slots, chunk = stage_ref.shape[0], stage_ref.shape[1]
    n_chunks = n_rows // chunk
    assert n_chunks * chunk == n_rows and n_slots >= 2

    def chunk_copy(i):
        return pltpu.make_async_copy(w_hbm.at[0, pl.ds(i * chunk, chunk), :],
                                     stage_ref.at[i % n_slots], sem_ref.at[i % n_slots])

    for i in range(min(n_slots - 1, n_chunks)):
        chunk_copy(i).start()
    for i in range(n_chunks):
        ahead = i + n_slots - 1
        if ahead < n_chunks:
            chunk_copy(ahead).start()
        chunk_copy(i).wait()
        rows = stage_ref[i % n_slots]
        if scale is not None:
            rows = rows * scale
        dst_ref[i * chunk:(i + 1) * chunk, :] = rows.astype(BF16)


def _layer_kernel(sinks_ref, xc_ref, xn_ref, win_hbm, wout_hbm, lbl_ref, nw_ref, lng_ref, lnb_ref,
                  cos_ref, sin_ref, o_ref,
                  ha_ref, hb_ref, xb_ref, mix_ref, out_ref, st_ref, kwin_ref, vt_ref,
                  kcarry_ref, vcarry_ref, win_ref, wout_ref, win_stage, wout_stage,
                  win_sems, wout_sems):
    b = pl.program_id(0)
    u = pl.program_id(1)
    tt = TIME_TILE
    h_bufs = (ha_ref, hb_ref)

    @pl.when(jnp.logical_and(b == 0, u == 0))
    def _first_step():
        _load_weight_as_bf16(win_hbm, win_stage, win_sems, win_ref)
        _load_weight_as_bf16(wout_hbm, wout_stage, wout_sems, wout_ref, scale=1.0 / DN_ALPHA)
        ha_ref[...] = jnp.dot(xc_ref[0, 0:tt, :].astype(BF16), win_ref[...],
                              preferred_element_type=F32)

    @pl.when(u == 0)
    def _reset_carries():
        st_ref[...] = jnp.zeros_like(st_ref)
        kcarry_ref[...] = jnp.zeros_like(kcarry_ref)
        vcarry_ref[:, 0:SW_HEAD_DIM, :] = jnp.zeros((SW_KV_HEADS, SW_HEAD_DIM, WINDOW), BF16)
        vcarry_ref[:, SW_HEAD_DIM:VT_ROWS, :] = jnp.ones((SW_KV_HEADS, BF16_SUBLANES, WINDOW), BF16)
        vt_ref[:, SW_HEAD_DIM:VT_ROWS, :] = jnp.ones(
            (SW_KV_HEADS, BF16_SUBLANES, STEP_BLOCKS * WINDOW), BF16)

    lbl = lbl_ref[...]
    lbl_e = jnp.exp(lbl - jnp.max(lbl, axis=0, keepdims=True))
    lb = lbl_e[0:1] / jnp.sum(lbl_e, axis=0, keepdims=True)
    nw = nw_ref[...] * (HG_HEAD_DIM ** 0.5)
    row = lax.broadcasted_iota(jnp.int32, (HG_CHUNK, HG_CHUNK), 0)
    col = lax.broadcasted_iota(jnp.int32, (HG_CHUNK, HG_CHUNK), 1)
    tril = col <= row
    cum_mat = tril.astype(BF16)
    cum3 = jnp.concatenate([cum_mat, cum_mat, cum_mat], axis=1)

    lane = lax.broadcasted_iota(jnp.int32, (1, LANES), 1)
    rope_first = (lane % SW_HEAD_DIM) < (ROPE_DIM // 2)
    lane_lo = lane < SW_HEAD_DIM
    key_idx = lax.broadcasted_iota(jnp.int32, (2 * WINDOW, 2 * WINDOW), 0)
    qry_idx = lax.broadcasted_iota(jnp.int32, (2 * WINDOW, 2 * WINDOW), 1) % WINDOW
    band = (key_idx > qry_idx) & (key_idx <= qry_idx + WINDOW)
    pair_lo = lax.broadcasted_iota(jnp.int32, (1, 2 * WINDOW), 1) < WINDOW
    scale = SW_HEAD_DIM ** -0.5 * LOG2_E
    first_valid = jnp.where(u > 0, 0, WINDOW)
    band_cap = jnp.where(band, jnp.inf, -jnp.inf)
    first_cap = jnp.where(band & (key_idx >= first_valid), jnp.inf, -jnp.inf)

    def rope(v, cos, sin):
        partner = jnp.where(rope_first,
                            pltpu.roll(v, LANES - ROPE_DIM // 2, 1),
                            pltpu.roll(v, ROPE_DIM // 2, 1))
        return v * cos + partner * sin

    def head_variants(v):
        sw = pltpu.roll(v, SW_HEAD_DIM, 1)
        zero = jnp.zeros_like(v)
        return (jnp.where(lane_lo, v, zero), jnp.where(lane_lo, zero, sw),
                jnp.where(lane_lo, sw, zero), jnp.where(lane_lo, zero, v))

    def in_projection_piece(k, piece):
        xb = xb_ref.at[k % 2]
        cols = slice(piece * PIECE, (piece + 1) * PIECE)
        h_bufs[k % 2][:, cols] = jnp.dot(xb[...], win_ref[:, cols], preferred_element_type=F32)

    def out_projection_piece(k, piece):
        cols = slice(piece * PIECE, (piece + 1) * PIECE)
        out_ref[:, cols] = jnp.dot(mix_ref[k], wout_ref[:, cols], preferred_element_type=F32)

    def post_norm(k):
        rows = slice(k * tt, (k + 1) * tt)
        z = xc_ref[0, rows, :] + out_ref[...]
        mu = jnp.mean(z, axis=-1, keepdims=True)
        zc = z - mu
        var = jnp.mean(zc * zc, axis=-1, keepdims=True)
        o_ref[0, rows, :] = ((zc * lax.rsqrt(var + LN_EPS / DN_ALPHA ** 2)) * lng_ref[...]
                             + lnb_ref[...])

    def make_fillers(k):
        stages = []
        if k > 0:
            for piece in range(N_OUT_PIECES):
                stages.append(lambda piece=piece: out_projection_piece(k - 1, piece))
            stages.append(lambda: post_norm(k - 1))
        for piece in range(N_IN_PIECES):
            def in_stage(piece=piece):
                in_projection_piece(k + 1, piece)
                projected[k + 1] = piece + 1
            stages.append(in_stage)
        return iter(stages)

    projected = {}

    def process_tile(k, head):
        h_ref = h_bufs[k % 2]
        base = k * tt
        mix = mix_ref.at[k]
        if k + 1 < TILES_PER_STEP:
            x_next_rows = xc_ref[0, (k + 1) * tt:(k + 2) * tt, :]
        else:
            x_next_rows = xn_ref[0]
        xb_ref[(k + 1) % 2] = x_next_rows.astype(BF16)
        fillers = make_fillers(k)

        def fill(n=1):
            for _ in range(n):
                stage = next(fillers, None)
                if stage is not None:
                    stage()

        def hg_pre(c, h=h_ref):
            rows = slice(c * HG_CHUNK, (c + 1) * HG_CHUNK)
            f = lb + (1.0 - lb) * _sigmoid(h[rows, OFF_HF:OFF_HF + HG_WIDTH])
            log_f = jnp.log2(f)
            hi = log_f.astype(BF16)
            rem = log_f - hi.astype(F32)
            mid = rem.astype(BF16)
            lo = (rem - mid.astype(F32)).astype(BF16)
            return 1.0 - f, jnp.concatenate([hi, mid, lo], axis=0)

        def hg_cum(parts):
            return jnp.dot(cum3, parts, preferred_element_type=F32)

        def hg_decays(c, k_in, g_cum, h=h_ref):
            rows = slice(c * HG_CHUNK, (c + 1) * HG_CHUNK)
            hq = h[rows, OFF_HQ:OFF_HQ + HG_WIDTH]
            g_last = g_cum[HG_CHUNK - 1:HG_CHUNK, :]
            q_dec = ((hq * _sigmoid(hq)) * jnp.exp2(g_cum)).astype(BF16)
            k_dec = (k_in * jnp.exp2(-g_cum)).astype(BF16)
            k_tail = (k_in * jnp.exp2(g_last - g_cum)).astype(BF16)
            return q_dec, k_dec, k_tail, jnp.exp2(g_last)

        def hg_scores(q_dec, k_dec, hd):
            sl = slice(hd * HG_HEAD_DIM, (hd + 1) * HG_HEAD_DIM)
            return lax.dot_general(q_dec[:, sl], k_dec[:, sl], NT_DIMS, preferred_element_type=F32)

        def hg_output(c, hd, a, q_dec, k_tail, decay):
            rows = slice(c * HG_CHUNK, (c + 1) * HG_CHUNK)
            sl = slice(hd * HG_HEAD_DIM, (hd + 1) * HG_HEAD_DIM)
            v = h_ref[rows, OFF_HI + hd * HG_HEAD_DIM:OFF_HI + (hd + 1) * HG_HEAD_DIM].astype(BF16)
            s_t = st_ref[hd]
            o = jnp.dot(jnp.where(tril, a, 0.0).astype(BF16), v, preferred_element_type=F32)
            o = o + lax.dot_general(q_dec[:, sl], s_t.astype(BF16), NT_DIMS,
                                    preferred_element_type=F32)
            inc_t = lax.dot_general(v, k_tail[:, sl], TN_DIMS, preferred_element_type=F32)
            st_ref[hd] = s_t * decay[:, sl] + inc_t
            o = o * lax.rsqrt(jnp.sum(o * o, axis=-1, keepdims=True) + HG_HEAD_DIM * RMS_EPS)
            o = o * nw[:, sl]
            hg = h_ref[rows, OFF_HG + hd * HG_HEAD_DIM:OFF_HG + (hd + 1) * HG_HEAD_DIM]
            mix[rows, sl] = (o * (hg * _sigmoid(hg))).astype(BF16)

        def sw_prepare(blk):
            r0 = blk * WINDOW
            rows = slice(r0, r0 + WINDOW)
            cur = slice(base + r0, base + r0 + WINDOW)
            cos = cos_ref[base + r0:base + r0 + WINDOW, :]
            sin = sin_ref[base + r0:base + r0 + WINDOW, :]
            k_rot = rope(h_ref[rows, OFF_AK:OFF_AK + SW_KV_WIDTH], cos, sin)
            for idx, kv in enumerate(head_variants(k_rot)):
                kwin_ref[idx, cur, :] = kv.astype(BF16)
            v_t = h_ref[rows, OFF_AV:OFF_AV + SW_KV_WIDTH].T
            for g in range(SW_KV_HEADS):
                vt_ref[g, 0:SW_HEAD_DIM, cur] = (
                    v_t[g * SW_HEAD_DIM:(g + 1) * SW_HEAD_DIM].astype(BF16))
            q_bf = []
            for tile in range(SW_WIDTH // LANES):
                q_rot = rope(h_ref[rows, OFF_AQ + tile * LANES:OFF_AQ + (tile + 1) * LANES], cos, sin)
                q_bf.append((q_rot * scale).astype(BF16))
            return [jnp.concatenate([q_bf[2 * g], q_bf[2 * g + 1]], axis=0)
                    for g in range(SW_KV_HEADS)]

        def key_window(blk, idx):
            r0 = base + blk * WINDOW
            if r0 == 0:
                return jnp.concatenate([kcarry_ref[idx], kwin_ref[idx, 0:WINDOW, :]], axis=0)
            return kwin_ref[idx, r0 - WINDOW:r0 + WINDOW, :]

        def value_window(blk, g):
            r0 = base + blk * WINDOW
            if r0 == 0:
                return jnp.concatenate([vcarry_ref[g], vt_ref[g, :, 0:WINDOW]], axis=1)
            return vt_ref[g, :, r0 - WINDOW:r0 + WINDOW]

        def sw_scores(blk, g, j, q_pair):
            return lax.dot_general(key_window(blk, 2 * g + j), q_pair, NT_DIMS,
                                   preferred_element_type=F32)

        def sw_values(blk, g, j, s_t, cap):
            sink_row = jnp.where(pair_lo, sinks_ref[SW_GROUP * g + j],
                                 sinks_ref[SW_GROUP * g + HEADS_PER_TILE + j]) * LOG2_E
            s_t = jnp.minimum(s_t, cap)
            m = jnp.maximum(jnp.max(s_t, axis=0, keepdims=True), sink_row)
            p_t = jnp.exp2(s_t - m).astype(BF16)
            acc = jnp.dot(value_window(blk, g), p_t, preferred_element_type=F32)
            denom = acc[SW_HEAD_DIM:SW_HEAD_DIM + 1, :] + jnp.exp2(sink_row - m)
            return acc[0:SW_HEAD_DIM, :] * (1.0 / denom)

        def sw_finish(blk, g, o_t):
            rows = slice(blk * WINDOW, (blk + 1) * WINDOW)
            for tl in range(2):
                tile = 2 * g + tl
                lanes = slice(tl * LANES, (tl + 1) * LANES)
                o_a = jnp.concatenate([o_t[0][:, lanes], o_t[1][:, lanes]], axis=0).T
                ag = h_ref[rows, OFF_AG + tile * LANES:OFF_AG + (tile + 1) * LANES]
                mix[rows, HG_WIDTH + tile * LANES:HG_WIDTH + (tile + 1) * LANES] = (
                    o_a * (ag * _sigmoid(ag))).astype(BF16)

        many = k > 0
        masks = [first_cap if (k == 0 and blk == 0) else band_cap for blk in range(N_BLOCKS)]
        chains = [(blk, g, j) for blk in range(N_BLOCKS) for g in range(SW_KV_HEADS)
                  for j in range(HEADS_PER_TILE)]
        q_pairs = {}
        scores = {}

        def sw_issue_scores(i):
            if i < len(chains) and i not in scores:
                blk, g, j = chains[i]
                if blk not in q_pairs:
                    q_pairs[blk] = sw_prepare(blk)
                scores[i] = sw_scores(blk, g, j, q_pairs[blk][g])

        head = head or {"pre": {}, "cum": {}, "dec": {}, "scores": {}}
        pre = dict(head["pre"])
        cum = dict(head["cum"])

        def hg_issue_cum(c):
            if c < N_CHUNKS and c not in cum:
                pre[c] = hg_pre(c)
                cum[c] = hg_cum(pre[c][1])

        fill(2 if many else 1)
        for c in range(min(2, N_CHUNKS)):
            hg_issue_cum(c)
            fill(1 if many else 0)
        for c in range(N_CHUNKS):
            if c in head["dec"]:
                q_dec, k_dec, k_tail, decay = head["dec"][c]
                hg_s = head["scores"][c]
            else:
                q_dec, k_dec, k_tail, decay = hg_decays(c, pre[c][0], cum[c])
                hg_s = [hg_scores(q_dec, k_dec, hd) for hd in range(HG_HEADS)]
            hg_issue_cum(c + 2)
            fill()
            for hd in range(HG_HEADS):
                hg_output(c, hd, hg_s[hd], q_dec, k_tail, decay)
            fill(1 if many else 0)

        for i in range(min(SW_SCORES_AHEAD, len(chains))):
            sw_issue_scores(i)
        fill()
        outs = {}
        has_next = k + 1 < TILES_PER_STEP
        next_head = {"pre": {}, "cum": {}, "dec": {}, "scores": {}} if has_next else None
        h_next = h_bufs[(k + 1) % 2]

        def head_start(i):
            assert projected.get(k + 1, 0) * PIECE >= OFF_HF + HG_WIDTH
            if i < HEAD_CUM_CHUNKS:
                next_head["pre"][i] = hg_pre(i, h_next)
                next_head["cum"][i] = hg_cum(next_head["pre"][i][1])
            c = i - (len(chains) - HEAD_DEC_CHUNKS)
            if 0 <= c < HEAD_DEC_CHUNKS:
                d = hg_decays(c, next_head["pre"][c][0], next_head["cum"][c], h_next)
                next_head["dec"][c] = d
                next_head["scores"][c] = [hg_scores(d[0], d[1], hd) for hd in range(HG_HEADS)]

        for i, (blk, g, j) in enumerate(chains):
            outs[(blk, g, j)] = sw_values(blk, g, j, scores.pop(i), masks[blk])
            sw_issue_scores(i + SW_SCORES_AHEAD)
            if j == HEADS_PER_TILE - 1:
                sw_finish(blk, g, [outs.pop((blk, g, jj)) for jj in range(HEADS_PER_TILE)])
            fill()
            if has_next:
                head_start(i)
        fill(N_OUT_PIECES + 1 + N_IN_PIECES)
        return next_head

    head = None
    for k in range(TILES_PER_STEP):
        head = process_tile(k, head)
    for piece in range(N_OUT_PIECES):
        out_projection_piece(TILES_PER_STEP - 1, piece)
    post_norm(TILES_PER_STEP - 1)

    last = slice((STEP_BLOCKS - 1) * WINDOW, STEP_BLOCKS * WINDOW)
    for idx in range(2 * SW_KV_HEADS):
        kcarry_ref[idx] = kwin_ref[idx, last, :]
    for g in range(SW_KV_HEADS):
        vcarry_ref[g, 0:SW_HEAD_DIM, :] = vt_ref[g, 0:SW_HEAD_DIM, last]


def _rope_tables(seq_len):
    pos = jnp.arange(seq_len, dtype=F32)
    inv_freq = ROPE_THETA ** (-jnp.arange(0, ROPE_DIM, 2, dtype=F32) / ROPE_DIM)
    ang = pos[:, None] * inv_freq[None, :]
    cos = jnp.cos(ang)
    sin = jnp.sin(ang)
    ones = jnp.ones((seq_len, SW_HEAD_DIM - ROPE_DIM), F32)
    cos_head = jnp.concatenate([cos, cos, ones], axis=1)
    sin_head = jnp.concatenate([-sin, sin, jnp.zeros_like(ones)], axis=1)
    return (jnp.tile(cos_head, (1, HEADS_PER_TILE)), jnp.tile(sin_head, (1, HEADS_PER_TILE)))


def kernel(x, w_in, lb_logits, hg_norm_w, sinks, w_out, ln_g, ln_b):
    batch, seq_len, d_model = x.shape
    assert d_model == D_MODEL and w_in.shape == (DEPTH, D_MODEL, IN_WIDTH)
    step_rows = TILES_PER_STEP * TIME_TILE
    assert seq_len % step_rows == 0
    cos_tab, sin_tab = _rope_tables(seq_len)
    n_steps = seq_len // step_rows
    last_step = batch * n_steps - 1

    def next_tile_index(b, u, *_):
        nxt = jnp.minimum(b * n_steps + u + 1, last_step)
        return (nxt // n_steps, TILES_PER_STEP * (nxt % n_steps), 0)

    const = lambda b, u, *_: (0, 0)
    grid_spec = pltpu.PrefetchScalarGridSpec(
        num_scalar_prefetch=1,
        grid=(batch, n_steps),
        in_specs=[
            pl.BlockSpec((1, step_rows, D_MODEL), lambda b, u, *_: (b, u, 0)),
            pl.BlockSpec((1, TIME_TILE, D_MODEL), next_tile_index),
            pl.BlockSpec(memory_space=pl.ANY),
            pl.BlockSpec(memory_space=pl.ANY),
            pl.BlockSpec((DEPTH + 1, HG_WIDTH), const),
            pl.BlockSpec((1, HG_WIDTH), const),
            pl.BlockSpec((1, D_MODEL), const),
            pl.BlockSpec((1, D_MODEL), const),
            pl.BlockSpec((step_rows, LANES), lambda b, u, *_: (u, 0)),
            pl.BlockSpec((step_rows, LANES), lambda b, u, *_: (u, 0)),
        ],
        out_specs=pl.BlockSpec((1, step_rows, D_MODEL), lambda b, u, *_: (b, u, 0)),
        scratch_shapes=[
            pltpu.VMEM((TIME_TILE, IN_WIDTH), F32),
            pltpu.VMEM((TIME_TILE, IN_WIDTH), F32),
            pltpu.VMEM((2, TIME_TILE, D_MODEL), BF16),
            pltpu.VMEM((TILES_PER_STEP, TIME_TILE, D_MODEL), BF16),
            pltpu.VMEM((TIME_TILE, D_MODEL), F32),
            pltpu.VMEM((HG_HEADS, HG_HEAD_DIM, HG_HEAD_DIM), F32),
            pltpu.VMEM((2 * SW_KV_HEADS, STEP_BLOCKS * WINDOW, LANES), BF16),
            pltpu.VMEM((SW_KV_HEADS, VT_ROWS, STEP_BLOCKS * WINDOW), BF16),
            pltpu.VMEM((2 * SW_KV_HEADS, WINDOW, LANES), BF16),
            pltpu.VMEM((SW_KV_HEADS, VT_ROWS, WINDOW), BF16),
            pltpu.VMEM((D_MODEL, IN_WIDTH), BF16),
            pltpu.VMEM((D_MODEL, D_MODEL), BF16),
            pltpu.VMEM((WEIGHT_SLOTS, WEIGHT_CHUNK_ROWS, IN_WIDTH), F32),
            pltpu.VMEM((WEIGHT_SLOTS, WEIGHT_CHUNK_ROWS, D_MODEL), F32),
            pltpu.SemaphoreType.DMA((WEIGHT_SLOTS,)),
            pltpu.SemaphoreType.DMA((WEIGHT_SLOTS,)),
        ],
    )
    return pl.pallas_call(
        _layer_kernel,
        grid_spec=grid_spec,
        out_shape=jax.ShapeDtypeStruct(x.shape, x.dtype),
        compiler_params=pltpu.CompilerParams(
            dimension_semantics=("arbitrary", "arbitrary"),
            vmem_limit_bytes=VMEM_LIMIT_BYTES),
        name="hybrid_hgrn2_swa_layer",
    )(sinks[0], x, x, w_in, w_out, lb_logits, hg_norm_w, ln_g, ln_b, cos_tab, sin_tab)
```

```python
import jax
import jax.numpy as jnp
from jax import lax
from jax.experimental import pallas as pl
from jax.experimental.pallas import tpu as pltpu

D_MODEL = 1024
DEPTH = 1
HG_WIDTH = 512
HG_HEAD_DIM = 128
HG_HEADS = HG_WIDTH // HG_HEAD_DIM
HG_CHUNK = 64
SW_WIDTH = 512
SW_HEAD_DIM = 64
SW_Q_HEADS = SW_WIDTH // SW_HEAD_DIM
SW_KV_HEADS = SW_Q_HEADS // 4
SW_KV_WIDTH = SW_KV_HEADS * SW_HEAD_DIM
SW_GROUP = SW_Q_HEADS // SW_KV_HEADS
WINDOW = 128
ROPE_THETA = 500000.0
ROPE_DIM = SW_HEAD_DIM // 4
DN_ALPHA = (2.0 * DEPTH) ** 0.25
LN_EPS = 1e-5
RMS_EPS = 1e-6
IN_WIDTH = 4 * HG_WIDTH + SW_WIDTH + 2 * SW_KV_WIDTH + SW_WIDTH

OFF_HQ = 0
OFF_HF = OFF_HQ + HG_WIDTH
OFF_HI = OFF_HF + HG_WIDTH
OFF_HG = OFF_HI + HG_WIDTH
OFF_AQ = OFF_HG + HG_WIDTH
OFF_AK = OFF_AQ + SW_WIDTH
OFF_AV = OFF_AK + SW_KV_WIDTH
OFF_AG = OFF_AV + SW_KV_WIDTH

LANES = 128
BF16_SUBLANES = 16
MXU_WIDTH = 256
HEADS_PER_TILE = LANES // SW_HEAD_DIM
assert HEADS_PER_TILE == 2 and SW_GROUP == 2 * HEADS_PER_TILE and SW_KV_WIDTH == LANES
VT_ROWS = SW_HEAD_DIM + BF16_SUBLANES
TIME_TILE = 256
TILES_PER_STEP = 4
PIECE = MXU_WIDTH
N_IN_PIECES = IN_WIDTH // PIECE
N_OUT_PIECES = D_MODEL // PIECE
assert IN_WIDTH % PIECE == 0 and D_MODEL % PIECE == 0
N_CHUNKS = TIME_TILE // HG_CHUNK
N_BLOCKS = TIME_TILE // WINDOW
STEP_BLOCKS = TILES_PER_STEP * N_BLOCKS
HEAD_CUM_CHUNKS = 2
HEAD_DEC_CHUNKS = 2
SW_SCORES_AHEAD = 8
WEIGHT_CHUNK_ROWS = 64
WEIGHT_SLOTS = 4
VMEM_LIMIT_BYTES = 58 * 1024 * 1024

F32 = jnp.float32
BF16 = jnp.bfloat16
NT_DIMS = (((1,), (1,)), ((), ()))
TN_DIMS = (((0,), (0,)), ((), ()))


LOG2_E = 1.4426950408889634


def _sigmoid(v):
    return 1.0 / (1.0 + jnp.exp2(v * (-LOG2_E)))


def _load_weight_as_bf16(w_hbm, stage_ref, sem_ref, dst_ref, scale=None):
    n_rows = dst_ref.shape[0]
    n_slots, chunk = stage_ref.shape[0], stage_ref.shape[1]
    n_chunks = n_rows // chunk
    assert n_chunks * chunk == n_rows and n_slots >= 2

    def chunk_copy(i):
        return pltpu.make_async_copy(w_hbm.at[0, pl.ds(i * chunk, chunk), :],
                                     stage_ref.at[i % n_slots], sem_ref.at[i % n_slots])

    for i in range(min(n_slots - 1, n_chunks)):
        chunk_copy(i).start()
    for i in range(n_chunks):
        ahead = i + n_slots - 1
        if ahead < n_chunks:
            chunk_copy(ahead).start()
        chunk_copy(i).wait()
        rows = stage_ref[i % n_slots]
        if scale is not None:
            rows = rows * scale
        dst_ref[i * chunk:(i + 1) * chunk, :] = rows.astype(BF16)


def _layer_kernel(sinks_ref, xc_ref, xn_ref, win_hbm, wout_hbm, lbl_ref, nw_ref, lng_ref, lnb_ref,
                  cos_ref, sin_ref, o_ref,
                  ha_ref, hb_ref, xb_ref, mix_ref, out_ref, st_ref, kwin_ref, vt_ref,
                  kcarry_ref, vcarry_ref, win_ref, wout_ref, win_stage, wout_stage,
                  win_sems, wout_sems):
    b = pl.program_id(0)
    u = pl.program_id(1)
    tt = TIME_TILE
    h_bufs = (ha_ref, hb_ref)

    @pl.when(jnp.logical_and(b == 0, u == 0))
    def _first_step():
        _load_weight_as_bf16(win_hbm, win_stage, win_sems, win_ref)
        _load_weight_as_bf16(wout_hbm, wout_stage, wout_sems, wout_ref, scale=1.0 / DN_ALPHA)
        ha_ref[...] = jnp.dot(xc_ref[0, 0:tt, :].astype(BF16), win_ref[...],
                              preferred_element_type=F32)

    @pl.when(u == 0)
    def _reset_carries():
        st_ref[...] = jnp.zeros_like(st_ref)
        kcarry_ref[...] = jnp.zeros_like(kcarry_ref)
        vcarry_ref[:, 0:SW_HEAD_DIM, :] = jnp.zeros((SW_KV_HEADS, SW_HEAD_DIM, WINDOW), BF16)
        vcarry_ref[:, SW_HEAD_DIM:VT_ROWS, :] = jnp.ones((SW_KV_HEADS, BF16_SUBLANES, WINDOW), BF16)
        vt_ref[:, SW_HEAD_DIM:VT_ROWS, :] = jnp.ones(
            (SW_KV_HEADS, BF16_SUBLANES, STEP_BLOCKS * WINDOW), BF16)

    lbl = lbl_ref[...]
    lbl_e = jnp.exp(lbl - jnp.max(lbl, axis=0, keepdims=True))
    lb = lbl_e[0:1] / jnp.sum(lbl_e, axis=0, keepdims=True)
    nw = nw_ref[...] * (HG_HEAD_DIM ** 0.5)
    row = lax.broadcasted_iota(jnp.int32, (HG_CHUNK, HG_CHUNK), 0)
    col = lax.broadcasted_iota(jnp.int32, (HG_CHUNK, HG_CHUNK), 1)
    tril = col <= row
    cum_mat = tril.astype(BF16)
    cum3 = jnp.concatenate([cum_mat, cum_mat, cum_mat], axis=1)

    lane = lax.broadcasted_iota(jnp.int32, (1, LANES), 1)
    rope_first = (lane % SW_HEAD_DIM) < (ROPE_DIM // 2)
    lane_lo = lane < SW_HEAD_DIM
    key_idx = lax.broadcasted_iota(jnp.int32, (2 * WINDOW, 2 * WINDOW), 0)
    qry_idx = lax.broadcasted_iota(jnp.int32, (2 * WINDOW, 2 * WINDOW), 1) % WINDOW
    band = (key_idx > qry_idx) & (key_idx <= qry_idx + WINDOW)
    pair_lo = lax.broadcasted_iota(jnp.int32, (1, 2 * WINDOW), 1) < WINDOW
    scale = SW_HEAD_DIM ** -0.5 * LOG2_E
    first_valid = jnp.where(u > 0, 0, WINDOW)
    band_cap = jnp.where(band, jnp.inf, -jnp.inf)
    first_cap = jnp.where(band & (key_idx >= first_valid), jnp.inf, -jnp.inf)

    def rope(v, cos, sin):
        partner = jnp.where(rope_first,
                            pltpu.roll(v, LANES - ROPE_DIM // 2, 1),
                            pltpu.roll(v, ROPE_DIM // 2, 1))
        return v * cos + partner * sin

    def head_variants(v):
        sw = pltpu.roll(v, SW_HEAD_DIM, 1)
        zero = jnp.zeros_like(v)
        return (jnp.where(lane_lo, v, zero), jnp.where(lane_lo, zero, sw),
                jnp.where(lane_lo, sw, zero), jnp.where(lane_lo, zero, v))

    def in_projection_piece(k, piece):
        xb = xb_ref.at[k % 2]
        cols = slice(piece * PIECE, (piece + 1) * PIECE)
        h_bufs[k % 2][:, cols] = jnp.dot(xb[...], win_ref[:, cols], preferred_element_type=F32)

    def out_projection_piece(k, piece):
        cols = slice(piece * PIECE, (piece + 1) * PIECE)
        out_ref[:, cols] = jnp.dot(mix_ref[k], wout_ref[:, cols], preferred_element_type=F32)

    def post_norm(k):
        rows = slice(k * tt, (k + 1) * tt)
        z = xc_ref[0, rows, :] + out_ref[...]
        mu = jnp.mean(z, axis=-1, keepdims=True)
        zc = z - mu
        var = jnp.mean(zc * zc, axis=-1, keepdims=True)
        o_ref[0, rows, :] = ((zc * lax.rsqrt(var + LN_EPS / DN_ALPHA ** 2)) * lng_ref[...]
                             + lnb_ref[...])

    def make_fillers(k):
        stages = []
        if k > 0:
            for piece in range(N_OUT_PIECES):
                stages.append(lambda piece=piece: out_projection_piece(k - 1, piece))
            stages.append(lambda: post_norm(k - 1))
        for piece in range(N_IN_PIECES):
            def in_stage(piece=piece):
                in_projection_piece(k + 1, piece)
                projected[k + 1] = piece + 1
            stages.append(in_stage)
        return iter(stages)

    projected = {}

    def process_tile(k, head):
        h_ref = h_bufs[k % 2]
        base = k * tt
        mix = mix_ref.at[k]
        if k + 1 < TILES_PER_STEP:
            x_next_rows = xc_ref[0, (k + 1) * tt:(k + 2) * tt, :]
        else:
            x_next_rows = xn_ref[0]
        xb_ref[(k + 1) % 2] = x_next_rows.astype(BF16)
        fillers = make_fillers(k)

        def fill(n=1):
            for _ in range(n):
                stage = next(fillers, None)
                if stage is not None:
                    stage()

        def hg_pre(c, h=h_ref):
            rows = slice(c * HG_CHUNK, (c + 1) * HG_CHUNK)
            f = lb + (1.0 - lb) * _sigmoid(h[rows, OFF_HF:OFF_HF + HG_WIDTH])
            log_f = jnp.log2(f)
            hi = log_f.astype(BF16)
            rem = log_f - hi.astype(F32)
            mid = rem.astype(BF16)
            lo = (rem - mid.astype(F32)).astype(BF16)
            return 1.0 - f, jnp.concatenate([hi, mid, lo], axis=0)

        def hg_cum(parts):
            return jnp.dot(cum3, parts, preferred_element_type=F32)

        def hg_decays(c, k_in, g_cum, h=h_ref):
            rows = slice(c * HG_CHUNK, (c + 1) * HG_CHUNK)
            hq = h[rows, OFF_HQ:OFF_HQ + HG_WIDTH]
            g_last = g_cum[HG_CHUNK - 1:HG_CHUNK, :]
            q_dec = ((hq * _sigmoid(hq)) * jnp.exp2(g_cum)).astype(BF16)
            k_dec = (k_in * jnp.exp2(-g_cum)).astype(BF16)
            k_tail = (k_in * jnp.exp2(g_last - g_cum)).astype(BF16)
            return q_dec, k_dec, k_tail, jnp.exp2(g_last)

        def hg_scores(q_dec, k_dec, hd):
            sl = slice(hd * HG_HEAD_DIM, (hd + 1) * HG_HEAD_DIM)
            return lax.dot_general(q_dec[:, sl], k_dec[:, sl], NT_DIMS, preferred_element_type=F32)

        def hg_output(c, hd, a, q_dec, k_tail, decay):
            rows = slice(c * HG_CHUNK, (c + 1) * HG_CHUNK)
            sl = slice(hd * HG_HEAD_DIM, (hd + 1) * HG_HEAD_DIM)
            v = h_ref[rows, OFF_HI + hd * HG_HEAD_DIM:OFF_HI + (hd + 1) * HG_HEAD_DIM].astype(BF16)
            s_t = st_ref[hd]
            o = jnp.dot(jnp.where(tril, a, 0.0).astype(BF16), v, preferred_element_type=F32)
            o = o + lax.dot_general(q_dec[:, sl], s_t.astype(BF16), NT_DIMS,
                                    preferred_element_type=F32)
            inc_t = lax.dot_general(v, k_tail[:, sl], TN_DIMS, preferred_element_type=F32)
            st_ref[hd] = s_t * decay[:, sl] + inc_t
            o = o * lax.rsqrt(jnp.sum(o * o, axis=-1, keepdims=True) + HG_HEAD_DIM * RMS_EPS)
            o = o * nw[:, sl]
            hg = h_ref[rows, OFF_HG + hd * HG_HEAD_DIM:OFF_HG + (hd + 1) * HG_HEAD_DIM]
            mix[rows, sl] = (o * (hg * _sigmoid(hg))).astype(BF16)

        def sw_prepare(blk):
            r0 = blk * WINDOW
            rows = slice(r0, r0 + WINDOW)
            cur = slice(base + r0, base + r0 + WINDOW)
            cos = cos_ref[base + r0:base + r0 + WINDOW, :]
            sin = sin_ref[base + r0:base + r0 + WINDOW, :]
            k_rot = rope(h_ref[rows, OFF_AK:OFF_AK + SW_KV_WIDTH], cos, sin)
            for idx, kv in enumerate(head_variants(k_rot)):
                kwin_ref[idx, cur, :] = kv.astype(BF16)
            v_t = h_ref[rows, OFF_AV:OFF_AV + SW_KV_WIDTH].T
            for g in range(SW_KV_HEADS):
                vt_ref[g, 0:SW_HEAD_DIM, cur] = (
                    v_t[g * SW_HEAD_DIM:(g + 1) * SW_HEAD_DIM].astype(BF16))
            q_bf = []
            for tile in range(SW_WIDTH // LANES):
                q_rot = rope(h_ref[rows, OFF_AQ + tile * LANES:OFF_AQ + (tile + 1) * LANES], cos, sin)
                q_bf.append((q_rot * scale).astype(BF16))
            return [jnp.concatenate([q_bf[2 * g], q_bf[2 * g + 1]], axis=0)
                    for g in range(SW_KV_HEADS)]

        def key_window(blk, idx):
            r0 = base + blk * WINDOW
            if r0 == 0:
                return jnp.concatenate([kcarry_ref[idx], kwin_ref[idx, 0:WINDOW, :]], axis=0)
            return kwin_ref[idx, r0 - WINDOW:r0 + WINDOW, :]

        def value_window(blk, g):
            r0 = base + blk * WINDOW
            if r0 == 0:
                return jnp.concatenate([vcarry_ref[g], vt_ref[g, :, 0:WINDOW]], axis=1)
            return vt_ref[g, :, r0 - WINDOW:r0 + WINDOW]

        def sw_scores(blk, g, j, q_pair):
            return lax.dot_general(key_window(blk, 2 * g + j), q_pair, NT_DIMS,
                                   preferred_element_type=F32)

        def sw_values(blk, g, j, s_t, cap):
            sink_row = jnp.where(pair_lo, sinks_ref[SW_GROUP * g + j],
                                 sinks_ref[SW_GROUP * g + HEADS_PER_TILE + j]) * LOG2_E
            s_t = jnp.minimum(s_t, cap)
            m = jnp.maximum(jnp.max(s_t, axis=0, keepdims=True), sink_row)
            p_t = jnp.exp2(s_t - m).astype(BF16)
            acc = jnp.dot(value_window(blk, g), p_t, preferred_element_type=F32)
            denom = acc[SW_HEAD_DIM:SW_HEAD_DIM + 1, :] + jnp.exp2(sink_row - m)
            return acc[0:SW_HEAD_DIM, :] * (1.0 / denom)

        def sw_finish(blk, g, o_t):
            rows = slice(blk * WINDOW, (blk + 1) * WINDOW)
            for tl in range(2):
                tile = 2 * g + tl
                lanes = slice(tl * LANES, (tl + 1) * LANES)
                o_a = jnp.concatenate([o_t[0][:, lanes], o_t[1][:, lanes]], axis=0).T
                ag = h_ref[rows, OFF_AG + tile * LANES:OFF_AG + (tile + 1) * LANES]
                mix[rows, HG_WIDTH + tile * LANES:HG_WIDTH + (tile + 1) * LANES] = (
                    o_a * (ag * _sigmoid(ag))).astype(BF16)

        many = k > 0
        masks = [first_cap if (k == 0 and blk == 0) else band_cap for blk in range(N_BLOCKS)]
        chains = [(blk, g, j) for blk in range(N_BLOCKS) for g in range(SW_KV_HEADS)
                  for j in range(HEADS_PER_TILE)]
        q_pairs = {}
        scores = {}

        def sw_issue_scores(i):
            if i < len(chains) and i not in scores:
                blk, g, j = chains[i]
                if blk not in q_pairs:
                    q_pairs[blk] = sw_prepare(blk)
                scores[i] = sw_scores(blk, g, j, q_pairs[blk][g])

        head = head or {"pre": {}, "cum": {}, "dec": {}, "scores": {}}
        pre = dict(head["pre"])
        cum = dict(head["cum"])

        def hg_issue_cum(c):
            if c < N_CHUNKS and c not in cum:
                pre[c] = hg_pre(c)
                cum[c] = hg_cum(pre[c][1])

        fill(2 if many else 1)
        for c in range(min(2, N_CHUNKS)):
            hg_issue_cum(c)
            fill(1 if many else 0)
        for c in range(N_CHUNKS):
            if c in head["dec"]:
                q_dec, k_dec, k_tail, decay = head["dec"][c]
                hg_s = head["scores"][c]
            else:
                q_dec, k_dec, k_tail, decay = hg_decays(c, pre[c][0], cum[c])
                hg_s = [hg_scores(q_dec, k_dec, hd) for hd in range(HG_HEADS)]
            hg_issue_cum(c + 2)
            fill()
            for hd in range(HG_HEADS):
                hg_output(c, hd, hg_s[hd], q_dec, k_tail, decay)
            fill(1 if many else 0)

        for i in range(min(SW_SCORES_AHEAD, len(chains))):
            sw_issue_scores(i)
        fill()
        outs = {}
        has_next = k + 1 < TILES_PER_STEP
        next_head = {"pre": {}, "cum": {}, "dec": {}, "scores": {}} if has_next else None
        h_next = h_bufs[(k + 1) % 2]

        def head_start(i):
            assert projected.get(k + 1, 0) * PIECE >= OFF_HF + HG_WIDTH
            if i < HEAD_CUM_CHUNKS:
                next_head["pre"][i] = hg_pre(i, h_next)
                next_head["cum"][i] = hg_cum(next_head["pre"][i][1])
            c = i - (len(chains) - HEAD_DEC_CHUNKS)
            if 0 <= c < HEAD_DEC_CHUNKS:
                d = hg_decays(c, next_head["pre"][c][0], next_head["cum"][c], h_next)
                next_head["dec"][c] = d
                next_head["scores"][c] = [hg_scores(d[0], d[1], hd) for hd in range(HG_HEADS)]

        for i, (blk, g, j) in enumerate(chains):
            outs[(blk, g, j)] = sw_values(blk, g, j, scores.pop(i), masks[blk])
            sw_issue_scores(i + SW_SCORES_AHEAD)
            if j == HEADS_PER_TILE - 1:
                sw_finish(blk, g, [outs.pop((blk, g, jj)) for jj in range(HEADS_PER_TILE)])
            fill()
            if has_next:
                head_start(i)
        fill(N_OUT_PIECES + 1 + N_IN_PIECES)
        return next_head

    head = None
    for k in range(TILES_PER_STEP):
        head = process_tile(k, head)
    for piece in range(N_OUT_PIECES):
        out_projection_piece(TILES_PER_STEP - 1, piece)
    post_norm(TILES_PER_STEP - 1)

    last = slice((STEP_BLOCKS - 1) * WINDOW, STEP_BLOCKS * WINDOW)
    for idx in range(2 * SW_KV_HEADS):
        kcarry_ref[idx] = kwin_ref[idx, last, :]
    for g in range(SW_KV_HEADS):
        vcarry_ref[g, 0:SW_HEAD_DIM, :] = vt_ref[g, 0:SW_HEAD_DIM, last]


def _rope_tables(seq_len):
    pos = jnp.arange(seq_len, dtype=F32)
    inv_freq = ROPE_THETA ** (-jnp.arange(0, ROPE_DIM, 2, dtype=F32) / ROPE_DIM)
    ang = pos[:, None] * inv_freq[None, :]
    cos = jnp.cos(ang)
    sin = jnp.sin(ang)
    ones = jnp.ones((seq_len, SW_HEAD_DIM - ROPE_DIM), F32)
    cos_head = jnp.concatenate([cos, cos, ones], axis=1)
    sin_head = jnp.concatenate([-sin, sin, jnp.zeros_like(ones)], axis=1)
    return (jnp.tile(cos_head, (1, HEADS_PER_TILE)), jnp.tile(sin_head, (1, HEADS_PER_TILE)))


def kernel(x, w_in, lb_logits, hg_norm_w, sinks, w_out, ln_g, ln_b):
    batch, seq_len, d_model = x.shape
    assert d_model == D_MODEL and w_in.shape == (DEPTH, D_MODEL, IN_WIDTH)
    step_rows = TILES_PER_STEP * TIME_TILE
    assert seq_len % step_rows == 0
    cos_tab, sin_tab = _rope_tables(seq_len)
    n_steps = seq_len // step_rows
    last_step = batch * n_steps - 1

    def next_tile_index(b, u, *_):
        nxt = jnp.minimum(b * n_steps + u + 1, last_step)
        return (nxt // n_steps, TILES_PER_STEP * (nxt % n_steps), 0)

    const = lambda b, u, *_: (0, 0)
    grid_spec = pltpu.PrefetchScalarGridSpec(
        num_scalar_prefetch=1,
        grid=(batch, n_steps),
        in_specs=[
            pl.BlockSpec((1, step_rows, D_MODEL), lambda b, u, *_: (b, u, 0)),
            pl.BlockSpec((1, TIME_TILE, D_MODEL), next_tile_index),
            pl.BlockSpec(memory_space=pl.ANY),
            pl.BlockSpec(memory_space=pl.ANY),
            pl.BlockSpec((DEPTH + 1, HG_WIDTH), const),
            pl.BlockSpec((1, HG_WIDTH), const),
            pl.BlockSpec((1, D_MODEL), const),
            pl.BlockSpec((1, D_MODEL), const),
            pl.BlockSpec((step_rows, LANES), lambda b, u, *_: (u, 0)),
            pl.BlockSpec((step_rows, LANES), lambda b, u, *_: (u, 0)),
        ],
        out_specs=pl.BlockSpec((1, step_rows, D_MODEL), lambda b, u, *_: (b, u, 0)),
        scratch_shapes=[
            pltpu.VMEM((TIME_TILE, IN_WIDTH), F32),
            pltpu.VMEM((TIME_TILE, IN_WIDTH), F32),
            pltpu.VMEM((2, TIME_TILE, D_MODEL), BF16),
            pltpu.VMEM((TILES_PER_STEP, TIME_TILE, D_MODEL), BF16),
            pltpu.VMEM((TIME_TILE, D_MODEL), F32),
            pltpu.VMEM((HG_HEADS, HG_HEAD_DIM, HG_HEAD_DIM), F32),
            pltpu.VMEM((2 * SW_KV_HEADS, STEP_BLOCKS * WINDOW, LANES), BF16),
            pltpu.VMEM((SW_KV_HEADS, VT_ROWS, STEP_BLOCKS * WINDOW), BF16),
            pltpu.VMEM((2 * SW_KV_HEADS, WINDOW, LANES), BF16),
            pltpu.VMEM((SW_KV_HEADS, VT_ROWS, WINDOW), BF16),
            pltpu.VMEM((D_MODEL, IN_WIDTH), BF16),
            pltpu.VMEM((D_MODEL, D_MODEL), BF16),
            pltpu.VMEM((WEIGHT_SLOTS, WEIGHT_CHUNK_ROWS, IN_WIDTH), F32),
            pltpu.VMEM((WEIGHT_SLOTS, WEIGHT_CHUNK_ROWS, D_MODEL), F32),
            pltpu.SemaphoreType.DMA((WEIGHT_SLOTS,)),
            pltpu.SemaphoreType.DMA((WEIGHT_SLOTS,)),
        ],
    )
    return pl.pallas_call(
        _layer_kernel,
        grid_spec=grid_spec,
        out_shape=jax.ShapeDtypeStruct(x.shape, x.dtype),
        compiler_params=pltpu.CompilerParams(
            dimension_semantics=("arbitrary", "arbitrary"),
            vmem_limit_bytes=VMEM_LIMIT_BYTES),
        name="hybrid_hgrn2_swa_layer",
    )(sinks[0], x, x, w_in, w_out, lb_logits, hg_norm_w, ln_g, ln_b, cos_tab, sin_tab)
```

```python
import jax
import jax.numpy as jnp
from jax import lax
from jax.experimental import pallas as pl
from jax.experimental.pallas import tpu as pltpu

D_MODEL = 1024
DEPTH = 1
HG_WIDTH = 512
HG_HEAD_DIM = 128
HG_HEADS = HG_WIDTH // HG_HEAD_DIM
HG_CHUNK = 64
SW_WIDTH = 512
SW_HEAD_DIM = 64
SW_Q_HEADS = SW_WIDTH // SW_HEAD_DIM
SW_KV_HEADS = SW_Q_HEADS // 4
SW_KV_WIDTH = SW_KV_HEADS * SW_HEAD_DIM
SW_GROUP = SW_Q_HEADS // SW_KV_HEADS
WINDOW = 128
ROPE_THETA = 500000.0
ROPE_DIM = SW_HEAD_DIM // 4
DN_ALPHA = (2.0 * DEPTH) ** 0.25
LN_EPS = 1e-5
RMS_EPS = 1e-6
IN_WIDTH = 4 * HG_WIDTH + SW_WIDTH + 2 * SW_KV_WIDTH + SW_WIDTH

OFF_HQ = 0
OFF_HF = OFF_HQ + HG_WIDTH
OFF_HI = OFF_HF + HG_WIDTH
OFF_HG = OFF_HI + HG_WIDTH
OFF_AQ = OFF_HG + HG_WIDTH
OFF_AK = OFF_AQ + SW_WIDTH
OFF_AV = OFF_AK + SW_KV_WIDTH
OFF_AG = OFF_AV + SW_KV_WIDTH

LANES = 128
BF16_SUBLANES = 16
MXU_WIDTH = 256
HEADS_PER_TILE = LANES // SW_HEAD_DIM
assert HEADS_PER_TILE == 2 and SW_GROUP == 2 * HEADS_PER_TILE and SW_KV_WIDTH == LANES
VT_ROWS = SW_HEAD_DIM + BF16_SUBLANES
TIME_TILE = 256
TILES_PER_STEP = 4
PIECE = MXU_WIDTH
N_IN_PIECES = IN_WIDTH // PIECE
N_OUT_PIECES = D_MODEL // PIECE
assert IN_WIDTH % PIECE == 0 and D_MODEL % PIECE == 0
N_CHUNKS = TIME_TILE // HG_CHUNK
N_BLOCKS = TIME_TILE // WINDOW
STEP_BLOCKS = TILES_PER_STEP * N_BLOCKS
HEAD_CUM_CHUNKS = 2
HEAD_DEC_CHUNKS = 1
HEAD_AFTER_CHAIN = 1
SW_SCORES_AHEAD = 8
WEIGHT_CHUNK_ROWS = 64
WEIGHT_SLOTS = 4
VMEM_LIMIT_BYTES = 58 * 1024 * 1024

F32 = jnp.float32
BF16 = jnp.bfloat16
NT_DIMS = (((1,), (1,)), ((), ()))
TN_DIMS = (((0,), (0,)), ((), ()))


LOG2_E = 1.4426950408889634


def _sigmoid(v):
    return 1.0 / (1.0 + jnp.exp2(v * (-LOG2_E)))


def _load_weight_as_bf16(w_hbm, stage_ref, sem_ref, dst_ref, scale=None):
    n_rows = dst_ref.shape[0]
    n_slots, chunk = stage_ref.shape[0], stage_ref.shape[1]
    n_chunks = n_rows // chunk
    assert n_chunks * chunk == n_rows and n_slots >= 2

    def chunk_copy(i):
        return pltpu.make_async_copy(w_hbm.at[0, pl.ds(i * chunk, chunk), :],
                                     stage_ref.at[i % n_slots], sem_ref.at[i % n_slots])

    for i in range(min(n_slots - 1, n_chunks)):
        chunk_copy(i).start()
    for i in range(n_chunks):
        ahead = i + n_slots - 1
        if ahead < n_chunks:
            chunk_copy(ahead).start()
        chunk_copy(i).wait()
        rows = stage_ref[i % n_slots]
        if scale is not None:
            rows = rows * scale
        dst_ref[i * chunk:(i + 1) * chunk, :] = rows.astype(BF16)


def _layer_kernel(sinks_ref, xc_ref, xn_ref, win_hbm, wout_hbm, lbl_ref, nw_ref, lng_ref, lnb_ref,
                  cos_ref, sin_ref, o_ref,
                  ha_ref, hb_ref, xb_ref, mix_ref, out_ref, st_ref, kwin_ref, vt_ref,
                  kcarry_ref, vcarry_ref, win_ref, wout_ref, win_stage, wout_stage,
                  win_sems, wout_sems):
    b = pl.program_id(0)
    u = pl.program_id(1)
    tt = TIME_TILE
    h_bufs = (ha_ref, hb_ref)

    @pl.when(jnp.logical_and(b == 0, u == 0))
    def _first_step():
        _load_weight_as_bf16(win_hbm, win_stage, win_sems, win_ref)
        _load_weight_as_bf16(wout_hbm, wout_stage, wout_sems, wout_ref, scale=1.0 / DN_ALPHA)
        ha_ref[...] = jnp.dot(xc_ref[0, 0:tt, :].astype(BF16), win_ref[...],
                              preferred_element_type=F32)

    @pl.when(u == 0)
    def _reset_carries():
        st_ref[...] = jnp.zeros_like(st_ref)
        kcarry_ref[...] = jnp.zeros_like(kcarry_ref)
        vcarry_ref[:, 0:SW_HEAD_DIM, :] = jnp.zeros((SW_KV_HEADS, SW_HEAD_DIM, WINDOW), BF16)
        vcarry_ref[:, SW_HEAD_DIM:VT_ROWS, :] = jnp.ones((SW_KV_HEADS, BF16_SUBLANES, WINDOW), BF16)
        vt_ref[:, SW_HEAD_DIM:VT_ROWS, :] = jnp.ones(
            (SW_KV_HEADS, BF16_SUBLANES, STEP_BLOCKS * WINDOW), BF16)

    lbl = lbl_ref[...]
    lbl_e = jnp.exp(lbl - jnp.max(lbl, axis=0, keepdims=True))
    lb = lbl_e[0:1] / jnp.sum(lbl_e, axis=0, keepdims=True)
    nw = nw_ref[...] * (HG_HEAD_DIM ** 0.5)
    row = lax.broadcasted_iota(jnp.int32, (HG_CHUNK, HG_CHUNK), 0)
    col = lax.broadcasted_iota(jnp.int32, (HG_CHUNK, HG_CHUNK), 1)
    tril = col <= row
    cum_mat = tril.astype(BF16)
    cum3 = jnp.concatenate([cum_mat, cum_mat, cum_mat], axis=1)

    lane = lax.broadcasted_iota(jnp.int32, (1, LANES), 1)
    rope_first = (lane % SW_HEAD_DIM) < (ROPE_DIM // 2)
    lane_lo = lane < SW_HEAD_DIM
    key_idx = lax.broadcasted_iota(jnp.int32, (2 * WINDOW, 2 * WINDOW), 0)
    qry_idx = lax.broadcasted_iota(jnp.int32, (2 * WINDOW, 2 * WINDOW), 1) % WINDOW
    band = (key_idx > qry_idx) & (key_idx <= qry_idx + WINDOW)
    pair_lo = lax.broadcasted_iota(jnp.int32, (1, 2 * WINDOW), 1) < WINDOW
    scale = SW_HEAD_DIM ** -0.5 * LOG2_E
    first_valid = jnp.where(u > 0, 0, WINDOW)
    band_cap = jnp.where(band, jnp.inf, -jnp.inf)
    first_cap = jnp.where(band & (key_idx >= first_valid), jnp.inf, -jnp.inf)

    def rope(v, cos, sin):
        partner = jnp.where(rope_first,
                            pltpu.roll(v, LANES - ROPE_DIM // 2, 1),
                            pltpu.roll(v, ROPE_DIM // 2, 1))
        return v * cos + partner * sin

    def head_variants(v):
        sw = pltpu.roll(v, SW_HEAD_DIM, 1)
        zero = jnp.zeros_like(v)
        return (jnp.where(lane_lo, v, zero), jnp.where(lane_lo, zero, sw),
                jnp.where(lane_lo, sw, zero), jnp.where(lane_lo, zero, v))

    def in_projection_piece(k, piece):
        xb = xb_ref.at[k % 2]
        cols = slice(piece * PIECE, (piece + 1) * PIECE)
        h_bufs[k % 2][:, cols] = jnp.dot(xb[...], win_ref[:, cols], preferred_element_type=F32)

    def out_projection_piece(k, piece):
        cols = slice(piece * PIECE, (piece + 1) * PIECE)
        out_ref[:, cols] = jnp.dot(mix_ref[k], wout_ref[:, cols], preferred_element_type=F32)

    def post_norm(k):
        rows = slice(k * tt, (k + 1) * tt)
        z = xc_ref[0, rows, :] + out_ref[...]
        mu = jnp.mean(z, axis=-1, keepdims=True)
        zc = z - mu
        var = jnp.mean(zc * zc, axis=-1, keepdims=True)
        o_ref[0, rows, :] = ((zc * lax.rsqrt(var + LN_EPS / DN_ALPHA ** 2)) * lng_ref[...]
                             + lnb_ref[...])

    def make_fillers(k):
        stages = []
        if k > 0:
            for piece in range(N_OUT_PIECES):
                stages.append(lambda piece=piece: out_projection_piece(k - 1, piece))
            stages.append(lambda: post_norm(k - 1))
        for piece in range(N_IN_PIECES):
            def in_stage(piece=piece):
                in_projection_piece(k + 1, piece)
                projected[k + 1] = piece + 1
            stages.append(in_stage)
        return iter(stages)

    projected = {}

    def process_tile(k, head):
        h_ref = h_bufs[k % 2]
        base = k * tt
        mix = mix_ref.at[k]
        if k + 1 < TILES_PER_STEP:
            x_next_rows = xc_ref[0, (k + 1) * tt:(k + 2) * tt, :]
        else:
            x_next_rows = xn_ref[0]
        xb_ref[(k + 1) % 2] = x_next_rows.astype(BF16)
        fillers = make_fillers(k)

        def fill(n=1):
            for _ in range(n):
                stage = next(fillers, None)
                if stage is not None:
                    stage()

        def hg_pre(c, h=h_ref):
            rows = slice(c * HG_CHUNK, (c + 1) * HG_CHUNK)
            f = lb + (1.0 - lb) * _sigmoid(h[rows, OFF_HF:OFF_HF + HG_WIDTH])
            log_f = jnp.log2(f)
            hi = log_f.astype(BF16)
            rem = log_f - hi.astype(F32)
            mid = rem.astype(BF16)
            lo = (rem - mid.astype(F32)).astype(BF16)
            return 1.0 - f, jnp.concatenate([hi, mid, lo], axis=0)

        def hg_cum(parts):
            return jnp.dot(cum3, parts, preferred_element_type=F32)

        def hg_decays(c, k_in, g_cum, h=h_ref):
            rows = slice(c * HG_CHUNK, (c + 1) * HG_CHUNK)
            hq = h[rows, OFF_HQ:OFF_HQ + HG_WIDTH]
            g_last = g_cum[HG_CHUNK - 1:HG_CHUNK, :]
            q_dec = ((hq * _sigmoid(hq)) * jnp.exp2(g_cum)).astype(BF16)
            k_dec = (k_in * jnp.exp2(-g_cum)).astype(BF16)
            k_tail = (k_in * jnp.exp2(g_last - g_cum)).astype(BF16)
            return q_dec, k_dec, k_tail, jnp.exp2(g_last)

        def hg_scores(q_dec, k_dec, hd):
            sl = slice(hd * HG_HEAD_DIM, (hd + 1) * HG_HEAD_DIM)
            return lax.dot_general(q_dec[:, sl], k_dec[:, sl], NT_DIMS, preferred_element_type=F32)

        def hg_output(c, hd, a, q_dec, k_tail, decay):
            rows = slice(c * HG_CHUNK, (c + 1) * HG_CHUNK)
            sl = slice(hd * HG_HEAD_DIM, (hd + 1) * HG_HEAD_DIM)
            v = h_ref[rows, OFF_HI + hd * HG_HEAD_DIM:OFF_HI + (hd + 1) * HG_HEAD_DIM].astype(BF16)
            s_t = st_ref[hd]
            o = jnp.dot(jnp.where(tril, a, 0.0).astype(BF16), v, preferred_element_type=F32)
            o = o + lax.dot_general(q_dec[:, sl], s_t.astype(BF16), NT_DIMS,
                                    preferred_element_type=F32)
            inc_t = lax.dot_general(v, k_tail[:, sl], TN_DIMS, preferred_element_type=F32)
            st_ref[hd] = s_t * decay[:, sl] + inc_t
            o = o * lax.rsqrt(jnp.sum(o * o, axis=-1, keepdims=True) + HG_HEAD_DIM * RMS_EPS)
            o = o * nw[:, sl]
            hg = h_ref[rows, OFF_HG + hd * HG_HEAD_DIM:OFF_HG + (hd + 1) * HG_HEAD_DIM]
            mix[rows, sl] = (o * (hg * _sigmoid(hg))).astype(BF16)

        def sw_prepare(blk):
            r0 = blk * WINDOW
            rows = slice(r0, r0 + WINDOW)
            cur = slice(base + r0, base + r0 + WINDOW)
            cos = cos_ref[base + r0:base + r0 + WINDOW, :]
            sin = sin_ref[base + r0:base + r0 + WINDOW, :]
            k_rot = rope(h_ref[rows, OFF_AK:OFF_AK + SW_KV_WIDTH], cos, sin)
            for idx, kv in enumerate(head_variants(k_rot)):
                kwin_ref[idx, cur, :] = kv.astype(BF16)
            v_t = h_ref[rows, OFF_AV:OFF_AV + SW_KV_WIDTH].T
            for g in range(SW_KV_HEADS):
                vt_ref[g, 0:SW_HEAD_DIM, cur] = (
                    v_t[g * SW_HEAD_DIM:(g + 1) * SW_HEAD_DIM].astype(BF16))
            q_bf = []
            for tile in range(SW_WIDTH // LANES):
                q_rot = rope(h_ref[rows, OFF_AQ + tile * LANES:OFF_AQ + (tile + 1) * LANES], cos, sin)
                q_bf.append((q_rot * scale).astype(BF16))
            return [jnp.concatenate([q_bf[2 * g], q_bf[2 * g + 1]], axis=0)
                    for g in range(SW_KV_HEADS)]

        def key_window(blk, idx):
            r0 = base + blk * WINDOW
            if r0 == 0:
                return jnp.concatenate([kcarry_ref[idx], kwin_ref[idx, 0:WINDOW, :]], axis=0)
            return kwin_ref[idx, r0 - WINDOW:r0 + WINDOW, :]

        def value_window(blk, g):
            r0 = base + blk * WINDOW
            if r0 == 0:
                return jnp.concatenate([vcarry_ref[g], vt_ref[g, :, 0:WINDOW]], axis=1)
            return vt_ref[g, :, r0 - WINDOW:r0 + WINDOW]

        def sw_scores(blk, g, j, q_pair):
            return lax.dot_general(key_window(blk, 2 * g + j), q_pair, NT_DIMS,
                                   preferred_element_type=F32)

        def sw_values(blk, g, j, s_t, cap):
            sink_row = jnp.where(pair_lo, sinks_ref[SW_GROUP * g + j],
                                 sinks_ref[SW_GROUP * g + HEADS_PER_TILE + j]) * LOG2_E
            s_t = jnp.minimum(s_t, cap)
            m = jnp.maximum(jnp.max(s_t, axis=0, keepdims=True), sink_row)
            p_t = jnp.exp2(s_t - m).astype(BF16)
            acc = jnp.dot(value_window(blk, g), p_t, preferred_element_type=F32)
            denom = acc[SW_HEAD_DIM:SW_HEAD_DIM + 1, :] + jnp.exp2(sink_row - m)
            return acc[0:SW_HEAD_DIM, :] * (1.0 / denom)

        def sw_finish(blk, g, o_t):
            rows = slice(blk * WINDOW, (blk + 1) * WINDOW)
            for tl in range(2):
                tile = 2 * g + tl
                lanes = slice(tl * LANES, (tl + 1) * LANES)
                o_a = jnp.concatenate([o_t[0][:, lanes], o_t[1][:, lanes]], axis=0).T
                ag = h_ref[rows, OFF_AG + tile * LANES:OFF_AG + (tile + 1) * LANES]
                mix[rows, HG_WIDTH + tile * LANES:HG_WIDTH + (tile + 1) * LANES] = (
                    o_a * (ag * _sigmoid(ag))).astype(BF16)

        many = k > 0
        masks = [first_cap if (k == 0 and blk == 0) else band_cap for blk in range(N_BLOCKS)]
        chains = [(blk, g, j) for blk in range(N_BLOCKS) for g in range(SW_KV_HEADS)
                  for j in range(HEADS_PER_TILE)]
        q_pairs = {}
        scores = {}

        def sw_issue_scores(i):
            if i < len(chains) and i not in scores:
                blk, g, j = chains[i]
                if blk not in q_pairs:
                    q_pairs[blk] = sw_prepare(blk)
                scores[i] = sw_scores(blk, g, j, q_pairs[blk][g])

        head = head or {"pre": {}, "cum": {}, "dec": {}, "scores": {}}
        pre = dict(head["pre"])
        cum = dict(head["cum"])

        def hg_issue_cum(c):
            if c < N_CHUNKS and c not in cum:
                pre[c] = hg_pre(c)
                cum[c] = hg_cum(pre[c][1])

        fill(2 if many else 1)
        for c in range(min(2, N_CHUNKS)):
            hg_issue_cum(c)
            fill(1 if many else 0)
        for c in range(N_CHUNKS):
            if c in head["dec"]:
                q_dec, k_dec, k_tail, decay = head["dec"][c]
                hg_s = head["scores"][c]
            else:
                q_dec, k_dec, k_tail, decay = hg_decays(c, pre[c][0], cum[c])
                hg_s = [hg_scores(q_dec, k_dec, hd) for hd in range(HG_HEADS)]
            hg_issue_cum(c + 2)
            fill()
            for hd in range(HG_HEADS):
                hg_output(c, hd, hg_s[hd], q_dec, k_tail, decay)
            fill(1 if many else 0)

        for i in range(min(SW_SCORES_AHEAD, len(chains))):
            sw_issue_scores(i)
        fill()
        outs = {}
        has_next = k + 1 < TILES_PER_STEP
        next_head = {"pre": {}, "cum": {}, "dec": {}, "scores": {}} if has_next else None
        h_next = h_bufs[(k + 1) % 2]

        def head_start(i):
            if i != HEAD_AFTER_CHAIN:
                return
            assert projected.get(k + 1, 0) * PIECE >= OFF_HF + HG_WIDTH
            for c in range(HEAD_CUM_CHUNKS):
                next_head["pre"][c] = hg_pre(c, h_next)
                next_head["cum"][c] = hg_cum(next_head["pre"][c][1])
            for c in range(HEAD_DEC_CHUNKS):
                d = hg_decays(c, next_head["pre"][c][0], next_head["cum"][c], h_next)
                next_head["dec"][c] = d
                next_head["scores"][c] = [hg_scores(d[0], d[1], hd) for hd in range(HG_HEADS)]

        for i, (blk, g, j) in enumerate(chains):
            outs[(blk, g, j)] = sw_values(blk, g, j, scores.pop(i), masks[blk])
            sw_issue_scores(i + SW_SCORES_AHEAD)
            if j == HEADS_PER_TILE - 1:
                sw_finish(blk, g, [outs.pop((blk, g, jj)) for jj in range(HEADS_PER_TILE)])
            fill()
            if has_next:
                head_start(i)
        fill(N_OUT_PIECES + 1 + N_IN_PIECES)
        return next_head

    head = None
    for k in range(TILES_PER_STEP):
        head = process_tile(k, head)
    for piece in range(N_OUT_PIECES):
        out_projection_piece(TILES_PER_STEP - 1, piece)
    post_norm(TILES_PER_STEP - 1)

    last = slice((STEP_BLOCKS - 1) * WINDOW, STEP_BLOCKS * WINDOW)
    for idx in range(2 * SW_KV_HEADS):
        kcarry_ref[idx] = kwin_ref[idx, last, :]
    for g in range(SW_KV_HEADS):
        vcarry_ref[g, 0:SW_HEAD_DIM, :] = vt_ref[g, 0:SW_HEAD_DIM, last]


def _rope_tables(seq_len):
    pos = jnp.arange(seq_len, dtype=F32)
    inv_freq = ROPE_THETA ** (-jnp.arange(0, ROPE_DIM, 2, dtype=F32) / ROPE_DIM)
    ang = pos[:, None] * inv_freq[None, :]
    cos = jnp.cos(ang)
    sin = jnp.sin(ang)
    ones = jnp.ones((seq_len, SW_HEAD_DIM - ROPE_DIM), F32)
    cos_head = jnp.concatenate([cos, cos, ones], axis=1)
    sin_head = jnp.concatenate([-sin, sin, jnp.zeros_like(ones)], axis=1)
    return (jnp.tile(cos_head, (1, HEADS_PER_TILE)), jnp.tile(sin_head, (1, HEADS_PER_TILE)))


def kernel(x, w_in, lb_logits, hg_norm_w, sinks, w_out, ln_g, ln_b):
    batch, seq_len, d_model = x.shape
    assert d_model == D_MODEL and w_in.shape == (DEPTH, D_MODEL, IN_WIDTH)
    step_rows = TILES_PER_STEP * TIME_TILE
    assert seq_len % step_rows == 0
    cos_tab, sin_tab = _rope_tables(seq_len)
    n_steps = seq_len // step_rows
    last_step = batch * n_steps - 1

    def next_tile_index(b, u, *_):
        nxt = jnp.minimum(b * n_steps + u + 1, last_step)
        return (nxt // n_steps, TILES_PER_STEP * (nxt % n_steps), 0)

    const = lambda b, u, *_: (0, 0)
    grid_spec = pltpu.PrefetchScalarGridSpec(
        num_scalar_prefetch=1,
        grid=(batch, n_steps),
        in_specs=[
            pl.BlockSpec((1, step_rows, D_MODEL), lambda b, u, *_: (b, u, 0)),
            pl.BlockSpec((1, TIME_TILE, D_MODEL), next_tile_index),
            pl.BlockSpec(memory_space=pl.ANY),
            pl.BlockSpec(memory_space=pl.ANY),
            pl.BlockSpec((DEPTH + 1, HG_WIDTH), const),
            pl.BlockSpec((1, HG_WIDTH), const),
            pl.BlockSpec((1, D_MODEL), const),
            pl.BlockSpec((1, D_MODEL), const),
            pl.BlockSpec((step_rows, LANES), lambda b, u, *_: (u, 0)),
            pl.BlockSpec((step_rows, LANES), lambda b, u, *_: (u, 0)),
        ],
        out_specs=pl.BlockSpec((1, step_rows, D_MODEL), lambda b, u, *_: (b, u, 0)),
        scratch_shapes=[
            pltpu.VMEM((TIME_TILE, IN_WIDTH), F32),
            pltpu.VMEM((TIME_TILE, IN_WIDTH), F32),
            pltpu.VMEM((2, TIME_TILE, D_MODEL), BF16),
            pltpu.VMEM((TILES_PER_STEP, TIME_TILE, D_MODEL), BF16),
            pltpu.VMEM((TIME_TILE, D_MODEL), F32),
            pltpu.VMEM((HG_HEADS, HG_HEAD_DIM, HG_HEAD_DIM), F32),
            pltpu.VMEM((2 * SW_KV_HEADS, STEP_BLOCKS * WINDOW, LANES), BF16),
            pltpu.VMEM((SW_KV_HEADS, VT_ROWS, STEP_BLOCKS * WINDOW), BF16),
            pltpu.VMEM((2 * SW_KV_HEADS, WINDOW, LANES), BF16),
            pltpu.VMEM((SW_KV_HEADS, VT_ROWS, WINDOW), BF16),
            pltpu.VMEM((D_MODEL, IN_WIDTH), BF16),
            pltpu.VMEM((D_MODEL, D_MODEL), BF16),
            pltpu.VMEM((WEIGHT_SLOTS, WEIGHT_CHUNK_ROWS, IN_WIDTH), F32),
            pltpu.VMEM((WEIGHT_SLOTS, WEIGHT_CHUNK_ROWS, D_MODEL), F32),
            pltpu.SemaphoreType.DMA((WEIGHT_SLOTS,)),
            pltpu.SemaphoreType.DMA((WEIGHT_SLOTS,)),
        ],
    )
    return pl.pallas_call(
        _layer_kernel,
        grid_spec=grid_spec,
        out_shape=jax.ShapeDtypeStruct(x.shape, x.dtype),
        compiler_params=pltpu.CompilerParams(
            dimension_semantics=("arbitrary", "arbitrary"),
            vmem_limit_bytes=VMEM_LIMIT_BYTES),
        name="hybrid_hgrn2_swa_layer",
    )(sinks[0], x, x, w_in, w_out, lb_logits, hg_norm_w, ln_g, ln_b, cos_tab, sin_tab)
```

```python
import jax
import jax.numpy as jnp
from jax import lax
from jax.experimental import pallas as pl
from jax.experimental.pallas import tpu as pltpu

D_MODEL = 1024
DEPTH = 1
HG_WIDTH = 512
HG_HEAD_DIM = 128
HG_HEADS = HG_WIDTH // HG_HEAD_DIM
HG_CHUNK = 64
SW_WIDTH = 512
SW_HEAD_DIM = 64
SW_Q_HEADS = SW_WIDTH // SW_HEAD_DIM
SW_KV_HEADS = SW_Q_HEADS // 4
SW_KV_WIDTH = SW_KV_HEADS * SW_HEAD_DIM
SW_GROUP = SW_Q_HEADS // SW_KV_HEADS
WINDOW = 128
ROPE_THETA = 500000.0
ROPE_DIM = SW_HEAD_DIM // 4
DN_ALPHA = (2.0 * DEPTH) ** 0.25
LN_EPS = 1e-5
RMS_EPS = 1e-6
IN_WIDTH = 4 * HG_WIDTH + SW_WIDTH + 2 * SW_KV_WIDTH + SW_WIDTH

OFF_HQ = 0
OFF_HF = OFF_HQ + HG_WIDTH
OFF_HI = OFF_HF + HG_WIDTH
OFF_HG = OFF_HI + HG_WIDTH
OFF_AQ = OFF_HG + HG_WIDTH
OFF_AK = OFF_AQ + SW_WIDTH
OFF_AV = OFF_AK + SW_KV_WIDTH
OFF_AG = OFF_AV + SW_KV_WIDTH

LANES = 128
BF16_SUBLANES = 16
MXU_WIDTH = 256
HEADS_PER_TILE = LANES // SW_HEAD_DIM
assert HEADS_PER_TILE == 2 and SW_GROUP == 2 * HEADS_PER_TILE and SW_KV_WIDTH == LANES
VT_ROWS = SW_HEAD_DIM + BF16_SUBLANES
TIME_TILE = 256
TILES_PER_STEP = 4
PIECE = MXU_WIDTH
N_IN_PIECES = IN_WIDTH // PIECE
N_OUT_PIECES = D_MODEL // PIECE
assert IN_WIDTH % PIECE == 0 and D_MODEL % PIECE == 0
N_CHUNKS = TIME_TILE // HG_CHUNK
N_BLOCKS = TIME_TILE // WINDOW
STEP_BLOCKS = TILES_PER_STEP * N_BLOCKS
HEAD_CUM_CHUNKS = 2
HEAD_DEC_CHUNKS = 1
HEAD_AFTER_CHAIN = 3
SW_SCORES_AHEAD = 8
WEIGHT_CHUNK_ROWS = 64
WEIGHT_SLOTS = 4
VMEM_LIMIT_BYTES = 58 * 1024 * 1024

F32 = jnp.float32
BF16 = jnp.bfloat16
NT_DIMS = (((1,), (1,)), ((), ()))
TN_DIMS = (((0,), (0,)), ((), ()))


LOG2_E = 1.4426950408889634


def _sigmoid(v):
    return 1.0 / (1.0 + jnp.exp2(v * (-LOG2_E)))


def _load_weight_as_bf16(w_hbm, stage_ref, sem_ref, dst_ref, scale=None):
    n_rows = dst_ref.shape[0]
    n_slots, chunk = stage_ref.shape[0], stage_ref.shape[1]
    n_chunks = n_rows // chunk
    assert n_chunks * chunk == n_rows and n_slots >= 2

    def chunk_copy(i):
        return pltpu.make_async_copy(w_hbm.at[0, pl.ds(i * chunk, chunk), :],
                                     stage_ref.at[i % n_slots], sem_ref.at[i % n_slots])

    for i in range(min(n_slots - 1, n_chunks)):
        chunk_copy(i).start()
    for i in range(n_chunks):
        ahead = i + n_slots - 1
        if ahead < n_chunks:
            chunk_copy(ahead).start()
        chunk_copy(i).wait()
        rows = stage_ref[i % n_slots]
        if scale is not None:
            rows = rows * scale
        dst_ref[i * chunk:(i + 1) * chunk, :] = rows.astype(BF16)


def _layer_kernel(sinks_ref, xc_ref, xn_ref, win_hbm, wout_hbm, lbl_ref, nw_ref, lng_ref, lnb_ref,
                  cos_ref, sin_ref, o_ref,
                  ha_ref, hb_ref, xb_ref, mix_ref, out_ref, st_ref, kwin_ref, vt_ref,
                  kcarry_ref, vcarry_ref, win_ref, wout_ref, win_stage, wout_stage,
                  win_sems, wout_sems):
    b = pl.program_id(0)
    u = pl.program_id(1)
    tt = TIME_TILE
    h_bufs = (ha_ref, hb_ref)

    @pl.when(jnp.logical_and(b == 0, u == 0))
    def _first_step():
        _load_weight_as_bf16(win_hbm, win_stage, win_sems, win_ref)
        _load_weight_as_bf16(wout_hbm, wout_stage, wout_sems, wout_ref, scale=1.0 / DN_ALPHA)
        ha_ref[...] = jnp.dot(xc_ref[0, 0:tt, :].astype(BF16), win_ref[...],
                              preferred_element_type=F32)

    @pl.when(u == 0)
    def _reset_carries():
        st_ref[...] = jnp.zeros_like(st_ref)
        kcarry_ref[...] = jnp.zeros_like(kcarry_ref)
        vcarry_ref[:, 0:SW_HEAD_DIM, :] = jnp.zeros((SW_KV_HEADS, SW_HEAD_DIM, WINDOW), BF16)
        vcarry_ref[:, SW_HEAD_DIM:VT_ROWS, :] = jnp.ones((SW_KV_HEADS, BF16_SUBLANES, WINDOW), BF16)
        vt_ref[:, SW_HEAD_DIM:VT_ROWS, :] = jnp.ones(
            (SW_KV_HEADS, BF16_SUBLANES, STEP_BLOCKS * WINDOW), BF16)

    lbl = lbl_ref[...]
    lbl_e = jnp.exp(lbl - jnp.max(lbl, axis=0, keepdims=True))
    lb = lbl_e[0:1] / jnp.sum(lbl_e, axis=0, keepdims=True)
    nw = nw_ref[...] * (HG_HEAD_DIM ** 0.5)
    row = lax.broadcasted_iota(jnp.int32, (HG_CHUNK, HG_CHUNK), 0)
    col = lax.broadcasted_iota(jnp.int32, (HG_CHUNK, HG_CHUNK), 1)
    tril = col <= row
    cum_mat = tril.astype(BF16)
    cum3 = jnp.concatenate([cum_mat, cum_mat, cum_mat], axis=1)

    lane = lax.broadcasted_iota(jnp.int32, (1, LANES), 1)
    rope_first = (lane % SW_HEAD_DIM) < (ROPE_DIM // 2)
    lane_lo = lane < SW_HEAD_DIM
    key_idx = lax.broadcasted_iota(jnp.int32, (2 * WINDOW, 2 * WINDOW), 0)
    qry_idx = lax.broadcasted_iota(jnp.int32, (2 * WINDOW, 2 * WINDOW), 1) % WINDOW
    band = (key_idx > qry_idx) & (key_idx <= qry_idx + WINDOW)
    pair_lo = lax.broadcasted_iota(jnp.int32, (1, 2 * WINDOW), 1) < WINDOW
    scale = SW_HEAD_DIM ** -0.5 * LOG2_E
    first_valid = jnp.where(u > 0, 0, WINDOW)
    band_cap = jnp.where(band, jnp.inf, -jnp.inf)
    first_cap = jnp.where(band & (key_idx >= first_valid), jnp.inf, -jnp.inf)

    def rope(v, cos, sin):
        partner = jnp.where(rope_first,
                            pltpu.roll(v, LANES - ROPE_DIM // 2, 1),
                            pltpu.roll(v, ROPE_DIM // 2, 1))
        return v * cos + partner * sin

    def head_variants(v):
        sw = pltpu.roll(v, SW_HEAD_DIM, 1)
        zero = jnp.zeros_like(v)
        return (jnp.where(lane_lo, v, zero), jnp.where(lane_lo, zero, sw),
                jnp.where(lane_lo, sw, zero), jnp.where(lane_lo, zero, v))

    def in_projection_piece(k, piece):
        xb = xb_ref.at[k % 2]
        cols = slice(piece * PIECE, (piece + 1) * PIECE)
        h_bufs[k % 2][:, cols] = jnp.dot(xb[...], win_ref[:, cols], preferred_element_type=F32)

    def out_projection_piece(k, piece):
        cols = slice(piece * PIECE, (piece + 1) * PIECE)
        out_ref[:, cols] = jnp.dot(mix_ref[k], wout_ref[:, cols], preferred_element_type=F32)

    def post_norm(k):
        rows = slice(k * tt, (k + 1) * tt)
        z = xc_ref[0, rows, :] + out_ref[...]
        mu = jnp.mean(z, axis=-1, keepdims=True)
        zc = z - mu
        var = jnp.mean(zc * zc, axis=-1, keepdims=True)
        o_ref[0, rows, :] = ((zc * lax.rsqrt(var + LN_EPS / DN_ALPHA ** 2)) * lng_ref[...]
                             + lnb_ref[...])

    def make_fillers(k):
        stages = []
        if k > 0:
            for piece in range(N_OUT_PIECES):
                stages.append(lambda piece=piece: out_projection_piece(k - 1, piece))
            stages.append(lambda: post_norm(k - 1))
        for piece in range(N_IN_PIECES):
            def in_stage(piece=piece):
                in_projection_piece(k + 1, piece)
                projected[k + 1] = piece + 1
            stages.append(in_stage)
        return iter(stages)

    projected = {}

    def process_tile(k, head):
        h_ref = h_bufs[k % 2]
        base = k * tt
        mix = mix_ref.at[k]
        if k + 1 < TILES_PER_STEP:
            x_next_rows = xc_ref[0, (k + 1) * tt:(k + 2) * tt, :]
        else:
            x_next_rows = xn_ref[0]
        xb_ref[(k + 1) % 2] = x_next_rows.astype(BF16)
        fillers = make_fillers(k)

        def fill(n=1):
            for _ in range(n):
                stage = next(fillers, None)
                if stage is not None:
                    stage()

        def hg_pre(c, h=h_ref):
            rows = slice(c * HG_CHUNK, (c + 1) * HG_CHUNK)
            f = lb + (1.0 - lb) * _sigmoid(h[rows, OFF_HF:OFF_HF + HG_WIDTH])
            log_f = jnp.log2(f)
            hi = log_f.astype(BF16)
            rem = log_f - hi.astype(F32)
            mid = rem.astype(BF16)
            lo = (rem - mid.astype(F32)).astype(BF16)
            return 1.0 - f, jnp.concatenate([hi, mid, lo], axis=0)

        def hg_cum(parts):
            return jnp.dot(cum3, parts, preferred_element_type=F32)

        def hg_decays(c, k_in, g_cum, h=h_ref):
            rows = slice(c * HG_CHUNK, (c + 1) * HG_CHUNK)
            hq = h[rows, OFF_HQ:OFF_HQ + HG_WIDTH]
            g_last = g_cum[HG_CHUNK - 1:HG_CHUNK, :]
            q_dec = ((hq * _sigmoid(hq)) * jnp.exp2(g_cum)).astype(BF16)
            k_dec = (k_in * jnp.exp2(-g_cum)).astype(BF16)
            k_tail = (k_in * jnp.exp2(g_last - g_cum)).astype(BF16)
            return q_dec, k_dec, k_tail, jnp.exp2(g_last)

        def hg_scores(q_dec, k_dec, hd):
            sl = slice(hd * HG_HEAD_DIM, (hd + 1) * HG_HEAD_DIM)
            return lax.dot_general(q_dec[:, sl], k_dec[:, sl], NT_DIMS, preferred_element_type=F32)

        def hg_output(c, hd, a, q_dec, k_tail, decay):
            rows = slice(c * HG_CHUNK, (c + 1) * HG_CHUNK)
            sl = slice(hd * HG_HEAD_DIM, (hd + 1) * HG_HEAD_DIM)
            v = h_ref[rows, OFF_HI + hd * HG_HEAD_DIM:OFF_HI + (hd + 1) * HG_HEAD_DIM].astype(BF16)
            s_t = st_ref[hd]
            o = jnp.dot(jnp.where(tril, a, 0.0).astype(BF16), v, preferred_element_type=F32)
            o = o + lax.dot_general(q_dec[:, sl], s_t.astype(BF16), NT_DIMS,
                                    preferred_element_type=F32)
            inc_t = lax.dot_general(v, k_tail[:, sl], TN_DIMS, preferred_element_type=F32)
            st_ref[hd] = s_t * decay[:, sl] + inc_t
            o = o * lax.rsqrt(jnp.sum(o * o, axis=-1, keepdims=True) + HG_HEAD_DIM * RMS_EPS)
            o = o * nw[:, sl]
            hg = h_ref[rows, OFF_HG + hd * HG_HEAD_DIM:OFF_HG + (hd + 1) * HG_HEAD_DIM]
            mix[rows, sl] = (o * (hg * _sigmoid(hg))).astype(BF16)

        def sw_prepare(blk):
            r0 = blk * WINDOW
            rows = slice(r0, r0 + WINDOW)
            cur = slice(base + r0, base + r0 + WINDOW)
            cos = cos_ref[base + r0:base + r0 + WINDOW, :]
            sin = sin_ref[base + r0:base + r0 + WINDOW, :]
            k_rot = rope(h_ref[rows, OFF_AK:OFF_AK + SW_KV_WIDTH], cos, sin)
            for idx, kv in enumerate(head_variants(k_rot)):
                kwin_ref[idx, cur, :] = kv.astype(BF16)
            v_t = h_ref[rows, OFF_AV:OFF_AV + SW_KV_WIDTH].T
            for g in range(SW_KV_HEADS):
                vt_ref[g, 0:SW_HEAD_DIM, cur] = (
                    v_t[g * SW_HEAD_DIM:(g + 1) * SW_HEAD_DIM].astype(BF16))
            q_bf = []
            for tile in range(SW_WIDTH // LANES):
                q_rot = rope(h_ref[rows, OFF_AQ + tile * LANES:OFF_AQ + (tile + 1) * LANES], cos, sin)
                q_bf.append((q_rot * scale).astype(BF16))
            return [jnp.concatenate([q_bf[2 * g], q_bf[2 * g + 1]], axis=0)
                    for g in range(SW_KV_HEADS)]

        def key_window(blk, idx):
            r0 = base + blk * WINDOW
            if r0 == 0:
                return jnp.concatenate([kcarry_ref[idx], kwin_ref[idx, 0:WINDOW, :]], axis=0)
            return kwin_ref[idx, r0 - WINDOW:r0 + WINDOW, :]

        def value_window(blk, g):
            r0 = base + blk * WINDOW
            if r0 == 0:
                return jnp.concatenate([vcarry_ref[g], vt_ref[g, :, 0:WINDOW]], axis=1)
            return vt_ref[g, :, r0 - WINDOW:r0 + WINDOW]

        def sw_scores(blk, g, j, q_pair):
            return lax.dot_general(key_window(blk, 2 * g + j), q_pair, NT_DIMS,
                                   preferred_element_type=F32)

        def sw_values(blk, g, j, s_t, cap):
            sink_row = jnp.where(pair_lo, sinks_ref[SW_GROUP * g + j],
                                 sinks_ref[SW_GROUP * g + HEADS_PER_TILE + j]) * LOG2_E
            s_t = jnp.minimum(s_t, cap)
            m = jnp.maximum(jnp.max(s_t, axis=0, keepdims=True), sink_row)
            p_t = jnp.exp2(s_t - m).astype(BF16)
            acc = jnp.dot(value_window(blk, g), p_t, preferred_element_type=F32)
            denom = acc[SW_HEAD_DIM:SW_HEAD_DIM + 1, :] + jnp.exp2(sink_row - m)
            return acc[0:SW_HEAD_DIM, :] * (1.0 / denom)

        def sw_finish(blk, g, o_t):
            rows = slice(blk * WINDOW, (blk + 1) * WINDOW)
            for tl in range(2):
                tile = 2 * g + tl
                lanes = slice(tl * LANES, (tl + 1) * LANES)
                o_a = jnp.concatenate([o_t[0][:, lanes], o_t[1][:, lanes]], axis=0).T
                ag = h_ref[rows, OFF_AG + tile * LANES:OFF_AG + (tile + 1) * LANES]
                mix[rows, HG_WIDTH + tile * LANES:HG_WIDTH + (tile + 1) * LANES] = (
                    o_a * (ag * _sigmoid(ag))).astype(BF16)

        many = k > 0
        masks = [first_cap if (k == 0 and blk == 0) else band_cap for blk in range(N_BLOCKS)]
        chains = [(blk, g, j) for blk in range(N_BLOCKS) for g in range(SW_KV_HEADS)
                  for j in range(HEADS_PER_TILE)]
        q_pairs = {}
        scores = {}

        def sw_issue_scores(i):
            if i < len(chains) and i not in scores:
                blk, g, j = chains[i]
                if blk not in q_pairs:
                    q_pairs[blk] = sw_prepare(blk)
                scores[i] = sw_scores(blk, g, j, q_pairs[blk][g])

        head = head or {"pre": {}, "cum": {}, "dec": {}, "scores": {}}
        pre = dict(head["pre"])
        cum = dict(head["cum"])

        def hg_issue_cum(c):
            if c < N_CHUNKS and c not in cum:
                pre[c] = hg_pre(c)
                cum[c] = hg_cum(pre[c][1])

        fill(2 if many else 1)
        for c in range(min(2, N_CHUNKS)):
            hg_issue_cum(c)
            fill(1 if many else 0)
        for c in range(N_CHUNKS):
            if c in head["dec"]:
                q_dec, k_dec, k_tail, decay = head["dec"][c]
                hg_s = head["scores"][c]
            else:
                q_dec, k_dec, k_tail, decay = hg_decays(c, pre[c][0], cum[c])
                hg_s = [hg_scores(q_dec, k_dec, hd) for hd in range(HG_HEADS)]
            hg_issue_cum(c + 2)
            fill()
            for hd in range(HG_HEADS):
                hg_output(c, hd, hg_s[hd], q_dec, k_tail, decay)
            fill(1 if many else 0)

        for i in range(min(SW_SCORES_AHEAD, len(chains))):
            sw_issue_scores(i)
        fill()
        outs = {}
        has_next = k + 1 < TILES_PER_STEP
        next_head = {"pre": {}, "cum": {}, "dec": {}, "scores": {}} if has_next else None
        h_next = h_bufs[(k + 1) % 2]

        def head_start(i):
            if i != HEAD_AFTER_CHAIN:
                return
            assert projected.get(k + 1, 0) * PIECE >= OFF_HF + HG_WIDTH
            for c in range(HEAD_CUM_CHUNKS):
                next_head["pre"][c] = hg_pre(c, h_next)
                next_head["cum"][c] = hg_cum(next_head["pre"][c][1])
            for c in range(HEAD_DEC_CHUNKS):
                d = hg_decays(c, next_head["pre"][c][0], next_head["cum"][c], h_next)
                next_head["dec"][c] = d
                next_head["scores"][c] = [hg_scores(d[0], d[1], hd) for hd in range(HG_HEADS)]

        for i, (blk, g, j) in enumerate(chains):
            outs[(blk, g, j)] = sw_values(blk, g, j, scores.pop(i), masks[blk])
            sw_issue_scores(i + SW_SCORES_AHEAD)
            if j == HEADS_PER_TILE - 1:
                sw_finish(blk, g, [outs.pop((blk, g, jj)) for jj in range(HEADS_PER_TILE)])
            fill()
            if has_next:
                head_start(i)
        fill(N_OUT_PIECES + 1 + N_IN_PIECES)
        return next_head

    head = None
    for k in range(TILES_PER_STEP):
        head = process_tile(k, head)
    for piece in range(N_OUT_PIECES):
        out_projection_piece(TILES_PER_STEP - 1, piece)
    post_norm(TILES_PER_STEP - 1)

    last = slice((STEP_BLOCKS - 1) * WINDOW, STEP_BLOCKS * WINDOW)
    for idx in range(2 * SW_KV_HEADS):
        kcarry_ref[idx] = kwin_ref[idx, last, :]
    for g in range(SW_KV_HEADS):
        vcarry_ref[g, 0:SW_HEAD_DIM, :] = vt_ref[g, 0:SW_HEAD_DIM, last]


def _rope_tables(seq_len):
    pos = jnp.arange(seq_len, dtype=F32)
    inv_freq = ROPE_THETA ** (-jnp.arange(0, ROPE_DIM, 2, dtype=F32) / ROPE_DIM)
    ang = pos[:, None] * inv_freq[None, :]
    cos = jnp.cos(ang)
    sin = jnp.sin(ang)
    ones = jnp.ones((seq_len, SW_HEAD_DIM - ROPE_DIM), F32)
    cos_head = jnp.concatenate([cos, cos, ones], axis=1)
    sin_head = jnp.concatenate([-sin, sin, jnp.zeros_like(ones)], axis=1)
    return (jnp.tile(cos_head, (1, HEADS_PER_TILE)), jnp.tile(sin_head, (1, HEADS_PER_TILE)))


def kernel(x, w_in, lb_logits, hg_norm_w, sinks, w_out, ln_g, ln_b):
    batch, seq_len, d_model = x.shape
    assert d_model == D_MODEL and w_in.shape == (DEPTH, D_MODEL, IN_WIDTH)
    step_rows = TILES_PER_STEP * TIME_TILE
    assert seq_len % step_rows == 0
    cos_tab, sin_tab = _rope_tables(seq_len)
    n_steps = seq_len // step_rows
    last_step = batch * n_steps - 1

    def next_tile_index(b, u, *_):
        nxt = jnp.minimum(b * n_steps + u + 1, last_step)
        return (nxt // n_steps, TILES_PER_STEP * (nxt % n_steps), 0)

    const = lambda b, u, *_: (0, 0)
    grid_spec = pltpu.PrefetchScalarGridSpec(
        num_scalar_prefetch=1,
        grid=(batch, n_steps),
        in_specs=[
            pl.BlockSpec((1, step_rows, D_MODEL), lambda b, u, *_: (b, u, 0)),
            pl.BlockSpec((1, TIME_TILE, D_MODEL), next_tile_index),
            pl.BlockSpec(memory_space=pl.ANY),
            pl.BlockSpec(memory_space=pl.ANY),
            pl.BlockSpec((DEPTH + 1, HG_WIDTH), const),
            pl.BlockSpec((1, HG_WIDTH), const),
            pl.BlockSpec((1, D_MODEL), const),
            pl.BlockSpec((1, D_MODEL), const),
            pl.BlockSpec((step_rows, LANES), lambda b, u, *_: (u, 0)),
            pl.BlockSpec((step_rows, LANES), lambda b, u, *_: (u, 0)),
        ],
        out_specs=pl.BlockSpec((1, step_rows, D_MODEL), lambda b, u, *_: (b, u, 0)),
        scratch_shapes=[
            pltpu.VMEM((TIME_TILE, IN_WIDTH), F32),
            pltpu.VMEM((TIME_TILE, IN_WIDTH), F32),
            pltpu.VMEM((2, TIME_TILE, D_MODEL), BF16),
            pltpu.VMEM((TILES_PER_STEP, TIME_TILE, D_MODEL), BF16),
            pltpu.VMEM((TIME_TILE, D_MODEL), F32),
            pltpu.VMEM((HG_HEADS, HG_HEAD_DIM, HG_HEAD_DIM), F32),
            pltpu.VMEM((2 * SW_KV_HEADS, STEP_BLOCKS * WINDOW, LANES), BF16),
            pltpu.VMEM((SW_KV_HEADS, VT_ROWS, STEP_BLOCKS * WINDOW), BF16),
            pltpu.VMEM((2 * SW_KV_HEADS, WINDOW, LANES), BF16),
            pltpu.VMEM((SW_KV_HEADS, VT_ROWS, WINDOW), BF16),
            pltpu.VMEM((D_MODEL, IN_WIDTH), BF16),
            pltpu.VMEM((D_MODEL, D_MODEL), BF16),
            pltpu.VMEM((WEIGHT_SLOTS, WEIGHT_CHUNK_ROWS, IN_WIDTH), F32),
            pltpu.VMEM((WEIGHT_SLOTS, WEIGHT_CHUNK_ROWS, D_MODEL), F32),
            pltpu.SemaphoreType.DMA((WEIGHT_SLOTS,)),
            pltpu.SemaphoreType.DMA((WEIGHT_SLOTS,)),
        ],
    )
    return pl.pallas_call(
        _layer_kernel,
        grid_spec=grid_spec,
        out_shape=jax.ShapeDtypeStruct(x.shape, x.dtype),
        compiler_params=pltpu.CompilerParams(
            dimension_semantics=("arbitrary", "arbitrary"),
            vmem_limit_bytes=VMEM_LIMIT_BYTES),
        name="hybrid_hgrn2_swa_layer",
    )(sinks[0], x, x, w_in, w_out, lb_logits, hg_norm_w, ln_g, ln_b, cos_tab, sin_tab)
```

```python
import jax
import jax.numpy as jnp
from jax import lax
from jax.experimental import pallas as pl
from jax.experimental.pallas import tpu as pltpu

D_MODEL = 1024
DEPTH = 1
HG_WIDTH = 512
HG_HEAD_DIM = 128
HG_HEADS = HG_WIDTH // HG_HEAD_DIM
HG_CHUNK = 64
SW_WIDTH = 512
SW_HEAD_DIM = 64
SW_Q_HEADS = SW_WIDTH // SW_HEAD_DIM
SW_KV_HEADS = SW_Q_HEADS // 4
SW_KV_WIDTH = SW_KV_HEADS * SW_HEAD_DIM
SW_GROUP = SW_Q_HEADS // SW_KV_HEADS
WINDOW = 128
ROPE_THETA = 500000.0
ROPE_DIM = SW_HEAD_DIM // 4
DN_ALPHA = (2.0 * DEPTH) ** 0.25
LN_EPS = 1e-5
RMS_EPS = 1e-6
IN_WIDTH = 4 * HG_WIDTH + SW_WIDTH + 2 * SW_KV_WIDTH + SW_WIDTH

OFF_HQ = 0
OFF_HF = OFF_HQ + HG_WIDTH
OFF_HI = OFF_HF + HG_WIDTH
OFF_HG = OFF_HI + HG_WIDTH
OFF_AQ = OFF_HG + HG_WIDTH
OFF_AK = OFF_AQ + SW_WIDTH
OFF_AV = OFF_AK + SW_KV_WIDTH
OFF_AG = OFF_AV + SW_KV_WIDTH

LANES = 128
BF16_SUBLANES = 16
MXU_WIDTH = 256
HEADS_PER_TILE = LANES // SW_HEAD_DIM
assert HEADS_PER_TILE == 2 and SW_GROUP == 2 * HEADS_PER_TILE and SW_KV_WIDTH == LANES
VT_ROWS = SW_HEAD_DIM + BF16_SUBLANES
TIME_TILE = 256
TILES_PER_STEP = 4
PIECE = MXU_WIDTH
N_IN_PIECES = IN_WIDTH // PIECE
N_OUT_PIECES = D_MODEL // PIECE
assert IN_WIDTH % PIECE == 0 and D_MODEL % PIECE == 0
N_CHUNKS = TIME_TILE // HG_CHUNK
N_BLOCKS = TIME_TILE // WINDOW
STEP_BLOCKS = TILES_PER_STEP * N_BLOCKS
HEAD_CUM_CHUNKS = 4
HEAD_DEC_CHUNKS = 1
HEAD_AFTER_CHAIN = 3
SW_SCORES_AHEAD = 8
WEIGHT_CHUNK_ROWS = 64
WEIGHT_SLOTS = 4
VMEM_LIMIT_BYTES = 58 * 1024 * 1024

F32 = jnp.float32
BF16 = jnp.bfloat16
NT_DIMS = (((1,), (1,)), ((), ()))
TN_DIMS = (((0,), (0,)), ((), ()))


LOG2_E = 1.4426950408889634


def _sigmoid(v):
    return 1.0 / (1.0 + jnp.exp2(v * (-LOG2_E)))


def _load_weight_as_bf16(w_hbm, stage_ref, sem_ref, dst_ref, scale=None):
    n_rows = dst_ref.shape[0]
    n_slots, chunk = stage_ref.shape[0], stage_ref.shape[1]
    n_chunks = n_rows // chunk
    assert n_chunks * chunk == n_rows and n_slots >= 2

    def chunk_copy(i):
        return pltpu.make_async_copy(w_hbm.at[0, pl.ds(i * chunk, chunk), :],
                                     stage_ref.at[i % n_slots], sem_ref.at[i % n_slots])

    for i in range(min(n_slots - 1, n_chunks)):
        chunk_copy(i).start()
    for i in range(n_chunks):
        ahead = i + n_slots - 1
        if ahead < n_chunks:
            chunk_copy(ahead).start()
        chunk_copy(i).wait()
        rows = stage_ref[i % n_slots]
        if scale is not None:
            rows = rows * scale
        dst_ref[i * chunk:(i + 1) * chunk, :] = rows.astype(BF16)


def _layer_kernel(sinks_ref, xc_ref, xn_ref, win_hbm, wout_hbm, lbl_ref, nw_ref, lng_ref, lnb_ref,
                  cos_ref, sin_ref, o_ref,
                  ha_ref, hb_ref, xb_ref, mix_ref, out_ref, st_ref, kwin_ref, vt_ref,
                  kcarry_ref, vcarry_ref, win_ref, wout_ref, win_stage, wout_stage,
                  win_sems, wout_sems):
    b = pl.program_id(0)
    u = pl.program_id(1)
    tt = TIME_TILE
    h_bufs = (ha_ref, hb_ref)

    @pl.when(jnp.logical_and(b == 0, u == 0))
    def _first_step():
        _load_weight_as_bf16(win_hbm, win_stage, win_sems, win_ref)
        _load_weight_as_bf16(wout_hbm, wout_stage, wout_sems, wout_ref, scale=1.0 / DN_ALPHA)
        ha_ref[...] = jnp.dot(xc_ref[0, 0:tt, :].astype(BF16), win_ref[...],
                              preferred_element_type=F32)

    @pl.when(u == 0)
    def _reset_carries():
        st_ref[...] = jnp.zeros_like(st_ref)
        kcarry_ref[...] = jnp.zeros_like(kcarry_ref)
        vcarry_ref[:, 0:SW_HEAD_DIM, :] = jnp.zeros((SW_KV_HEADS, SW_HEAD_DIM, WINDOW), BF16)
        vcarry_ref[:, SW_HEAD_DIM:VT_ROWS, :] = jnp.ones((SW_KV_HEADS, BF16_SUBLANES, WINDOW), BF16)
        vt_ref[:, SW_HEAD_DIM:VT_ROWS, :] = jnp.ones(
            (SW_KV_HEADS, BF16_SUBLANES, STEP_BLOCKS * WINDOW), BF16)

    lbl = lbl_ref[...]
    lbl_e = jnp.exp(lbl - jnp.max(lbl, axis=0, keepdims=True))
    lb = lbl_e[0:1] / jnp.sum(lbl_e, axis=0, keepdims=True)
    nw = nw_ref[...] * (HG_HEAD_DIM ** 0.5)
    row = lax.broadcasted_iota(jnp.int32, (HG_CHUNK, HG_CHUNK), 0)
    col = lax.broadcasted_iota(jnp.int32, (HG_CHUNK, HG_CHUNK), 1)
    tril = col <= row
    cum_mat = tril.astype(BF16)
    cum3 = jnp.concatenate([cum_mat, cum_mat, cum_mat], axis=1)

    lane = lax.broadcasted_iota(jnp.int32, (1, LANES), 1)
    rope_first = (lane % SW_HEAD_DIM) < (ROPE_DIM // 2)
    lane_lo = lane < SW_HEAD_DIM
    key_idx = lax.broadcasted_iota(jnp.int32, (2 * WINDOW, 2 * WINDOW), 0)
    qry_idx = lax.broadcasted_iota(jnp.int32, (2 * WINDOW, 2 * WINDOW), 1) % WINDOW
    band = (key_idx > qry_idx) & (key_idx <= qry_idx + WINDOW)
    pair_lo = lax.broadcasted_iota(jnp.int32, (1, 2 * WINDOW), 1) < WINDOW
    scale = SW_HEAD_DIM ** -0.5 * LOG2_E
    first_valid = jnp.where(u > 0, 0, WINDOW)
    band_cap = jnp.where(band, jnp.inf, -jnp.inf)
    first_cap = jnp.where(band & (key_idx >= first_valid), jnp.inf, -jnp.inf)

    def rope(v, cos, sin):
        partner = jnp.where(rope_first,
                            pltpu.roll(v, LANES - ROPE_DIM // 2, 1),
                            pltpu.roll(v, ROPE_DIM // 2, 1))
        return v * cos + partner * sin

    def head_variants(v):
        sw = pltpu.roll(v, SW_HEAD_DIM, 1)
        zero = jnp.zeros_like(v)
        return (jnp.where(lane_lo, v, zero), jnp.where(lane_lo, zero, sw),
                jnp.where(lane_lo, sw, zero), jnp.where(lane_lo, zero, v))

    def in_projection_piece(k, piece):
        xb = xb_ref.at[k % 2]
        cols = slice(piece * PIECE, (piece + 1) * PIECE)
        h_bufs[k % 2][:, cols] = jnp.dot(xb[...], win_ref[:, cols], preferred_element_type=F32)

    def out_projection_piece(k, piece):
        cols = slice(piece * PIECE, (piece + 1) * PIECE)
        out_ref[:, cols] = jnp.dot(mix_ref[k], wout_ref[:, cols], preferred_element_type=F32)

    def post_norm(k):
        rows = slice(k * tt, (k + 1) * tt)
        z = xc_ref[0, rows, :] + out_ref[...]
        mu = jnp.mean(z, axis=-1, keepdims=True)
        zc = z - mu
        var = jnp.mean(zc * zc, axis=-1, keepdims=True)
        o_ref[0, rows, :] = ((zc * lax.rsqrt(var + LN_EPS / DN_ALPHA ** 2)) * lng_ref[...]
                             + lnb_ref[...])

    def make_fillers(k):
        stages = []
        if k > 0:
            for piece in range(N_OUT_PIECES):
                stages.append(lambda piece=piece: out_projection_piece(k - 1, piece))
            stages.append(lambda: post_norm(k - 1))
        for piece in range(N_IN_PIECES):
            def in_stage(piece=piece):
                in_projection_piece(k + 1, piece)
                projected[k + 1] = piece + 1
            stages.append(in_stage)
        return iter(stages)

    projected = {}

    def process_tile(k, head):
        h_ref = h_bufs[k % 2]
        base = k * tt
        mix = mix_ref.at[k]
        if k + 1 < TILES_PER_STEP:
            x_next_rows = xc_ref[0, (k + 1) * tt:(k + 2) * tt, :]
        else:
            x_next_rows = xn_ref[0]
        xb_ref[(k + 1) % 2] = x_next_rows.astype(BF16)
        fillers = make_fillers(k)

        def fill(n=1):
            for _ in range(n):
                stage = next(fillers, None)
                if stage is not None:
                    stage()

        def hg_pre(c, h=h_ref):
            rows = slice(c * HG_CHUNK, (c + 1) * HG_CHUNK)
            f = lb + (1.0 - lb) * _sigmoid(h[rows, OFF_HF:OFF_HF + HG_WIDTH])
            log_f = jnp.log2(f)
            hi = log_f.astype(BF16)
            rem = log_f - hi.astype(F32)
            mid = rem.astype(BF16)
            lo = (rem - mid.astype(F32)).astype(BF16)
            return 1.0 - f, jnp.concatenate([hi, mid, lo], axis=0)

        def hg_cum(parts):
            return jnp.dot(cum3, parts, preferred_element_type=F32)

        def hg_decays(c, k_in, g_cum, h=h_ref):
            rows = slice(c * HG_CHUNK, (c + 1) * HG_CHUNK)
            hq = h[rows, OFF_HQ:OFF_HQ + HG_WIDTH]
            g_last = g_cum[HG_CHUNK - 1:HG_CHUNK, :]
            q_dec = ((hq * _sigmoid(hq)) * jnp.exp2(g_cum)).astype(BF16)
            k_dec = (k_in * jnp.exp2(-g_cum)).astype(BF16)
            k_tail = (k_in * jnp.exp2(g_last - g_cum)).astype(BF16)
            return q_dec, k_dec, k_tail, jnp.exp2(g_last)

        def hg_scores(q_dec, k_dec, hd):
            sl = slice(hd * HG_HEAD_DIM, (hd + 1) * HG_HEAD_DIM)
            return lax.dot_general(q_dec[:, sl], k_dec[:, sl], NT_DIMS, preferred_element_type=F32)

        def hg_output(c, hd, a, q_dec, k_tail, decay):
            rows = slice(c * HG_CHUNK, (c + 1) * HG_CHUNK)
            sl = slice(hd * HG_HEAD_DIM, (hd + 1) * HG_HEAD_DIM)
            v = h_ref[rows, OFF_HI + hd * HG_HEAD_DIM:OFF_HI + (hd + 1) * HG_HEAD_DIM].astype(BF16)
            s_t = st_ref[hd]
            o = jnp.dot(jnp.where(tril, a, 0.0).astype(BF16), v, preferred_element_type=F32)
            o = o + lax.dot_general(q_dec[:, sl], s_t.astype(BF16), NT_DIMS,
                                    preferred_element_type=F32)
            inc_t = lax.dot_general(v, k_tail[:, sl], TN_DIMS, preferred_element_type=F32)
            st_ref[hd] = s_t * decay[:, sl] + inc_t
            o = o * lax.rsqrt(jnp.sum(o * o, axis=-1, keepdims=True) + HG_HEAD_DIM * RMS_EPS)
            o = o * nw[:, sl]
            hg = h_ref[rows, OFF_HG + hd * HG_HEAD_DIM:OFF_HG + (hd + 1) * HG_HEAD_DIM]
            mix[rows, sl] = (o * (hg * _sigmoid(hg))).astype(BF16)

        def sw_prepare(blk):
            r0 = blk * WINDOW
            rows = slice(r0, r0 + WINDOW)
            cur = slice(base + r0, base + r0 + WINDOW)
            cos = cos_ref[base + r0:base + r0 + WINDOW, :]
            sin = sin_ref[base + r0:base + r0 + WINDOW, :]
            k_rot = rope(h_ref[rows, OFF_AK:OFF_AK + SW_KV_WIDTH], cos, sin)
            for idx, kv in enumerate(head_variants(k_rot)):
                kwin_ref[idx, cur, :] = kv.astype(BF16)
            v_t = h_ref[rows, OFF_AV:OFF_AV + SW_KV_WIDTH].T
            for g in range(SW_KV_HEADS):
                vt_ref[g, 0:SW_HEAD_DIM, cur] = (
                    v_t[g * SW_HEAD_DIM:(g + 1) * SW_HEAD_DIM].astype(BF16))
            q_bf = []
            for tile in range(SW_WIDTH // LANES):
                q_rot = rope(h_ref[rows, OFF_AQ + tile * LANES:OFF_AQ + (tile + 1) * LANES], cos, sin)
                q_bf.append((q_rot * scale).astype(BF16))
            return [jnp.concatenate([q_bf[2 * g], q_bf[2 * g + 1]], axis=0)
                    for g in range(SW_KV_HEADS)]

        def key_window(blk, idx):
            r0 = base + blk * WINDOW
            if r0 == 0:
                return jnp.concatenate([kcarry_ref[idx], kwin_ref[idx, 0:WINDOW, :]], axis=0)
            return kwin_ref[idx, r0 - WINDOW:r0 + WINDOW, :]

        def value_window(blk, g):
            r0 = base + blk * WINDOW
            if r0 == 0:
                return jnp.concatenate([vcarry_ref[g], vt_ref[g, :, 0:WINDOW]], axis=1)
            return vt_ref[g, :, r0 - WINDOW:r0 + WINDOW]

        def sw_scores(blk, g, j, q_pair):
            return lax.dot_general(key_window(blk, 2 * g + j), q_pair, NT_DIMS,
                                   preferred_element_type=F32)

        def sw_values(blk, g, j, s_t, cap):
            sink_row = jnp.where(pair_lo, sinks_ref[SW_GROUP * g + j],
                                 sinks_ref[SW_GROUP * g + HEADS_PER_TILE + j]) * LOG2_E
            s_t = jnp.minimum(s_t, cap)
            m = jnp.maximum(jnp.max(s_t, axis=0, keepdims=True), sink_row)
            p_t = jnp.exp2(s_t - m).astype(BF16)
            acc = jnp.dot(value_window(blk, g), p_t, preferred_element_type=F32)
            denom = acc[SW_HEAD_DIM:SW_HEAD_DIM + 1, :] + jnp.exp2(sink_row - m)
            return acc[0:SW_HEAD_DIM, :] * (1.0 / denom)

        def sw_finish(blk, g, o_t):
            rows = slice(blk * WINDOW, (blk + 1) * WINDOW)
            for tl in range(2):
                tile = 2 * g + tl
                lanes = slice(tl * LANES, (tl + 1) * LANES)
                o_a = jnp.concatenate([o_t[0][:, lanes], o_t[1][:, lanes]], axis=0).T
                ag = h_ref[rows, OFF_AG + tile * LANES:OFF_AG + (tile + 1) * LANES]
                mix[rows, HG_WIDTH + tile * LANES:HG_WIDTH + (tile + 1) * LANES] = (
                    o_a * (ag * _sigmoid(ag))).astype(BF16)

        many = k > 0
        masks = [first_cap if (k == 0 and blk == 0) else band_cap for blk in range(N_BLOCKS)]
        chains = [(blk, g, j) for blk in range(N_BLOCKS) for g in range(SW_KV_HEADS)
                  for j in range(HEADS_PER_TILE)]
        q_pairs = {}
        scores = {}

        def sw_issue_scores(i):
            if i < len(chains) and i not in scores:
                blk, g, j = chains[i]
                if blk not in q_pairs:
                    q_pairs[blk] = sw_prepare(blk)
                scores[i] = sw_scores(blk, g, j, q_pairs[blk][g])

        head = head or {"pre": {}, "cum": {}, "dec": {}, "scores": {}}
        pre = dict(head["pre"])
        cum = dict(head["cum"])

        def hg_issue_cum(c):
            if c < N_CHUNKS and c not in cum:
                pre[c] = hg_pre(c)
                cum[c] = hg_cum(pre[c][1])

        fill(2 if many else 1)
        for c in range(min(2, N_CHUNKS)):
            hg_issue_cum(c)
            fill(1 if many else 0)
        for c in range(N_CHUNKS):
            if c in head["dec"]:
                q_dec, k_dec, k_tail, decay = head["dec"][c]
                hg_s = head["scores"][c]
            else:
                q_dec, k_dec, k_tail, decay = hg_decays(c, pre[c][0], cum[c])
                hg_s = [hg_scores(q_dec, k_dec, hd) for hd in range(HG_HEADS)]
            hg_issue_cum(c + 2)
            fill()
            for hd in range(HG_HEADS):
                hg_output(c, hd, hg_s[hd], q_dec, k_tail, decay)
            fill(1 if many else 0)

        for i in range(min(SW_SCORES_AHEAD, len(chains))):
            sw_issue_scores(i)
        fill()
        outs = {}
        has_next = k + 1 < TILES_PER_STEP
        next_head = {"pre": {}, "cum": {}, "dec": {}, "scores": {}} if has_next else None
        h_next = h_bufs[(k + 1) % 2]

        def head_start(i):
            if i != HEAD_AFTER_CHAIN:
                return
            assert projected.get(k + 1, 0) * PIECE >= OFF_HF + HG_WIDTH
            for c in range(HEAD_CUM_CHUNKS):
                next_head["pre"][c] = hg_pre(c, h_next)
                next_head["cum"][c] = hg_cum(next_head["pre"][c][1])
            for c in range(HEAD_DEC_CHUNKS):
                d = hg_decays(c, next_head["pre"][c][0], next_head["cum"][c], h_next)
                next_head["dec"][c] = d
                next_head["scores"][c] = [hg_scores(d[0], d[1], hd) for hd in range(HG_HEADS)]

        for i, (blk, g, j) in enumerate(chains):
            outs[(blk, g, j)] = sw_values(blk, g, j, scores.pop(i), masks[blk])
            sw_issue_scores(i + SW_SCORES_AHEAD)
            if j == HEADS_PER_TILE - 1:
                sw_finish(blk, g, [outs.pop((blk, g, jj)) for jj in range(HEADS_PER_TILE)])
            fill()
            if has_next:
                head_start(i)
        fill(N_OUT_PIECES + 1 + N_IN_PIECES)
        return next_head

    head = None
    for k in range(TILES_PER_STEP):
        head = process_tile(k, head)
    for piece in range(N_OUT_PIECES):
        out_projection_piece(TILES_PER_STEP - 1, piece)
    post_norm(TILES_PER_STEP - 1)

    last = slice((STEP_BLOCKS - 1) * WINDOW, STEP_BLOCKS * WINDOW)
    for idx in range(2 * SW_KV_HEADS):
        kcarry_ref[idx] = kwin_ref[idx, last, :]
    for g in range(SW_KV_HEADS):
        vcarry_ref[g, 0:SW_HEAD_DIM, :] = vt_ref[g, 0:SW_HEAD_DIM, last]


def _rope_tables(seq_len):
    pos = jnp.arange(seq_len, dtype=F32)
    inv_freq = ROPE_THETA ** (-jnp.arange(0, ROPE_DIM, 2, dtype=F32) / ROPE_DIM)
    ang = pos[:, None] * inv_freq[None, :]
    cos = jnp.cos(ang)
    sin = jnp.sin(ang)
    ones = jnp.ones((seq_len, SW_HEAD_DIM - ROPE_DIM), F32)
    cos_head = jnp.concatenate([cos, cos, ones], axis=1)
    sin_head = jnp.concatenate([-sin, sin, jnp.zeros_like(ones)], axis=1)
    return (jnp.tile(cos_head, (1, HEADS_PER_TILE)), jnp.tile(sin_head, (1, HEADS_PER_TILE)))


def kernel(x, w_in, lb_logits, hg_norm_w, sinks, w_out, ln_g, ln_b):
    batch, seq_len, d_model = x.shape
    assert d_model == D_MODEL and w_in.shape == (DEPTH, D_MODEL, IN_WIDTH)
    step_rows = TILES_PER_STEP * TIME_TILE
    assert seq_len % step_rows == 0
    cos_tab, sin_tab = _rope_tables(seq_len)
    n_steps = seq_len // step_rows
    last_step = batch * n_steps - 1

    def next_tile_index(b, u, *_):
        nxt = jnp.minimum(b * n_steps + u + 1, last_step)
        return (nxt // n_steps, TILES_PER_STEP * (nxt % n_steps), 0)

    const = lambda b, u, *_: (0, 0)
    grid_spec = pltpu.PrefetchScalarGridSpec(
        num_scalar_prefetch=1,
        grid=(batch, n_steps),
        in_specs=[
            pl.BlockSpec((1, step_rows, D_MODEL), lambda b, u, *_: (b, u, 0)),
            pl.BlockSpec((1, TIME_TILE, D_MODEL), next_tile_index),
            pl.BlockSpec(memory_space=pl.ANY),
            pl.BlockSpec(memory_space=pl.ANY),
            pl.BlockSpec((DEPTH + 1, HG_WIDTH), const),
            pl.BlockSpec((1, HG_WIDTH), const),
            pl.BlockSpec((1, D_MODEL), const),
            pl.BlockSpec((1, D_MODEL), const),
            pl.BlockSpec((step_rows, LANES), lambda b, u, *_: (u, 0)),
            pl.BlockSpec((step_rows, LANES), lambda b, u, *_: (u, 0)),
        ],
        out_specs=pl.BlockSpec((1, step_rows, D_MODEL), lambda b, u, *_: (b, u, 0)),
        scratch_shapes=[
            pltpu.VMEM((TIME_TILE, IN_WIDTH), F32),
            pltpu.VMEM((TIME_TILE, IN_WIDTH), F32),
            pltpu.VMEM((2, TIME_TILE, D_MODEL), BF16),
            pltpu.VMEM((TILES_PER_STEP, TIME_TILE, D_MODEL), BF16),
            pltpu.VMEM((TIME_TILE, D_MODEL), F32),
            pltpu.VMEM((HG_HEADS, HG_HEAD_DIM, HG_HEAD_DIM), F32),
            pltpu.VMEM((2 * SW_KV_HEADS, STEP_BLOCKS * WINDOW, LANES), BF16),
            pltpu.VMEM((SW_KV_HEADS, VT_ROWS, STEP_BLOCKS * WINDOW), BF16),
            pltpu.VMEM((2 * SW_KV_HEADS, WINDOW, LANES), BF16),
            pltpu.VMEM((SW_KV_HEADS, VT_ROWS, WINDOW), BF16),
            pltpu.VMEM((D_MODEL, IN_WIDTH), BF16),
            pltpu.VMEM((D_MODEL, D_MODEL), BF16),
            pltpu.VMEM((WEIGHT_SLOTS, WEIGHT_CHUNK_ROWS, IN_WIDTH), F32),
            pltpu.VMEM((WEIGHT_SLOTS, WEIGHT_CHUNK_ROWS, D_MODEL), F32),
            pltpu.SemaphoreType.DMA((WEIGHT_SLOTS,)),
            pltpu.SemaphoreType.DMA((WEIGHT_SLOTS,)),
        ],
    )
    return pl.pallas_call(
        _layer_kernel,
        grid_spec=grid_spec,
        out_shape=jax.ShapeDtypeStruct(x.shape, x.dtype),
        compiler_params=pltpu.CompilerParams(
            dimension_semantics=("arbitrary", "arbitrary"),
            vmem_limit_bytes=VMEM_LIMIT_BYTES),
        name="hybrid_hgrn2_swa_layer",
    )(sinks[0], x, x, w_in, w_out, lb_logits, hg_norm_w, ln_g, ln_b, cos_tab, sin_tab)
```

```python
import jax
import jax.numpy as jnp
from jax import lax
from jax.experimental import pallas as pl
from jax.experimental.pallas import tpu as pltpu

D_MODEL = 1024
DEPTH = 1
HG_WIDTH = 512
HG_HEAD_DIM = 128
HG_HEADS = HG_WIDTH // HG_HEAD_DIM
HG_CHUNK = 64
SW_WIDTH = 512
SW_HEAD_DIM = 64
SW_Q_HEADS = SW_WIDTH // SW_HEAD_DIM
SW_KV_HEADS = SW_Q_HEADS // 4
SW_KV_WIDTH = SW_KV_HEADS * SW_HEAD_DIM
SW_GROUP = SW_Q_HEADS // SW_KV_HEADS
WINDOW = 128
ROPE_THETA = 500000.0
ROPE_DIM = SW_HEAD_DIM // 4
DN_ALPHA = (2.0 * DEPTH) ** 0.25
LN_EPS = 1e-5
RMS_EPS = 1e-6
IN_WIDTH = 4 * HG_WIDTH + SW_WIDTH + 2 * SW_KV_WIDTH + SW_WIDTH

OFF_HQ = 0
OFF_HF = OFF_HQ + HG_WIDTH
OFF_HI = OFF_HF + HG_WIDTH
OFF_HG = OFF_HI + HG_WIDTH
OFF_AQ = OFF_HG + HG_WIDTH
OFF_AK = OFF_AQ + SW_WIDTH
OFF_AV = OFF_AK + SW_KV_WIDTH
OFF_AG = OFF_AV + SW_KV_WIDTH

LANES = 128
BF16_SUBLANES = 16
MXU_WIDTH = 256
HEADS_PER_TILE = LANES // SW_HEAD_DIM
assert HEADS_PER_TILE == 2 and SW_GROUP == 2 * HEADS_PER_TILE and SW_KV_WIDTH == LANES
VT_ROWS = SW_HEAD_DIM + BF16_SUBLANES
TIME_TILE = 256
TILES_PER_STEP = 4
PIECE = MXU_WIDTH
N_IN_PIECES = IN_WIDTH // PIECE
N_OUT_PIECES = D_MODEL // PIECE
assert IN_WIDTH % PIECE == 0 and D_MODEL % PIECE == 0
N_CHUNKS = TIME_TILE // HG_CHUNK
N_BLOCKS = TIME_TILE // WINDOW
STEP_BLOCKS = TILES_PER_STEP * N_BLOCKS
HEAD_CUM_CHUNKS = 3
HEAD_DEC_CHUNKS = 1
HEAD_AFTER_CHAIN = 2
SW_SCORES_AHEAD = 8
WEIGHT_CHUNK_ROWS = 64
WEIGHT_SLOTS = 4
VMEM_LIMIT_BYTES = 58 * 1024 * 1024

F32 = jnp.float32
BF16 = jnp.bfloat16
NT_DIMS = (((1,), (1,)), ((), ()))
TN_DIMS = (((0,), (0,)), ((), ()))


LOG2_E = 1.4426950408889634


def _sigmoid(v):
    return 1.0 / (1.0 + jnp.exp2(v * (-LOG2_E)))


def _load_weight_as_bf16(w_hbm, stage_ref, sem_ref, dst_ref, scale=None):
    n_rows = dst_ref.shape[0]
    n_slots, chunk = stage_ref.shape[0], stage_ref.shape[1]
    n_chunks = n_rows // chunk
    assert n_chunks * chunk == n_rows and n_slots >= 2

    def chunk_copy(i):
        return pltpu.make_async_copy(w_hbm.at[0, pl.ds(i * chunk, chunk), :],
                                     stage_ref.at[i % n_slots], sem_ref.at[i % n_slots])

    for i in range(min(n_slots - 1, n_chunks)):
        chunk_copy(i).start()
    for i in range(n_chunks):
        ahead = i + n_slots - 1
        if ahead < n_chunks:
            chunk_copy(ahead).start()
        chunk_copy(i).wait()
        rows = stage_ref[i % n_slots]
        if scale is not None:
            rows = rows * scale
        dst_ref[i * chunk:(i + 1) * chunk, :] = rows.astype(BF16)


def _layer_kernel(sinks_ref, xc_ref, xn_ref, win_hbm, wout_hbm, lbl_ref, nw_ref, lng_ref, lnb_ref,
                  cos_ref, sin_ref, o_ref,
                  ha_ref, hb_ref, xb_ref, mix_ref, out_ref, st_ref, kwin_ref, vt_ref,
                  kcarry_ref, vcarry_ref, win_ref, wout_ref, win_stage, wout_stage,
                  win_sems, wout_sems):
    b = pl.program_id(0)
    u = pl.program_id(1)
    tt = TIME_TILE
    h_bufs = (ha_ref, hb_ref)

    @pl.when(jnp.logical_and(b == 0, u == 0))
    def _first_step():
        _load_weight_as_bf16(win_hbm, win_stage, win_sems, win_ref)
        _load_weight_as_bf16(wout_hbm, wout_stage, wout_sems, wout_ref, scale=1.0 / DN_ALPHA)
        ha_ref[...] = jnp.dot(xc_ref[0, 0:tt, :].astype(BF16), win_ref[...],
                              preferred_element_type=F32)

    @pl.when(u == 0)
    def _reset_carries():
        st_ref[...] = jnp.zeros_like(st_ref)
        kcarry_ref[...] = jnp.zeros_like(kcarry_ref)
        vcarry_ref[:, 0:SW_HEAD_DIM, :] = jnp.zeros((SW_KV_HEADS, SW_HEAD_DIM, WINDOW), BF16)
        vcarry_ref[:, SW_HEAD_DIM:VT_ROWS, :] = jnp.ones((SW_KV_HEADS, BF16_SUBLANES, WINDOW), BF16)
        vt_ref[:, SW_HEAD_DIM:VT_ROWS, :] = jnp.ones(
            (SW_KV_HEADS, BF16_SUBLANES, STEP_BLOCKS * WINDOW), BF16)

    lbl = lbl_ref[...]
    lbl_e = jnp.exp(lbl - jnp.max(lbl, axis=0, keepdims=True))
    lb = lbl_e[0:1] / jnp.sum(lbl_e, axis=0, keepdims=True)
    nw = nw_ref[...] * (HG_HEAD_DIM ** 0.5)
    row = lax.broadcasted_iota(jnp.int32, (HG_CHUNK, HG_CHUNK), 0)
    col = lax.broadcasted_iota(jnp.int32, (HG_CHUNK, HG_CHUNK), 1)
    tril = col <= row
    cum_mat = tril.astype(BF16)
    cum3 = jnp.concatenate([cum_mat, cum_mat, cum_mat], axis=1)

    lane = lax.broadcasted_iota(jnp.int32, (1, LANES), 1)
    rope_first = (lane % SW_HEAD_DIM) < (ROPE_DIM // 2)
    lane_lo = lane < SW_HEAD_DIM
    key_idx = lax.broadcasted_iota(jnp.int32, (2 * WINDOW, 2 * WINDOW), 0)
    qry_idx = lax.broadcasted_iota(jnp.int32, (2 * WINDOW, 2 * WINDOW), 1) % WINDOW
    band = (key_idx > qry_idx) & (key_idx <= qry_idx + WINDOW)
    pair_lo = lax.broadcasted_iota(jnp.int32, (1, 2 * WINDOW), 1) < WINDOW
    scale = SW_HEAD_DIM ** -0.5 * LOG2_E
    first_valid = jnp.where(u > 0, 0, WINDOW)
    band_cap = jnp.where(band, jnp.inf, -jnp.inf)
    first_cap = jnp.where(band & (key_idx >= first_valid), jnp.inf, -jnp.inf)

    def rope(v, cos, sin):
        partner = jnp.where(rope_first,
                            pltpu.roll(v, LANES - ROPE_DIM // 2, 1),
                            pltpu.roll(v, ROPE_DIM // 2, 1))
        return v * cos + partner * sin

    def head_variants(v):
        sw = pltpu.roll(v, SW_HEAD_DIM, 1)
        zero = jnp.zeros_like(v)
        return (jnp.where(lane_lo, v, zero), jnp.where(lane_lo, zero, sw),
                jnp.where(lane_lo, sw, zero), jnp.where(lane_lo, zero, v))

    def in_projection_piece(k, piece):
        xb = xb_ref.at[k % 2]
        cols = slice(piece * PIECE, (piece + 1) * PIECE)
        h_bufs[k % 2][:, cols] = jnp.dot(xb[...], win_ref[:, cols], preferred_element_type=F32)

    def out_projection_piece(k, piece):
        cols = slice(piece * PIECE, (piece + 1) * PIECE)
        out_ref[:, cols] = jnp.dot(mix_ref[k], wout_ref[:, cols], preferred_element_type=F32)

    def post_norm(k):
        rows = slice(k * tt, (k + 1) * tt)
        z = xc_ref[0, rows, :] + out_ref[...]
        mu = jnp.mean(z, axis=-1, keepdims=True)
        zc = z - mu
        var = jnp.mean(zc * zc, axis=-1, keepdims=True)
        o_ref[0, rows, :] = ((zc * lax.rsqrt(var + LN_EPS / DN_ALPHA ** 2)) * lng_ref[...]
                             + lnb_ref[...])

    def make_fillers(k):
        stages = []
        if k > 0:
            for piece in range(N_OUT_PIECES):
                stages.append(lambda piece=piece: out_projection_piece(k - 1, piece))
            stages.append(lambda: post_norm(k - 1))
        for piece in range(N_IN_PIECES):
            def in_stage(piece=piece):
                in_projection_piece(k + 1, piece)
                projected[k + 1] = piece + 1
            stages.append(in_stage)
        return iter(stages)

    projected = {}

    def process_tile(k, head):
        h_ref = h_bufs[k % 2]
        base = k * tt
        mix = mix_ref.at[k]
        if k + 1 < TILES_PER_STEP:
            x_next_rows = xc_ref[0, (k + 1) * tt:(k + 2) * tt, :]
        else:
            x_next_rows = xn_ref[0]
        xb_ref[(k + 1) % 2] = x_next_rows.astype(BF16)
        fillers = make_fillers(k)

        def fill(n=1):
            for _ in range(n):
                stage = next(fillers, None)
                if stage is not None:
                    stage()

        def hg_pre(c, h=h_ref):
            rows = slice(c * HG_CHUNK, (c + 1) * HG_CHUNK)
            f = lb + (1.0 - lb) * _sigmoid(h[rows, OFF_HF:OFF_HF + HG_WIDTH])
            log_f = jnp.log2(f)
            hi = log_f.astype(BF16)
            rem = log_f - hi.astype(F32)
            mid = rem.astype(BF16)
            lo = (rem - mid.astype(F32)).astype(BF16)
            return 1.0 - f, jnp.concatenate([hi, mid, lo], axis=0)

        def hg_cum(parts):
            return jnp.dot(cum3, parts, preferred_element_type=F32)

        def hg_decays(c, k_in, g_cum, h=h_ref):
            rows = slice(c * HG_CHUNK, (c + 1) * HG_CHUNK)
            hq = h[rows, OFF_HQ:OFF_HQ + HG_WIDTH]
            g_last = g_cum[HG_CHUNK - 1:HG_CHUNK, :]
            q_dec = ((hq * _sigmoid(hq)) * jnp.exp2(g_cum)).astype(BF16)
            k_dec = (k_in * jnp.exp2(-g_cum)).astype(BF16)
            k_tail = (k_in * jnp.exp2(g_last - g_cum)).astype(BF16)
            return q_dec, k_dec, k_tail, jnp.exp2(g_last)

        def hg_scores(q_dec, k_dec, hd):
            sl = slice(hd * HG_HEAD_DIM, (hd + 1) * HG_HEAD_DIM)
            return lax.dot_general(q_dec[:, sl], k_dec[:, sl], NT_DIMS, preferred_element_type=F32)

        def hg_output(c, hd, a, q_dec, k_tail, decay):
            rows = slice(c * HG_CHUNK, (c + 1) * HG_CHUNK)
            sl = slice(hd * HG_HEAD_DIM, (hd + 1) * HG_HEAD_DIM)
            v = h_ref[rows, OFF_HI + hd * HG_HEAD_DIM:OFF_HI + (hd + 1) * HG_HEAD_DIM].astype(BF16)
            s_t = st_ref[hd]
            o = jnp.dot(jnp.where(tril, a, 0.0).astype(BF16), v, preferred_element_type=F32)
            o = o + lax.dot_general(q_dec[:, sl], s_t.astype(BF16), NT_DIMS,
                                    preferred_element_type=F32)
            inc_t = lax.dot_general(v, k_tail[:, sl], TN_DIMS, preferred_element_type=F32)
            st_ref[hd] = s_t * decay[:, sl] + inc_t
            o = o * lax.rsqrt(jnp.sum(o * o, axis=-1, keepdims=True) + HG_HEAD_DIM * RMS_EPS)
            o = o * nw[:, sl]
            hg = h_ref[rows, OFF_HG + hd * HG_HEAD_DIM:OFF_HG + (hd + 1) * HG_HEAD_DIM]
            mix[rows, sl] = (o * (hg * _sigmoid(hg))).astype(BF16)

        def sw_prepare(blk):
            r0 = blk * WINDOW
            rows = slice(r0, r0 + WINDOW)
            cur = slice(base + r0, base + r0 + WINDOW)
            cos = cos_ref[base + r0:base + r0 + WINDOW, :]
            sin = sin_ref[base + r0:base + r0 + WINDOW, :]
            k_rot = rope(h_ref[rows, OFF_AK:OFF_AK + SW_KV_WIDTH], cos, sin)
            for idx, kv in enumerate(head_variants(k_rot)):
                kwin_ref[idx, cur, :] = kv.astype(BF16)
            v_t = h_ref[rows, OFF_AV:OFF_AV + SW_KV_WIDTH].T
            for g in range(SW_KV_HEADS):
                vt_ref[g, 0:SW_HEAD_DIM, cur] = (
                    v_t[g * SW_HEAD_DIM:(g + 1) * SW_HEAD_DIM].astype(BF16))
            q_bf = []
            for tile in range(SW_WIDTH // LANES):
                q_rot = rope(h_ref[rows, OFF_AQ + tile * LANES:OFF_AQ + (tile + 1) * LANES], cos, sin)
                q_bf.append((q_rot * scale).astype(BF16))
            return [jnp.concatenate([q_bf[2 * g], q_bf[2 * g + 1]], axis=0)
                    for g in range(SW_KV_HEADS)]

        def key_window(blk, idx):
            r0 = base + blk * WINDOW
            if r0 == 0:
                return jnp.concatenate([kcarry_ref[idx], kwin_ref[idx, 0:WINDOW, :]], axis=0)
            return kwin_ref[idx, r0 - WINDOW:r0 + WINDOW, :]

        def value_window(blk, g):
            r0 = base + blk * WINDOW
            if r0 == 0:
                return jnp.concatenate([vcarry_ref[g], vt_ref[g, :, 0:WINDOW]], axis=1)
            return vt_ref[g, :, r0 - WINDOW:r0 + WINDOW]

        def sw_scores(blk, g, j, q_pair):
            return lax.dot_general(key_window(blk, 2 * g + j), q_pair, NT_DIMS,
                                   preferred_element_type=F32)

        def sw_values(blk, g, j, s_t, cap):
            sink_row = jnp.where(pair_lo, sinks_ref[SW_GROUP * g + j],
                                 sinks_ref[SW_GROUP * g + HEADS_PER_TILE + j]) * LOG2_E
            s_t = jnp.minimum(s_t, cap)
            m = jnp.maximum(jnp.max(s_t, axis=0, keepdims=True), sink_row)
            p_t = jnp.exp2(s_t - m).astype(BF16)
            acc = jnp.dot(value_window(blk, g), p_t, preferred_element_type=F32)
            denom = acc[SW_HEAD_DIM:SW_HEAD_DIM + 1, :] + jnp.exp2(sink_row - m)
            return acc[0:SW_HEAD_DIM, :] * (1.0 / denom)

        def sw_finish(blk, g, o_t):
            rows = slice(blk * WINDOW, (blk + 1) * WINDOW)
            for tl in range(2):
                tile = 2 * g + tl
                lanes = slice(tl * LANES, (tl + 1) * LANES)
                o_a = jnp.concatenate([o_t[0][:, lanes], o_t[1][:, lanes]], axis=0).T
                ag = h_ref[rows, OFF_AG + tile * LANES:OFF_AG + (tile + 1) * LANES]
                mix[rows, HG_WIDTH + tile * LANES:HG_WIDTH + (tile + 1) * LANES] = (
                    o_a * (ag * _sigmoid(ag))).astype(BF16)

        many = k > 0
        masks = [first_cap if (k == 0 and blk == 0) else band_cap for blk in range(N_BLOCKS)]
        chains = [(blk, g, j) for blk in range(N_BLOCKS) for g in range(SW_KV_HEADS)
                  for j in range(HEADS_PER_TILE)]
        q_pairs = {}
        scores = {}

        def sw_issue_scores(i):
            if i < len(chains) and i not in scores:
                blk, g, j = chains[i]
                if blk not in q_pairs:
                    q_pairs[blk] = sw_prepare(blk)
                scores[i] = sw_scores(blk, g, j, q_pairs[blk][g])

        head = head or {"pre": {}, "cum": {}, "dec": {}, "scores": {}}
        pre = dict(head["pre"])
        cum = dict(head["cum"])

        def hg_issue_cum(c):
            if c < N_CHUNKS and c not in cum:
                pre[c] = hg_pre(c)
                cum[c] = hg_cum(pre[c][1])

        fill(2 if many else 1)
        for c in range(min(2, N_CHUNKS)):
            hg_issue_cum(c)
            fill(1 if many else 0)
        for c in range(N_CHUNKS):
            if c in head["dec"]:
                q_dec, k_dec, k_tail, decay = head["dec"][c]
                hg_s = head["scores"][c]
            else:
                q_dec, k_dec, k_tail, decay = hg_decays(c, pre[c][0], cum[c])
                hg_s = [hg_scores(q_dec, k_dec, hd) for hd in range(HG_HEADS)]
            hg_issue_cum(c + 2)
            fill()
            for hd in range(HG_HEADS):
                hg_output(c, hd, hg_s[hd], q_dec, k_tail, decay)
            fill(1 if many else 0)

        for i in range(min(SW_SCORES_AHEAD, len(chains))):
            sw_issue_scores(i)
        fill()
        outs = {}
        has_next = k + 1 < TILES_PER_STEP
        next_head = {"pre": {}, "cum": {}, "dec": {}, "scores": {}} if has_next else None
        h_next = h_bufs[(k + 1) % 2]

        def head_start(i):
            if i != HEAD_AFTER_CHAIN:
                return
            assert projected.get(k + 1, 0) * PIECE >= OFF_HF + HG_WIDTH
            for c in range(HEAD_CUM_CHUNKS):
                next_head["pre"][c] = hg_pre(c, h_next)
                next_head["cum"][c] = hg_cum(next_head["pre"][c][1])
            for c in range(HEAD_DEC_CHUNKS):
                d = hg_decays(c, next_head["pre"][c][0], next_head["cum"][c], h_next)
                next_head["dec"][c] = d
                next_head["scores"][c] = [hg_scores(d[0], d[1], hd) for hd in range(HG_HEADS)]

        for i, (blk, g, j) in enumerate(chains):
            outs[(blk, g, j)] = sw_values(blk, g, j, scores.pop(i), masks[blk])
            sw_issue_scores(i + SW_SCORES_AHEAD)
            if j == HEADS_PER_TILE - 1:
                sw_finish(blk, g, [outs.pop((blk, g, jj)) for jj in range(HEADS_PER_TILE)])
            fill()
            if has_next:
                head_start(i)
        fill(N_OUT_PIECES + 1 + N_IN_PIECES)
        return next_head

    head = None
    for k in range(TILES_PER_STEP):
        head = process_tile(k, head)
    for piece in range(N_OUT_PIECES):
        out_projection_piece(TILES_PER_STEP - 1, piece)
    post_norm(TILES_PER_STEP - 1)

    last = slice((STEP_BLOCKS - 1) * WINDOW, STEP_BLOCKS * WINDOW)
    for idx in range(2 * SW_KV_HEADS):
        kcarry_ref[idx] = kwin_ref[idx, last, :]
    for g in range(SW_KV_HEADS):
        vcarry_ref[g, 0:SW_HEAD_DIM, :] = vt_ref[g, 0:SW_HEAD_DIM, last]


def _rope_tables(seq_len):
    pos = jnp.arange(seq_len, dtype=F32)
    inv_freq = ROPE_THETA ** (-jnp.arange(0, ROPE_DIM, 2, dtype=F32) / ROPE_DIM)
    ang = pos[:, None] * inv_freq[None, :]
    cos = jnp.cos(ang)
    sin = jnp.sin(ang)
    ones = jnp.ones((seq_len, SW_HEAD_DIM - ROPE_DIM), F32)
    cos_head = jnp.concatenate([cos, cos, ones], axis=1)
    sin_head = jnp.concatenate([-sin, sin, jnp.zeros_like(ones)], axis=1)
    return (jnp.tile(cos_head, (1, HEADS_PER_TILE)), jnp.tile(sin_head, (1, HEADS_PER_TILE)))


def kernel(x, w_in, lb_logits, hg_norm_w, sinks, w_out, ln_g, ln_b):
    batch, seq_len, d_model = x.shape
    assert d_model == D_MODEL and w_in.shape == (DEPTH, D_MODEL, IN_WIDTH)
    step_rows = TILES_PER_STEP * TIME_TILE
    assert seq_len % step_rows == 0
    cos_tab, sin_tab = _rope_tables(seq_len)
    n_steps = seq_len // step_rows
    last_step = batch * n_steps - 1

    def next_tile_index(b, u, *_):
        nxt = jnp.minimum(b * n_steps + u + 1, last_step)
        return (nxt // n_steps, TILES_PER_STEP * (nxt % n_steps), 0)

    const = lambda b, u, *_: (0, 0)
    grid_spec = pltpu.PrefetchScalarGridSpec(
        num_scalar_prefetch=1,
        grid=(batch, n_steps),
        in_specs=[
            pl.BlockSpec((1, step_rows, D_MODEL), lambda b, u, *_: (b, u, 0)),
            pl.BlockSpec((1, TIME_TILE, D_MODEL), next_tile_index),
            pl.BlockSpec(memory_space=pl.ANY),
            pl.BlockSpec(memory_space=pl.ANY),
            pl.BlockSpec((DEPTH + 1, HG_WIDTH), const),
            pl.BlockSpec((1, HG_WIDTH), const),
            pl.BlockSpec((1, D_MODEL), const),
            pl.BlockSpec((1, D_MODEL), const),
            pl.BlockSpec((step_rows, LANES), lambda b, u, *_: (u, 0)),
            pl.BlockSpec((step_rows, LANES), lambda b, u, *_: (u, 0)),
        ],
        out_specs=pl.BlockSpec((1, step_rows, D_MODEL), lambda b, u, *_: (b, u, 0)),
        scratch_shapes=[
            pltpu.VMEM((TIME_TILE, IN_WIDTH), F32),
            pltpu.VMEM((TIME_TILE, IN_WIDTH), F32),
            pltpu.VMEM((2, TIME_TILE, D_MODEL), BF16),
            pltpu.VMEM((TILES_PER_STEP, TIME_TILE, D_MODEL), BF16),
            pltpu.VMEM((TIME_TILE, D_MODEL), F32),
            pltpu.VMEM((HG_HEADS, HG_HEAD_DIM, HG_HEAD_DIM), F32),
            pltpu.VMEM((2 * SW_KV_HEADS, STEP_BLOCKS * WINDOW, LANES), BF16),
            pltpu.VMEM((SW_KV_HEADS, VT_ROWS, STEP_BLOCKS * WINDOW), BF16),
            pltpu.VMEM((2 * SW_KV_HEADS, WINDOW, LANES), BF16),
            pltpu.VMEM((SW_KV_HEADS, VT_ROWS, WINDOW), BF16),
            pltpu.VMEM((D_MODEL, IN_WIDTH), BF16),
            pltpu.VMEM((D_MODEL, D_MODEL), BF16),
            pltpu.VMEM((WEIGHT_SLOTS, WEIGHT_CHUNK_ROWS, IN_WIDTH), F32),
            pltpu.VMEM((WEIGHT_SLOTS, WEIGHT_CHUNK_ROWS, D_MODEL), F32),
            pltpu.SemaphoreType.DMA((WEIGHT_SLOTS,)),
            pltpu.SemaphoreType.DMA((WEIGHT_SLOTS,)),
        ],
    )
    return pl.pallas_call(
        _layer_kernel,
        grid_spec=grid_spec,
        out_shape=jax.ShapeDtypeStruct(x.shape, x.dtype),
        compiler_params=pltpu.CompilerParams(
            dimension_semantics=("arbitrary", "arbitrary"),
            vmem_limit_bytes=VMEM_LIMIT_BYTES),
        name="hybrid_hgrn2_swa_layer",
    )(sinks[0], x, x, w_in, w_out, lb_logits, hg_norm_w, ln_g, ln_b, cos_tab, sin_tab)
```

```python
import jax
import jax.numpy as jnp
from jax import lax
from jax.experimental import pallas as pl
from jax.experimental.pallas import tpu as pltpu

D_MODEL = 1024
DEPTH = 1
HG_WIDTH = 512
HG_HEAD_DIM = 128
HG_HEADS = HG_WIDTH // HG_HEAD_DIM
HG_CHUNK = 64
SW_WIDTH = 512
SW_HEAD_DIM = 64
SW_Q_HEADS = SW_WIDTH // SW_HEAD_DIM
SW_KV_HEADS = SW_Q_HEADS // 4
SW_KV_WIDTH = SW_KV_HEADS * SW_HEAD_DIM
SW_GROUP = SW_Q_HEADS // SW_KV_HEADS
WINDOW = 128
ROPE_THETA = 500000.0
ROPE_DIM = SW_HEAD_DIM // 4
DN_ALPHA = (2.0 * DEPTH) ** 0.25
LN_EPS = 1e-5
RMS_EPS = 1e-6
IN_WIDTH = 4 * HG_WIDTH + SW_WIDTH + 2 * SW_KV_WIDTH + SW_WIDTH

OFF_HQ = 0
OFF_HF = OFF_HQ + HG_WIDTH
OFF_HI = OFF_HF + HG_WIDTH
OFF_HG = OFF_HI + HG_WIDTH
OFF_AQ = OFF_HG + HG_WIDTH
OFF_AK = OFF_AQ + SW_WIDTH
OFF_AV = OFF_AK + SW_KV_WIDTH
OFF_AG = OFF_AV + SW_KV_WIDTH

LANES = 128
BF16_SUBLANES = 16
MXU_WIDTH = 256
HEADS_PER_TILE = LANES // SW_HEAD_DIM
assert HEADS_PER_TILE == 2 and SW_GROUP == 2 * HEADS_PER_TILE and SW_KV_WIDTH == LANES
VT_ROWS = SW_HEAD_DIM + BF16_SUBLANES
TIME_TILE = 256
TILES_PER_STEP = 4
PIECE = MXU_WIDTH
N_IN_PIECES = IN_WIDTH // PIECE
N_OUT_PIECES = D_MODEL // PIECE
assert IN_WIDTH % PIECE == 0 and D_MODEL % PIECE == 0
N_CHUNKS = TIME_TILE // HG_CHUNK
N_BLOCKS = TIME_TILE // WINDOW
STEP_BLOCKS = TILES_PER_STEP * N_BLOCKS
HEAD_CUM_CHUNKS = 3
HEAD_DEC_CHUNKS = 2
HEAD_AFTER_CHAIN = 3
SW_SCORES_AHEAD = 8
WEIGHT_CHUNK_ROWS = 64
WEIGHT_SLOTS = 4
VMEM_LIMIT_BYTES = 58 * 1024 * 1024

F32 = jnp.float32
BF16 = jnp.bfloat16
NT_DIMS = (((1,), (1,)), ((), ()))
TN_DIMS = (((0,), (0,)), ((), ()))


LOG2_E = 1.4426950408889634


def _sigmoid(v):
    return 1.0 / (1.0 + jnp.exp2(v * (-LOG2_E)))


def _load_weight_as_bf16(w_hbm, stage_ref, sem_ref, dst_ref, scale=None):
    n_rows = dst_ref.shape[0]
    n_slots, chunk = stage_ref.shape[0], stage_ref.shape[1]
    n_chunks = n_rows // chunk
    assert n_chunks * chunk == n_rows and n_slots >= 2

    def chunk_copy(i):
        return pltpu.make_async_copy(w_hbm.at[0, pl.ds(i * chunk, chunk), :],
                                     stage_ref.at[i % n_slots], sem_ref.at[i % n_slots])

    for i in range(min(n_slots - 1, n_chunks)):
        chunk_copy(i).start()
    for i in range(n_chunks):
        ahead = i + n_slots - 1
        if ahead < n_chunks:
            chunk_copy(ahead).start()
        chunk_copy(i).wait()
        rows = stage_ref[i % n_slots]
        if scale is not None:
            rows = rows * scale
        dst_ref[i * chunk:(i + 1) * chunk, :] = rows.astype(BF16)


def _layer_kernel(sinks_ref, xc_ref, xn_ref, win_hbm, wout_hbm, lbl_ref, nw_ref, lng_ref, lnb_ref,
                  cos_ref, sin_ref, o_ref,
                  ha_ref, hb_ref, xb_ref, mix_ref, out_ref, st_ref, kwin_ref, vt_ref,
                  kcarry_ref, vcarry_ref, win_ref, wout_ref, win_stage, wout_stage,
                  win_sems, wout_sems):
    b = pl.program_id(0)
    u = pl.program_id(1)
    tt = TIME_TILE
    h_bufs = (ha_ref, hb_ref)

    @pl.when(jnp.logical_and(b == 0, u == 0))
    def _first_step():
        _load_weight_as_bf16(win_hbm, win_stage, win_sems, win_ref)
        _load_weight_as_bf16(wout_hbm, wout_stage, wout_sems, wout_ref, scale=1.0 / DN_ALPHA)
        ha_ref[...] = jnp.dot(xc_ref[0, 0:tt, :].astype(BF16), win_ref[...],
                              preferred_element_type=F32)

    @pl.when(u == 0)
    def _reset_carries():
        st_ref[...] = jnp.zeros_like(st_ref)
        kcarry_ref[...] = jnp.zeros_like(kcarry_ref)
        vcarry_ref[:, 0:SW_HEAD_DIM, :] = jnp.zeros((SW_KV_HEADS, SW_HEAD_DIM, WINDOW), BF16)
        vcarry_ref[:, SW_HEAD_DIM:VT_ROWS, :] = jnp.ones((SW_KV_HEADS, BF16_SUBLANES, WINDOW), BF16)
        vt_ref[:, SW_HEAD_DIM:VT_ROWS, :] = jnp.ones(
            (SW_KV_HEADS, BF16_SUBLANES, STEP_BLOCKS * WINDOW), BF16)

    lbl = lbl_ref[...]
    lbl_e = jnp.exp(lbl - jnp.max(lbl, axis=0, keepdims=True))
    lb = lbl_e[0:1] / jnp.sum(lbl_e, axis=0, keepdims=True)
    nw = nw_ref[...] * (HG_HEAD_DIM ** 0.5)
    row = lax.broadcasted_iota(jnp.int32, (HG_CHUNK, HG_CHUNK), 0)
    col = lax.broadcasted_iota(jnp.int32, (HG_CHUNK, HG_CHUNK), 1)
    tril = col <= row
    cum_mat = tril.astype(BF16)
    cum3 = jnp.concatenate([cum_mat, cum_mat, cum_mat], axis=1)

    lane = lax.broadcasted_iota(jnp.int32, (1, LANES), 1)
    rope_first = (lane % SW_HEAD_DIM) < (ROPE_DIM // 2)
    lane_lo = lane < SW_HEAD_DIM
    key_idx = lax.broadcasted_iota(jnp.int32, (2 * WINDOW, 2 * WINDOW), 0)
    qry_idx = lax.broadcasted_iota(jnp.int32, (2 * WINDOW, 2 * WINDOW), 1) % WINDOW
    band = (key_idx > qry_idx) & (key_idx <= qry_idx + WINDOW)
    pair_lo = lax.broadcasted_iota(jnp.int32, (1, 2 * WINDOW), 1) < WINDOW
    scale = SW_HEAD_DIM ** -0.5 * LOG2_E
    first_valid = jnp.where(u > 0, 0, WINDOW)
    band_cap = jnp.where(band, jnp.inf, -jnp.inf)
    first_cap = jnp.where(band & (key_idx >= first_valid), jnp.inf, -jnp.inf)

    def rope(v, cos, sin):
        partner = jnp.where(rope_first,
                            pltpu.roll(v, LANES - ROPE_DIM // 2, 1),
                            pltpu.roll(v, ROPE_DIM // 2, 1))
        return v * cos + partner * sin

    def head_variants(v):
        sw = pltpu.roll(v, SW_HEAD_DIM, 1)
        zero = jnp.zeros_like(v)
        return (jnp.where(lane_lo, v, zero), jnp.where(lane_lo, zero, sw),
                jnp.where(lane_lo, sw, zero), jnp.where(lane_lo, zero, v))

    def in_projection_piece(k, piece):
        xb = xb_ref.at[k % 2]
        cols = slice(piece * PIECE, (piece + 1) * PIECE)
        h_bufs[k % 2][:, cols] = jnp.dot(xb[...], win_ref[:, cols], preferred_element_type=F32)

    def out_projection_piece(k, piece):
        cols = slice(piece * PIECE, (piece + 1) * PIECE)
        out_ref[:, cols] = jnp.dot(mix_ref[k], wout_ref[:, cols], preferred_element_type=F32)

    def post_norm(k):
        rows = slice(k * tt, (k + 1) * tt)
        z = xc_ref[0, rows, :] + out_ref[...]
        mu = jnp.mean(z, axis=-1, keepdims=True)
        zc = z - mu
        var = jnp.mean(zc * zc, axis=-1, keepdims=True)
        o_ref[0, rows, :] = ((zc * lax.rsqrt(var + LN_EPS / DN_ALPHA ** 2)) * lng_ref[...]
                             + lnb_ref[...])

    def make_fillers(k):
        stages = []
        if k > 0:
            for piece in range(N_OUT_PIECES):
                stages.append(lambda piece=piece: out_projection_piece(k - 1, piece))
            stages.append(lambda: post_norm(k - 1))
        for piece in range(N_IN_PIECES):
            def in_stage(piece=piece):
                in_projection_piece(k + 1, piece)
                projected[k + 1] = piece + 1
            stages.append(in_stage)
        return iter(stages)

    projected = {}

    def process_tile(k, head):
        h_ref = h_bufs[k % 2]
        base = k * tt
        mix = mix_ref.at[k]
        if k + 1 < TILES_PER_STEP:
            x_next_rows = xc_ref[0, (k + 1) * tt:(k + 2) * tt, :]
        else:
            x_next_rows = xn_ref[0]
        xb_ref[(k + 1) % 2] = x_next_rows.astype(BF16)
        fillers = make_fillers(k)

        def fill(n=1):
            for _ in range(n):
                stage = next(fillers, None)
                if stage is not None:
                    stage()

        def hg_pre(c, h=h_ref):
            rows = slice(c * HG_CHUNK, (c + 1) * HG_CHUNK)
            f = lb + (1.0 - lb) * _sigmoid(h[rows, OFF_HF:OFF_HF + HG_WIDTH])
            log_f = jnp.log2(f)
            hi = log_f.astype(BF16)
            rem = log_f - hi.astype(F32)
            mid = rem.astype(BF16)
            lo = (rem - mid.astype(F32)).astype(BF16)
            return 1.0 - f, jnp.concatenate([hi, mid, lo], axis=0)

        def hg_cum(parts):
            return jnp.dot(cum3, parts, preferred_element_type=F32)

        def hg_decays(c, k_in, g_cum, h=h_ref):
            rows = slice(c * HG_CHUNK, (c + 1) * HG_CHUNK)
            hq = h[rows, OFF_HQ:OFF_HQ + HG_WIDTH]
            g_last = g_cum[HG_CHUNK - 1:HG_CHUNK, :]
            q_dec = ((hq * _sigmoid(hq)) * jnp.exp2(g_cum)).astype(BF16)
            k_dec = (k_in * jnp.exp2(-g_cum)).astype(BF16)
            k_tail = (k_in * jnp.exp2(g_last - g_cum)).astype(BF16)
            return q_dec, k_dec, k_tail, jnp.exp2(g_last)

        def hg_scores(q_dec, k_dec, hd):
            sl = slice(hd * HG_HEAD_DIM, (hd + 1) * HG_HEAD_DIM)
            return lax.dot_general(q_dec[:, sl], k_dec[:, sl], NT_DIMS, preferred_element_type=F32)

        def hg_output(c, hd, a, q_dec, k_tail, decay):
            rows = slice(c * HG_CHUNK, (c + 1) * HG_CHUNK)
            sl = slice(hd * HG_HEAD_DIM, (hd + 1) * HG_HEAD_DIM)
            v = h_ref[rows, OFF_HI + hd * HG_HEAD_DIM:OFF_HI + (hd + 1) * HG_HEAD_DIM].astype(BF16)
            s_t = st_ref[hd]
            o = jnp.dot(jnp.where(tril, a, 0.0).astype(BF16), v, preferred_element_type=F32)
            o = o + lax.dot_general(q_dec[:, sl], s_t.astype(BF16), NT_DIMS,
                                    preferred_element_type=F32)
            inc_t = lax.dot_general(v, k_tail[:, sl], TN_DIMS, preferred_element_type=F32)
            st_ref[hd] = s_t * decay[:, sl] + inc_t
            o = o * lax.rsqrt(jnp.sum(o * o, axis=-1, keepdims=True) + HG_HEAD_DIM * RMS_EPS)
            o = o * nw[:, sl]
            hg = h_ref[rows, OFF_HG + hd * HG_HEAD_DIM:OFF_HG + (hd + 1) * HG_HEAD_DIM]
            mix[rows, sl] = (o * (hg * _sigmoid(hg))).astype(BF16)

        def sw_prepare(blk):
            r0 = blk * WINDOW
            rows = slice(r0, r0 + WINDOW)
            cur = slice(base + r0, base + r0 + WINDOW)
            cos = cos_ref[base + r0:base + r0 + WINDOW, :]
            sin = sin_ref[base + r0:base + r0 + WINDOW, :]
            k_rot = rope(h_ref[rows, OFF_AK:OFF_AK + SW_KV_WIDTH], cos, sin)
            for idx, kv in enumerate(head_variants(k_rot)):
                kwin_ref[idx, cur, :] = kv.astype(BF16)
            v_t = h_ref[rows, OFF_AV:OFF_AV + SW_KV_WIDTH].T
            for g in range(SW_KV_HEADS):
                vt_ref[g, 0:SW_HEAD_DIM, cur] = (
                    v_t[g * SW_HEAD_DIM:(g + 1) * SW_HEAD_DIM].astype(BF16))
            q_bf = []
            for tile in range(SW_WIDTH // LANES):
                q_rot = rope(h_ref[rows, OFF_AQ + tile * LANES:OFF_AQ + (tile + 1) * LANES], cos, sin)
                q_bf.append((q_rot * scale).astype(BF16))
            return [jnp.concatenate([q_bf[2 * g], q_bf[2 * g + 1]], axis=0)
                    for g in range(SW_KV_HEADS)]

        def key_window(blk, idx):
            r0 = base + blk * WINDOW
            if r0 == 0:
                return jnp.concatenate([kcarry_ref[idx], kwin_ref[idx, 0:WINDOW, :]], axis=0)
            return kwin_ref[idx, r0 - WINDOW:r0 + WINDOW, :]

        def value_window(blk, g):
            r0 = base + blk * WINDOW
            if r0 == 0:
                return jnp.concatenate([vcarry_ref[g], vt_ref[g, :, 0:WINDOW]], axis=1)
            return vt_ref[g, :, r0 - WINDOW:r0 + WINDOW]

        def sw_scores(blk, g, j, q_pair):
            return lax.dot_general(key_window(blk, 2 * g + j), q_pair, NT_DIMS,
                                   preferred_element_type=F32)

        def sw_values(blk, g, j, s_t, cap):
            sink_row = jnp.where(pair_lo, sinks_ref[SW_GROUP * g + j],
                                 sinks_ref[SW_GROUP * g + HEADS_PER_TILE + j]) * LOG2_E
            s_t = jnp.minimum(s_t, cap)
            m = jnp.maximum(jnp.max(s_t, axis=0, keepdims=True), sink_row)
            p_t = jnp.exp2(s_t - m).astype(BF16)
            acc = jnp.dot(value_window(blk, g), p_t, preferred_element_type=F32)
            denom = acc[SW_HEAD_DIM:SW_HEAD_DIM + 1, :] + jnp.exp2(sink_row - m)
            return acc[0:SW_HEAD_DIM, :] * (1.0 / denom)

        def sw_finish(blk, g, o_t):
            rows = slice(blk * WINDOW, (blk + 1) * WINDOW)
            for tl in range(2):
                tile = 2 * g + tl
                lanes = slice(tl * LANES, (tl + 1) * LANES)
                o_a = jnp.concatenate([o_t[0][:, lanes], o_t[1][:, lanes]], axis=0).T
                ag = h_ref[rows, OFF_AG + tile * LANES:OFF_AG + (tile + 1) * LANES]
                mix[rows, HG_WIDTH + tile * LANES:HG_WIDTH + (tile + 1) * LANES] = (
                    o_a * (ag * _sigmoid(ag))).astype(BF16)

        many = k > 0
        masks = [first_cap if (k == 0 and blk == 0) else band_cap for blk in range(N_BLOCKS)]
        chains = [(blk, g, j) for blk in range(N_BLOCKS) for g in range(SW_KV_HEADS)
                  for j in range(HEADS_PER_TILE)]
        q_pairs = {}
        scores = {}

        def sw_issue_scores(i):
            if i < len(chains) and i not in scores:
                blk, g, j = chains[i]
                if blk not in q_pairs:
                    q_pairs[blk] = sw_prepare(blk)
                scores[i] = sw_scores(blk, g, j, q_pairs[blk][g])

        head = head or {"pre": {}, "cum": {}, "dec": {}, "scores": {}}
        pre = dict(head["pre"])
        cum = dict(head["cum"])

        def hg_issue_cum(c):
            if c < N_CHUNKS and c not in cum:
                pre[c] = hg_pre(c)
                cum[c] = hg_cum(pre[c][1])

        fill(2 if many else 1)
        for c in range(min(2, N_CHUNKS)):
            hg_issue_cum(c)
            fill(1 if many else 0)
        for c in range(N_CHUNKS):
            if c in head["dec"]:
                q_dec, k_dec, k_tail, decay = head["dec"][c]
                hg_s = head["scores"][c]
            else:
                q_dec, k_dec, k_tail, decay = hg_decays(c, pre[c][0], cum[c])
                hg_s = [hg_scores(q_dec, k_dec, hd) for hd in range(HG_HEADS)]
            hg_issue_cum(c + 2)
            fill()
            for hd in range(HG_HEADS):
                hg_output(c, hd, hg_s[hd], q_dec, k_tail, decay)
            fill(1 if many else 0)

        for i in range(min(SW_SCORES_AHEAD, len(chains))):
            sw_issue_scores(i)
        fill()
        outs = {}
        has_next = k + 1 < TILES_PER_STEP
        next_head = {"pre": {}, "cum": {}, "dec": {}, "scores": {}} if has_next else None
        h_next = h_bufs[(k + 1) % 2]

        def head_start(i):
            if i != HEAD_AFTER_CHAIN:
                return
            assert projected.get(k + 1, 0) * PIECE >= OFF_HF + HG_WIDTH
            for c in range(HEAD_CUM_CHUNKS):
                next_head["pre"][c] = hg_pre(c, h_next)
                next_head["cum"][c] = hg_cum(next_head["pre"][c][1])
            for c in range(HEAD_DEC_CHUNKS):
                d = hg_decays(c, next_head["pre"][c][0], next_head["cum"][c], h_next)
                next_head["dec"][c] = d
                next_head["scores"][c] = [hg_scores(d[0], d[1], hd) for hd in range(HG_HEADS)]

        for i, (blk, g, j) in enumerate(chains):
            outs[(blk, g, j)] = sw_values(blk, g, j, scores.pop(i), masks[blk])
            sw_issue_scores(i + SW_SCORES_AHEAD)
            if j == HEADS_PER_TILE - 1:
                sw_finish(blk, g, [outs.pop((blk, g, jj)) for jj in range(HEADS_PER_TILE)])
            fill()
            if has_next:
                head_start(i)
        fill(N_OUT_PIECES + 1 + N_IN_PIECES)
        return next_head

    head = None
    for k in range(TILES_PER_STEP):
        head = process_tile(k, head)
    for piece in range(N_OUT_PIECES):
        out_projection_piece(TILES_PER_STEP - 1, piece)
    post_norm(TILES_PER_STEP - 1)

    last = slice((STEP_BLOCKS - 1) * WINDOW, STEP_BLOCKS * WINDOW)
    for idx in range(2 * SW_KV_HEADS):
        kcarry_ref[idx] = kwin_ref[idx, last, :]
    for g in range(SW_KV_HEADS):
        vcarry_ref[g, 0:SW_HEAD_DIM, :] = vt_ref[g, 0:SW_HEAD_DIM, last]


def _rope_tables(seq_len):
    pos = jnp.arange(seq_len, dtype=F32)
    inv_freq = ROPE_THETA ** (-jnp.arange(0, ROPE_DIM, 2, dtype=F32) / ROPE_DIM)
    ang = pos[:, None] * inv_freq[None, :]
    cos = jnp.cos(ang)
    sin = jnp.sin(ang)
    ones = jnp.ones((seq_len, SW_HEAD_DIM - ROPE_DIM), F32)
    cos_head = jnp.concatenate([cos, cos, ones], axis=1)
    sin_head = jnp.concatenate([-sin, sin, jnp.zeros_like(ones)], axis=1)
    return (jnp.tile(cos_head, (1, HEADS_PER_TILE)), jnp.tile(sin_head, (1, HEADS_PER_TILE)))


def kernel(x, w_in, lb_logits, hg_norm_w, sinks, w_out, ln_g, ln_b):
    batch, seq_len, d_model = x.shape
    assert d_model == D_MODEL and w_in.shape == (DEPTH, D_MODEL, IN_WIDTH)
    step_rows = TILES_PER_STEP * TIME_TILE
    assert seq_len % step_rows == 0
    cos_tab, sin_tab = _rope_tables(seq_len)
    n_steps = seq_len // step_rows
    last_step = batch * n_steps - 1

    def next_tile_index(b, u, *_):
        nxt = jnp.minimum(b * n_steps + u + 1, last_step)
        return (nxt // n_steps, TILES_PER_STEP * (nxt % n_steps), 0)

    const = lambda b, u, *_: (0, 0)
    grid_spec = pltpu.PrefetchScalarGridSpec(
        num_scalar_prefetch=1,
        grid=(batch, n_steps),
        in_specs=[
            pl.BlockSpec((1, step_rows, D_MODEL), lambda b, u, *_: (b, u, 0)),
            pl.BlockSpec((1, TIME_TILE, D_MODEL), next_tile_index),
            pl.BlockSpec(memory_space=pl.ANY),
            pl.BlockSpec(memory_space=pl.ANY),
            pl.BlockSpec((DEPTH + 1, HG_WIDTH), const),
            pl.BlockSpec((1, HG_WIDTH), const),
            pl.BlockSpec((1, D_MODEL), const),
            pl.BlockSpec((1, D_MODEL), const),
            pl.BlockSpec((step_rows, LANES), lambda b, u, *_: (u, 0)),
            pl.BlockSpec((step_rows, LANES), lambda b, u, *_: (u, 0)),
        ],
        out_specs=pl.BlockSpec((1, step_rows, D_MODEL), lambda b, u, *_: (b, u, 0)),
        scratch_shapes=[
            pltpu.VMEM((TIME_TILE, IN_WIDTH), F32),
            pltpu.VMEM((TIME_TILE, IN_WIDTH), F32),
            pltpu.VMEM((2, TIME_TILE, D_MODEL), BF16),
            pltpu.VMEM((TILES_PER_STEP, TIME_TILE, D_MODEL), BF16),
            pltpu.VMEM((TIME_TILE, D_MODEL), F32),
            pltpu.VMEM((HG_HEADS, HG_HEAD_DIM, HG_HEAD_DIM), F32),
            pltpu.VMEM((2 * SW_KV_HEADS, STEP_BLOCKS * WINDOW, LANES), BF16),
            pltpu.VMEM((SW_KV_HEADS, VT_ROWS, STEP_BLOCKS * WINDOW), BF16),
            pltpu.VMEM((2 * SW_KV_HEADS, WINDOW, LANES), BF16),
            pltpu.VMEM((SW_KV_HEADS, VT_ROWS, WINDOW), BF16),
            pltpu.VMEM((D_MODEL, IN_WIDTH), BF16),
            pltpu.VMEM((D_MODEL, D_MODEL), BF16),
            pltpu.VMEM((WEIGHT_SLOTS, WEIGHT_CHUNK_ROWS, IN_WIDTH), F32),
            pltpu.VMEM((WEIGHT_SLOTS, WEIGHT_CHUNK_ROWS, D_MODEL), F32),
            pltpu.SemaphoreType.DMA((WEIGHT_SLOTS,)),
            pltpu.SemaphoreType.DMA((WEIGHT_SLOTS,)),
        ],
    )
    return pl.pallas_call(
        _layer_kernel,
        grid_spec=grid_spec,
        out_shape=jax.ShapeDtypeStruct(x.shape, x.dtype),
        compiler_params=pltpu.CompilerParams(
            dimension_semantics=("arbitrary", "arbitrary"),
            vmem_limit_bytes=VMEM_LIMIT_BYTES),
        name="hybrid_hgrn2_swa_layer",
    )(sinks[0], x, x, w_in, w_out, lb_logits, hg_norm_w, ln_g, ln_b, cos_tab, sin_tab)
```

```python
import jax
import jax.numpy as jnp
from jax import lax
from jax.experimental import pallas as pl
from jax.experimental.pallas import tpu as pltpu

D_MODEL = 1024
DEPTH = 1
HG_WIDTH = 512
HG_HEAD_DIM = 128
HG_HEADS = HG_WIDTH // HG_HEAD_DIM
HG_CHUNK = 64
SW_WIDTH = 512
SW_HEAD_DIM = 64
SW_Q_HEADS = SW_WIDTH // SW_HEAD_DIM
SW_KV_HEADS = SW_Q_HEADS // 4
SW_KV_WIDTH = SW_KV_HEADS * SW_HEAD_DIM
SW_GROUP = SW_Q_HEADS // SW_KV_HEADS
WINDOW = 128
ROPE_THETA = 500000.0
ROPE_DIM = SW_HEAD_DIM // 4
DN_ALPHA = (2.0 * DEPTH) ** 0.25
LN_EPS = 1e-5
RMS_EPS = 1e-6
IN_WIDTH = 4 * HG_WIDTH + SW_WIDTH + 2 * SW_KV_WIDTH + SW_WIDTH

OFF_HQ = 0
OFF_HF = OFF_HQ + HG_WIDTH
OFF_HI = OFF_HF + HG_WIDTH
OFF_HG = OFF_HI + HG_WIDTH
OFF_AQ = OFF_HG + HG_WIDTH
OFF_AK = OFF_AQ + SW_WIDTH
OFF_AV = OFF_AK + SW_KV_WIDTH
OFF_AG = OFF_AV + SW_KV_WIDTH

LANES = 128
BF16_SUBLANES = 16
MXU_WIDTH = 256
HEADS_PER_TILE = LANES // SW_HEAD_DIM
assert HEADS_PER_TILE == 2 and SW_GROUP == 2 * HEADS_PER_TILE and SW_KV_WIDTH == LANES
VT_ROWS = SW_HEAD_DIM + BF16_SUBLANES
TIME_TILE = 256
TILES_PER_STEP = 4
PIECE = MXU_WIDTH
N_IN_PIECES = IN_WIDTH // PIECE
N_OUT_PIECES = D_MODEL // PIECE
assert IN_WIDTH % PIECE == 0 and D_MODEL % PIECE == 0
N_CHUNKS = TIME_TILE // HG_CHUNK
N_BLOCKS = TIME_TILE // WINDOW
STEP_BLOCKS = TILES_PER_STEP * N_BLOCKS
HEAD_CUM_CHUNKS = 3
HEAD_DEC_CHUNKS = 1
HEAD_AFTER_CHAIN = 3
SW_SCORES_AHEAD = 6
WEIGHT_CHUNK_ROWS = 64
WEIGHT_SLOTS = 4
VMEM_LIMIT_BYTES = 58 * 1024 * 1024

F32 = jnp.float32
BF16 = jnp.bfloat16
NT_DIMS = (((1,), (1,)), ((), ()))
TN_DIMS = (((0,), (0,)), ((), ()))


LOG2_E = 1.4426950408889634


def _sigmoid(v):
    return 1.0 / (1.0 + jnp.exp2(v * (-LOG2_E)))


def _load_weight_as_bf16(w_hbm, stage_ref, sem_ref, dst_ref, scale=None):
    n_rows = dst_ref.shape[0]
    n_slots, chunk = stage_ref.shape[0], stage_ref.shape[1]
    n_chunks = n_rows // chunk
    assert n_chunks * chunk == n_rows and n_slots >= 2

    def chunk_copy(i):
        return pltpu.make_async_copy(w_hbm.at[0, pl.ds(i * chunk, chunk), :],
                                     stage_ref.at[i % n_slots], sem_ref.at[i % n_slots])

    for i in range(min(n_slots - 1, n_chunks)):
        chunk_copy(i).start()
    for i in range(n_chunks):
        ahead = i + n_slots - 1
        if ahead < n_chunks:
            chunk_copy(ahead).start()
        chunk_copy(i).wait()
        rows = stage_ref[i % n_slots]
        if scale is not None:
            rows = rows * scale
        dst_ref[i * chunk:(i + 1) * chunk, :] = rows.astype(BF16)


def _layer_kernel(sinks_ref, xc_ref, xn_ref, win_hbm, wout_hbm, lbl_ref, nw_ref, lng_ref, lnb_ref,
                  cos_ref, sin_ref, o_ref,
                  ha_ref, hb_ref, xb_ref, mix_ref, out_ref, st_ref, kwin_ref, vt_ref,
                  kcarry_ref, vcarry_ref, win_ref, wout_ref, win_stage, wout_stage,
                  win_sems, wout_sems):
    b = pl.program_id(0)
    u = pl.program_id(1)
    tt = TIME_TILE
    h_bufs = (ha_ref, hb_ref)

    @pl.when(jnp.logical_and(b == 0, u == 0))
    def _first_step():
        _load_weight_as_bf16(win_hbm, win_stage, win_sems, win_ref)
        _load_weight_as_bf16(wout_hbm, wout_stage, wout_sems, wout_ref, scale=1.0 / DN_ALPHA)
        ha_ref[...] = jnp.dot(xc_ref[0, 0:tt, :].astype(BF16), win_ref[...],
                              preferred_element_type=F32)

    @pl.when(u == 0)
    def _reset_carries():
        st_ref[...] = jnp.zeros_like(st_ref)
        kcarry_ref[...] = jnp.zeros_like(kcarry_ref)
        vcarry_ref[:, 0:SW_HEAD_DIM, :] = jnp.zeros((SW_KV_HEADS, SW_HEAD_DIM, WINDOW), BF16)
        vcarry_ref[:, SW_HEAD_DIM:VT_ROWS, :] = jnp.ones((SW_KV_HEADS, BF16_SUBLANES, WINDOW), BF16)
        vt_ref[:, SW_HEAD_DIM:VT_ROWS, :] = jnp.ones(
            (SW_KV_HEADS, BF16_SUBLANES, STEP_BLOCKS * WINDOW), BF16)

    lbl = lbl_ref[...]
    lbl_e = jnp.exp(lbl - jnp.max(lbl, axis=0, keepdims=True))
    lb = lbl_e[0:1] / jnp.sum(lbl_e, axis=0, keepdims=True)
    nw = nw_ref[...] * (HG_HEAD_DIM ** 0.5)
    row = lax.broadcasted_iota(jnp.int32, (HG_CHUNK, HG_CHUNK), 0)
    col = lax.broadcasted_iota(jnp.int32, (HG_CHUNK, HG_CHUNK), 1)
    tril = col <= row
    cum_mat = tril.astype(BF16)
    cum3 = jnp.concatenate([cum_mat, cum_mat, cum_mat], axis=1)

    lane = lax.broadcasted_iota(jnp.int32, (1, LANES), 1)
    rope_first = (lane % SW_HEAD_DIM) < (ROPE_DIM // 2)
    lane_lo = lane < SW_HEAD_DIM
    key_idx = lax.broadcasted_iota(jnp.int32, (2 * WINDOW, 2 * WINDOW), 0)
    qry_idx = lax.broadcasted_iota(jnp.int32, (2 * WINDOW, 2 * WINDOW), 1) % WINDOW
    band = (key_idx > qry_idx) & (key_idx <= qry_idx + WINDOW)
    pair_lo = lax.broadcasted_iota(jnp.int32, (1, 2 * WINDOW), 1) < WINDOW
    scale = SW_HEAD_DIM ** -0.5 * LOG2_E
    first_valid = jnp.where(u > 0, 0, WINDOW)
    band_cap = jnp.where(band, jnp.inf, -jnp.inf)
    first_cap = jnp.where(band & (key_idx >= first_valid), jnp.inf, -jnp.inf)

    def rope(v, cos, sin):
        partner = jnp.where(rope_first,
                            pltpu.roll(v, LANES - ROPE_DIM // 2, 1),
                            pltpu.roll(v, ROPE_DIM // 2, 1))
        return v * cos + partner * sin

    def head_variants(v):
        sw = pltpu.roll(v, SW_HEAD_DIM, 1)
        zero = jnp.zeros_like(v)
        return (jnp.where(lane_lo, v, zero), jnp.where(lane_lo, zero, sw),
                jnp.where(lane_lo, sw, zero), jnp.where(lane_lo, zero, v))

    def in_projection_piece(k, piece):
        xb = xb_ref.at[k % 2]
        cols = slice(piece * PIECE, (piece + 1) * PIECE)
        h_bufs[k % 2][:, cols] = jnp.dot(xb[...], win_ref[:, cols], preferred_element_type=F32)

    def out_projection_piece(k, piece):
        cols = slice(piece * PIECE, (piece + 1) * PIECE)
        out_ref[:, cols] = jnp.dot(mix_ref[k], wout_ref[:, cols], preferred_element_type=F32)

    def post_norm(k):
        rows = slice(k * tt, (k + 1) * tt)
        z = xc_ref[0, rows, :] + out_ref[...]
        mu = jnp.mean(z, axis=-1, keepdims=True)
        zc = z - mu
        var = jnp.mean(zc * zc, axis=-1, keepdims=True)
        o_ref[0, rows, :] = ((zc * lax.rsqrt(var + LN_EPS / DN_ALPHA ** 2)) * lng_ref[...]
                             + lnb_ref[...])

    def make_fillers(k):
        stages = []
        if k > 0:
            for piece in range(N_OUT_PIECES):
                stages.append(lambda piece=piece: out_projection_piece(k - 1, piece))
            stages.append(lambda: post_norm(k - 1))
        for piece in range(N_IN_PIECES):
            def in_stage(piece=piece):
                in_projection_piece(k + 1, piece)
                projected[k + 1] = piece + 1
            stages.append(in_stage)
        return iter(stages)

    projected = {}

    def process_tile(k, head):
        h_ref = h_bufs[k % 2]
        base = k * tt
        mix = mix_ref.at[k]
        if k + 1 < TILES_PER_STEP:
            x_next_rows = xc_ref[0, (k + 1) * tt:(k + 2) * tt, :]
        else:
            x_next_rows = xn_ref[0]
        xb_ref[(k + 1) % 2] = x_next_rows.astype(BF16)
        fillers = make_fillers(k)

        def fill(n=1):
            for _ in range(n):
                stage = next(fillers, None)
                if stage is not None:
                    stage()

        def hg_pre(c, h=h_ref):
            rows = slice(c * HG_CHUNK, (c + 1) * HG_CHUNK)
            f = lb + (1.0 - lb) * _sigmoid(h[rows, OFF_HF:OFF_HF + HG_WIDTH])
            log_f = jnp.log2(f)
            hi = log_f.astype(BF16)
            rem = log_f - hi.astype(F32)
            mid = rem.astype(BF16)
            lo = (rem - mid.astype(F32)).astype(BF16)
            return 1.0 - f, jnp.concatenate([hi, mid, lo], axis=0)

        def hg_cum(parts):
            return jnp.dot(cum3, parts, preferred_element_type=F32)

        def hg_decays(c, k_in, g_cum, h=h_ref):
            rows = slice(c * HG_CHUNK, (c + 1) * HG_CHUNK)
            hq = h[rows, OFF_HQ:OFF_HQ + HG_WIDTH]
            g_last = g_cum[HG_CHUNK - 1:HG_CHUNK, :]
            q_dec = ((hq * _sigmoid(hq)) * jnp.exp2(g_cum)).astype(BF16)
            k_dec = (k_in * jnp.exp2(-g_cum)).astype(BF16)
            k_tail = (k_in * jnp.exp2(g_last - g_cum)).astype(BF16)
            return q_dec, k_dec, k_tail, jnp.exp2(g_last)

        def hg_scores(q_dec, k_dec, hd):
            sl = slice(hd * HG_HEAD_DIM, (hd + 1) * HG_HEAD_DIM)
            return lax.dot_general(q_dec[:, sl], k_dec[:, sl], NT_DIMS, preferred_element_type=F32)

        def hg_output(c, hd, a, q_dec, k_tail, decay):
            rows = slice(c * HG_CHUNK, (c + 1) * HG_CHUNK)
            sl = slice(hd * HG_HEAD_DIM, (hd + 1) * HG_HEAD_DIM)
            v = h_ref[rows, OFF_HI + hd * HG_HEAD_DIM:OFF_HI + (hd + 1) * HG_HEAD_DIM].astype(BF16)
            s_t = st_ref[hd]
            o = jnp.dot(jnp.where(tril, a, 0.0).astype(BF16), v, preferred_element_type=F32)
            o = o + lax.dot_general(q_dec[:, sl], s_t.astype(BF16), NT_DIMS,
                                    preferred_element_type=F32)
            inc_t = lax.dot_general(v, k_tail[:, sl], TN_DIMS, preferred_element_type=F32)
            st_ref[hd] = s_t * decay[:, sl] + inc_t
            o = o * lax.rsqrt(jnp.sum(o * o, axis=-1, keepdims=True) + HG_HEAD_DIM * RMS_EPS)
            o = o * nw[:, sl]
            hg = h_ref[rows, OFF_HG + hd * HG_HEAD_DIM:OFF_HG + (hd + 1) * HG_HEAD_DIM]
            mix[rows, sl] = (o * (hg * _sigmoid(hg))).astype(BF16)

        def sw_prepare(blk):
            r0 = blk * WINDOW
            rows = slice(r0, r0 + WINDOW)
            cur = slice(base + r0, base + r0 + WINDOW)
            cos = cos_ref[base + r0:base + r0 + WINDOW, :]
            sin = sin_ref[base + r0:base + r0 + WINDOW, :]
            k_rot = rope(h_ref[rows, OFF_AK:OFF_AK + SW_KV_WIDTH], cos, sin)
            for idx, kv in enumerate(head_variants(k_rot)):
                kwin_ref[idx, cur, :] = kv.astype(BF16)
            v_t = h_ref[rows, OFF_AV:OFF_AV + SW_KV_WIDTH].T
            for g in range(SW_KV_HEADS):
                vt_ref[g, 0:SW_HEAD_DIM, cur] = (
                    v_t[g * SW_HEAD_DIM:(g + 1) * SW_HEAD_DIM].astype(BF16))
            q_bf = []
            for tile in range(SW_WIDTH // LANES):
                q_rot = rope(h_ref[rows, OFF_AQ + tile * LANES:OFF_AQ + (tile + 1) * LANES], cos, sin)
                q_bf.append((q_rot * scale).astype(BF16))
            return [jnp.concatenate([q_bf[2 * g], q_bf[2 * g + 1]], axis=0)
                    for g in range(SW_KV_HEADS)]

        def key_window(blk, idx):
            r0 = base + blk * WINDOW
            if r0 == 0:
                return jnp.concatenate([kcarry_ref[idx], kwin_ref[idx, 0:WINDOW, :]], axis=0)
            return kwin_ref[idx, r0 - WINDOW:r0 + WINDOW, :]

        def value_window(blk, g):
            r0 = base + blk * WINDOW
            if r0 == 0:
                return jnp.concatenate([vcarry_ref[g], vt_ref[g, :, 0:WINDOW]], axis=1)
            return vt_ref[g, :, r0 - WINDOW:r0 + WINDOW]

        def sw_scores(blk, g, j, q_pair):
            return lax.dot_general(key_window(blk, 2 * g + j), q_pair, NT_DIMS,
                                   preferred_element_type=F32)

        def sw_values(blk, g, j, s_t, cap):
            sink_row = jnp.where(pair_lo, sinks_ref[SW_GROUP * g + j],
                                 sinks_ref[SW_GROUP * g + HEADS_PER_TILE + j]) * LOG2_E
            s_t = jnp.minimum(s_t, cap)
            m = jnp.maximum(jnp.max(s_t, axis=0, keepdims=True), sink_row)
            p_t = jnp.exp2(s_t - m).astype(BF16)
            acc = jnp.dot(value_window(blk, g), p_t, preferred_element_type=F32)
            denom = acc[SW_HEAD_DIM:SW_HEAD_DIM + 1, :] + jnp.exp2(sink_row - m)
            return acc[0:SW_HEAD_DIM, :] * (1.0 / denom)

        def sw_finish(blk, g, o_t):
            rows = slice(blk * WINDOW, (blk + 1) * WINDOW)
            for tl in range(2):
                tile = 2 * g + tl
                lanes = slice(tl * LANES, (tl + 1) * LANES)
                o_a = jnp.concatenate([o_t[0][:, lanes], o_t[1][:, lanes]], axis=0).T
                ag = h_ref[rows, OFF_AG + tile * LANES:OFF_AG + (tile + 1) * LANES]
                mix[rows, HG_WIDTH + tile * LANES:HG_WIDTH + (tile + 1) * LANES] = (
                    o_a * (ag * _sigmoid(ag))).astype(BF16)

        many = k > 0
        masks = [first_cap if (k == 0 and blk == 0) else band_cap for blk in range(N_BLOCKS)]
        chains = [(blk, g, j) for blk in range(N_BLOCKS) for g in range(SW_KV_HEADS)
                  for j in range(HEADS_PER_TILE)]
        q_pairs = {}
        scores = {}

        def sw_issue_scores(i):
            if i < len(chains) and i not in scores:
                blk, g, j = chains[i]
                if blk not in q_pairs:
                    q_pairs[blk] = sw_prepare(blk)
                scores[i] = sw_scores(blk, g, j, q_pairs[blk][g])

        head = head or {"pre": {}, "cum": {}, "dec": {}, "scores": {}}
        pre = dict(head["pre"])
        cum = dict(head["cum"])

        def hg_issue_cum(c):
            if c < N_CHUNKS and c not in cum:
                pre[c] = hg_pre(c)
                cum[c] = hg_cum(pre[c][1])

        fill(2 if many else 1)
        for c in range(min(2, N_CHUNKS)):
            hg_issue_cum(c)
            fill(1 if many else 0)
        for c in range(N_CHUNKS):
            if c in head["dec"]:
                q_dec, k_dec, k_tail, decay = head["dec"][c]
                hg_s = head["scores"][c]
            else:
                q_dec, k_dec, k_tail, decay = hg_decays(c, pre[c][0], cum[c])
                hg_s = [hg_scores(q_dec, k_dec, hd) for hd in range(HG_HEADS)]
            hg_issue_cum(c + 2)
            fill()
            for hd in range(HG_HEADS):
                hg_output(c, hd, hg_s[hd], q_dec, k_tail, decay)
            fill(1 if many else 0)

        for i in range(min(SW_SCORES_AHEAD, len(chains))):
            sw_issue_scores(i)
        fill()
        outs = {}
        has_next = k + 1 < TILES_PER_STEP
        next_head = {"pre": {}, "cum": {}, "dec": {}, "scores": {}} if has_next else None
        h_next = h_bufs[(k + 1) % 2]

        def head_start(i):
            if i != HEAD_AFTER_CHAIN:
                return
            assert projected.get(k + 1, 0) * PIECE >= OFF_HF + HG_WIDTH
            for c in range(HEAD_CUM_CHUNKS):
                next_head["pre"][c] = hg_pre(c, h_next)
                next_head["cum"][c] = hg_cum(next_head["pre"][c][1])
            for c in range(HEAD_DEC_CHUNKS):
                d = hg_decays(c, next_head["pre"][c][0], next_head["cum"][c], h_next)
                next_head["dec"][c] = d
                next_head["scores"][c] = [hg_scores(d[0], d[1], hd) for hd in range(HG_HEADS)]

        for i, (blk, g, j) in enumerate(chains):
            outs[(blk, g, j)] = sw_values(blk, g, j, scores.pop(i), masks[blk])
            sw_issue_scores(i + SW_SCORES_AHEAD)
            if j == HEADS_PER_TILE - 1:
                sw_finish(blk, g, [outs.pop((blk, g, jj)) for jj in range(HEADS_PER_TILE)])
            fill()
            if has_next:
                head_start(i)
        fill(N_OUT_PIECES + 1 + N_IN_PIECES)
        return next_head

    head = None
    for k in range(TILES_PER_STEP):
        head = process_tile(k, head)
    for piece in range(N_OUT_PIECES):
        out_projection_piece(TILES_PER_STEP - 1, piece)
    post_norm(TILES_PER_STEP - 1)

    last = slice((STEP_BLOCKS - 1) * WINDOW, STEP_BLOCKS * WINDOW)
    for idx in range(2 * SW_KV_HEADS):
        kcarry_ref[idx] = kwin_ref[idx, last, :]
    for g in range(SW_KV_HEADS):
        vcarry_ref[g, 0:SW_HEAD_DIM, :] = vt_ref[g, 0:SW_HEAD_DIM, last]


def _rope_tables(seq_len):
    pos = jnp.arange(seq_len, dtype=F32)
    inv_freq = ROPE_THETA ** (-jnp.arange(0, ROPE_DIM, 2, dtype=F32) / ROPE_DIM)
    ang = pos[:, None] * inv_freq[None, :]
    cos = jnp.cos(ang)
    sin = jnp.sin(ang)
    ones = jnp.ones((seq_len, SW_HEAD_DIM - ROPE_DIM), F32)
    cos_head = jnp.concatenate([cos, cos, ones], axis=1)
    sin_head = jnp.concatenate([-sin, sin, jnp.zeros_like(ones)], axis=1)
    return (jnp.tile(cos_head, (1, HEADS_PER_TILE)), jnp.tile(sin_head, (1, HEADS_PER_TILE)))


def kernel(x, w_in, lb_logits, hg_norm_w, sinks, w_out, ln_g, ln_b):
    batch, seq_len, d_model = x.shape
    assert d_model == D_MODEL and w_in.shape == (DEPTH, D_MODEL, IN_WIDTH)
    step_rows = TILES_PER_STEP * TIME_TILE
    assert seq_len % step_rows == 0
    cos_tab, sin_tab = _rope_tables(seq_len)
    n_steps = seq_len // step_rows
    last_step = batch * n_steps - 1

    def next_tile_index(b, u, *_):
        nxt = jnp.minimum(b * n_steps + u + 1, last_step)
        return (nxt // n_steps, TILES_PER_STEP * (nxt % n_steps), 0)

    const = lambda b, u, *_: (0, 0)
    grid_spec = pltpu.PrefetchScalarGridSpec(
        num_scalar_prefetch=1,
        grid=(batch, n_steps),
        in_specs=[
            pl.BlockSpec((1, step_rows, D_MODEL), lambda b, u, *_: (b, u, 0)),
            pl.BlockSpec((1, TIME_TILE, D_MODEL), next_tile_index),
            pl.BlockSpec(memory_space=pl.ANY),
            pl.BlockSpec(memory_space=pl.ANY),
            pl.BlockSpec((DEPTH + 1, HG_WIDTH), const),
            pl.BlockSpec((1, HG_WIDTH), const),
            pl.BlockSpec((1, D_MODEL), const),
            pl.BlockSpec((1, D_MODEL), const),
            pl.BlockSpec((step_rows, LANES), lambda b, u, *_: (u, 0)),
            pl.BlockSpec((step_rows, LANES), lambda b, u, *_: (u, 0)),
        ],
        out_specs=pl.BlockSpec((1, step_rows, D_MODEL), lambda b, u, *_: (b, u, 0)),
        scratch_shapes=[
            pltpu.VMEM((TIME_TILE, IN_WIDTH), F32),
            pltpu.VMEM((TIME_TILE, IN_WIDTH), F32),
            pltpu.VMEM((2, TIME_TILE, D_MODEL), BF16),
            pltpu.VMEM((TILES_PER_STEP, TIME_TILE, D_MODEL), BF16),
            pltpu.VMEM((TIME_TILE, D_MODEL), F32),
            pltpu.VMEM((HG_HEADS, HG_HEAD_DIM, HG_HEAD_DIM), F32),
            pltpu.VMEM((2 * SW_KV_HEADS, STEP_BLOCKS * WINDOW, LANES), BF16),
            pltpu.VMEM((SW_KV_HEADS, VT_ROWS, STEP_BLOCKS * WINDOW), BF16),
            pltpu.VMEM((2 * SW_KV_HEADS, WINDOW, LANES), BF16),
            pltpu.VMEM((SW_KV_HEADS, VT_ROWS, WINDOW), BF16),
            pltpu.VMEM((D_MODEL, IN_WIDTH), BF16),
            pltpu.VMEM((D_MODEL, D_MODEL), BF16),
            pltpu.VMEM((WEIGHT_SLOTS, WEIGHT_CHUNK_ROWS, IN_WIDTH), F32),
            pltpu.VMEM((WEIGHT_SLOTS, WEIGHT_CHUNK_ROWS, D_MODEL), F32),
            pltpu.SemaphoreType.DMA((WEIGHT_SLOTS,)),
            pltpu.SemaphoreType.DMA((WEIGHT_SLOTS,)),
        ],
    )
    return pl.pallas_call(
        _layer_kernel,
        grid_spec=grid_spec,
        out_shape=jax.ShapeDtypeStruct(x.shape, x.dtype),
        compiler_params=pltpu.CompilerParams(
            dimension_semantics=("arbitrary", "arbitrary"),
            vmem_limit_bytes=VMEM_LIMIT_BYTES),
        name="hybrid_hgrn2_swa_layer",
    )(sinks[0], x, x, w_in, w_out, lb_logits, hg_norm_w, ln_g, ln_b, cos_tab, sin_tab)
```

```python
import jax
import jax.numpy as jnp
from jax import lax
from jax.experimental import pallas as pl
from jax.experimental.pallas import tpu as pltpu

D_MODEL = 1024
DEPTH = 1
HG_WIDTH = 512
HG_HEAD_DIM = 128
HG_HEADS = HG_WIDTH // HG_HEAD_DIM
HG_CHUNK = 64
SW_WIDTH = 512
SW_HEAD_DIM = 64
SW_Q_HEADS = SW_WIDTH // SW_HEAD_DIM
SW_KV_HEADS = SW_Q_HEADS // 4
SW_KV_WIDTH = SW_KV_HEADS * SW_HEAD_DIM
SW_GROUP = SW_Q_HEADS // SW_KV_HEADS
WINDOW = 128
ROPE_THETA = 500000.0
ROPE_DIM = SW_HEAD_DIM // 4
DN_ALPHA = (2.0 * DEPTH) ** 0.25
LN_EPS = 1e-5
RMS_EPS = 1e-6
IN_WIDTH = 4 * HG_WIDTH + SW_WIDTH + 2 * SW_KV_WIDTH + SW_WIDTH

OFF_HQ = 0
OFF_HF = OFF_HQ + HG_WIDTH
OFF_HI = OFF_HF + HG_WIDTH
OFF_HG = OFF_HI + HG_WIDTH
OFF_AQ = OFF_HG + HG_WIDTH
OFF_AK = OFF_AQ + SW_WIDTH
OFF_AV = OFF_AK + SW_KV_WIDTH
OFF_AG = OFF_AV + SW_KV_WIDTH

LANES = 128
BF16_SUBLANES = 16
MXU_WIDTH = 256
HEADS_PER_TILE = LANES // SW_HEAD_DIM
assert HEADS_PER_TILE == 2 and SW_GROUP == 2 * HEADS_PER_TILE and SW_KV_WIDTH == LANES
VT_ROWS = SW_HEAD_DIM + BF16_SUBLANES
TIME_TILE = 256
TILES_PER_STEP = 4
PIECE = MXU_WIDTH
N_IN_PIECES = IN_WIDTH // PIECE
N_OUT_PIECES = D_MODEL // PIECE
assert IN_WIDTH % PIECE == 0 and D_MODEL % PIECE == 0
N_CHUNKS = TIME_TILE // HG_CHUNK
N_BLOCKS = TIME_TILE // WINDOW
STEP_BLOCKS = TILES_PER_STEP * N_BLOCKS
HEAD_CUM_CHUNKS = 3
HEAD_DEC_CHUNKS = 1
HEAD_AFTER_CHAIN = 3
SW_SCORES_AHEAD = 8
WEIGHT_CHUNK_ROWS = 64
WEIGHT_SLOTS = 4
VMEM_LIMIT_BYTES = 58 * 1024 * 1024

F32 = jnp.float32
BF16 = jnp.bfloat16
NT_DIMS = (((1,), (1,)), ((), ()))
TN_DIMS = (((0,), (0,)), ((), ()))


LOG2_E = 1.4426950408889634


def _sigmoid(v):
    return 1.0 / (1.0 + jnp.exp2(v * (-LOG2_E)))


def _load_weight_as_bf16(w_hbm, stage_ref, sem_ref, dst_ref, scale=None):
    n_rows = dst_ref.shape[0]
    n_slots, chunk = stage_ref.shape[0], stage_ref.shape[1]
    n_chunks = n_rows // chunk
    assert n_chunks * chunk == n_rows and n_slots >= 2

    def chunk_copy(i):
        return pltpu.make_async_copy(w_hbm.at[0, pl.ds(i * chunk, chunk), :],
                                     stage_ref.at[i % n_slots], sem_ref.at[i % n_slots])

    for i in range(min(n_slots - 1, n_chunks)):
        chunk_copy(i).start()
    for i in range(n_chunks):
        ahead = i + n_slots - 1
        if ahead < n_chunks:
            chunk_copy(ahead).start()
        chunk_copy(i).wait()
        rows = stage_ref[i % n_slots]
        if scale is not None:
            rows = rows * scale
        dst_ref[i * chunk:(i + 1) * chunk, :] = rows.astype(BF16)


def _layer_kernel(sinks_ref, xc_ref, xn_ref, win_hbm, wout_hbm, lbl_ref, nw_ref, lng_ref, lnb_ref,
                  cos_ref, sin_ref, o_ref,
                  ha_ref, hb_ref, xb_ref, mix_ref, out_ref, st_ref, kwin_ref, vt_ref,
                  kcarry_ref, vcarry_ref, win_ref, wout_ref, win_stage, wout_stage,
                  win_sems, wout_sems):
    b = pl.program_id(0)
    u = pl.program_id(1)
    tt = TIME_TILE
    h_bufs = (ha_ref, hb_ref)

    @pl.when(jnp.logical_and(b == 0, u == 0))
    def _first_step():
        _load_weight_as_bf16(win_hbm, win_stage, win_sems, win_ref)
        _load_weight_as_bf16(wout_hbm, wout_stage, wout_sems, wout_ref, scale=1.0 / DN_ALPHA)
        ha_ref[...] = jnp.dot(xc_ref[0, 0:tt, :].astype(BF16), win_ref[...],
                              preferred_element_type=F32)

    @pl.when(u == 0)
    def _reset_carries():
        st_ref[...] = jnp.zeros_like(st_ref)
        kcarry_ref[...] = jnp.zeros_like(kcarry_ref)
        vcarry_ref[:, 0:SW_HEAD_DIM, :] = jnp.zeros((SW_KV_HEADS, SW_HEAD_DIM, WINDOW), BF16)
        vcarry_ref[:, SW_HEAD_DIM:VT_ROWS, :] = jnp.ones((SW_KV_HEADS, BF16_SUBLANES, WINDOW), BF16)
        vt_ref[:, SW_HEAD_DIM:VT_ROWS, :] = jnp.ones(
            (SW_KV_HEADS, BF16_SUBLANES, STEP_BLOCKS * WINDOW), BF16)

    lbl = lbl_ref[...]
    lbl_e = jnp.exp(lbl - jnp.max(lbl, axis=0, keepdims=True))
    lb = lbl_e[0:1] / jnp.sum(lbl_e, axis=0, keepdims=True)
    nw = nw_ref[...] * (HG_HEAD_DIM ** 0.5)
    row = lax.broadcasted_iota(jnp.int32, (HG_CHUNK, HG_CHUNK), 0)
    col = lax.broadcasted_iota(jnp.int32, (HG_CHUNK, HG_CHUNK), 1)
    tril = col <= row
    cum_mat = tril.astype(BF16)
    cum3 = jnp.concatenate([cum_mat, cum_mat, cum_mat], axis=1)

    lane = lax.broadcasted_iota(jnp.int32, (1, LANES), 1)
    rope_first = (lane % SW_HEAD_DIM) < (ROPE_DIM // 2)
    lane_lo = lane < SW_HEAD_DIM
    key_idx = lax.broadcasted_iota(jnp.int32, (2 * WINDOW, 2 * WINDOW), 0)
    qry_idx = lax.broadcasted_iota(jnp.int32, (2 * WINDOW, 2 * WINDOW), 1) % WINDOW
    band = (key_idx > qry_idx) & (key_idx <= qry_idx + WINDOW)
    pair_lo = lax.broadcasted_iota(jnp.int32, (1, 2 * WINDOW), 1) < WINDOW
    scale = SW_HEAD_DIM ** -0.5 * LOG2_E
    first_valid = jnp.where(u > 0, 0, WINDOW)
    band_cap = jnp.where(band, jnp.inf, -jnp.inf)
    first_cap = jnp.where(band & (key_idx >= first_valid), jnp.inf, -jnp.inf)

    def rope(v, cos, sin):
        partner = jnp.where(rope_first,
                            pltpu.roll(v, LANES - ROPE_DIM // 2, 1),
                            pltpu.roll(v, ROPE_DIM // 2, 1))
        return v * cos + partner * sin

    def head_variants(v):
        sw = pltpu.roll(v, SW_HEAD_DIM, 1)
        zero = jnp.zeros_like(v)
        return (jnp.where(lane_lo, v, zero), jnp.where(lane_lo, zero, sw),
                jnp.where(lane_lo, sw, zero), jnp.where(lane_lo, zero, v))

    def in_projection_piece(k, piece):
        xb = xb_ref.at[k % 2]
        cols = slice(piece * PIECE, (piece + 1) * PIECE)
        h_bufs[k % 2][:, cols] = jnp.dot(xb[...], win_ref[:, cols], preferred_element_type=F32)

    def out_projection_piece(k, piece):
        cols = slice(piece * PIECE, (piece + 1) * PIECE)
        out_ref[:, cols] = jnp.dot(mix_ref[k], wout_ref[:, cols], preferred_element_type=F32)

    def post_norm(k):
        rows = slice(k * tt, (k + 1) * tt)
        z = xc_ref[0, rows, :] + out_ref[...]
        mu = jnp.mean(z, axis=-1, keepdims=True)
        zc = z - mu
        var = jnp.mean(zc * zc, axis=-1, keepdims=True)
        o_ref[0, rows, :] = ((zc * lax.rsqrt(var + LN_EPS / DN_ALPHA ** 2)) * lng_ref[...]
                             + lnb_ref[...])

    def make_fillers(k):
        stages = []
        if k > 0:
            for piece in range(N_OUT_PIECES):
                stages.append(lambda piece=piece: out_projection_piece(k - 1, piece))
            stages.append(lambda: post_norm(k - 1))
        for piece in range(N_IN_PIECES):
            def in_stage(piece=piece):
                in_projection_piece(k + 1, piece)
                projected[k + 1] = piece + 1
            stages.append(in_stage)
        return iter(stages)

    projected = {}

    def process_tile(k, head):
        h_ref = h_bufs[k % 2]
        base = k * tt
        mix = mix_ref.at[k]
        if k + 1 < TILES_PER_STEP:
            x_next_rows = xc_ref[0, (k + 1) * tt:(k + 2) * tt, :]
        else:
            x_next_rows = xn_ref[0]
        xb_ref[(k + 1) % 2] = x_next_rows.astype(BF16)
        fillers = make_fillers(k)

        def fill(n=1):
            for _ in range(n):
                stage = next(fillers, None)
                if stage is not None:
                    stage()

        def hg_pre(c, h=h_ref):
            rows = slice(c * HG_CHUNK, (c + 1) * HG_CHUNK)
            f = lb + (1.0 - lb) * _sigmoid(h[rows, OFF_HF:OFF_HF + HG_WIDTH])
            log_f = jnp.log2(f)
            hi = log_f.astype(BF16)
            rem = log_f - hi.astype(F32)
            mid = rem.astype(BF16)
            lo = (rem - mid.astype(F32)).astype(BF16)
            return 1.0 - f, jnp.concatenate([hi, mid, lo], axis=0)

        def hg_cum(parts):
            return jnp.dot(cum3, parts, preferred_element_type=F32)

        def hg_decays(c, k_in, g_cum, h=h_ref):
            rows = slice(c * HG_CHUNK, (c + 1) * HG_CHUNK)
            hq = h[rows, OFF_HQ:OFF_HQ + HG_WIDTH]
            g_last = g_cum[HG_CHUNK - 1:HG_CHUNK, :]
            q_dec = ((hq * _sigmoid(hq)) * jnp.exp2(g_cum)).astype(BF16)
            k_dec = (k_in * jnp.exp2(-g_cum)).astype(BF16)
            k_tail = (k_in * jnp.exp2(g_last - g_cum)).astype(BF16)
            return q_dec, k_dec, k_tail, jnp.exp2(g_last)

        def hg_scores(q_dec, k_dec, hd):
            sl = slice(hd * HG_HEAD_DIM, (hd + 1) * HG_HEAD_DIM)
            return lax.dot_general(q_dec[:, sl], k_dec[:, sl], NT_DIMS, preferred_element_type=F32)

        def hg_output(c, hd, a, q_dec, k_tail, decay):
            rows = slice(c * HG_CHUNK, (c + 1) * HG_CHUNK)
            sl = slice(hd * HG_HEAD_DIM, (hd + 1) * HG_HEAD_DIM)
            v = h_ref[rows, OFF_HI + hd * HG_HEAD_DIM:OFF_HI + (hd + 1) * HG_HEAD_DIM].astype(BF16)
            s_t = st_ref[hd]
            o = jnp.dot(jnp.where(tril, a, 0.0).astype(BF16), v, preferred_element_type=F32)
            o = o + lax.dot_general(q_dec[:, sl], s_t.astype(BF16), NT_DIMS,
                                    preferred_element_type=F32)
            inc_t = lax.dot_general(v, k_tail[:, sl], TN_DIMS, preferred_element_type=F32)
            st_ref[hd] = s_t * decay[:, sl] + inc_t
            o = o * lax.rsqrt(jnp.sum(o * o, axis=-1, keepdims=True) + HG_HEAD_DIM * RMS_EPS)
            o = o * nw[:, sl]
            hg = h_ref[rows, OFF_HG + hd * HG_HEAD_DIM:OFF_HG + (hd + 1) * HG_HEAD_DIM]
            mix[rows, sl] = (o * (hg * _sigmoid(hg))).astype(BF16)

        def sw_prepare(blk):
            r0 = blk * WINDOW
            rows = slice(r0, r0 + WINDOW)
            cur = slice(base + r0, base + r0 + WINDOW)
            cos = cos_ref[base + r0:base + r0 + WINDOW, :]
            sin = sin_ref[base + r0:base + r0 + WINDOW, :]
            k_rot = rope(h_ref[rows, OFF_AK:OFF_AK + SW_KV_WIDTH], cos, sin)
            for idx, kv in enumerate(head_variants(k_rot)):
                kwin_ref[idx, cur, :] = kv.astype(BF16)
            v_t = h_ref[rows, OFF_AV:OFF_AV + SW_KV_WIDTH].T
            for g in range(SW_KV_HEADS):
                vt_ref[g, 0:SW_HEAD_DIM, cur] = (
                    v_t[g * SW_HEAD_DIM:(g + 1) * SW_HEAD_DIM].astype(BF16))
            q_bf = []
            for tile in range(SW_WIDTH // LANES):
                q_rot = rope(h_ref[rows, OFF_AQ + tile * LANES:OFF_AQ + (tile + 1) * LANES], cos, sin)
                q_bf.append((q_rot * scale).astype(BF16))
            return [jnp.concatenate([q_bf[2 * g], q_bf[2 * g + 1]], axis=0)
                    for g in range(SW_KV_HEADS)]

        def key_window(blk, idx):
            r0 = base + blk * WINDOW
            if r0 == 0:
                return jnp.concatenate([kcarry_ref[idx], kwin_ref[idx, 0:WINDOW, :]], axis=0)
            return kwin_ref[idx, r0 - WINDOW:r0 + WINDOW, :]

        def value_window(blk, g):
            r0 = base + blk * WINDOW
            if r0 == 0:
                return jnp.concatenate([vcarry_ref[g], vt_ref[g, :, 0:WINDOW]], axis=1)
            return vt_ref[g, :, r0 - WINDOW:r0 + WINDOW]

        def sw_scores(blk, g, j, q_pair):
            return lax.dot_general(key_window(blk, 2 * g + j), q_pair, NT_DIMS,
                                   preferred_element_type=F32)

        def sw_values(blk, g, j, s_t, cap):
            sink_row = jnp.where(pair_lo, sinks_ref[SW_GROUP * g + j],
                                 sinks_ref[SW_GROUP * g + HEADS_PER_TILE + j]) * LOG2_E
            s_t = jnp.minimum(s_t, cap)
            m = jnp.maximum(jnp.max(s_t, axis=0, keepdims=True), sink_row)
            p_t = jnp.exp2(s_t - m).astype(BF16)
            acc = jnp.dot(value_window(blk, g), p_t, preferred_element_type=F32)
            denom = acc[SW_HEAD_DIM:SW_HEAD_DIM + 1, :] + jnp.exp2(sink_row - m)
            return acc[0:SW_HEAD_DIM, :] * (1.0 / denom)

        def sw_finish(blk, g, o_t):
            rows = slice(blk * WINDOW, (blk + 1) * WINDOW)
            for tl in range(2):
                tile = 2 * g + tl
                lanes = slice(tl * LANES, (tl + 1) * LANES)
                o_a = jnp.concatenate([o_t[0][:, lanes], o_t[1][:, lanes]], axis=0).T
                ag = h_ref[rows, OFF_AG + tile * LANES:OFF_AG + (tile + 1) * LANES]
                mix[rows, HG_WIDTH + tile * LANES:HG_WIDTH + (tile + 1) * LANES] = (
                    o_a * (ag * _sigmoid(ag))).astype(BF16)

        many = k > 0
        masks = [first_cap if (k == 0 and blk == 0) else band_cap for blk in range(N_BLOCKS)]
        chains = [(blk, g, j) for blk in range(N_BLOCKS) for g in range(SW_KV_HEADS)
                  for j in range(HEADS_PER_TILE)]
        q_pairs = {}
        scores = {}

        def sw_issue_scores(i):
            if i < len(chains) and i not in scores:
                blk, g, j = chains[i]
                if blk not in q_pairs:
                    q_pairs[blk] = sw_prepare(blk)
                scores[i] = sw_scores(blk, g, j, q_pairs[blk][g])

        head = head or {"pre": {}, "cum": {}, "dec": {}, "scores": {}}
        pre = dict(head["pre"])
        cum = dict(head["cum"])

        def hg_issue_cum(c):
            if c < N_CHUNKS and c not in cum:
                pre[c] = hg_pre(c)
                cum[c] = hg_cum(pre[c][1])

        fill(2 if many else 1)
        for c in range(min(2, N_CHUNKS)):
            hg_issue_cum(c)
            fill(1 if many else 0)
        for c in range(N_CHUNKS):
            if c in head["dec"]:
                q_dec, k_dec, k_tail, decay = head["dec"][c]
                hg_s = head["scores"][c]
            else:
                q_dec, k_dec, k_tail, decay = hg_decays(c, pre[c][0], cum[c])
                hg_s = [hg_scores(q_dec, k_dec, hd) for hd in range(HG_HEADS)]
            hg_issue_cum(c + 2)
            hg_issue_cum(c + 3)
            fill()
            for hd in range(HG_HEADS):
                hg_output(c, hd, hg_s[hd], q_dec, k_tail, decay)
            fill(1 if many else 0)

        for i in range(min(SW_SCORES_AHEAD, len(chains))):
            sw_issue_scores(i)
        fill()
        outs = {}
        has_next = k + 1 < TILES_PER_STEP
        next_head = {"pre": {}, "cum": {}, "dec": {}, "scores": {}} if has_next else None
        h_next = h_bufs[(k + 1) % 2]

        def head_start(i):
            if i != HEAD_AFTER_CHAIN:
                return
            assert projected.get(k + 1, 0) * PIECE >= OFF_HF + HG_WIDTH
            for c in range(HEAD_CUM_CHUNKS):
                next_head["pre"][c] = hg_pre(c, h_next)
                next_head["cum"][c] = hg_cum(next_head["pre"][c][1])
            for c in range(HEAD_DEC_CHUNKS):
                d = hg_decays(c, next_head["pre"][c][0], next_head["cum"][c], h_next)
                next_head["dec"][c] = d
                next_head["scores"][c] = [hg_scores(d[0], d[1], hd) for hd in range(HG_HEADS)]

        for i, (blk, g, j) in enumerate(chains):
            outs[(blk, g, j)] = sw_values(blk, g, j, scores.pop(i), masks[blk])
            sw_issue_scores(i + SW_SCORES_AHEAD)
            if j == HEADS_PER_TILE - 1:
                sw_finish(blk, g, [outs.pop((blk, g, jj)) for jj in range(HEADS_PER_TILE)])
            fill()
            if has_next:
                head_start(i)
        fill(N_OUT_PIECES + 1 + N_IN_PIECES)
        return next_head

    head = None
    for k in range(TILES_PER_STEP):
        head = process_tile(k, head)
    for piece in range(N_OUT_PIECES):
        out_projection_piece(TILES_PER_STEP - 1, piece)
    post_norm(TILES_PER_STEP - 1)

    last = slice((STEP_BLOCKS - 1) * WINDOW, STEP_BLOCKS * WINDOW)
    for idx in range(2 * SW_KV_HEADS):
        kcarry_ref[idx] = kwin_ref[idx, last, :]
    for g in range(SW_KV_HEADS):
        vcarry_ref[g, 0:SW_HEAD_DIM, :] = vt_ref[g, 0:SW_HEAD_DIM, last]


def _rope_tables(seq_len):
    pos = jnp.arange(seq_len, dtype=F32)
    inv_freq = ROPE_THETA ** (-jnp.arange(0, ROPE_DIM, 2, dtype=F32) / ROPE_DIM)
    ang = pos[:, None] * inv_freq[None, :]
    cos = jnp.cos(ang)
    sin = jnp.sin(ang)
    ones = jnp.ones((seq_len, SW_HEAD_DIM - ROPE_DIM), F32)
    cos_head = jnp.concatenate([cos, cos, ones], axis=1)
    sin_head = jnp.concatenate([-sin, sin, jnp.zeros_like(ones)], axis=1)
    return (jnp.tile(cos_head, (1, HEADS_PER_TILE)), jnp.tile(sin_head, (1, HEADS_PER_TILE)))


def kernel(x, w_in, lb_logits, hg_norm_w, sinks, w_out, ln_g, ln_b):
    batch, seq_len, d_model = x.shape
    assert d_model == D_MODEL and w_in.shape == (DEPTH, D_MODEL, IN_WIDTH)
    step_rows = TILES_PER_STEP * TIME_TILE
    assert seq_len % step_rows == 0
    cos_tab, sin_tab = _rope_tables(seq_len)
    n_steps = seq_len // step_rows
    last_step = batch * n_steps - 1

    def next_tile_index(b, u, *_):
        nxt = jnp.minimum(b * n_steps + u + 1, last_step)
        return (nxt // n_steps, TILES_PER_STEP * (nxt % n_steps), 0)

    const = lambda b, u, *_: (0, 0)
    grid_spec = pltpu.PrefetchScalarGridSpec(
        num_scalar_prefetch=1,
        grid=(batch, n_steps),
        in_specs=[
            pl.BlockSpec((1, step_rows, D_MODEL), lambda b, u, *_: (b, u, 0)),
            pl.BlockSpec((1, TIME_TILE, D_MODEL), next_tile_index),
            pl.BlockSpec(memory_space=pl.ANY),
            pl.BlockSpec(memory_space=pl.ANY),
            pl.BlockSpec((DEPTH + 1, HG_WIDTH), const),
            pl.BlockSpec((1, HG_WIDTH), const),
            pl.BlockSpec((1, D_MODEL), const),
            pl.BlockSpec((1, D_MODEL), const),
            pl.BlockSpec((step_rows, LANES), lambda b, u, *_: (u, 0)),
            pl.BlockSpec((step_rows, LANES), lambda b, u, *_: (u, 0)),
        ],
        out_specs=pl.BlockSpec((1, step_rows, D_MODEL), lambda b, u, *_: (b, u, 0)),
        scratch_shapes=[
            pltpu.VMEM((TIME_TILE, IN_WIDTH), F32),
            pltpu.VMEM((TIME_TILE, IN_WIDTH), F32),
            pltpu.VMEM((2, TIME_TILE, D_MODEL), BF16),
            pltpu.VMEM((TILES_PER_STEP, TIME_TILE, D_MODEL), BF16),
            pltpu.VMEM((TIME_TILE, D_MODEL), F32),
            pltpu.VMEM((HG_HEADS, HG_HEAD_DIM, HG_HEAD_DIM), F32),
            pltpu.VMEM((2 * SW_KV_HEADS, STEP_BLOCKS * WINDOW, LANES), BF16),
            pltpu.VMEM((SW_KV_HEADS, VT_ROWS, STEP_BLOCKS * WINDOW), BF16),
            pltpu.VMEM((2 * SW_KV_HEADS, WINDOW, LANES), BF16),
            pltpu.VMEM((SW_KV_HEADS, VT_ROWS, WINDOW), BF16),
            pltpu.VMEM((D_MODEL, IN_WIDTH), BF16),
            pltpu.VMEM((D_MODEL, D_MODEL), BF16),
            pltpu.VMEM((WEIGHT_SLOTS, WEIGHT_CHUNK_ROWS, IN_WIDTH), F32),
            pltpu.VMEM((WEIGHT_SLOTS, WEIGHT_CHUNK_ROWS, D_MODEL), F32),
            pltpu.SemaphoreType.DMA((WEIGHT_SLOTS,)),
            pltpu.SemaphoreType.DMA((WEIGHT_SLOTS,)),
        ],
    )
    return pl.pallas_call(
        _layer_kernel,
        grid_spec=grid_spec,
        out_shape=jax.ShapeDtypeStruct(x.shape, x.dtype),
        compiler_params=pltpu.CompilerParams(
            dimension_semantics=("arbitrary", "arbitrary"),
            vmem_limit_bytes=VMEM_LIMIT_BYTES),
        name="hybrid_hgrn2_swa_layer",
    )(sinks[0], x, x, w_in, w_out, lb_logits, hg_norm_w, ln_g, ln_b, cos_tab, sin_tab)
```

```python
import jax
import jax.numpy as jnp
from jax import lax
from jax.experimental import pallas as pl
from jax.experimental.pallas import tpu as pltpu

D_MODEL = 1024
DEPTH = 1
HG_WIDTH = 512
HG_HEAD_DIM = 128
HG_HEADS = HG_WIDTH // HG_HEAD_DIM
HG_CHUNK = 64
SW_WIDTH = 512
SW_HEAD_DIM = 64
SW_Q_HEADS = SW_WIDTH // SW_HEAD_DIM
SW_KV_HEADS = SW_Q_HEADS // 4
SW_KV_WIDTH = SW_KV_HEADS * SW_HEAD_DIM
SW_GROUP = SW_Q_HEADS // SW_KV_HEADS
WINDOW = 128
ROPE_THETA = 500000.0
ROPE_DIM = SW_HEAD_DIM // 4
DN_ALPHA = (2.0 * DEPTH) ** 0.25
LN_EPS = 1e-5
RMS_EPS = 1e-6
IN_WIDTH = 4 * HG_WIDTH + SW_WIDTH + 2 * SW_KV_WIDTH + SW_WIDTH

OFF_HQ = 0
OFF_HF = OFF_HQ + HG_WIDTH
OFF_HI = OFF_HF + HG_WIDTH
OFF_HG = OFF_HI + HG_WIDTH
OFF_AQ = OFF_HG + HG_WIDTH
OFF_AK = OFF_AQ + SW_WIDTH
OFF_AV = OFF_AK + SW_KV_WIDTH
OFF_AG = OFF_AV + SW_KV_WIDTH

LANES = 128
BF16_SUBLANES = 16
MXU_WIDTH = 256
HEADS_PER_TILE = LANES // SW_HEAD_DIM
assert HEADS_PER_TILE == 2 and SW_GROUP == 2 * HEADS_PER_TILE and SW_KV_WIDTH == LANES
VT_ROWS = SW_HEAD_DIM + BF16_SUBLANES
TIME_TILE = 256
TILES_PER_STEP = 4
PIECE = MXU_WIDTH
N_IN_PIECES = IN_WIDTH // PIECE
N_OUT_PIECES = D_MODEL // PIECE
assert IN_WIDTH % PIECE == 0 and D_MODEL % PIECE == 0
N_CHUNKS = TIME_TILE // HG_CHUNK
N_BLOCKS = TIME_TILE // WINDOW
STEP_BLOCKS = TILES_PER_STEP * N_BLOCKS
HEAD_CUM_CHUNKS = 3
HEAD_DEC_CHUNKS = 1
HEAD_AFTER_CHAIN = 3
SW_SCORES_AHEAD = 8
WEIGHT_CHUNK_ROWS = 64
WEIGHT_SLOTS = 4
VMEM_LIMIT_BYTES = 58 * 1024 * 1024

F32 = jnp.float32
BF16 = jnp.bfloat16
NT_DIMS = (((1,), (1,)), ((), ()))
TN_DIMS = (((0,), (0,)), ((), ()))


LOG2_E = 1.4426950408889634


def _sigmoid(v):
    return 1.0 / (1.0 + jnp.exp2(v * (-LOG2_E)))


def _load_weight_as_bf16(w_hbm, stage_ref, sem_ref, dst_ref, scale=None):
    n_rows = dst_ref.shape[0]
    n_slots, chunk = stage_ref.shape[0], stage_ref.shape[1]
    n_chunks = n_rows // chunk
    assert n_chunks * chunk == n_rows and n_slots >= 2

    def chunk_copy(i):
        return pltpu.make_async_copy(w_hbm.at[0, pl.ds(i * chunk, chunk), :],
                                     stage_ref.at[i % n_slots], sem_ref.at[i % n_slots])

    for i in range(min(n_slots - 1, n_chunks)):
        chunk_copy(i).start()
    for i in range(n_chunks):
        ahead = i + n_slots - 1
        if ahead < n_chunks:
            chunk_copy(ahead).start()
        chunk_copy(i).wait()
        rows = stage_ref[i % n_slots]
        if scale is not None:
            rows = rows * scale
        dst_ref[i * chunk:(i + 1) * chunk, :] = rows.astype(BF16)


def _layer_kernel(sinks_ref, xc_ref, xn_ref, win_hbm, wout_hbm, lbl_ref, nw_ref, lng_ref, lnb_ref,
                  cos_ref, sin_ref, o_ref,
                  ha_ref, hb_ref, xb_ref, mix_ref, out_ref, st_ref, kwin_ref, vt_ref,
                  kcarry_ref, vcarry_ref, win_ref, wout_ref, win_stage, wout_stage,
                  win_sems, wout_sems):
    b = pl.program_id(0)
    u = pl.program_id(1)
    tt = TIME_TILE
    h_bufs = (ha_ref, hb_ref)

    @pl.when(jnp.logical_and(b == 0, u == 0))
    def _first_step():
        _load_weight_as_bf16(win_hbm, win_stage, win_sems, win_ref)
        _load_weight_as_bf16(wout_hbm, wout_stage, wout_sems, wout_ref, scale=1.0 / DN_ALPHA)
        ha_ref[...] = jnp.dot(xc_ref[0, 0:tt, :].astype(BF16), win_ref[...],
                              preferred_element_type=F32)

    @pl.when(u == 0)
    def _reset_carries():
        st_ref[...] = jnp.zeros_like(st_ref)
        kcarry_ref[...] = jnp.zeros_like(kcarry_ref)
        vcarry_ref[:, 0:SW_HEAD_DIM, :] = jnp.zeros((SW_KV_HEADS, SW_HEAD_DIM, WINDOW), BF16)
        vcarry_ref[:, SW_HEAD_DIM:VT_ROWS, :] = jnp.ones((SW_KV_HEADS, BF16_SUBLANES, WINDOW), BF16)
        vt_ref[:, SW_HEAD_DIM:VT_ROWS, :] = jnp.ones(
            (SW_KV_HEADS, BF16_SUBLANES, STEP_BLOCKS * WINDOW), BF16)

    lbl = lbl_ref[...]
    lbl_e = jnp.exp(lbl - jnp.max(lbl, axis=0, keepdims=True))
    lb = lbl_e[0:1] / jnp.sum(lbl_e, axis=0, keepdims=True)
    nw = nw_ref[...] * (HG_HEAD_DIM ** 0.5)
    row = lax.broadcasted_iota(jnp.int32, (HG_CHUNK, HG_CHUNK), 0)
    col = lax.broadcasted_iota(jnp.int32, (HG_CHUNK, HG_CHUNK), 1)
    tril = col <= row
    cum_mat = tril.astype(BF16)
    cum3 = jnp.concatenate([cum_mat, cum_mat, cum_mat], axis=1)

    lane = lax.broadcasted_iota(jnp.int32, (1, LANES), 1)
    rope_first = (lane % SW_HEAD_DIM) < (ROPE_DIM // 2)
    lane_lo = lane < SW_HEAD_DIM
    key_idx = lax.broadcasted_iota(jnp.int32, (2 * WINDOW, 2 * WINDOW), 0)
    qry_idx = lax.broadcasted_iota(jnp.int32, (2 * WINDOW, 2 * WINDOW), 1) % WINDOW
    band = (key_idx > qry_idx) & (key_idx <= qry_idx + WINDOW)
    pair_lo = lax.broadcasted_iota(jnp.int32, (1, 2 * WINDOW), 1) < WINDOW
    scale = SW_HEAD_DIM ** -0.5 * LOG2_E
    first_valid = jnp.where(u > 0, 0, WINDOW)
    band_cap = jnp.where(band, jnp.inf, -jnp.inf)
    first_cap = jnp.where(band & (key_idx >= first_valid), jnp.inf, -jnp.inf)

    def rope(v, cos, sin):
        partner = jnp.where(rope_first,
                            pltpu.roll(v, LANES - ROPE_DIM // 2, 1),
                            pltpu.roll(v, ROPE_DIM // 2, 1))
        return v * cos + partner * sin

    def head_variants(v):
        sw = pltpu.roll(v, SW_HEAD_DIM, 1)
        zero = jnp.zeros_like(v)
        return (jnp.where(lane_lo, v, zero), jnp.where(lane_lo, zero, sw),
                jnp.where(lane_lo, sw, zero), jnp.where(lane_lo, zero, v))

    def in_projection_piece(k, piece):
        xb = xb_ref.at[k % 2]
        cols = slice(piece * PIECE, (piece + 1) * PIECE)
        h_bufs[k % 2][:, cols] = jnp.dot(xb[...], win_ref[:, cols], preferred_element_type=F32)

    def out_projection_piece(k, piece):
        cols = slice(piece * PIECE, (piece + 1) * PIECE)
        out_ref[:, cols] = jnp.dot(mix_ref[k], wout_ref[:, cols], preferred_element_type=F32)

    def post_norm(k):
        rows = slice(k * tt, (k + 1) * tt)
        z = xc_ref[0, rows, :] + out_ref[...]
        mu = jnp.mean(z, axis=-1, keepdims=True)
        zc = z - mu
        var = jnp.mean(zc * zc, axis=-1, keepdims=True)
        o_ref[0, rows, :] = ((zc * lax.rsqrt(var + LN_EPS / DN_ALPHA ** 2)) * lng_ref[...]
                             + lnb_ref[...])

    def make_fillers(k):
        stages = []
        if k > 0:
            for piece in range(N_OUT_PIECES):
                stages.append(lambda piece=piece: out_projection_piece(k - 1, piece))
            stages.append(lambda: post_norm(k - 1))
        for piece in range(N_IN_PIECES):
            def in_stage(piece=piece):
                in_projection_piece(k + 1, piece)
                projected[k + 1] = piece + 1
            stages.append(in_stage)
        return iter(stages)

    projected = {}

    def process_tile(k, head):
        h_ref = h_bufs[k % 2]
        base = k * tt
        mix = mix_ref.at[k]
        if k + 1 < TILES_PER_STEP:
            x_next_rows = xc_ref[0, (k + 1) * tt:(k + 2) * tt, :]
        else:
            x_next_rows = xn_ref[0]
        xb_ref[(k + 1) % 2] = x_next_rows.astype(BF16)
        fillers = make_fillers(k)

        def fill(n=1):
            for _ in range(n):
                stage = next(fillers, None)
                if stage is not None:
                    stage()

        def hg_pre(c, h=h_ref):
            rows = slice(c * HG_CHUNK, (c + 1) * HG_CHUNK)
            f = lb + (1.0 - lb) * _sigmoid(h[rows, OFF_HF:OFF_HF + HG_WIDTH])
            log_f = jnp.log2(f)
            hi = log_f.astype(BF16)
            rem = log_f - hi.astype(F32)
            mid = rem.astype(BF16)
            lo = (rem - mid.astype(F32)).astype(BF16)
            return 1.0 - f, jnp.concatenate([hi, mid, lo], axis=0)

        def hg_cum(parts):
            return jnp.dot(cum3, parts, preferred_element_type=F32)

        def hg_decays(c, k_in, g_cum, h=h_ref):
            rows = slice(c * HG_CHUNK, (c + 1) * HG_CHUNK)
            hq = h[rows, OFF_HQ:OFF_HQ + HG_WIDTH]
            g_last = g_cum[HG_CHUNK - 1:HG_CHUNK, :]
            q_dec = ((hq * _sigmoid(hq)) * jnp.exp2(g_cum)).astype(BF16)
            k_dec = (k_in * jnp.exp2(-g_cum)).astype(BF16)
            k_tail = (k_in * jnp.exp2(g_last - g_cum)).astype(BF16)
            return q_dec, k_dec, k_tail, jnp.exp2(g_last)

        def hg_scores(q_dec, k_dec, hd):
            sl = slice(hd * HG_HEAD_DIM, (hd + 1) * HG_HEAD_DIM)
            return lax.dot_general(q_dec[:, sl], k_dec[:, sl], NT_DIMS, preferred_element_type=F32)

        def hg_advance_state(c, hd, k_tail, decay):
            rows = slice(c * HG_CHUNK, (c + 1) * HG_CHUNK)
            sl = slice(hd * HG_HEAD_DIM, (hd + 1) * HG_HEAD_DIM)
            v = h_ref[rows, OFF_HI + hd * HG_HEAD_DIM:OFF_HI + (hd + 1) * HG_HEAD_DIM].astype(BF16)
            s_t = st_ref[hd]
            inc_t = lax.dot_general(v, k_tail[:, sl], TN_DIMS, preferred_element_type=F32)
            st_ref[hd] = s_t * decay[:, sl] + inc_t
            return v, s_t

        def hg_output(c, hd, a, q_dec, v, s_t):
            rows = slice(c * HG_CHUNK, (c + 1) * HG_CHUNK)
            sl = slice(hd * HG_HEAD_DIM, (hd + 1) * HG_HEAD_DIM)
            o = jnp.dot(jnp.where(tril, a, 0.0).astype(BF16), v, preferred_element_type=F32)
            o = o + lax.dot_general(q_dec[:, sl], s_t.astype(BF16), NT_DIMS,
                                    preferred_element_type=F32)
            o = o * lax.rsqrt(jnp.sum(o * o, axis=-1, keepdims=True) + HG_HEAD_DIM * RMS_EPS)
            o = o * nw[:, sl]
            hg = h_ref[rows, OFF_HG + hd * HG_HEAD_DIM:OFF_HG + (hd + 1) * HG_HEAD_DIM]
            mix[rows, sl] = (o * (hg * _sigmoid(hg))).astype(BF16)

        def sw_prepare(blk):
            r0 = blk * WINDOW
            rows = slice(r0, r0 + WINDOW)
            cur = slice(base + r0, base + r0 + WINDOW)
            cos = cos_ref[base + r0:base + r0 + WINDOW, :]
            sin = sin_ref[base + r0:base + r0 + WINDOW, :]
            k_rot = rope(h_ref[rows, OFF_AK:OFF_AK + SW_KV_WIDTH], cos, sin)
            for idx, kv in enumerate(head_variants(k_rot)):
                kwin_ref[idx, cur, :] = kv.astype(BF16)
            v_t = h_ref[rows, OFF_AV:OFF_AV + SW_KV_WIDTH].T
            for g in range(SW_KV_HEADS):
                vt_ref[g, 0:SW_HEAD_DIM, cur] = (
                    v_t[g * SW_HEAD_DIM:(g + 1) * SW_HEAD_DIM].astype(BF16))
            q_bf = []
            for tile in range(SW_WIDTH // LANES):
                q_rot = rope(h_ref[rows, OFF_AQ + tile * LANES:OFF_AQ + (tile + 1) * LANES], cos, sin)
                q_bf.append((q_rot * scale).astype(BF16))
            return [jnp.concatenate([q_bf[2 * g], q_bf[2 * g + 1]], axis=0)
                    for g in range(SW_KV_HEADS)]

        def key_window(blk, idx):
            r0 = base + blk * WINDOW
            if r0 == 0:
                return jnp.concatenate([kcarry_ref[idx], kwin_ref[idx, 0:WINDOW, :]], axis=0)
            return kwin_ref[idx, r0 - WINDOW:r0 + WINDOW, :]

        def value_window(blk, g):
            r0 = base + blk * WINDOW
            if r0 == 0:
                return jnp.concatenate([vcarry_ref[g], vt_ref[g, :, 0:WINDOW]], axis=1)
            return vt_ref[g, :, r0 - WINDOW:r0 + WINDOW]

        def sw_scores(blk, g, j, q_pair):
            return lax.dot_general(key_window(blk, 2 * g + j), q_pair, NT_DIMS,
                                   preferred_element_type=F32)

        def sw_values(blk, g, j, s_t, cap):
            sink_row = jnp.where(pair_lo, sinks_ref[SW_GROUP * g + j],
                                 sinks_ref[SW_GROUP * g + HEADS_PER_TILE + j]) * LOG2_E
            s_t = jnp.minimum(s_t, cap)
            m = jnp.maximum(jnp.max(s_t, axis=0, keepdims=True), sink_row)
            p_t = jnp.exp2(s_t - m).astype(BF16)
            acc = jnp.dot(value_window(blk, g), p_t, preferred_element_type=F32)
            denom = acc[SW_HEAD_DIM:SW_HEAD_DIM + 1, :] + jnp.exp2(sink_row - m)
            return acc[0:SW_HEAD_DIM, :] * (1.0 / denom)

        def sw_finish(blk, g, o_t):
            rows = slice(blk * WINDOW, (blk + 1) * WINDOW)
            for tl in range(2):
                tile = 2 * g + tl
                lanes = slice(tl * LANES, (tl + 1) * LANES)
                o_a = jnp.concatenate([o_t[0][:, lanes], o_t[1][:, lanes]], axis=0).T
                ag = h_ref[rows, OFF_AG + tile * LANES:OFF_AG + (tile + 1) * LANES]
                mix[rows, HG_WIDTH + tile * LANES:HG_WIDTH + (tile + 1) * LANES] = (
                    o_a * (ag * _sigmoid(ag))).astype(BF16)

        many = k > 0
        masks = [first_cap if (k == 0 and blk == 0) else band_cap for blk in range(N_BLOCKS)]
        chains = [(blk, g, j) for blk in range(N_BLOCKS) for g in range(SW_KV_HEADS)
                  for j in range(HEADS_PER_TILE)]
        q_pairs = {}
        scores = {}

        def sw_issue_scores(i):
            if i < len(chains) and i not in scores:
                blk, g, j = chains[i]
                if blk not in q_pairs:
                    q_pairs[blk] = sw_prepare(blk)
                scores[i] = sw_scores(blk, g, j, q_pairs[blk][g])

        head = head or {"pre": {}, "cum": {}, "dec": {}, "scores": {}}
        pre = dict(head["pre"])
        cum = dict(head["cum"])

        def hg_issue_cum(c):
            if c < N_CHUNKS and c not in cum:
                pre[c] = hg_pre(c)
                cum[c] = hg_cum(pre[c][1])

        fill(2 if many else 1)
        for c in range(min(2, N_CHUNKS)):
            hg_issue_cum(c)
            fill(1 if many else 0)
        for c in range(N_CHUNKS):
            if c in head["dec"]:
                q_dec, k_dec, k_tail, decay = head["dec"][c]
                hg_s = head["scores"][c]
            else:
                q_dec, k_dec, k_tail, decay = hg_decays(c, pre[c][0], cum[c])
                hg_s = [hg_scores(q_dec, k_dec, hd) for hd in range(HG_HEADS)]
            hg_issue_cum(c + 2)
            fill()
            advanced = [hg_advance_state(c, hd, k_tail, decay) for hd in range(HG_HEADS)]
            for hd in range(HG_HEADS):
                hg_output(c, hd, hg_s[hd], q_dec, *advanced[hd])
            fill(1 if many else 0)

        for i in range(min(SW_SCORES_AHEAD, len(chains))):
            sw_issue_scores(i)
        fill()
        outs = {}
        has_next = k + 1 < TILES_PER_STEP
        next_head = {"pre": {}, "cum": {}, "dec": {}, "scores": {}} if has_next else None
        h_next = h_bufs[(k + 1) % 2]

        def head_start(i):
            if i != HEAD_AFTER_CHAIN:
                return
            assert projected.get(k + 1, 0) * PIECE >= OFF_HF + HG_WIDTH
            for c in range(HEAD_CUM_CHUNKS):
                next_head["pre"][c] = hg_pre(c, h_next)
                next_head["cum"][c] = hg_cum(next_head["pre"][c][1])
            for c in range(HEAD_DEC_CHUNKS):
                d = hg_decays(c, next_head["pre"][c][0], next_head["cum"][c], h_next)
                next_head["dec"][c] = d
                next_head["scores"][c] = [hg_scores(d[0], d[1], hd) for hd in range(HG_HEADS)]

        for i, (blk, g, j) in enumerate(chains):
            outs[(blk, g, j)] = sw_values(blk, g, j, scores.pop(i), masks[blk])
            sw_issue_scores(i + SW_SCORES_AHEAD)
            if j == HEADS_PER_TILE - 1:
                sw_finish(blk, g, [outs.pop((blk, g, jj)) for jj in range(HEADS_PER_TILE)])
            fill()
            if has_next:
                head_start(i)
        fill(N_OUT_PIECES + 1 + N_IN_PIECES)
        return next_head

    head = None
    for k in range(TILES_PER_STEP):
        head = process_tile(k, head)
    for piece in range(N_OUT_PIECES):
        out_projection_piece(TILES_PER_STEP - 1, piece)
    post_norm(TILES_PER_STEP - 1)

    last = slice((STEP_BLOCKS - 1) * WINDOW, STEP_BLOCKS * WINDOW)
    for idx in range(2 * SW_KV_HEADS):
        kcarry_ref[idx] = kwin_ref[idx, last, :]
    for g in range(SW_KV_HEADS):
        vcarry_ref[g, 0:SW_HEAD_DIM, :] = vt_ref[g, 0:SW_HEAD_DIM, last]


def _rope_tables(seq_len):
    pos = jnp.arange(seq_len, dtype=F32)
    inv_freq = ROPE_THETA ** (-jnp.arange(0, ROPE_DIM, 2, dtype=F32) / ROPE_DIM)
    ang = pos[:, None] * inv_freq[None, :]
    cos = jnp.cos(ang)
    sin = jnp.sin(ang)
    ones = jnp.ones((seq_len, SW_HEAD_DIM - ROPE_DIM), F32)
    cos_head = jnp.concatenate([cos, cos, ones], axis=1)
    sin_head = jnp.concatenate([-sin, sin, jnp.zeros_like(ones)], axis=1)
    return (jnp.tile(cos_head, (1, HEADS_PER_TILE)), jnp.tile(sin_head, (1, HEADS_PER_TILE)))


def kernel(x, w_in, lb_logits, hg_norm_w, sinks, w_out, ln_g, ln_b):
    batch, seq_len, d_model = x.shape
    assert d_model == D_MODEL and w_in.shape == (DEPTH, D_MODEL, IN_WIDTH)
    step_rows = TILES_PER_STEP * TIME_TILE
    assert seq_len % step_rows == 0
    cos_tab, sin_tab = _rope_tables(seq_len)
    n_steps = seq_len // step_rows
    last_step = batch * n_steps - 1

    def next_tile_index(b, u, *_):
        nxt = jnp.minimum(b * n_steps + u + 1, last_step)
        return (nxt // n_steps, TILES_PER_STEP * (nxt % n_steps), 0)

    const = lambda b, u, *_: (0, 0)
    grid_spec = pltpu.PrefetchScalarGridSpec(
        num_scalar_prefetch=1,
        grid=(batch, n_steps),
        in_specs=[
            pl.BlockSpec((1, step_rows, D_MODEL), lambda b, u, *_: (b, u, 0)),
            pl.BlockSpec((1, TIME_TILE, D_MODEL), next_tile_index),
            pl.BlockSpec(memory_space=pl.ANY),
            pl.BlockSpec(memory_space=pl.ANY),
            pl.BlockSpec((DEPTH + 1, HG_WIDTH), const),
            pl.BlockSpec((1, HG_WIDTH), const),
            pl.BlockSpec((1, D_MODEL), const),
            pl.BlockSpec((1, D_MODEL), const),
            pl.BlockSpec((step_rows, LANES), lambda b, u, *_: (u, 0)),
            pl.BlockSpec((step_rows, LANES), lambda b, u, *_: (u, 0)),
        ],
        out_specs=pl.BlockSpec((1, step_rows, D_MODEL), lambda b, u, *_: (b, u, 0)),
        scratch_shapes=[
            pltpu.VMEM((TIME_TILE, IN_WIDTH), F32),
            pltpu.VMEM((TIME_TILE, IN_WIDTH), F32),
            pltpu.VMEM((2, TIME_TILE, D_MODEL), BF16),
            pltpu.VMEM((TILES_PER_STEP, TIME_TILE, D_MODEL), BF16),
            pltpu.VMEM((TIME_TILE, D_MODEL), F32),
            pltpu.VMEM((HG_HEADS, HG_HEAD_DIM, HG_HEAD_DIM), F32),
            pltpu.VMEM((2 * SW_KV_HEADS, STEP_BLOCKS * WINDOW, LANES), BF16),
            pltpu.VMEM((SW_KV_HEADS, VT_ROWS, STEP_BLOCKS * WINDOW), BF16),
            pltpu.VMEM((2 * SW_KV_HEADS, WINDOW, LANES), BF16),
            pltpu.VMEM((SW_KV_HEADS, VT_ROWS, WINDOW), BF16),
            pltpu.VMEM((D_MODEL, IN_WIDTH), BF16),
            pltpu.VMEM((D_MODEL, D_MODEL), BF16),
            pltpu.VMEM((WEIGHT_SLOTS, WEIGHT_CHUNK_ROWS, IN_WIDTH), F32),
            pltpu.VMEM((WEIGHT_SLOTS, WEIGHT_CHUNK_ROWS, D_MODEL), F32),
            pltpu.SemaphoreType.DMA((WEIGHT_SLOTS,)),
            pltpu.SemaphoreType.DMA((WEIGHT_SLOTS,)),
        ],
    )
    return pl.pallas_call(
        _layer_kernel,
        grid_spec=grid_spec,
        out_shape=jax.ShapeDtypeStruct(x.shape, x.dtype),
        compiler_params=pltpu.CompilerParams(
            dimension_semantics=("arbitrary", "arbitrary"),
            vmem_limit_bytes=VMEM_LIMIT_BYTES),
        name="hybrid_hgrn2_swa_layer",
    )(sinks[0], x, x, w_in, w_out, lb_logits, hg_norm_w, ln_g, ln_b, cos_tab, sin_tab)
```

```python
import jax
import jax.numpy as jnp
from jax import lax
from jax.experimental import pallas as pl
from jax.experimental.pallas import tpu as pltpu

D_MODEL = 1024
DEPTH = 1
HG_WIDTH = 512
HG_HEAD_DIM = 128
HG_HEADS = HG_WIDTH // HG_HEAD_DIM
HG_CHUNK = 64
SW_WIDTH = 512
SW_HEAD_DIM = 64
SW_Q_HEADS = SW_WIDTH // SW_HEAD_DIM
SW_KV_HEADS = SW_Q_HEADS // 4
SW_KV_WIDTH = SW_KV_HEADS * SW_HEAD_DIM
SW_GROUP = SW_Q_HEADS // SW_KV_HEADS
WINDOW = 128
ROPE_THETA = 500000.0
ROPE_DIM = SW_HEAD_DIM // 4
DN_ALPHA = (2.0 * DEPTH) ** 0.25
LN_EPS = 1e-5
RMS_EPS = 1e-6
IN_WIDTH = 4 * HG_WIDTH + SW_WIDTH + 2 * SW_KV_WIDTH + SW_WIDTH

OFF_HQ = 0
OFF_HF = OFF_HQ + HG_WIDTH
OFF_HI = OFF_HF + HG_WIDTH
OFF_HG = OFF_HI + HG_WIDTH
OFF_AQ = OFF_HG + HG_WIDTH
OFF_AK = OFF_AQ + SW_WIDTH
OFF_AV = OFF_AK + SW_KV_WIDTH
OFF_AG = OFF_AV + SW_KV_WIDTH

LANES = 128
BF16_SUBLANES = 16
MXU_WIDTH = 256
HEADS_PER_TILE = LANES // SW_HEAD_DIM
assert HEADS_PER_TILE == 2 and SW_GROUP == 2 * HEADS_PER_TILE and SW_KV_WIDTH == LANES
VT_ROWS = SW_HEAD_DIM + BF16_SUBLANES
TIME_TILE = 256
TILES_PER_STEP = 4
PIECE = MXU_WIDTH
N_IN_PIECES = IN_WIDTH // PIECE
N_OUT_PIECES = D_MODEL // PIECE
assert IN_WIDTH % PIECE == 0 and D_MODEL % PIECE == 0
N_CHUNKS = TIME_TILE // HG_CHUNK
N_BLOCKS = TIME_TILE // WINDOW
STEP_BLOCKS = TILES_PER_STEP * N_BLOCKS
HEAD_CUM_CHUNKS = 3
HEAD_DEC_CHUNKS = 1
HEAD_AFTER_CHAIN = 3
SW_SCORES_AHEAD = 8
WEIGHT_CHUNK_ROWS = 64
WEIGHT_SLOTS = 4
VMEM_LIMIT_BYTES = 58 * 1024 * 1024

F32 = jnp.float32
BF16 = jnp.bfloat16
NT_DIMS = (((1,), (1,)), ((), ()))
TN_DIMS = (((0,), (0,)), ((), ()))


LOG2_E = 1.4426950408889634


def _sigmoid(v):
    return 1.0 / (1.0 + jnp.exp2(v * (-LOG2_E)))


def _load_weight_as_bf16(w_hbm, stage_ref, sem_ref, dst_ref, scale=None):
    n_rows = dst_ref.shape[0]
    n_slots, chunk = stage_ref.shape[0], stage_ref.shape[1]
    n_chunks = n_rows // chunk
    assert n_chunks * chunk == n_rows and n_slots >= 2

    def chunk_copy(i):
        return pltpu.make_async_copy(w_hbm.at[0, pl.ds(i * chunk, chunk), :],
                                     stage_ref.at[i % n_slots], sem_ref.at[i % n_slots])

    for i in range(min(n_slots - 1, n_chunks)):
        chunk_copy(i).start()
    for i in range(n_chunks):
        ahead = i + n_slots - 1
        if ahead < n_chunks:
            chunk_copy(ahead).start()
        chunk_copy(i).wait()
        rows = stage_ref[i % n_slots]
        if scale is not None:
            rows = rows * scale
        dst_ref[i * chunk:(i + 1) * chunk, :] = rows.astype(BF16)


def _layer_kernel(sinks_ref, xc_ref, xn_ref, win_hbm, wout_hbm, lbl_ref, nw_ref, lng_ref, lnb_ref,
                  cos_ref, sin_ref, o_ref,
                  ha_ref, hb_ref, xb_ref, mix_ref, out_ref, st_ref, kwin_ref, vt_ref,
                  kcarry_ref, vcarry_ref, win_ref, wout_ref, win_stage, wout_stage,
                  win_sems, wout_sems):
    b = pl.program_id(0)
    u = pl.program_id(1)
    tt = TIME_TILE
    h_bufs = (ha_ref, hb_ref)

    @pl.when(jnp.logical_and(b == 0, u == 0))
    def _first_step():
        _load_weight_as_bf16(win_hbm, win_stage, win_sems, win_ref)
        _load_weight_as_bf16(wout_hbm, wout_stage, wout_sems, wout_ref, scale=1.0 / DN_ALPHA)
        ha_ref[...] = jnp.dot(xc_ref[0, 0:tt, :].astype(BF16), win_ref[...],
                              preferred_element_type=F32)

    @pl.when(u == 0)
    def _reset_carries():
        st_ref[...] = jnp.zeros_like(st_ref)
        kcarry_ref[...] = jnp.zeros_like(kcarry_ref)
        vcarry_ref[:, 0:SW_HEAD_DIM, :] = jnp.zeros((SW_KV_HEADS, SW_HEAD_DIM, WINDOW), BF16)
        vcarry_ref[:, SW_HEAD_DIM:VT_ROWS, :] = jnp.ones((SW_KV_HEADS, BF16_SUBLANES, WINDOW), BF16)
        vt_ref[:, SW_HEAD_DIM:VT_ROWS, :] = jnp.ones(
            (SW_KV_HEADS, BF16_SUBLANES, STEP_BLOCKS * WINDOW), BF16)

    lbl = lbl_ref[...]
    lbl_e = jnp.exp(lbl - jnp.max(lbl, axis=0, keepdims=True))
    lb = lbl_e[0:1] / jnp.sum(lbl_e, axis=0, keepdims=True)
    nw = nw_ref[...] * (HG_HEAD_DIM ** 0.5)
    row = lax.broadcasted_iota(jnp.int32, (HG_CHUNK, HG_CHUNK), 0)
    col = lax.broadcasted_iota(jnp.int32, (HG_CHUNK, HG_CHUNK), 1)
    tril = col <= row
    cum_mat = tril.astype(BF16)
    cum3 = jnp.concatenate([cum_mat, cum_mat, cum_mat], axis=1)

    lane = lax.broadcasted_iota(jnp.int32, (1, LANES), 1)
    rope_first = (lane % SW_HEAD_DIM) < (ROPE_DIM // 2)
    lane_lo = lane < SW_HEAD_DIM
    key_idx = lax.broadcasted_iota(jnp.int32, (2 * WINDOW, 2 * WINDOW), 0)
    qry_idx = lax.broadcasted_iota(jnp.int32, (2 * WINDOW, 2 * WINDOW), 1) % WINDOW
    band = (key_idx > qry_idx) & (key_idx <= qry_idx + WINDOW)
    pair_lo = lax.broadcasted_iota(jnp.int32, (1, 2 * WINDOW), 1) < WINDOW
    scale = SW_HEAD_DIM ** -0.5 * LOG2_E
    first_valid = jnp.where(u > 0, 0, WINDOW)
    band_cap = jnp.where(band, jnp.inf, -jnp.inf)
    first_cap = jnp.where(band & (key_idx >= first_valid), jnp.inf, -jnp.inf)

    def rope(v, cos, sin):
        partner = jnp.where(rope_first,
                            pltpu.roll(v, LANES - ROPE_DIM // 2, 1),
                            pltpu.roll(v, ROPE_DIM // 2, 1))
        return v * cos + partner * sin

    def head_variants(v):
        sw = pltpu.roll(v, SW_HEAD_DIM, 1)
        zero = jnp.zeros_like(v)
        return (jnp.where(lane_lo, v, zero), jnp.where(lane_lo, zero, sw),
                jnp.where(lane_lo, sw, zero), jnp.where(lane_lo, zero, v))

    def in_projection_piece(k, piece):
        xb = xb_ref.at[k % 2]
        cols = slice(piece * PIECE, (piece + 1) * PIECE)
        h_bufs[k % 2][:, cols] = jnp.dot(xb[...], win_ref[:, cols], preferred_element_type=F32)

    def out_projection_piece(k, piece):
        cols = slice(piece * PIECE, (piece + 1) * PIECE)
        out_ref[:, cols] = jnp.dot(mix_ref[k], wout_ref[:, cols], preferred_element_type=F32)

    def post_norm(k):
        rows = slice(k * tt, (k + 1) * tt)
        z = xc_ref[0, rows, :] + out_ref[...]
        mu = jnp.mean(z, axis=-1, keepdims=True)
        zc = z - mu
        var = jnp.mean(zc * zc, axis=-1, keepdims=True)
        o_ref[0, rows, :] = ((zc * lax.rsqrt(var + LN_EPS / DN_ALPHA ** 2)) * lng_ref[...]
                             + lnb_ref[...])

    def make_fillers(k):
        stages = []
        if k > 0:
            for piece in range(N_OUT_PIECES):
                stages.append(lambda piece=piece: out_projection_piece(k - 1, piece))
            stages.append(lambda: post_norm(k - 1))
        for piece in range(N_IN_PIECES):
            def in_stage(piece=piece):
                in_projection_piece(k + 1, piece)
                projected[k + 1] = piece + 1
            stages.append(in_stage)
        return iter(stages)

    projected = {}

    def process_tile(k, head):
        h_ref = h_bufs[k % 2]
        base = k * tt
        mix = mix_ref.at[k]
        if k + 1 < TILES_PER_STEP:
            x_next_rows = xc_ref[0, (k + 1) * tt:(k + 2) * tt, :]
        else:
            x_next_rows = xn_ref[0]
        xb_ref[(k + 1) % 2] = x_next_rows.astype(BF16)
        fillers = make_fillers(k)

        def fill(n=1):
            for _ in range(n):
                stage = next(fillers, None)
                if stage is not None:
                    stage()

        def hg_pre(c, h=h_ref):
            rows = slice(c * HG_CHUNK, (c + 1) * HG_CHUNK)
            f = lb + (1.0 - lb) * _sigmoid(h[rows, OFF_HF:OFF_HF + HG_WIDTH])
            log_f = jnp.log2(f)
            hi = log_f.astype(BF16)
            rem = log_f - hi.astype(F32)
            mid = rem.astype(BF16)
            lo = (rem - mid.astype(F32)).astype(BF16)
            return 1.0 - f, jnp.concatenate([hi, mid, lo], axis=0)

        def hg_cum(parts):
            return jnp.dot(cum3, parts, preferred_element_type=F32)

        def hg_decays(c, k_in, g_cum, h=h_ref):
            rows = slice(c * HG_CHUNK, (c + 1) * HG_CHUNK)
            hq = h[rows, OFF_HQ:OFF_HQ + HG_WIDTH]
            g_last = g_cum[HG_CHUNK - 1:HG_CHUNK, :]
            q_dec = ((hq * _sigmoid(hq)) * jnp.exp2(g_cum)).astype(BF16)
            k_dec = (k_in * jnp.exp2(-g_cum)).astype(BF16)
            k_tail = (k_in * jnp.exp2(g_last - g_cum)).astype(BF16)
            return q_dec, k_dec, k_tail, jnp.exp2(g_last)

        def hg_scores(q_dec, k_dec, hd):
            sl = slice(hd * HG_HEAD_DIM, (hd + 1) * HG_HEAD_DIM)
            return lax.dot_general(q_dec[:, sl], k_dec[:, sl], NT_DIMS, preferred_element_type=F32)

        def hg_output(c, hd, a, q_dec, k_tail, decay):
            rows = slice(c * HG_CHUNK, (c + 1) * HG_CHUNK)
            sl = slice(hd * HG_HEAD_DIM, (hd + 1) * HG_HEAD_DIM)
            v = h_ref[rows, OFF_HI + hd * HG_HEAD_DIM:OFF_HI + (hd + 1) * HG_HEAD_DIM].astype(BF16)
            s_t = st_ref[hd]
            o = lax.dot_general(q_dec[:, sl], s_t.astype(BF16), NT_DIMS, preferred_element_type=F32)
            o = o + jnp.dot(jnp.where(tril, a, 0.0).astype(BF16), v, preferred_element_type=F32)
            inc_t = lax.dot_general(v, k_tail[:, sl], TN_DIMS, preferred_element_type=F32)
            st_ref[hd] = s_t * decay[:, sl] + inc_t
            o = o * lax.rsqrt(jnp.sum(o * o, axis=-1, keepdims=True) + HG_HEAD_DIM * RMS_EPS)
            o = o * nw[:, sl]
            hg = h_ref[rows, OFF_HG + hd * HG_HEAD_DIM:OFF_HG + (hd + 1) * HG_HEAD_DIM]
            mix[rows, sl] = (o * (hg * _sigmoid(hg))).astype(BF16)

        def sw_prepare(blk):
            r0 = blk * WINDOW
            rows = slice(r0, r0 + WINDOW)
            cur = slice(base + r0, base + r0 + WINDOW)
            cos = cos_ref[base + r0:base + r0 + WINDOW, :]
            sin = sin_ref[base + r0:base + r0 + WINDOW, :]
            k_rot = rope(h_ref[rows, OFF_AK:OFF_AK + SW_KV_WIDTH], cos, sin)
            for idx, kv in enumerate(head_variants(k_rot)):
                kwin_ref[idx, cur, :] = kv.astype(BF16)
            v_t = h_ref[rows, OFF_AV:OFF_AV + SW_KV_WIDTH].T
            for g in range(SW_KV_HEADS):
                vt_ref[g, 0:SW_HEAD_DIM, cur] = (
                    v_t[g * SW_HEAD_DIM:(g + 1) * SW_HEAD_DIM].astype(BF16))
            q_bf = []
            for tile in range(SW_WIDTH // LANES):
                q_rot = rope(h_ref[rows, OFF_AQ + tile * LANES:OFF_AQ + (tile + 1) * LANES], cos, sin)
                q_bf.append((q_rot * scale).astype(BF16))
            return [jnp.concatenate([q_bf[2 * g], q_bf[2 * g + 1]], axis=0)
                    for g in range(SW_KV_HEADS)]

        def key_window(blk, idx):
            r0 = base + blk * WINDOW
            if r0 == 0:
                return jnp.concatenate([kcarry_ref[idx], kwin_ref[idx, 0:WINDOW, :]], axis=0)
            return kwin_ref[idx, r0 - WINDOW:r0 + WINDOW, :]

        def value_window(blk, g):
            r0 = base + blk * WINDOW
            if r0 == 0:
                return jnp.concatenate([vcarry_ref[g], vt_ref[g, :, 0:WINDOW]], axis=1)
            return vt_ref[g, :, r0 - WINDOW:r0 + WINDOW]

        def sw_scores(blk, g, j, q_pair):
            return lax.dot_general(key_window(blk, 2 * g + j), q_pair, NT_DIMS,
                                   preferred_element_type=F32)

        def sw_values(blk, g, j, s_t, cap):
            sink_row = jnp.where(pair_lo, sinks_ref[SW_GROUP * g + j],
                                 sinks_ref[SW_GROUP * g + HEADS_PER_TILE + j]) * LOG2_E
            s_t = jnp.minimum(s_t, cap)
            m = jnp.maximum(jnp.max(s_t, axis=0, keepdims=True), sink_row)
            p_t = jnp.exp2(s_t - m).astype(BF16)
            acc = jnp.dot(value_window(blk, g), p_t, preferred_element_type=F32)
            denom = acc[SW_HEAD_DIM:SW_HEAD_DIM + 1, :] + jnp.exp2(sink_row - m)
            return acc[0:SW_HEAD_DIM, :] * (1.0 / denom)

        def sw_finish(blk, g, o_t):
            rows = slice(blk * WINDOW, (blk + 1) * WINDOW)
            for tl in range(2):
                tile = 2 * g + tl
                lanes = slice(tl * LANES, (tl + 1) * LANES)
                o_a = jnp.concatenate([o_t[0][:, lanes], o_t[1][:, lanes]], axis=0).T
                ag = h_ref[rows, OFF_AG + tile * LANES:OFF_AG + (tile + 1) * LANES]
                mix[rows, HG_WIDTH + tile * LANES:HG_WIDTH + (tile + 1) * LANES] = (
                    o_a * (ag * _sigmoid(ag))).astype(BF16)

        many = k > 0
        masks = [first_cap if (k == 0 and blk == 0) else band_cap for blk in range(N_BLOCKS)]
        chains = [(blk, g, j) for blk in range(N_BLOCKS) for g in range(SW_KV_HEADS)
                  for j in range(HEADS_PER_TILE)]
        q_pairs = {}
        scores = {}

        def sw_issue_scores(i):
            if i < len(chains) and i not in scores:
                blk, g, j = chains[i]
                if blk not in q_pairs:
                    q_pairs[blk] = sw_prepare(blk)
                scores[i] = sw_scores(blk, g, j, q_pairs[blk][g])

        head = head or {"pre": {}, "cum": {}, "dec": {}, "scores": {}}
        pre = dict(head["pre"])
        cum = dict(head["cum"])

        def hg_issue_cum(c):
            if c < N_CHUNKS and c not in cum:
                pre[c] = hg_pre(c)
                cum[c] = hg_cum(pre[c][1])

        fill(2 if many else 1)
        for c in range(min(2, N_CHUNKS)):
            hg_issue_cum(c)
            fill(1 if many else 0)
        for c in range(N_CHUNKS):
            if c in head["dec"]:
                q_dec, k_dec, k_tail, decay = head["dec"][c]
                hg_s = head["scores"][c]
            else:
                q_dec, k_dec, k_tail, decay = hg_decays(c, pre[c][0], cum[c])
                hg_s = [hg_scores(q_dec, k_dec, hd) for hd in range(HG_HEADS)]
            hg_issue_cum(c + 2)
            fill()
            for hd in range(HG_HEADS):
                hg_output(c, hd, hg_s[hd], q_dec, k_tail, decay)
            fill(1 if many else 0)

        for i in range(min(SW_SCORES_AHEAD, len(chains))):
            sw_issue_scores(i)
        fill()
        outs = {}
        has_next = k + 1 < TILES_PER_STEP
        next_head = {"pre": {}, "cum": {}, "dec": {}, "scores": {}} if has_next else None
        h_next = h_bufs[(k + 1) % 2]

        def head_start(i):
            if i != HEAD_AFTER_CHAIN:
                return
            assert projected.get(k + 1, 0) * PIECE >= OFF_HF + HG_WIDTH
            for c in range(HEAD_CUM_CHUNKS):
                next_head["pre"][c] = hg_pre(c, h_next)
                next_head["cum"][c] = hg_cum(next_head["pre"][c][1])
            for c in range(HEAD_DEC_CHUNKS):
                d = hg_decays(c, next_head["pre"][c][0], next_head["cum"][c], h_next)
                next_head["dec"][c] = d
                next_head["scores"][c] = [hg_scores(d[0], d[1], hd) for hd in range(HG_HEADS)]

        for i, (blk, g, j) in enumerate(chains):
            outs[(blk, g, j)] = sw_values(blk, g, j, scores.pop(i), masks[blk])
            sw_issue_scores(i + SW_SCORES_AHEAD)
            if j == HEADS_PER_TILE - 1:
                sw_finish(blk, g, [outs.pop((blk, g, jj)) for jj in range(HEADS_PER_TILE)])
            fill()
            if has_next:
                head_start(i)
        fill(N_OUT_PIECES + 1 + N_IN_PIECES)
        return next_head

    head = None
    for k in range(TILES_PER_STEP):
        head = process_tile(k, head)
    for piece in range(N_OUT_PIECES):
        out_projection_piece(TILES_PER_STEP - 1, piece)
    post_norm(TILES_PER_STEP - 1)

    last = slice((STEP_BLOCKS - 1) * WINDOW, STEP_BLOCKS * WINDOW)
    for idx in range(2 * SW_KV_HEADS):
        kcarry_ref[idx] = kwin_ref[idx, last, :]
    for g in range(SW_KV_HEADS):
        vcarry_ref[g, 0:SW_HEAD_DIM, :] = vt_ref[g, 0:SW_HEAD_DIM, last]


def _rope_tables(seq_len):
    pos = jnp.arange(seq_len, dtype=F32)
    inv_freq = ROPE_THETA ** (-jnp.arange(0, ROPE_DIM, 2, dtype=F32) / ROPE_DIM)
    ang = pos[:, None] * inv_freq[None, :]
    cos = jnp.cos(ang)
    sin = jnp.sin(ang)
    ones = jnp.ones((seq_len, SW_HEAD_DIM - ROPE_DIM), F32)
    cos_head = jnp.concatenate([cos, cos, ones], axis=1)
    sin_head = jnp.concatenate([-sin, sin, jnp.zeros_like(ones)], axis=1)
    return (jnp.tile(cos_head, (1, HEADS_PER_TILE)), jnp.tile(sin_head, (1, HEADS_PER_TILE)))


def kernel(x, w_in, lb_logits, hg_norm_w, sinks, w_out, ln_g, ln_b):
    batch, seq_len, d_model = x.shape
    assert d_model == D_MODEL and w_in.shape == (DEPTH, D_MODEL, IN_WIDTH)
    step_rows = TILES_PER_STEP * TIME_TILE
    assert seq_len % step_rows == 0
    cos_tab, sin_tab = _rope_tables(seq_len)
    n_steps = seq_len // step_rows
    last_step = batch * n_steps - 1

    def next_tile_index(b, u, *_):
        nxt = jnp.minimum(b * n_steps + u + 1, last_step)
        return (nxt // n_steps, TILES_PER_STEP * (nxt % n_steps), 0)

    const = lambda b, u, *_: (0, 0)
    grid_spec = pltpu.PrefetchScalarGridSpec(
        num_scalar_prefetch=1,
        grid=(batch, n_steps),
        in_specs=[
            pl.BlockSpec((1, step_rows, D_MODEL), lambda b, u, *_: (b, u, 0)),
            pl.BlockSpec((1, TIME_TILE, D_MODEL), next_tile_index),
            pl.BlockSpec(memory_space=pl.ANY),
            pl.BlockSpec(memory_space=pl.ANY),
            pl.BlockSpec((DEPTH + 1, HG_WIDTH), const),
            pl.BlockSpec((1, HG_WIDTH), const),
            pl.BlockSpec((1, D_MODEL), const),
            pl.BlockSpec((1, D_MODEL), const),
            pl.BlockSpec((step_rows, LANES), lambda b, u, *_: (u, 0)),
            pl.BlockSpec((step_rows, LANES), lambda b, u, *_: (u, 0)),
        ],
        out_specs=pl.BlockSpec((1, step_rows, D_MODEL), lambda b, u, *_: (b, u, 0)),
        scratch_shapes=[
            pltpu.VMEM((TIME_TILE, IN_WIDTH), F32),
            pltpu.VMEM((TIME_TILE, IN_WIDTH), F32),
            pltpu.VMEM((2, TIME_TILE, D_MODEL), BF16),
            pltpu.VMEM((TILES_PER_STEP, TIME_TILE, D_MODEL), BF16),
            pltpu.VMEM((TIME_TILE, D_MODEL), F32),
            pltpu.VMEM((HG_HEADS, HG_HEAD_DIM, HG_HEAD_DIM), F32),
            pltpu.VMEM((2 * SW_KV_HEADS, STEP_BLOCKS * WINDOW, LANES), BF16),
            pltpu.VMEM((SW_KV_HEADS, VT_ROWS, STEP_BLOCKS * WINDOW), BF16),
            pltpu.VMEM((2 * SW_KV_HEADS, WINDOW, LANES), BF16),
            pltpu.VMEM((SW_KV_HEADS, VT_ROWS, WINDOW), BF16),
            pltpu.VMEM((D_MODEL, IN_WIDTH), BF16),
            pltpu.VMEM((D_MODEL, D_MODEL), BF16),
            pltpu.VMEM((WEIGHT_SLOTS, WEIGHT_CHUNK_ROWS, IN_WIDTH), F32),
            pltpu.VMEM((WEIGHT_SLOTS, WEIGHT_CHUNK_ROWS, D_MODEL), F32),
            pltpu.SemaphoreType.DMA((WEIGHT_SLOTS,)),
            pltpu.SemaphoreType.DMA((WEIGHT_SLOTS,)),
        ],
    )
    return pl.pallas_call(
        _layer_kernel,
        grid_spec=grid_spec,
        out_shape=jax.ShapeDtypeStruct(x.shape, x.dtype),
        compiler_params=pltpu.CompilerParams(
            dimension_semantics=("arbitrary", "arbitrary"),
            vmem_limit_bytes=VMEM_LIMIT_BYTES),
        name="hybrid_hgrn2_swa_layer",
    )(sinks[0], x, x, w_in, w_out, lb_logits, hg_norm_w, ln_g, ln_b, cos_tab, sin_tab)
```

```python
import jax
import jax.numpy as jnp
from jax import lax
from jax.experimental import pallas as pl
from jax.experimental.pallas import tpu as pltpu

D_MODEL = 1024
DEPTH = 1
HG_WIDTH = 512
HG_HEAD_DIM = 128
HG_HEADS = HG_WIDTH // HG_HEAD_DIM
HG_CHUNK = 64
SW_WIDTH = 512
SW_HEAD_DIM = 64
SW_Q_HEADS = SW_WIDTH // SW_HEAD_DIM
SW_KV_HEADS = SW_Q_HEADS // 4
SW_KV_WIDTH = SW_KV_HEADS * SW_HEAD_DIM
SW_GROUP = SW_Q_HEADS // SW_KV_HEADS
WINDOW = 128
ROPE_THETA = 500000.0
ROPE_DIM = SW_HEAD_DIM // 4
DN_ALPHA = (2.0 * DEPTH) ** 0.25
LN_EPS = 1e-5
RMS_EPS = 1e-6
IN_WIDTH = 4 * HG_WIDTH + SW_WIDTH + 2 * SW_KV_WIDTH + SW_WIDTH

OFF_HQ = 0
OFF_HF = OFF_HQ + HG_WIDTH
OFF_HI = OFF_HF + HG_WIDTH
OFF_HG = OFF_HI + HG_WIDTH
OFF_AQ = OFF_HG + HG_WIDTH
OFF_AK = OFF_AQ + SW_WIDTH
OFF_AV = OFF_AK + SW_KV_WIDTH
OFF_AG = OFF_AV + SW_KV_WIDTH

LANES = 128
BF16_SUBLANES = 16
MXU_WIDTH = 256
HEADS_PER_TILE = LANES // SW_HEAD_DIM
assert HEADS_PER_TILE == 2 and SW_GROUP == 2 * HEADS_PER_TILE and SW_KV_WIDTH == LANES
VT_ROWS = SW_HEAD_DIM + BF16_SUBLANES
TIME_TILE = 256
TILES_PER_STEP = 4
PIECE = MXU_WIDTH
N_IN_PIECES = IN_WIDTH // PIECE
N_OUT_PIECES = D_MODEL // PIECE
assert IN_WIDTH % PIECE == 0 and D_MODEL % PIECE == 0
N_CHUNKS = TIME_TILE // HG_CHUNK
N_BLOCKS = TIME_TILE // WINDOW
STEP_BLOCKS = TILES_PER_STEP * N_BLOCKS
OUT_AFTER_IN_PIECES = 4
HEAD_CUM_CHUNKS = 3
HEAD_DEC_CHUNKS = 1
HEAD_AFTER_CHAIN = 3
SW_SCORES_AHEAD = 8
WEIGHT_CHUNK_ROWS = 64
WEIGHT_SLOTS = 4
VMEM_LIMIT_BYTES = 58 * 1024 * 1024

F32 = jnp.float32
BF16 = jnp.bfloat16
NT_DIMS = (((1,), (1,)), ((), ()))
TN_DIMS = (((0,), (0,)), ((), ()))


LOG2_E = 1.4426950408889634


def _sigmoid(v):
    return 1.0 / (1.0 + jnp.exp2(v * (-LOG2_E)))


def _load_weight_as_bf16(w_hbm, stage_ref, sem_ref, dst_ref, scale=None):
    n_rows = dst_ref.shape[0]
    n_slots, chunk = stage_ref.shape[0], stage_ref.shape[1]
    n_chunks = n_rows // chunk
    assert n_chunks * chunk == n_rows and n_slots >= 2

    def chunk_copy(i):
        return pltpu.make_async_copy(w_hbm.at[0, pl.ds(i * chunk, chunk), :],
                                     stage_ref.at[i % n_slots], sem_ref.at[i % n_slots])

    for i in range(min(n_slots - 1, n_chunks)):
        chunk_copy(i).start()
    for i in range(n_chunks):
        ahead = i + n_slots - 1
        if ahead < n_chunks:
            chunk_copy(ahead).start()
        chunk_copy(i).wait()
        rows = stage_ref[i % n_slots]
        if scale is not None:
            rows = rows * scale
        dst_ref[i * chunk:(i + 1) * chunk, :] = rows.astype(BF16)


def _layer_kernel(sinks_ref, xc_ref, xn_ref, win_hbm, wout_hbm, lbl_ref, nw_ref, lng_ref, lnb_ref,
                  cos_ref, sin_ref, o_ref,
                  ha_ref, hb_ref, xb_ref, mix_ref, out_ref, st_ref, kwin_ref, vt_ref,
                  kcarry_ref, vcarry_ref, win_ref, wout_ref, win_stage, wout_stage,
                  win_sems, wout_sems):
    b = pl.program_id(0)
    u = pl.program_id(1)
    tt = TIME_TILE
    h_bufs = (ha_ref, hb_ref)

    @pl.when(jnp.logical_and(b == 0, u == 0))
    def _first_step():
        _load_weight_as_bf16(win_hbm, win_stage, win_sems, win_ref)
        _load_weight_as_bf16(wout_hbm, wout_stage, wout_sems, wout_ref, scale=1.0 / DN_ALPHA)
        ha_ref[...] = jnp.dot(xc_ref[0, 0:tt, :].astype(BF16), win_ref[...],
                              preferred_element_type=F32)

    @pl.when(u == 0)
    def _reset_carries():
        st_ref[...] = jnp.zeros_like(st_ref)
        kcarry_ref[...] = jnp.zeros_like(kcarry_ref)
        vcarry_ref[:, 0:SW_HEAD_DIM, :] = jnp.zeros((SW_KV_HEADS, SW_HEAD_DIM, WINDOW), BF16)
        vcarry_ref[:, SW_HEAD_DIM:VT_ROWS, :] = jnp.ones((SW_KV_HEADS, BF16_SUBLANES, WINDOW), BF16)
        vt_ref[:, SW_HEAD_DIM:VT_ROWS, :] = jnp.ones(
            (SW_KV_HEADS, BF16_SUBLANES, STEP_BLOCKS * WINDOW), BF16)

    lbl = lbl_ref[...]
    lbl_e = jnp.exp(lbl - jnp.max(lbl, axis=0, keepdims=True))
    lb = lbl_e[0:1] / jnp.sum(lbl_e, axis=0, keepdims=True)
    nw = nw_ref[...] * (HG_HEAD_DIM ** 0.5)
    row = lax.broadcasted_iota(jnp.int32, (HG_CHUNK, HG_CHUNK), 0)
    col = lax.broadcasted_iota(jnp.int32, (HG_CHUNK, HG_CHUNK), 1)
    tril = col <= row
    cum_mat = tril.astype(BF16)
    cum3 = jnp.concatenate([cum_mat, cum_mat, cum_mat], axis=1)

    lane = lax.broadcasted_iota(jnp.int32, (1, LANES), 1)
    rope_first = (lane % SW_HEAD_DIM) < (ROPE_DIM // 2)
    lane_lo = lane < SW_HEAD_DIM
    key_idx = lax.broadcasted_iota(jnp.int32, (2 * WINDOW, 2 * WINDOW), 0)
    qry_idx = lax.broadcasted_iota(jnp.int32, (2 * WINDOW, 2 * WINDOW), 1) % WINDOW
    band = (key_idx > qry_idx) & (key_idx <= qry_idx + WINDOW)
    pair_lo = lax.broadcasted_iota(jnp.int32, (1, 2 * WINDOW), 1) < WINDOW
    scale = SW_HEAD_DIM ** -0.5 * LOG2_E
    first_valid = jnp.where(u > 0, 0, WINDOW)
    band_cap = jnp.where(band, jnp.inf, -jnp.inf)
    first_cap = jnp.where(band & (key_idx >= first_valid), jnp.inf, -jnp.inf)

    def rope(v, cos, sin):
        partner = jnp.where(rope_first,
                            pltpu.roll(v, LANES - ROPE_DIM // 2, 1),
                            pltpu.roll(v, ROPE_DIM // 2, 1))
        return v * cos + partner * sin

    def head_variants(v):
        sw = pltpu.roll(v, SW_HEAD_DIM, 1)
        zero = jnp.zeros_like(v)
        return (jnp.where(lane_lo, v, zero), jnp.where(lane_lo, zero, sw),
                jnp.where(lane_lo, sw, zero), jnp.where(lane_lo, zero, v))

    def in_projection_piece(k, piece):
        xb = xb_ref.at[k % 2]
        cols = slice(piece * PIECE, (piece + 1) * PIECE)
        h_bufs[k % 2][:, cols] = jnp.dot(xb[...], win_ref[:, cols], preferred_element_type=F32)

    def out_projection_piece(k, piece):
        cols = slice(piece * PIECE, (piece + 1) * PIECE)
        out_ref[:, cols] = jnp.dot(mix_ref[k], wout_ref[:, cols], preferred_element_type=F32)

    def post_norm(k):
        rows = slice(k * tt, (k + 1) * tt)
        z = xc_ref[0, rows, :] + out_ref[...]
        mu = jnp.mean(z, axis=-1, keepdims=True)
        zc = z - mu
        var = jnp.mean(zc * zc, axis=-1, keepdims=True)
        o_ref[0, rows, :] = ((zc * lax.rsqrt(var + LN_EPS / DN_ALPHA ** 2)) * lng_ref[...]
                             + lnb_ref[...])

    def make_fillers(k):
        ins = []
        for piece in range(N_IN_PIECES):
            def in_stage(piece=piece):
                in_projection_piece(k + 1, piece)
                projected[k + 1] = piece + 1
            ins.append(in_stage)
        outs_ = []
        if k > 0:
            for piece in range(N_OUT_PIECES):
                outs_.append(lambda piece=piece: out_projection_piece(k - 1, piece))
            outs_.append(lambda: post_norm(k - 1))
        return iter(ins[:OUT_AFTER_IN_PIECES] + outs_ + ins[OUT_AFTER_IN_PIECES:])

    projected = {}

    def process_tile(k, head):
        h_ref = h_bufs[k % 2]
        base = k * tt
        mix = mix_ref.at[k]
        if k + 1 < TILES_PER_STEP:
            x_next_rows = xc_ref[0, (k + 1) * tt:(k + 2) * tt, :]
        else:
            x_next_rows = xn_ref[0]
        xb_ref[(k + 1) % 2] = x_next_rows.astype(BF16)
        fillers = make_fillers(k)

        def fill(n=1):
            for _ in range(n):
                stage = next(fillers, None)
                if stage is not None:
                    stage()

        def hg_pre(c, h=h_ref):
            rows = slice(c * HG_CHUNK, (c + 1) * HG_CHUNK)
            f = lb + (1.0 - lb) * _sigmoid(h[rows, OFF_HF:OFF_HF + HG_WIDTH])
            log_f = jnp.log2(f)
            hi = log_f.astype(BF16)
            rem = log_f - hi.astype(F32)
            mid = rem.astype(BF16)
            lo = (rem - mid.astype(F32)).astype(BF16)
            return 1.0 - f, jnp.concatenate([hi, mid, lo], axis=0)

        def hg_cum(parts):
            return jnp.dot(cum3, parts, preferred_element_type=F32)

        def hg_decays(c, k_in, g_cum, h=h_ref):
            rows = slice(c * HG_CHUNK, (c + 1) * HG_CHUNK)
            hq = h[rows, OFF_HQ:OFF_HQ + HG_WIDTH]
            g_last = g_cum[HG_CHUNK - 1:HG_CHUNK, :]
            q_dec = ((hq * _sigmoid(hq)) * jnp.exp2(g_cum)).astype(BF16)
            k_dec = (k_in * jnp.exp2(-g_cum)).astype(BF16)
            k_tail = (k_in * jnp.exp2(g_last - g_cum)).astype(BF16)
            return q_dec, k_dec, k_tail, jnp.exp2(g_last)

        def hg_scores(q_dec, k_dec, hd):
            sl = slice(hd * HG_HEAD_DIM, (hd + 1) * HG_HEAD_DIM)
            return lax.dot_general(q_dec[:, sl], k_dec[:, sl], NT_DIMS, preferred_element_type=F32)

        def hg_output(c, hd, a, q_dec, k_tail, decay):
            rows = slice(c * HG_CHUNK, (c + 1) * HG_CHUNK)
            sl = slice(hd * HG_HEAD_DIM, (hd + 1) * HG_HEAD_DIM)
            v = h_ref[rows, OFF_HI + hd * HG_HEAD_DIM:OFF_HI + (hd + 1) * HG_HEAD_DIM].astype(BF16)
            s_t = st_ref[hd]
            o = jnp.dot(jnp.where(tril, a, 0.0).astype(BF16), v, preferred_element_type=F32)
            o = o + lax.dot_general(q_dec[:, sl], s_t.astype(BF16), NT_DIMS,
                                    preferred_element_type=F32)
            inc_t = lax.dot_general(v, k_tail[:, sl], TN_DIMS, preferred_element_type=F32)
            st_ref[hd] = s_t * decay[:, sl] + inc_t
            o = o * lax.rsqrt(jnp.sum(o * o, axis=-1, keepdims=True) + HG_HEAD_DIM * RMS_EPS)
            o = o * nw[:, sl]
            hg = h_ref[rows, OFF_HG + hd * HG_HEAD_DIM:OFF_HG + (hd + 1) * HG_HEAD_DIM]
            mix[rows, sl] = (o * (hg * _sigmoid(hg))).astype(BF16)

        def sw_prepare(blk):
            r0 = blk * WINDOW
            rows = slice(r0, r0 + WINDOW)
            cur = slice(base + r0, base + r0 + WINDOW)
            cos = cos_ref[base + r0:base + r0 + WINDOW, :]
            sin = sin_ref[base + r0:base + r0 + WINDOW, :]
            k_rot = rope(h_ref[rows, OFF_AK:OFF_AK + SW_KV_WIDTH], cos, sin)
            for idx, kv in enumerate(head_variants(k_rot)):
                kwin_ref[idx, cur, :] = kv.astype(BF16)
            v_t = h_ref[rows, OFF_AV:OFF_AV + SW_KV_WIDTH].T
            for g in range(SW_KV_HEADS):
                vt_ref[g, 0:SW_HEAD_DIM, cur] = (
                    v_t[g * SW_HEAD_DIM:(g + 1) * SW_HEAD_DIM].astype(BF16))
            q_bf = []
            for tile in range(SW_WIDTH // LANES):
                q_rot = rope(h_ref[rows, OFF_AQ + tile * LANES:OFF_AQ + (tile + 1) * LANES], cos, sin)
                q_bf.append((q_rot * scale).astype(BF16))
            return [jnp.concatenate([q_bf[2 * g], q_bf[2 * g + 1]], axis=0)
                    for g in range(SW_KV_HEADS)]

        def key_window(blk, idx):
            r0 = base + blk * WINDOW
            if r0 == 0:
                return jnp.concatenate([kcarry_ref[idx], kwin_ref[idx, 0:WINDOW, :]], axis=0)
            return kwin_ref[idx, r0 - WINDOW:r0 + WINDOW, :]

        def value_window(blk, g):
            r0 = base + blk * WINDOW
            if r0 == 0:
                return jnp.concatenate([vcarry_ref[g], vt_ref[g, :, 0:WINDOW]], axis=1)
            return vt_ref[g, :, r0 - WINDOW:r0 + WINDOW]

        def sw_scores(blk, g, j, q_pair):
            return lax.dot_general(key_window(blk, 2 * g + j), q_pair, NT_DIMS,
                                   preferred_element_type=F32)

        def sw_values(blk, g, j, s_t, cap):
            sink_row = jnp.where(pair_lo, sinks_ref[SW_GROUP * g + j],
                                 sinks_ref[SW_GROUP * g + HEADS_PER_TILE + j]) * LOG2_E
            s_t = jnp.minimum(s_t, cap)
            m = jnp.maximum(jnp.max(s_t, axis=0, keepdims=True), sink_row)
            p_t = jnp.exp2(s_t - m).astype(BF16)
            acc = jnp.dot(value_window(blk, g), p_t, preferred_element_type=F32)
            denom = acc[SW_HEAD_DIM:SW_HEAD_DIM + 1, :] + jnp.exp2(sink_row - m)
            return acc[0:SW_HEAD_DIM, :] * (1.0 / denom)

        def sw_finish(blk, g, o_t):
            rows = slice(blk * WINDOW, (blk + 1) * WINDOW)
            for tl in range(2):
                tile = 2 * g + tl
                lanes = slice(tl * LANES, (tl + 1) * LANES)
                o_a = jnp.concatenate([o_t[0][:, lanes], o_t[1][:, lanes]], axis=0).T
                ag = h_ref[rows, OFF_AG + tile * LANES:OFF_AG + (tile + 1) * LANES]
                mix[rows, HG_WIDTH + tile * LANES:HG_WIDTH + (tile + 1) * LANES] = (
                    o_a * (ag * _sigmoid(ag))).astype(BF16)

        many = k > 0
        masks = [first_cap if (k == 0 and blk == 0) else band_cap for blk in range(N_BLOCKS)]
        chains = [(blk, g, j) for blk in range(N_BLOCKS) for g in range(SW_KV_HEADS)
                  for j in range(HEADS_PER_TILE)]
        q_pairs = {}
        scores = {}

        def sw_issue_scores(i):
            if i < len(chains) and i not in scores:
                blk, g, j = chains[i]
                if blk not in q_pairs:
                    q_pairs[blk] = sw_prepare(blk)
                scores[i] = sw_scores(blk, g, j, q_pairs[blk][g])

        head = head or {"pre": {}, "cum": {}, "dec": {}, "scores": {}}
        pre = dict(head["pre"])
        cum = dict(head["cum"])

        def hg_issue_cum(c):
            if c < N_CHUNKS and c not in cum:
                pre[c] = hg_pre(c)
                cum[c] = hg_cum(pre[c][1])

        fill(2 if many else 1)
        for c in range(min(2, N_CHUNKS)):
            hg_issue_cum(c)
            fill(1 if many else 0)
        for c in range(N_CHUNKS):
            if c in head["dec"]:
                q_dec, k_dec, k_tail, decay = head["dec"][c]
                hg_s = head["scores"][c]
            else:
                q_dec, k_dec, k_tail, decay = hg_decays(c, pre[c][0], cum[c])
                hg_s = [hg_scores(q_dec, k_dec, hd) for hd in range(HG_HEADS)]
            hg_issue_cum(c + 2)
            fill()
            for hd in range(HG_HEADS):
                hg_output(c, hd, hg_s[hd], q_dec, k_tail, decay)
            fill(1 if many else 0)

        for i in range(min(SW_SCORES_AHEAD, len(chains))):
            sw_issue_scores(i)
        fill()
        outs = {}
        has_next = k + 1 < TILES_PER_STEP
        next_head = {"pre": {}, "cum": {}, "dec": {}, "scores": {}} if has_next else None
        h_next = h_bufs[(k + 1) % 2]

        def head_start(i):
            if i != HEAD_AFTER_CHAIN:
                return
            assert projected.get(k + 1, 0) * PIECE >= OFF_HF + HG_WIDTH
            for c in range(HEAD_CUM_CHUNKS):
                next_head["pre"][c] = hg_pre(c, h_next)
                next_head["cum"][c] = hg_cum(next_head["pre"][c][1])
            for c in range(HEAD_DEC_CHUNKS):
                d = hg_decays(c, next_head["pre"][c][0], next_head["cum"][c], h_next)
                next_head["dec"][c] = d
                next_head["scores"][c] = [hg_scores(d[0], d[1], hd) for hd in range(HG_HEADS)]

        for i, (blk, g, j) in enumerate(chains):
            outs[(blk, g, j)] = sw_values(blk, g, j, scores.pop(i), masks[blk])
            sw_issue_scores(i + SW_SCORES_AHEAD)
            if j == HEADS_PER_TILE - 1:
                sw_finish(blk, g, [outs.pop((blk, g, jj)) for jj in range(HEADS_PER_TILE)])
            fill()
            if has_next:
                head_start(i)
        fill(N_OUT_PIECES + 1 + N_IN_PIECES)
        return next_head

    head = None
    for k in range(TILES_PER_STEP):
        head = process_tile(k, head)
    for piece in range(N_OUT_PIECES):
        out_projection_piece(TILES_PER_STEP - 1, piece)
    post_norm(TILES_PER_STEP - 1)

    last = slice((STEP_BLOCKS - 1) * WINDOW, STEP_BLOCKS * WINDOW)
    for idx in range(2 * SW_KV_HEADS):
        kcarry_ref[idx] = kwin_ref[idx, last, :]
    for g in range(SW_KV_HEADS):
        vcarry_ref[g, 0:SW_HEAD_DIM, :] = vt_ref[g, 0:SW_HEAD_DIM, last]


def _rope_tables(seq_len):
    pos = jnp.arange(seq_len, dtype=F32)
    inv_freq = ROPE_THETA ** (-jnp.arange(0, ROPE_DIM, 2, dtype=F32) / ROPE_DIM)
    ang = pos[:, None] * inv_freq[None, :]
    cos = jnp.cos(ang)
    sin = jnp.sin(ang)
    ones = jnp.ones((seq_len, SW_HEAD_DIM - ROPE_DIM), F32)
    cos_head = jnp.concatenate([cos, cos, ones], axis=1)
    sin_head = jnp.concatenate([-sin, sin, jnp.zeros_like(ones)], axis=1)
    return (jnp.tile(cos_head, (1, HEADS_PER_TILE)), jnp.tile(sin_head, (1, HEADS_PER_TILE)))


def kernel(x, w_in, lb_logits, hg_norm_w, sinks, w_out, ln_g, ln_b):
    batch, seq_len, d_model = x.shape
    assert d_model == D_MODEL and w_in.shape == (DEPTH, D_MODEL, IN_WIDTH)
    step_rows = TILES_PER_STEP * TIME_TILE
    assert seq_len % step_rows == 0
    cos_tab, sin_tab = _rope_tables(seq_len)
    n_steps = seq_len // step_rows
    last_step = batch * n_steps - 1

    def next_tile_index(b, u, *_):
        nxt = jnp.minimum(b * n_steps + u + 1, last_step)
        return (nxt // n_steps, TILES_PER_STEP * (nxt % n_steps), 0)

    const = lambda b, u, *_: (0, 0)
    grid_spec = pltpu.PrefetchScalarGridSpec(
        num_scalar_prefetch=1,
        grid=(batch, n_steps),
        in_specs=[
            pl.BlockSpec((1, step_rows, D_MODEL), lambda b, u, *_: (b, u, 0)),
            pl.BlockSpec((1, TIME_TILE, D_MODEL), next_tile_index),
            pl.BlockSpec(memory_space=pl.ANY),
            pl.BlockSpec(memory_space=pl.ANY),
            pl.BlockSpec((DEPTH + 1, HG_WIDTH), const),
            pl.BlockSpec((1, HG_WIDTH), const),
            pl.BlockSpec((1, D_MODEL), const),
            pl.BlockSpec((1, D_MODEL), const),
            pl.BlockSpec((step_rows, LANES), lambda b, u, *_: (u, 0)),
            pl.BlockSpec((step_rows, LANES), lambda b, u, *_: (u, 0)),
        ],
        out_specs=pl.BlockSpec((1, step_rows, D_MODEL), lambda b, u, *_: (b, u, 0)),
        scratch_shapes=[
            pltpu.VMEM((TIME_TILE, IN_WIDTH), F32),
            pltpu.VMEM((TIME_TILE, IN_WIDTH), F32),
            pltpu.VMEM((2, TIME_TILE, D_MODEL), BF16),
            pltpu.VMEM((TILES_PER_STEP, TIME_TILE, D_MODEL), BF16),
            pltpu.VMEM((TIME_TILE, D_MODEL), F32),
            pltpu.VMEM((HG_HEADS, HG_HEAD_DIM, HG_HEAD_DIM), F32),
            pltpu.VMEM((2 * SW_KV_HEADS, STEP_BLOCKS * WINDOW, LANES), BF16),
            pltpu.VMEM((SW_KV_HEADS, VT_ROWS, STEP_BLOCKS * WINDOW), BF16),
            pltpu.VMEM((2 * SW_KV_HEADS, WINDOW, LANES), BF16),
            pltpu.VMEM((SW_KV_HEADS, VT_ROWS, WINDOW), BF16),
            pltpu.VMEM((D_MODEL, IN_WIDTH), BF16),
            pltpu.VMEM((D_MODEL, D_MODEL), BF16),
            pltpu.VMEM((WEIGHT_SLOTS, WEIGHT_CHUNK_ROWS, IN_WIDTH), F32),
            pltpu.VMEM((WEIGHT_SLOTS, WEIGHT_CHUNK_ROWS, D_MODEL), F32),
            pltpu.SemaphoreType.DMA((WEIGHT_SLOTS,)),
            pltpu.SemaphoreType.DMA((WEIGHT_SLOTS,)),
        ],
    )
    return pl.pallas_call(
        _layer_kernel,
        grid_spec=grid_spec,
        out_shape=jax.ShapeDtypeStruct(x.shape, x.dtype),
        compiler_params=pltpu.CompilerParams(
            dimension_semantics=("arbitrary", "arbitrary"),
            vmem_limit_bytes=VMEM_LIMIT_BYTES),
        name="hybrid_hgrn2_swa_layer",
    )(sinks[0], x, x, w_in, w_out, lb_logits, hg_norm_w, ln_g, ln_b, cos_tab, sin_tab)
```

```python
import jax
import jax.numpy as jnp
from jax import lax
from jax.experimental import pallas as pl
from jax.experimental.pallas import tpu as pltpu

D_MODEL = 1024
DEPTH = 1
HG_WIDTH = 512
HG_HEAD_DIM = 128
HG_HEADS = HG_WIDTH // HG_HEAD_DIM
HG_CHUNK = 64
SW_WIDTH = 512
SW_HEAD_DIM = 64
SW_Q_HEADS = SW_WIDTH // SW_HEAD_DIM
SW_KV_HEADS = SW_Q_HEADS // 4
SW_KV_WIDTH = SW_KV_HEADS * SW_HEAD_DIM
SW_GROUP = SW_Q_HEADS // SW_KV_HEADS
WINDOW = 128
ROPE_THETA = 500000.0
ROPE_DIM = SW_HEAD_DIM // 4
DN_ALPHA = (2.0 * DEPTH) ** 0.25
LN_EPS = 1e-5
RMS_EPS = 1e-6
IN_WIDTH = 4 * HG_WIDTH + SW_WIDTH + 2 * SW_KV_WIDTH + SW_WIDTH

OFF_HQ = 0
OFF_HF = OFF_HQ + HG_WIDTH
OFF_HI = OFF_HF + HG_WIDTH
OFF_HG = OFF_HI + HG_WIDTH
OFF_AQ = OFF_HG + HG_WIDTH
OFF_AK = OFF_AQ + SW_WIDTH
OFF_AV = OFF_AK + SW_KV_WIDTH
OFF_AG = OFF_AV + SW_KV_WIDTH

LANES = 128
BF16_SUBLANES = 16
MXU_WIDTH = 256
HEADS_PER_TILE = LANES // SW_HEAD_DIM
assert HEADS_PER_TILE == 2 and SW_GROUP == 2 * HEADS_PER_TILE and SW_KV_WIDTH == LANES
VT_ROWS = SW_HEAD_DIM + BF16_SUBLANES
TIME_TILE = 256
TILES_PER_STEP = 4
PIECE = MXU_WIDTH
N_IN_PIECES = IN_WIDTH // PIECE
N_OUT_PIECES = D_MODEL // PIECE
assert IN_WIDTH % PIECE == 0 and D_MODEL % PIECE == 0
N_CHUNKS = TIME_TILE // HG_CHUNK
N_BLOCKS = TIME_TILE // WINDOW
STEP_BLOCKS = TILES_PER_STEP * N_BLOCKS
HEAD_CUM_CHUNKS = 3
HEAD_DEC_CHUNKS = 1
HEAD_AFTER_CHAIN = 4
SW_SCORES_AHEAD = 8
WEIGHT_CHUNK_ROWS = 64
WEIGHT_SLOTS = 4
VMEM_LIMIT_BYTES = 58 * 1024 * 1024

F32 = jnp.float32
BF16 = jnp.bfloat16
NT_DIMS = (((1,), (1,)), ((), ()))
TN_DIMS = (((0,), (0,)), ((), ()))


LOG2_E = 1.4426950408889634


def _sigmoid(v):
    return 1.0 / (1.0 + jnp.exp2(v * (-LOG2_E)))


def _load_weight_as_bf16(w_hbm, stage_ref, sem_ref, dst_ref, scale=None):
    n_rows = dst_ref.shape[0]
    n_slots, chunk = stage_ref.shape[0], stage_ref.shape[1]
    n_chunks = n_rows // chunk
    assert n_chunks * chunk == n_rows and n_slots >= 2

    def chunk_copy(i):
        return pltpu.make_async_copy(w_hbm.at[0, pl.ds(i * chunk, chunk), :],
                                     stage_ref.at[i % n_slots], sem_ref.at[i % n_slots])

    for i in range(min(n_slots - 1, n_chunks)):
        chunk_copy(i).start()
    for i in range(n_chunks):
        ahead = i + n_slots - 1
        if ahead < n_chunks:
            chunk_copy(ahead).start()
        chunk_copy(i).wait()
        rows = stage_ref[i % n_slots]
        if scale is not None:
            rows = rows * scale
        dst_ref[i * chunk:(i + 1) * chunk, :] = rows.astype(BF16)


def _layer_kernel(sinks_ref, xc_ref, xn_ref, win_hbm, wout_hbm, lbl_ref, nw_ref, lng_ref, lnb_ref,
                  cos_ref, sin_ref, o_ref,
                  ha_ref, hb_ref, xb_ref, mix_ref, out_ref, st_ref, kwin_ref, vt_ref,
                  kcarry_ref, vcarry_ref, win_ref, wout_ref, win_stage, wout_stage,
                  win_sems, wout_sems):
    b = pl.program_id(0)
    u = pl.program_id(1)
    tt = TIME_TILE
    h_bufs = (ha_ref, hb_ref)

    @pl.when(jnp.logical_and(b == 0, u == 0))
    def _first_step():
        _load_weight_as_bf16(win_hbm, win_stage, win_sems, win_ref)
        _load_weight_as_bf16(wout_hbm, wout_stage, wout_sems, wout_ref, scale=1.0 / DN_ALPHA)
        ha_ref[...] = jnp.dot(xc_ref[0, 0:tt, :].astype(BF16), win_ref[...],
                              preferred_element_type=F32)

    @pl.when(u == 0)
    def _reset_carries():
        st_ref[...] = jnp.zeros_like(st_ref)
        kcarry_ref[...] = jnp.zeros_like(kcarry_ref)
        vcarry_ref[:, 0:SW_HEAD_DIM, :] = jnp.zeros((SW_KV_HEADS, SW_HEAD_DIM, WINDOW), BF16)
        vcarry_ref[:, SW_HEAD_DIM:VT_ROWS, :] = jnp.ones((SW_KV_HEADS, BF16_SUBLANES, WINDOW), BF16)
        vt_ref[:, SW_HEAD_DIM:VT_ROWS, :] = jnp.ones(
            (SW_KV_HEADS, BF16_SUBLANES, STEP_BLOCKS * WINDOW), BF16)

    lbl = lbl_ref[...]
    lbl_e = jnp.exp(lbl - jnp.max(lbl, axis=0, keepdims=True))
    lb = lbl_e[0:1] / jnp.sum(lbl_e, axis=0, keepdims=True)
    nw = nw_ref[...] * (HG_HEAD_DIM ** 0.5)
    row = lax.broadcasted_iota(jnp.int32, (HG_CHUNK, HG_CHUNK), 0)
    col = lax.broadcasted_iota(jnp.int32, (HG_CHUNK, HG_CHUNK), 1)
    tril = col <= row
    cum_mat = tril.astype(BF16)
    cum3 = jnp.concatenate([cum_mat, cum_mat, cum_mat], axis=1)

    lane = lax.broadcasted_iota(jnp.int32, (1, LANES), 1)
    rope_first = (lane % SW_HEAD_DIM) < (ROPE_DIM // 2)
    lane_lo = lane < SW_HEAD_DIM
    key_idx = lax.broadcasted_iota(jnp.int32, (2 * WINDOW, 2 * WINDOW), 0)
    qry_idx = lax.broadcasted_iota(jnp.int32, (2 * WINDOW, 2 * WINDOW), 1) % WINDOW
    band = (key_idx > qry_idx) & (key_idx <= qry_idx + WINDOW)
    pair_lo = lax.broadcasted_iota(jnp.int32, (1, 2 * WINDOW), 1) < WINDOW
    scale = SW_HEAD_DIM ** -0.5 * LOG2_E
    first_valid = jnp.where(u > 0, 0, WINDOW)
    band_cap = jnp.where(band, jnp.inf, -jnp.inf)
    first_cap = jnp.where(band & (key_idx >= first_valid), jnp.inf, -jnp.inf)

    def rope(v, cos, sin):
        partner = jnp.where(rope_first,
                            pltpu.roll(v, LANES - ROPE_DIM // 2, 1),
                            pltpu.roll(v, ROPE_DIM // 2, 1))
        return v * cos + partner * sin

    def head_variants(v):
        sw = pltpu.roll(v, SW_HEAD_DIM, 1)
        zero = jnp.zeros_like(v)
        return (jnp.where(lane_lo, v, zero), jnp.where(lane_lo, zero, sw),
                jnp.where(lane_lo, sw, zero), jnp.where(lane_lo, zero, v))

    def in_projection_piece(k, piece):
        xb = xb_ref.at[k % 2]
        cols = slice(piece * PIECE, (piece + 1) * PIECE)
        h_bufs[k % 2][:, cols] = jnp.dot(xb[...], win_ref[:, cols], preferred_element_type=F32)

    def out_projection_piece(k, piece):
        cols = slice(piece * PIECE, (piece + 1) * PIECE)
        out_ref[:, cols] = jnp.dot(mix_ref[k], wout_ref[:, cols], preferred_element_type=F32)

    def post_norm(k):
        rows = slice(k * tt, (k + 1) * tt)
        z = xc_ref[0, rows, :] + out_ref[...]
        mu = jnp.mean(z, axis=-1, keepdims=True)
        zc = z - mu
        var = jnp.mean(zc * zc, axis=-1, keepdims=True)
        o_ref[0, rows, :] = ((zc * lax.rsqrt(var + LN_EPS / DN_ALPHA ** 2)) * lng_ref[...]
                             + lnb_ref[...])

    def make_fillers(k):
        stages = []
        if k > 0:
            for piece in range(N_OUT_PIECES):
                stages.append(lambda piece=piece: out_projection_piece(k - 1, piece))
            stages.append(lambda: post_norm(k - 1))
        for piece in range(N_IN_PIECES):
            def in_stage(piece=piece):
                in_projection_piece(k + 1, piece)
                projected[k + 1] = piece + 1
            stages.append(in_stage)
        return iter(stages)

    projected = {}

    def process_tile(k, head):
        h_ref = h_bufs[k % 2]
        base = k * tt
        mix = mix_ref.at[k]
        if k + 1 < TILES_PER_STEP:
            x_next_rows = xc_ref[0, (k + 1) * tt:(k + 2) * tt, :]
        else:
            x_next_rows = xn_ref[0]
        xb_ref[(k + 1) % 2] = x_next_rows.astype(BF16)
        fillers = make_fillers(k)

        def fill(n=1):
            for _ in range(n):
                stage = next(fillers, None)
                if stage is not None:
                    stage()

        def hg_pre(c, h=h_ref):
            rows = slice(c * HG_CHUNK, (c + 1) * HG_CHUNK)
            f = lb + (1.0 - lb) * _sigmoid(h[rows, OFF_HF:OFF_HF + HG_WIDTH])
            log_f = jnp.log2(f)
            hi = log_f.astype(BF16)
            rem = log_f - hi.astype(F32)
            mid = rem.astype(BF16)
            lo = (rem - mid.astype(F32)).astype(BF16)
            return 1.0 - f, jnp.concatenate([hi, mid, lo], axis=0)

        def hg_cum(parts):
            return jnp.dot(cum3, parts, preferred_element_type=F32)

        def hg_decays(c, k_in, g_cum, h=h_ref):
            rows = slice(c * HG_CHUNK, (c + 1) * HG_CHUNK)
            hq = h[rows, OFF_HQ:OFF_HQ + HG_WIDTH]
            g_last = g_cum[HG_CHUNK - 1:HG_CHUNK, :]
            q_dec = ((hq * _sigmoid(hq)) * jnp.exp2(g_cum)).astype(BF16)
            k_dec = (k_in * jnp.exp2(-g_cum)).astype(BF16)
            k_tail = (k_in * jnp.exp2(g_last - g_cum)).astype(BF16)
            return q_dec, k_dec, k_tail, jnp.exp2(g_last)

        def hg_scores(q_dec, k_dec, hd):
            sl = slice(hd * HG_HEAD_DIM, (hd + 1) * HG_HEAD_DIM)
            return lax.dot_general(q_dec[:, sl], k_dec[:, sl], NT_DIMS, preferred_element_type=F32)

        def hg_output(c, hd, a, q_dec, k_tail, decay):
            rows = slice(c * HG_CHUNK, (c + 1) * HG_CHUNK)
            sl = slice(hd * HG_HEAD_DIM, (hd + 1) * HG_HEAD_DIM)
            v = h_ref[rows, OFF_HI + hd * HG_HEAD_DIM:OFF_HI + (hd + 1) * HG_HEAD_DIM].astype(BF16)
            s_t = st_ref[hd]
            o = jnp.dot(jnp.where(tril, a, 0.0).astype(BF16), v, preferred_element_type=F32)
            o = o + lax.dot_general(q_dec[:, sl], s_t.astype(BF16), NT_DIMS,
                                    preferred_element_type=F32)
            inc_t = lax.dot_general(v, k_tail[:, sl], TN_DIMS, preferred_element_type=F32)
            st_ref[hd] = s_t * decay[:, sl] + inc_t
            o = o * lax.rsqrt(jnp.sum(o * o, axis=-1, keepdims=True) + HG_HEAD_DIM * RMS_EPS)
            o = o * nw[:, sl]
            hg = h_ref[rows, OFF_HG + hd * HG_HEAD_DIM:OFF_HG + (hd + 1) * HG_HEAD_DIM]
            mix[rows, sl] = (o * (hg * _sigmoid(hg))).astype(BF16)

        def sw_prepare(blk):
            r0 = blk * WINDOW
            rows = slice(r0, r0 + WINDOW)
            cur = slice(base + r0, base + r0 + WINDOW)
            cos = cos_ref[base + r0:base + r0 + WINDOW, :]
            sin = sin_ref[base + r0:base + r0 + WINDOW, :]
            k_rot = rope(h_ref[rows, OFF_AK:OFF_AK + SW_KV_WIDTH], cos, sin)
            for idx, kv in enumerate(head_variants(k_rot)):
                kwin_ref[idx, cur, :] = kv.astype(BF16)
            v_t = h_ref[rows, OFF_AV:OFF_AV + SW_KV_WIDTH].T
            for g in range(SW_KV_HEADS):
                vt_ref[g, 0:SW_HEAD_DIM, cur] = (
                    v_t[g * SW_HEAD_DIM:(g + 1) * SW_HEAD_DIM].astype(BF16))
            q_bf = []
            for tile in range(SW_WIDTH // LANES):
                q_rot = rope(h_ref[rows, OFF_AQ + tile * LANES:OFF_AQ + (tile + 1) * LANES], cos, sin)
                q_bf.append((q_rot * scale).astype(BF16))
            return [jnp.concatenate([q_bf[2 * g], q_bf[2 * g + 1]], axis=0)
                    for g in range(SW_KV_HEADS)]

        def key_window(blk, idx):
            r0 = base + blk * WINDOW
            if r0 == 0:
                return jnp.concatenate([kcarry_ref[idx], kwin_ref[idx, 0:WINDOW, :]], axis=0)
            return kwin_ref[idx, r0 - WINDOW:r0 + WINDOW, :]

        def value_window(blk, g):
            r0 = base + blk * WINDOW
            if r0 == 0:
                return jnp.concatenate([vcarry_ref[g], vt_ref[g, :, 0:WINDOW]], axis=1)
            return vt_ref[g, :, r0 - WINDOW:r0 + WINDOW]

        def sw_scores(blk, g, j, q_pair):
            return lax.dot_general(key_window(blk, 2 * g + j), q_pair, NT_DIMS,
                                   preferred_element_type=F32)

        def sw_values(blk, g, j, s_t, cap):
            sink_row = jnp.where(pair_lo, sinks_ref[SW_GROUP * g + j],
                                 sinks_ref[SW_GROUP * g + HEADS_PER_TILE + j]) * LOG2_E
            s_t = jnp.minimum(s_t, cap)
            m = jnp.maximum(jnp.max(s_t, axis=0, keepdims=True), sink_row)
            p_t = jnp.exp2(s_t - m).astype(BF16)
            acc = jnp.dot(value_window(blk, g), p_t, preferred_element_type=F32)
            denom = acc[SW_HEAD_DIM:SW_HEAD_DIM + 1, :] + jnp.exp2(sink_row - m)
            return acc[0:SW_HEAD_DIM, :] * (1.0 / denom)

        def sw_finish(blk, g, o_t):
            rows = slice(blk * WINDOW, (blk + 1) * WINDOW)
            for tl in range(2):
                tile = 2 * g + tl
                lanes = slice(tl * LANES, (tl + 1) * LANES)
                o_a = jnp.concatenate([o_t[0][:, lanes], o_t[1][:, lanes]], axis=0).T
                ag = h_ref[rows, OFF_AG + tile * LANES:OFF_AG + (tile + 1) * LANES]
                mix[rows, HG_WIDTH + tile * LANES:HG_WIDTH + (tile + 1) * LANES] = (
                    o_a * (ag * _sigmoid(ag))).astype(BF16)

        many = k > 0
        masks = [first_cap if (k == 0 and blk == 0) else band_cap for blk in range(N_BLOCKS)]
        chains = [(blk, g, j) for blk in range(N_BLOCKS) for g in range(SW_KV_HEADS)
                  for j in range(HEADS_PER_TILE)]
        q_pairs = {}
        scores = {}

        def sw_issue_scores(i):
            if i < len(chains) and i not in scores:
                blk, g, j = chains[i]
                if blk not in q_pairs:
                    q_pairs[blk] = sw_prepare(blk)
                scores[i] = sw_scores(blk, g, j, q_pairs[blk][g])

        head = head or {"pre": {}, "cum": {}, "dec": {}, "scores": {}}
        pre = dict(head["pre"])
        cum = dict(head["cum"])

        def hg_issue_cum(c):
            if c < N_CHUNKS and c not in cum:
                pre[c] = hg_pre(c)
                cum[c] = hg_cum(pre[c][1])

        fill(2 if many else 1)
        for c in range(min(2, N_CHUNKS)):
            hg_issue_cum(c)
            fill(1 if many else 0)
        for c in range(N_CHUNKS):
            if c in head["dec"]:
                q_dec, k_dec, k_tail, decay = head["dec"][c]
                hg_s = head["scores"][c]
            else:
                q_dec, k_dec, k_tail, decay = hg_decays(c, pre[c][0], cum[c])
                hg_s = [hg_scores(q_dec, k_dec, hd) for hd in range(HG_HEADS)]
            hg_issue_cum(c + 2)
            fill()
            for hd in range(HG_HEADS):
                hg_output(c, hd, hg_s[hd], q_dec, k_tail, decay)
            fill(1 if many else 0)

        for i in range(min(SW_SCORES_AHEAD, len(chains))):
            sw_issue_scores(i)
        fill()
        outs = {}
        has_next = k + 1 < TILES_PER_STEP
        next_head = {"pre": {}, "cum": {}, "dec": {}, "scores": {}} if has_next else None
        h_next = h_bufs[(k + 1) % 2]

        def head_start(i):
            if i != HEAD_AFTER_CHAIN:
                return
            assert projected.get(k + 1, 0) * PIECE >= OFF_HF + HG_WIDTH
            for c in range(HEAD_CUM_CHUNKS):
                next_head["pre"][c] = hg_pre(c, h_next)
                next_head["cum"][c] = hg_cum(next_head["pre"][c][1])
            for c in range(HEAD_DEC_CHUNKS):
                d = hg_decays(c, next_head["pre"][c][0], next_head["cum"][c], h_next)
                next_head["dec"][c] = d
                next_head["scores"][c] = [hg_scores(d[0], d[1], hd) for hd in range(HG_HEADS)]

        for i, (blk, g, j) in enumerate(chains):
            outs[(blk, g, j)] = sw_values(blk, g, j, scores.pop(i), masks[blk])
            sw_issue_scores(i + SW_SCORES_AHEAD)
            if j == HEADS_PER_TILE - 1:
                sw_finish(blk, g, [outs.pop((blk, g, jj)) for jj in range(HEADS_PER_TILE)])
            fill()
            if has_next:
                head_start(i)
        fill(N_OUT_PIECES + 1 + N_IN_PIECES)
        return next_head

    head = None
    for k in range(TILES_PER_STEP):
        head = process_tile(k, head)
    for piece in range(N_OUT_PIECES):
        out_projection_piece(TILES_PER_STEP - 1, piece)
    post_norm(TILES_PER_STEP - 1)

    last = slice((STEP_BLOCKS - 1) * WINDOW, STEP_BLOCKS * WINDOW)
    for idx in range(2 * SW_KV_HEADS):
        kcarry_ref[idx] = kwin_ref[idx, last, :]
    for g in range(SW_KV_HEADS):
        vcarry_ref[g, 0:SW_HEAD_DIM, :] = vt_ref[g, 0:SW_HEAD_DIM, last]


def _rope_tables(seq_len):
    pos = jnp.arange(seq_len, dtype=F32)
    inv_freq = ROPE_THETA ** (-jnp.arange(0, ROPE_DIM, 2, dtype=F32) / ROPE_DIM)
    ang = pos[:, None] * inv_freq[None, :]
    cos = jnp.cos(ang)
    sin = jnp.sin(ang)
    ones = jnp.ones((seq_len, SW_HEAD_DIM - ROPE_DIM), F32)
    cos_head = jnp.concatenate([cos, cos, ones], axis=1)
    sin_head = jnp.concatenate([-sin, sin, jnp.zeros_like(ones)], axis=1)
    return (jnp.tile(cos_head, (1, HEADS_PER_TILE)), jnp.tile(sin_head, (1, HEADS_PER_TILE)))


def kernel(x, w_in, lb_logits, hg_norm_w, sinks, w_out, ln_g, ln_b):
    batch, seq_len, d_model = x.shape
    assert d_model == D_MODEL and w_in.shape == (DEPTH, D_MODEL, IN_WIDTH)
    step_rows = TILES_PER_STEP * TIME_TILE
    assert seq_len % step_rows == 0
    cos_tab, sin_tab = _rope_tables(seq_len)
    n_steps = seq_len // step_rows
    last_step = batch * n_steps - 1

    def next_tile_index(b, u, *_):
        nxt = jnp.minimum(b * n_steps + u + 1, last_step)
        return (nxt // n_steps, TILES_PER_STEP * (nxt % n_steps), 0)

    const = lambda b, u, *_: (0, 0)
    grid_spec = pltpu.PrefetchScalarGridSpec(
        num_scalar_prefetch=1,
        grid=(batch, n_steps),
        in_specs=[
            pl.BlockSpec((1, step_rows, D_MODEL), lambda b, u, *_: (b, u, 0)),
            pl.BlockSpec((1, TIME_TILE, D_MODEL), next_tile_index),
            pl.BlockSpec(memory_space=pl.ANY),
            pl.BlockSpec(memory_space=pl.ANY),
            pl.BlockSpec((DEPTH + 1, HG_WIDTH), const),
            pl.BlockSpec((1, HG_WIDTH), const),
            pl.BlockSpec((1, D_MODEL), const),
            pl.BlockSpec((1, D_MODEL), const),
            pl.BlockSpec((step_rows, LANES), lambda b, u, *_: (u, 0)),
            pl.BlockSpec((step_rows, LANES), lambda b, u, *_: (u, 0)),
        ],
        out_specs=pl.BlockSpec((1, step_rows, D_MODEL), lambda b, u, *_: (b, u, 0)),
        scratch_shapes=[
            pltpu.VMEM((TIME_TILE, IN_WIDTH), F32),
            pltpu.VMEM((TIME_TILE, IN_WIDTH), F32),
            pltpu.VMEM((2, TIME_TILE, D_MODEL), BF16),
            pltpu.VMEM((TILES_PER_STEP, TIME_TILE, D_MODEL), BF16),
            pltpu.VMEM((TIME_TILE, D_MODEL), F32),
            pltpu.VMEM((HG_HEADS, HG_HEAD_DIM, HG_HEAD_DIM), F32),
            pltpu.VMEM((2 * SW_KV_HEADS, STEP_BLOCKS * WINDOW, LANES), BF16),
            pltpu.VMEM((SW_KV_HEADS, VT_ROWS, STEP_BLOCKS * WINDOW), BF16),
            pltpu.VMEM((2 * SW_KV_HEADS, WINDOW, LANES), BF16),
            pltpu.VMEM((SW_KV_HEADS, VT_ROWS, WINDOW), BF16),
            pltpu.VMEM((D_MODEL, IN_WIDTH), BF16),
            pltpu.VMEM((D_MODEL, D_MODEL), BF16),
            pltpu.VMEM((WEIGHT_SLOTS, WEIGHT_CHUNK_ROWS, IN_WIDTH), F32),
            pltpu.VMEM((WEIGHT_SLOTS, WEIGHT_CHUNK_ROWS, D_MODEL), F32),
            pltpu.SemaphoreType.DMA((WEIGHT_SLOTS,)),
            pltpu.SemaphoreType.DMA((WEIGHT_SLOTS,)),
        ],
    )
    return pl.pallas_call(
        _layer_kernel,
        grid_spec=grid_spec,
        out_shape=jax.ShapeDtypeStruct(x.shape, x.dtype),
        compiler_params=pltpu.CompilerParams(
            dimension_semantics=("arbitrary", "arbitrary"),
            vmem_limit_bytes=VMEM_LIMIT_BYTES),
        name="hybrid_hgrn2_swa_layer",
    )(sinks[0], x, x, w_in, w_out, lb_logits, hg_norm_w, ln_g, ln_b, cos_tab, sin_tab)
```

```python
import jax
import jax.numpy as jnp
from jax import lax
from jax.experimental import pallas as pl
from jax.experimental.pallas import tpu as pltpu

D_MODEL = 1024
DEPTH = 1
HG_WIDTH = 512
HG_HEAD_DIM = 128
HG_HEADS = HG_WIDTH // HG_HEAD_DIM
HG_CHUNK = 64
SW_WIDTH = 512
SW_HEAD_DIM = 64
SW_Q_HEADS = SW_WIDTH // SW_HEAD_DIM
SW_KV_HEADS = SW_Q_HEADS // 4
SW_KV_WIDTH = SW_KV_HEADS * SW_HEAD_DIM
SW_GROUP = SW_Q_HEADS // SW_KV_HEADS
WINDOW = 128
ROPE_THETA = 500000.0
ROPE_DIM = SW_HEAD_DIM // 4
DN_ALPHA = (2.0 * DEPTH) ** 0.25
LN_EPS = 1e-5
RMS_EPS = 1e-6
IN_WIDTH = 4 * HG_WIDTH + SW_WIDTH + 2 * SW_KV_WIDTH + SW_WIDTH

OFF_HQ = 0
OFF_HF = OFF_HQ + HG_WIDTH
OFF_HI = OFF_HF + HG_WIDTH
OFF_HG = OFF_HI + HG_WIDTH
OFF_AQ = OFF_HG + HG_WIDTH
OFF_AK = OFF_AQ + SW_WIDTH
OFF_AV = OFF_AK + SW_KV_WIDTH
OFF_AG = OFF_AV + SW_KV_WIDTH

LANES = 128
BF16_SUBLANES = 16
MXU_WIDTH = 256
HEADS_PER_TILE = LANES // SW_HEAD_DIM
assert HEADS_PER_TILE == 2 and SW_GROUP == 2 * HEADS_PER_TILE and SW_KV_WIDTH == LANES
VT_ROWS = SW_HEAD_DIM + BF16_SUBLANES
TIME_TILE = 256
TILES_PER_STEP = 4
PIECE = MXU_WIDTH
N_IN_PIECES = IN_WIDTH // PIECE
N_OUT_PIECES = D_MODEL // PIECE
assert IN_WIDTH % PIECE == 0 and D_MODEL % PIECE == 0
N_CHUNKS = TIME_TILE // HG_CHUNK
N_BLOCKS = TIME_TILE // WINDOW
STEP_BLOCKS = TILES_PER_STEP * N_BLOCKS
HEAD_CUM_CHUNKS = 3
HEAD_DEC_CHUNKS = 1
HEAD_AFTER_CHAIN = 3
SW_SCORES_AHEAD = 8
WEIGHT_CHUNK_ROWS = 64
WEIGHT_SLOTS = 4
VMEM_LIMIT_BYTES = 58 * 1024 * 1024

F32 = jnp.float32
BF16 = jnp.bfloat16
NT_DIMS = (((1,), (1,)), ((), ()))
TN_DIMS = (((0,), (0,)), ((), ()))


LOG2_E = 1.4426950408889634


def _sigmoid(v):
    return 1.0 / (1.0 + jnp.exp2(v * (-LOG2_E)))


def _load_weight_as_bf16(w_hbm, stage_ref, sem_ref, dst_ref, scale=None):
    n_rows = dst_ref.shape[0]
    n_slots, chunk = stage_ref.shape[0], stage_ref.shape[1]
    n_chunks = n_rows // chunk
    assert n_chunks * chunk == n_rows and n_slots >= 2

    def chunk_copy(i):
        return pltpu.make_async_copy(w_hbm.at[0, pl.ds(i * chunk, chunk), :],
                                     stage_ref.at[i % n_slots], sem_ref.at[i % n_slots])

    for i in range(min(n_slots - 1, n_chunks)):
        chunk_copy(i).start()
    for i in range(n_chunks):
        ahead = i + n_slots - 1
        if ahead < n_chunks:
            chunk_copy(ahead).start()
        chunk_copy(i).wait()
        rows = stage_ref[i % n_slots]
        if scale is not None:
            rows = rows * scale
        dst_ref[i * chunk:(i + 1) * chunk, :] = rows.astype(BF16)


def _layer_kernel(sinks_ref, xc_ref, xn_ref, win_hbm, wout_hbm, lbl_ref, nw_ref, lng_ref, lnb_ref,
                  cos_ref, sin_ref, o_ref,
                  ha_ref, hb_ref, xb_ref, mix_ref, out_ref, st_ref, kwin_ref, vt_ref,
                  kcarry_ref, vcarry_ref, win_ref, wout_ref, win_stage, wout_stage,
                  win_sems, wout_sems):
    b = pl.program_id(0)
    u = pl.program_id(1)
    tt = TIME_TILE
    h_bufs = (ha_ref, hb_ref)

    @pl.when(jnp.logical_and(b == 0, u == 0))
    def _first_step():
        _load_weight_as_bf16(win_hbm, win_stage, win_sems, win_ref)
        _load_weight_as_bf16(wout_hbm, wout_stage, wout_sems, wout_ref, scale=1.0 / DN_ALPHA)
        ha_ref[...] = jnp.dot(xc_ref[0, 0:tt, :].astype(BF16), win_ref[...],
                              preferred_element_type=F32)

    @pl.when(u == 0)
    def _reset_carries():
        st_ref[...] = jnp.zeros_like(st_ref)
        kcarry_ref[...] = jnp.zeros_like(kcarry_ref)
        vcarry_ref[:, 0:SW_HEAD_DIM, :] = jnp.zeros((SW_KV_HEADS, SW_HEAD_DIM, WINDOW), BF16)
        vcarry_ref[:, SW_HEAD_DIM:VT_ROWS, :] = jnp.ones((SW_KV_HEADS, BF16_SUBLANES, WINDOW), BF16)
        vt_ref[:, SW_HEAD_DIM:VT_ROWS, :] = jnp.ones(
            (SW_KV_HEADS, BF16_SUBLANES, STEP_BLOCKS * WINDOW), BF16)

    lbl = lbl_ref[...]
    lbl_e = jnp.exp(lbl - jnp.max(lbl, axis=0, keepdims=True))
    lb = lbl_e[0:1] / jnp.sum(lbl_e, axis=0, keepdims=True)
    nw = nw_ref[...] * (HG_HEAD_DIM ** 0.5)
    row = lax.broadcasted_iota(jnp.int32, (HG_CHUNK, HG_CHUNK), 0)
    col = lax.broadcasted_iota(jnp.int32, (HG_CHUNK, HG_CHUNK), 1)
    tril = col <= row
    cum_mat = tril.astype(BF16)
    cum3 = jnp.concatenate([cum_mat, cum_mat, cum_mat], axis=1)

    lane = lax.broadcasted_iota(jnp.int32, (1, LANES), 1)
    rope_first = (lane % SW_HEAD_DIM) < (ROPE_DIM // 2)
    lane_lo = lane < SW_HEAD_DIM
    key_idx = lax.broadcasted_iota(jnp.int32, (2 * WINDOW, 2 * WINDOW), 0)
    qry_idx = lax.broadcasted_iota(jnp.int32, (2 * WINDOW, 2 * WINDOW), 1) % WINDOW
    band = (key_idx > qry_idx) & (key_idx <= qry_idx + WINDOW)
    pair_lo = lax.broadcasted_iota(jnp.int32, (1, 2 * WINDOW), 1) < WINDOW
    scale = SW_HEAD_DIM ** -0.5 * LOG2_E
    first_valid = jnp.where(u > 0, 0, WINDOW)
    band_cap = jnp.where(band, jnp.inf, -jnp.inf)
    first_cap = jnp.where(band & (key_idx >= first_valid), jnp.inf, -jnp.inf)

    def rope(v, cos, sin):
        partner = jnp.where(rope_first,
                            pltpu.roll(v, LANES - ROPE_DIM // 2, 1),
                            pltpu.roll(v, ROPE_DIM // 2, 1))
        return v * cos + partner * sin

    def head_variants(v):
        sw = pltpu.roll(v, SW_HEAD_DIM, 1)
        zero = jnp.zeros_like(v)
        return (jnp.where(lane_lo, v, zero), jnp.where(lane_lo, zero, sw),
                jnp.where(lane_lo, sw, zero), jnp.where(lane_lo, zero, v))

    def in_projection_piece(k, piece):
        xb = xb_ref.at[k % 2]
        cols = slice(piece * PIECE, (piece + 1) * PIECE)
        h_bufs[k % 2][:, cols] = jnp.dot(xb[...], win_ref[:, cols], preferred_element_type=F32)

    def out_projection_piece(k, piece):
        cols = slice(piece * PIECE, (piece + 1) * PIECE)
        out_ref[:, cols] = jnp.dot(mix_ref[k], wout_ref[:, cols], preferred_element_type=F32)

    def post_norm(k):
        rows = slice(k * tt, (k + 1) * tt)
        z = xc_ref[0, rows, :] + out_ref[...]
        mu = jnp.mean(z, axis=-1, keepdims=True)
        zc = z - mu
        var = jnp.mean(zc * zc, axis=-1, keepdims=True)
        o_ref[0, rows, :] = ((zc * lax.rsqrt(var + LN_EPS / DN_ALPHA ** 2)) * lng_ref[...]
                             + lnb_ref[...])

    def make_fillers(k):
        stages = []
        if k > 0:
            for piece in range(N_OUT_PIECES):
                stages.append(lambda piece=piece: out_projection_piece(k - 1, piece))
            stages.append(lambda: post_norm(k - 1))
        for piece in range(N_IN_PIECES):
            def in_stage(piece=piece):
                in_projection_piece(k + 1, piece)
                projected[k + 1] = piece + 1
            stages.append(in_stage)
        return iter(stages)

    projected = {}

    def process_tile(k, head):
        h_ref = h_bufs[k % 2]
        base = k * tt
        mix = mix_ref.at[k]
        if k + 1 < TILES_PER_STEP:
            x_next_rows = xc_ref[0, (k + 1) * tt:(k + 2) * tt, :]
        else:
            x_next_rows = xn_ref[0]
        xb_ref[(k + 1) % 2] = x_next_rows.astype(BF16)
        fillers = make_fillers(k)

        def fill(n=1):
            for _ in range(n):
                stage = next(fillers, None)
                if stage is not None:
                    stage()

        def hg_pre(c, h=h_ref):
            rows = slice(c * HG_CHUNK, (c + 1) * HG_CHUNK)
            f = lb + (1.0 - lb) * _sigmoid(h[rows, OFF_HF:OFF_HF + HG_WIDTH])
            log_f = jnp.log2(f)
            hi = log_f.astype(BF16)
            rem = log_f - hi.astype(F32)
            mid = rem.astype(BF16)
            lo = (rem - mid.astype(F32)).astype(BF16)
            return 1.0 - f, jnp.concatenate([hi, mid, lo], axis=0)

        def hg_cum(parts):
            return jnp.dot(cum3, parts, preferred_element_type=F32)

        def hg_decays(c, k_in, g_cum, h=h_ref):
            rows = slice(c * HG_CHUNK, (c + 1) * HG_CHUNK)
            hq = h[rows, OFF_HQ:OFF_HQ + HG_WIDTH]
            g_last = g_cum[HG_CHUNK - 1:HG_CHUNK, :]
            q_dec = ((hq * _sigmoid(hq)) * jnp.exp2(g_cum)).astype(BF16)
            k_dec = (k_in * jnp.exp2(-g_cum)).astype(BF16)
            k_tail = (k_in * jnp.exp2(g_last - g_cum)).astype(BF16)
            return q_dec, k_dec, k_tail, jnp.exp2(g_last)

        def hg_scores(q_dec, k_dec, hd):
            sl = slice(hd * HG_HEAD_DIM, (hd + 1) * HG_HEAD_DIM)
            return lax.dot_general(q_dec[:, sl], k_dec[:, sl], NT_DIMS, preferred_element_type=F32)

        def hg_output(c, hd, a, q_dec, k_tail, decay):
            rows = slice(c * HG_CHUNK, (c + 1) * HG_CHUNK)
            sl = slice(hd * HG_HEAD_DIM, (hd + 1) * HG_HEAD_DIM)
            v = h_ref[rows, OFF_HI + hd * HG_HEAD_DIM:OFF_HI + (hd + 1) * HG_HEAD_DIM].astype(BF16)
            s_t = st_ref[hd]
            o = jnp.dot(jnp.where(tril, a, 0.0).astype(BF16), v, preferred_element_type=F32)
            o = o + lax.dot_general(q_dec[:, sl], s_t.astype(BF16), NT_DIMS,
                                    preferred_element_type=F32)
            inc_t = lax.dot_general(v, k_tail[:, sl], TN_DIMS, preferred_element_type=F32)
            st_ref[hd] = s_t * decay[:, sl] + inc_t
            o = o * lax.rsqrt(jnp.sum(o * o, axis=-1, keepdims=True) + HG_HEAD_DIM * RMS_EPS)
            o = o * nw[:, sl]
            hg = h_ref[rows, OFF_HG + hd * HG_HEAD_DIM:OFF_HG + (hd + 1) * HG_HEAD_DIM]
            mix[rows, sl] = (o * (hg * _sigmoid(hg))).astype(BF16)

        def sw_prepare(blk):
            r0 = blk * WINDOW
            rows = slice(r0, r0 + WINDOW)
            cur = slice(base + r0, base + r0 + WINDOW)
            cos = cos_ref[base + r0:base + r0 + WINDOW, :]
            sin = sin_ref[base + r0:base + r0 + WINDOW, :]
            k_rot = rope(h_ref[rows, OFF_AK:OFF_AK + SW_KV_WIDTH], cos, sin)
            for idx, kv in enumerate(head_variants(k_rot)):
                kwin_ref[idx, cur, :] = kv.astype(BF16)
            v_t = h_ref[rows, OFF_AV:OFF_AV + SW_KV_WIDTH].T
            for g in range(SW_KV_HEADS):
                vt_ref[g, 0:SW_HEAD_DIM, cur] = (
                    v_t[g * SW_HEAD_DIM:(g + 1) * SW_HEAD_DIM].astype(BF16))
            q_bf = []
            for tile in range(SW_WIDTH // LANES):
                q_rot = rope(h_ref[rows, OFF_AQ + tile * LANES:OFF_AQ + (tile + 1) * LANES], cos, sin)
                q_bf.append((q_rot * scale).astype(BF16))
            return [jnp.concatenate([q_bf[2 * g], q_bf[2 * g + 1]], axis=0)
                    for g in range(SW_KV_HEADS)]

        def key_window(blk, idx):
            r0 = base + blk * WINDOW
            if r0 == 0:
                return jnp.concatenate([kcarry_ref[idx], kwin_ref[idx, 0:WINDOW, :]], axis=0)
            return kwin_ref[idx, r0 - WINDOW:r0 + WINDOW, :]

        def value_window(blk, g):
            r0 = base + blk * WINDOW
            if r0 == 0:
                return jnp.concatenate([vcarry_ref[g], vt_ref[g, :, 0:WINDOW]], axis=1)
            return vt_ref[g, :, r0 - WINDOW:r0 + WINDOW]

        def sw_scores(blk, g, j, q_pair):
            return lax.dot_general(key_window(blk, 2 * g + j), q_pair, NT_DIMS,
                                   preferred_element_type=F32)

        def sw_values(blk, g, j, s_t, cap):
            v_win = value_window(blk, g)
            halves = []
            for tl in range(2):
                lanes = slice(tl * WINDOW, (tl + 1) * WINDOW)
                sink = sinks_ref[SW_GROUP * g + HEADS_PER_TILE * tl + j] * LOG2_E
                s_h = jnp.minimum(s_t[:, lanes], cap[:, lanes])
                m = jnp.maximum(jnp.max(s_h, axis=0, keepdims=True), sink)
                p_h = jnp.exp2(s_h - m).astype(BF16)
                acc = jnp.dot(v_win, p_h, preferred_element_type=F32)
                denom = acc[SW_HEAD_DIM:SW_HEAD_DIM + 1, :] + jnp.exp2(sink - m)
                halves.append(acc[0:SW_HEAD_DIM, :] * (1.0 / denom))
            return jnp.concatenate(halves, axis=1)

        def sw_finish(blk, g, o_t):
            rows = slice(blk * WINDOW, (blk + 1) * WINDOW)
            for tl in range(2):
                tile = 2 * g + tl
                lanes = slice(tl * LANES, (tl + 1) * LANES)
                o_a = jnp.concatenate([o_t[0][:, lanes], o_t[1][:, lanes]], axis=0).T
                ag = h_ref[rows, OFF_AG + tile * LANES:OFF_AG + (tile + 1) * LANES]
                mix[rows, HG_WIDTH + tile * LANES:HG_WIDTH + (tile + 1) * LANES] = (
                    o_a * (ag * _sigmoid(ag))).astype(BF16)

        many = k > 0
        masks = [first_cap if (k == 0 and blk == 0) else band_cap for blk in range(N_BLOCKS)]
        chains = [(blk, g, j) for blk in range(N_BLOCKS) for g in range(SW_KV_HEADS)
                  for j in range(HEADS_PER_TILE)]
        q_pairs = {}
        scores = {}

        def sw_issue_scores(i):
            if i < len(chains) and i not in scores:
                blk, g, j = chains[i]
                if blk not in q_pairs:
                    q_pairs[blk] = sw_prepare(blk)
                scores[i] = sw_scores(blk, g, j, q_pairs[blk][g])

        head = head or {"pre": {}, "cum": {}, "dec": {}, "scores": {}}
        pre = dict(head["pre"])
        cum = dict(head["cum"])

        def hg_issue_cum(c):
            if c < N_CHUNKS and c not in cum:
                pre[c] = hg_pre(c)
                cum[c] = hg_cum(pre[c][1])

        fill(2 if many else 1)
        for c in range(min(2, N_CHUNKS)):
            hg_issue_cum(c)
            fill(1 if many else 0)
        for c in range(N_CHUNKS):
            if c in head["dec"]:
                q_dec, k_dec, k_tail, decay = head["dec"][c]
                hg_s = head["scores"][c]
            else:
                q_dec, k_dec, k_tail, decay = hg_decays(c, pre[c][0], cum[c])
                hg_s = [hg_scores(q_dec, k_dec, hd) for hd in range(HG_HEADS)]
            hg_issue_cum(c + 2)
            fill()
            for hd in range(HG_HEADS):
                hg_output(c, hd, hg_s[hd], q_dec, k_tail, decay)
            fill(1 if many else 0)

        for i in range(min(SW_SCORES_AHEAD, len(chains))):
            sw_issue_scores(i)
        fill()
        outs = {}
        has_next = k + 1 < TILES_PER_STEP
        next_head = {"pre": {}, "cum": {}, "dec": {}, "scores": {}} if has_next else None
        h_next = h_bufs[(k + 1) % 2]

        def head_start(i):
            if i != HEAD_AFTER_CHAIN:
                return
            assert projected.get(k + 1, 0) * PIECE >= OFF_HF + HG_WIDTH
            for c in range(HEAD_CUM_CHUNKS):
                next_head["pre"][c] = hg_pre(c, h_next)
                next_head["cum"][c] = hg_cum(next_head["pre"][c][1])
            for c in range(HEAD_DEC_CHUNKS):
                d = hg_decays(c, next_head["pre"][c][0], next_head["cum"][c], h_next)
                next_head["dec"][c] = d
                next_head["scores"][c] = [hg_scores(d[0], d[1], hd) for hd in range(HG_HEADS)]

        for i, (blk, g, j) in enumerate(chains):
            outs[(blk, g, j)] = sw_values(blk, g, j, scores.pop(i), masks[blk])
            sw_issue_scores(i + SW_SCORES_AHEAD)
            if j == HEADS_PER_TILE - 1:
                sw_finish(blk, g, [outs.pop((blk, g, jj)) for jj in range(HEADS_PER_TILE)])
            fill()
            if has_next:
                head_start(i)
        fill(N_OUT_PIECES + 1 + N_IN_PIECES)
        return next_head

    head = None
    for k in range(TILES_PER_STEP):
        head = process_tile(k, head)
    for piece in range(N_OUT_PIECES):
        out_projection_piece(TILES_PER_STEP - 1, piece)
    post_norm(TILES_PER_STEP - 1)

    last = slice((STEP_BLOCKS - 1) * WINDOW, STEP_BLOCKS * WINDOW)
    for idx in range(2 * SW_KV_HEADS):
        kcarry_ref[idx] = kwin_ref[idx, last, :]
    for g in range(SW_KV_HEADS):
        vcarry_ref[g, 0:SW_HEAD_DIM, :] = vt_ref[g, 0:SW_HEAD_DIM, last]


def _rope_tables(seq_len):
    pos = jnp.arange(seq_len, dtype=F32)
    inv_freq = ROPE_THETA ** (-jnp.arange(0, ROPE_DIM, 2, dtype=F32) / ROPE_DIM)
    ang = pos[:, None] * inv_freq[None, :]
    cos = jnp.cos(ang)
    sin = jnp.sin(ang)
    ones = jnp.ones((seq_len, SW_HEAD_DIM - ROPE_DIM), F32)
    cos_head = jnp.concatenate([cos, cos, ones], axis=1)
    sin_head = jnp.concatenate([-sin, sin, jnp.zeros_like(ones)], axis=1)
    return (jnp.tile(cos_head, (1, HEADS_PER_TILE)), jnp.tile(sin_head, (1, HEADS_PER_TILE)))


def kernel(x, w_in, lb_logits, hg_norm_w, sinks, w_out, ln_g, ln_b):
    batch, seq_len, d_model = x.shape
    assert d_model == D_MODEL and w_in.shape == (DEPTH, D_MODEL, IN_WIDTH)
    step_rows = TILES_PER_STEP * TIME_TILE
    assert seq_len % step_rows == 0
    cos_tab, sin_tab = _rope_tables(seq_len)
    n_steps = seq_len // step_rows
    last_step = batch * n_steps - 1

    def next_tile_index(b, u, *_):
        nxt = jnp.minimum(b * n_steps + u + 1, last_step)
        return (nxt // n_steps, TILES_PER_STEP * (nxt % n_steps), 0)

    const = lambda b, u, *_: (0, 0)
    grid_spec = pltpu.PrefetchScalarGridSpec(
        num_scalar_prefetch=1,
        grid=(batch, n_steps),
        in_specs=[
            pl.BlockSpec((1, step_rows, D_MODEL), lambda b, u, *_: (b, u, 0)),
            pl.BlockSpec((1, TIME_TILE, D_MODEL), next_tile_index),
            pl.BlockSpec(memory_space=pl.ANY),
            pl.BlockSpec(memory_space=pl.ANY),
            pl.BlockSpec((DEPTH + 1, HG_WIDTH), const),
            pl.BlockSpec((1, HG_WIDTH), const),
            pl.BlockSpec((1, D_MODEL), const),
            pl.BlockSpec((1, D_MODEL), const),
            pl.BlockSpec((step_rows, LANES), lambda b, u, *_: (u, 0)),
            pl.BlockSpec((step_rows, LANES), lambda b, u, *_: (u, 0)),
        ],
        out_specs=pl.BlockSpec((1, step_rows, D_MODEL), lambda b, u, *_: (b, u, 0)),
        scratch_shapes=[
            pltpu.VMEM((TIME_TILE, IN_WIDTH), F32),
            pltpu.VMEM((TIME_TILE, IN_WIDTH), F32),
            pltpu.VMEM((2, TIME_TILE, D_MODEL), BF16),
            pltpu.VMEM((TILES_PER_STEP, TIME_TILE, D_MODEL), BF16),
            pltpu.VMEM((TIME_TILE, D_MODEL), F32),
            pltpu.VMEM((HG_HEADS, HG_HEAD_DIM, HG_HEAD_DIM), F32),
            pltpu.VMEM((2 * SW_KV_HEADS, STEP_BLOCKS * WINDOW, LANES), BF16),
            pltpu.VMEM((SW_KV_HEADS, VT_ROWS, STEP_BLOCKS * WINDOW), BF16),
            pltpu.VMEM((2 * SW_KV_HEADS, WINDOW, LANES), BF16),
            pltpu.VMEM((SW_KV_HEADS, VT_ROWS, WINDOW), BF16),
            pltpu.VMEM((D_MODEL, IN_WIDTH), BF16),
            pltpu.VMEM((D_MODEL, D_MODEL), BF16),
            pltpu.VMEM((WEIGHT_SLOTS, WEIGHT_CHUNK_ROWS, IN_WIDTH), F32),
            pltpu.VMEM((WEIGHT_SLOTS, WEIGHT_CHUNK_ROWS, D_MODEL), F32),
            pltpu.SemaphoreType.DMA((WEIGHT_SLOTS,)),
            pltpu.SemaphoreType.DMA((WEIGHT_SLOTS,)),
        ],
    )
    return pl.pallas_call(
        _layer_kernel,
        grid_spec=grid_spec,
        out_shape=jax.ShapeDtypeStruct(x.shape, x.dtype),
        compiler_params=pltpu.CompilerParams(
            dimension_semantics=("arbitrary", "arbitrary"),
            vmem_limit_bytes=VMEM_LIMIT_BYTES),
        name="hybrid_hgrn2_swa_layer",
    )(sinks[0], x, x, w_in, w_out, lb_logits, hg_norm_w, ln_g, ln_b, cos_tab, sin_tab)
```

```python
import jax
import jax.numpy as jnp
from jax import lax
from jax.experimental import pallas as pl
from jax.experimental.pallas import tpu as pltpu

D_MODEL = 1024
DEPTH = 1
HG_WIDTH = 512
HG_HEAD_DIM = 128
HG_HEADS = HG_WIDTH // HG_HEAD_DIM
HG_CHUNK = 64
SW_WIDTH = 512
SW_HEAD_DIM = 64
SW_Q_HEADS = SW_WIDTH // SW_HEAD_DIM
SW_KV_HEADS = SW_Q_HEADS // 4
SW_KV_WIDTH = SW_KV_HEADS * SW_HEAD_DIM
SW_GROUP = SW_Q_HEADS // SW_KV_HEADS
WINDOW = 128
ROPE_THETA = 500000.0
ROPE_DIM = SW_HEAD_DIM // 4
DN_ALPHA = (2.0 * DEPTH) ** 0.25
LN_EPS = 1e-5
RMS_EPS = 1e-6
IN_WIDTH = 4 * HG_WIDTH + SW_WIDTH + 2 * SW_KV_WIDTH + SW_WIDTH

OFF_HQ = 0
OFF_HF = OFF_HQ + HG_WIDTH
OFF_HI = OFF_HF + HG_WIDTH
OFF_HG = OFF_HI + HG_WIDTH
OFF_AQ = OFF_HG + HG_WIDTH
OFF_AK = OFF_AQ + SW_WIDTH
OFF_AV = OFF_AK + SW_KV_WIDTH
OFF_AG = OFF_AV + SW_KV_WIDTH

LANES = 128
BF16_SUBLANES = 16
MXU_WIDTH = 256
HEADS_PER_TILE = LANES // SW_HEAD_DIM
assert HEADS_PER_TILE == 2 and SW_GROUP == 2 * HEADS_PER_TILE and SW_KV_WIDTH == LANES
VT_ROWS = SW_HEAD_DIM + BF16_SUBLANES
TIME_TILE = 256
TILES_PER_STEP = 4
PIECE = MXU_WIDTH
N_IN_PIECES = IN_WIDTH // PIECE
N_OUT_PIECES = D_MODEL // PIECE
assert IN_WIDTH % PIECE == 0 and D_MODEL % PIECE == 0
N_CHUNKS = TIME_TILE // HG_CHUNK
N_BLOCKS = TIME_TILE // WINDOW
STEP_BLOCKS = TILES_PER_STEP * N_BLOCKS
LN_ROWS = 32
HEAD_CUM_CHUNKS = 3
HEAD_DEC_CHUNKS = 1
HEAD_AFTER_CHAIN = 3
SW_SCORES_AHEAD = 8
WEIGHT_CHUNK_ROWS = 64
WEIGHT_SLOTS = 4
VMEM_LIMIT_BYTES = 58 * 1024 * 1024

F32 = jnp.float32
BF16 = jnp.bfloat16
NT_DIMS = (((1,), (1,)), ((), ()))
TN_DIMS = (((0,), (0,)), ((), ()))


LOG2_E = 1.4426950408889634


def _sigmoid(v):
    return 1.0 / (1.0 + jnp.exp2(v * (-LOG2_E)))


def _load_weight_as_bf16(w_hbm, stage_ref, sem_ref, dst_ref, scale=None):
    n_rows = dst_ref.shape[0]
    n_slots, chunk = stage_ref.shape[0], stage_ref.shape[1]
    n_chunks = n_rows // chunk
    assert n_chunks * chunk == n_rows and n_slots >= 2

    def chunk_copy(i):
        return pltpu.make_async_copy(w_hbm.at[0, pl.ds(i * chunk, chunk), :],
                                     stage_ref.at[i % n_slots], sem_ref.at[i % n_slots])

    for i in range(min(n_slots - 1, n_chunks)):
        chunk_copy(i).start()
    for i in range(n_chunks):
        ahead = i + n_slots - 1
        if ahead < n_chunks:
            chunk_copy(ahead).start()
        chunk_copy(i).wait()
        rows = stage_ref[i % n_slots]
        if scale is not None:
            rows = rows * scale
        dst_ref[i * chunk:(i + 1) * chunk, :] = rows.astype(BF16)


def _layer_kernel(sinks_ref, xc_ref, xn_ref, win_hbm, wout_hbm, lbl_ref, nw_ref, lng_ref, lnb_ref,
                  cos_ref, sin_ref, o_ref,
                  ha_ref, hb_ref, xb_ref, mix_ref, out_ref, st_ref, kwin_ref, vt_ref,
                  kcarry_ref, vcarry_ref, win_ref, wout_ref, win_stage, wout_stage,
                  win_sems, wout_sems):
    b = pl.program_id(0)
    u = pl.program_id(1)
    tt = TIME_TILE
    h_bufs = (ha_ref, hb_ref)

    @pl.when(jnp.logical_and(b == 0, u == 0))
    def _first_step():
        _load_weight_as_bf16(win_hbm, win_stage, win_sems, win_ref)
        _load_weight_as_bf16(wout_hbm, wout_stage, wout_sems, wout_ref, scale=1.0 / DN_ALPHA)
        ha_ref[...] = jnp.dot(xc_ref[0, 0:tt, :].astype(BF16), win_ref[...],
                              preferred_element_type=F32)

    @pl.when(u == 0)
    def _reset_carries():
        st_ref[...] = jnp.zeros_like(st_ref)
        kcarry_ref[...] = jnp.zeros_like(kcarry_ref)
        vcarry_ref[:, 0:SW_HEAD_DIM, :] = jnp.zeros((SW_KV_HEADS, SW_HEAD_DIM, WINDOW), BF16)
        vcarry_ref[:, SW_HEAD_DIM:VT_ROWS, :] = jnp.ones((SW_KV_HEADS, BF16_SUBLANES, WINDOW), BF16)
        vt_ref[:, SW_HEAD_DIM:VT_ROWS, :] = jnp.ones(
            (SW_KV_HEADS, BF16_SUBLANES, STEP_BLOCKS * WINDOW), BF16)

    lbl = lbl_ref[...]
    lbl_e = jnp.exp(lbl - jnp.max(lbl, axis=0, keepdims=True))
    lb = lbl_e[0:1] / jnp.sum(lbl_e, axis=0, keepdims=True)
    nw = nw_ref[...] * (HG_HEAD_DIM ** 0.5)
    row = lax.broadcasted_iota(jnp.int32, (HG_CHUNK, HG_CHUNK), 0)
    col = lax.broadcasted_iota(jnp.int32, (HG_CHUNK, HG_CHUNK), 1)
    tril = col <= row
    cum_mat = tril.astype(BF16)
    cum3 = jnp.concatenate([cum_mat, cum_mat, cum_mat], axis=1)

    lane = lax.broadcasted_iota(jnp.int32, (1, LANES), 1)
    rope_first = (lane % SW_HEAD_DIM) < (ROPE_DIM // 2)
    lane_lo = lane < SW_HEAD_DIM
    key_idx = lax.broadcasted_iota(jnp.int32, (2 * WINDOW, 2 * WINDOW), 0)
    qry_idx = lax.broadcasted_iota(jnp.int32, (2 * WINDOW, 2 * WINDOW), 1) % WINDOW
    band = (key_idx > qry_idx) & (key_idx <= qry_idx + WINDOW)
    pair_lo = lax.broadcasted_iota(jnp.int32, (1, 2 * WINDOW), 1) < WINDOW
    scale = SW_HEAD_DIM ** -0.5 * LOG2_E
    first_valid = jnp.where(u > 0, 0, WINDOW)
    band_cap = jnp.where(band, jnp.inf, -jnp.inf)
    first_cap = jnp.where(band & (key_idx >= first_valid), jnp.inf, -jnp.inf)

    def rope(v, cos, sin):
        partner = jnp.where(rope_first,
                            pltpu.roll(v, LANES - ROPE_DIM // 2, 1),
                            pltpu.roll(v, ROPE_DIM // 2, 1))
        return v * cos + partner * sin

    def head_variants(v):
        sw = pltpu.roll(v, SW_HEAD_DIM, 1)
        zero = jnp.zeros_like(v)
        return (jnp.where(lane_lo, v, zero), jnp.where(lane_lo, zero, sw),
                jnp.where(lane_lo, sw, zero), jnp.where(lane_lo, zero, v))

    def in_projection_piece(k, piece):
        xb = xb_ref.at[k % 2]
        cols = slice(piece * PIECE, (piece + 1) * PIECE)
        h_bufs[k % 2][:, cols] = jnp.dot(xb[...], win_ref[:, cols], preferred_element_type=F32)

    def out_projection_piece(k, piece):
        cols = slice(piece * PIECE, (piece + 1) * PIECE)
        out_ref[:, cols] = jnp.dot(mix_ref[k], wout_ref[:, cols], preferred_element_type=F32)

    def post_norm(k):
        for r0 in range(0, tt, LN_ROWS):
            rows = slice(k * tt + r0, k * tt + r0 + LN_ROWS)
            z = xc_ref[0, rows, :] + out_ref[r0:r0 + LN_ROWS, :]
            mu = jnp.mean(z, axis=-1, keepdims=True)
            zc = z - mu
            var = jnp.mean(zc * zc, axis=-1, keepdims=True)
            o_ref[0, rows, :] = ((zc * lax.rsqrt(var + LN_EPS / DN_ALPHA ** 2)) * lng_ref[...]
                                 + lnb_ref[...])

    def make_fillers(k):
        stages = []
        if k > 0:
            for piece in range(N_OUT_PIECES):
                stages.append(lambda piece=piece: out_projection_piece(k - 1, piece))
            stages.append(lambda: post_norm(k - 1))
        for piece in range(N_IN_PIECES):
            def in_stage(piece=piece):
                in_projection_piece(k + 1, piece)
                projected[k + 1] = piece + 1
            stages.append(in_stage)
        return iter(stages)

    projected = {}

    def process_tile(k, head):
        h_ref = h_bufs[k % 2]
        base = k * tt
        mix = mix_ref.at[k]
        if k + 1 < TILES_PER_STEP:
            x_next_rows = xc_ref[0, (k + 1) * tt:(k + 2) * tt, :]
        else:
            x_next_rows = xn_ref[0]
        xb_ref[(k + 1) % 2] = x_next_rows.astype(BF16)
        fillers = make_fillers(k)

        def fill(n=1):
            for _ in range(n):
                stage = next(fillers, None)
                if stage is not None:
                    stage()

        def hg_pre(c, h=h_ref):
            rows = slice(c * HG_CHUNK, (c + 1) * HG_CHUNK)
            f = lb + (1.0 - lb) * _sigmoid(h[rows, OFF_HF:OFF_HF + HG_WIDTH])
            log_f = jnp.log2(f)
            hi = log_f.astype(BF16)
            rem = log_f - hi.astype(F32)
            mid = rem.astype(BF16)
            lo = (rem - mid.astype(F32)).astype(BF16)
            return 1.0 - f, jnp.concatenate([hi, mid, lo], axis=0)

        def hg_cum(parts):
            return jnp.dot(cum3, parts, preferred_element_type=F32)

        def hg_decays(c, k_in, g_cum, h=h_ref):
            rows = slice(c * HG_CHUNK, (c + 1) * HG_CHUNK)
            hq = h[rows, OFF_HQ:OFF_HQ + HG_WIDTH]
            g_last = g_cum[HG_CHUNK - 1:HG_CHUNK, :]
            q_dec = ((hq * _sigmoid(hq)) * jnp.exp2(g_cum)).astype(BF16)
            k_dec = (k_in * jnp.exp2(-g_cum)).astype(BF16)
            k_tail = (k_in * jnp.exp2(g_last - g_cum)).astype(BF16)
            return q_dec, k_dec, k_tail, jnp.exp2(g_last)

        def hg_scores(q_dec, k_dec, hd):
            sl = slice(hd * HG_HEAD_DIM, (hd + 1) * HG_HEAD_DIM)
            return lax.dot_general(q_dec[:, sl], k_dec[:, sl], NT_DIMS, preferred_element_type=F32)

        def hg_output(c, hd, a, q_dec, k_tail, decay):
            rows = slice(c * HG_CHUNK, (c + 1) * HG_CHUNK)
            sl = slice(hd * HG_HEAD_DIM, (hd + 1) * HG_HEAD_DIM)
            v = h_ref[rows, OFF_HI + hd * HG_HEAD_DIM:OFF_HI + (hd + 1) * HG_HEAD_DIM].astype(BF16)
            s_t = st_ref[hd]
            o = jnp.dot(jnp.where(tril, a, 0.0).astype(BF16), v, preferred_element_type=F32)
            o = o + lax.dot_general(q_dec[:, sl], s_t.astype(BF16), NT_DIMS,
                                    preferred_element_type=F32)
            inc_t = lax.dot_general(v, k_tail[:, sl], TN_DIMS, preferred_element_type=F32)
            st_ref[hd] = s_t * decay[:, sl] + inc_t
            o = o * lax.rsqrt(jnp.sum(o * o, axis=-1, keepdims=True) + HG_HEAD_DIM * RMS_EPS)
            o = o * nw[:, sl]
            hg = h_ref[rows, OFF_HG + hd * HG_HEAD_DIM:OFF_HG + (hd + 1) * HG_HEAD_DIM]
            mix[rows, sl] = (o * (hg * _sigmoid(hg))).astype(BF16)

        def sw_prepare(blk):
            r0 = blk * WINDOW
            rows = slice(r0, r0 + WINDOW)
            cur = slice(base + r0, base + r0 + WINDOW)
            cos = cos_ref[base + r0:base + r0 + WINDOW, :]
            sin = sin_ref[base + r0:base + r0 + WINDOW, :]
            k_rot = rope(h_ref[rows, OFF_AK:OFF_AK + SW_KV_WIDTH], cos, sin)
            for idx, kv in enumerate(head_variants(k_rot)):
                kwin_ref[idx, cur, :] = kv.astype(BF16)
            v_t = h_ref[rows, OFF_AV:OFF_AV + SW_KV_WIDTH].T
            for g in range(SW_KV_HEADS):
                vt_ref[g, 0:SW_HEAD_DIM, cur] = (
                    v_t[g * SW_HEAD_DIM:(g + 1) * SW_HEAD_DIM].astype(BF16))
            q_bf = []
            for tile in range(SW_WIDTH // LANES):
                q_rot = rope(h_ref[rows, OFF_AQ + tile * LANES:OFF_AQ + (tile + 1) * LANES], cos, sin)
                q_bf.append((q_rot * scale).astype(BF16))
            return [jnp.concatenate([q_bf[2 * g], q_bf[2 * g + 1]], axis=0)
                    for g in range(SW_KV_HEADS)]

        def key_window(blk, idx):
            r0 = base + blk * WINDOW
            if r0 == 0:
                return jnp.concatenate([kcarry_ref[idx], kwin_ref[idx, 0:WINDOW, :]], axis=0)
            return kwin_ref[idx, r0 - WINDOW:r0 + WINDOW, :]

        def value_window(blk, g):
            r0 = base + blk * WINDOW
            if r0 == 0:
                return jnp.concatenate([vcarry_ref[g], vt_ref[g, :, 0:WINDOW]], axis=1)
            return vt_ref[g, :, r0 - WINDOW:r0 + WINDOW]

        def sw_scores(blk, g, j, q_pair):
            return lax.dot_general(key_window(blk, 2 * g + j), q_pair, NT_DIMS,
                                   preferred_element_type=F32)

        def sw_values(blk, g, j, s_t, cap):
            sink_row = jnp.where(pair_lo, sinks_ref[SW_GROUP * g + j],
                                 sinks_ref[SW_GROUP * g + HEADS_PER_TILE + j]) * LOG2_E
            s_t = jnp.minimum(s_t, cap)
            m = jnp.maximum(jnp.max(s_t, axis=0, keepdims=True), sink_row)
            p_t = jnp.exp2(s_t - m).astype(BF16)
            acc = jnp.dot(value_window(blk, g), p_t, preferred_element_type=F32)
            denom = acc[SW_HEAD_DIM:SW_HEAD_DIM + 1, :] + jnp.exp2(sink_row - m)
            return acc[0:SW_HEAD_DIM, :] * (1.0 / denom)

        def sw_finish(blk, g, o_t):
            rows = slice(blk * WINDOW, (blk + 1) * WINDOW)
            for tl in range(2):
                tile = 2 * g + tl
                lanes = slice(tl * LANES, (tl + 1) * LANES)
                o_a = jnp.concatenate([o_t[0][:, lanes], o_t[1][:, lanes]], axis=0).T
                ag = h_ref[rows, OFF_AG + tile * LANES:OFF_AG + (tile + 1) * LANES]
                mix[rows, HG_WIDTH + tile * LANES:HG_WIDTH + (tile + 1) * LANES] = (
                    o_a * (ag * _sigmoid(ag))).astype(BF16)

        many = k > 0
        masks = [first_cap if (k == 0 and blk == 0) else band_cap for blk in range(N_BLOCKS)]
        chains = [(blk, g, j) for blk in range(N_BLOCKS) for g in range(SW_KV_HEADS)
                  for j in range(HEADS_PER_TILE)]
        q_pairs = {}
        scores = {}

        def sw_issue_scores(i):
            if i < len(chains) and i not in scores:
                blk, g, j = chains[i]
                if blk not in q_pairs:
                    q_pairs[blk] = sw_prepare(blk)
                scores[i] = sw_scores(blk, g, j, q_pairs[blk][g])

        head = head or {"pre": {}, "cum": {}, "dec": {}, "scores": {}}
        pre = dict(head["pre"])
        cum = dict(head["cum"])

        def hg_issue_cum(c):
            if c < N_CHUNKS and c not in cum:
                pre[c] = hg_pre(c)
                cum[c] = hg_cum(pre[c][1])

        fill(2 if many else 1)
        for c in range(min(2, N_CHUNKS)):
            hg_issue_cum(c)
            fill(1 if many else 0)
        for c in range(N_CHUNKS):
            if c in head["dec"]:
                q_dec, k_dec, k_tail, decay = head["dec"][c]
                hg_s = head["scores"][c]
            else:
                q_dec, k_dec, k_tail, decay = hg_decays(c, pre[c][0], cum[c])
                hg_s = [hg_scores(q_dec, k_dec, hd) for hd in range(HG_HEADS)]
            hg_issue_cum(c + 2)
            fill()
            for hd in range(HG_HEADS):
                hg_output(c, hd, hg_s[hd], q_dec, k_tail, decay)
            fill(1 if many else 0)

        for i in range(min(SW_SCORES_AHEAD, len(chains))):
            sw_issue_scores(i)
        fill()
        outs = {}
        has_next = k + 1 < TILES_PER_STEP
        next_head = {"pre": {}, "cum": {}, "dec": {}, "scores": {}} if has_next else None
        h_next = h_bufs[(k + 1) % 2]

        def head_start(i):
            if i != HEAD_AFTER_CHAIN:
                return
            assert projected.get(k + 1, 0) * PIECE >= OFF_HF + HG_WIDTH
            for c in range(HEAD_CUM_CHUNKS):
                next_head["pre"][c] = hg_pre(c, h_next)
                next_head["cum"][c] = hg_cum(next_head["pre"][c][1])
            for c in range(HEAD_DEC_CHUNKS):
                d = hg_decays(c, next_head["pre"][c][0], next_head["cum"][c], h_next)
                next_head["dec"][c] = d
                next_head["scores"][c] = [hg_scores(d[0], d[1], hd) for hd in range(HG_HEADS)]

        for i, (blk, g, j) in enumerate(chains):
            outs[(blk, g, j)] = sw_values(blk, g, j, scores.pop(i), masks[blk])
            sw_issue_scores(i + SW_SCORES_AHEAD)
            if j == HEADS_PER_TILE - 1:
                sw_finish(blk, g, [outs.pop((blk, g, jj)) for jj in range(HEADS_PER_TILE)])
            fill()
            if has_next:
                head_start(i)
        fill(N_OUT_PIECES + 1 + N_IN_PIECES)
        return next_head

    head = None
    for k in range(TILES_PER_STEP):
        head = process_tile(k, head)
    for piece in range(N_OUT_PIECES):
        out_projection_piece(TILES_PER_STEP - 1, piece)
    post_norm(TILES_PER_STEP - 1)

    last = slice((STEP_BLOCKS - 1) * WINDOW, STEP_BLOCKS * WINDOW)
    for idx in range(2 * SW_KV_HEADS):
        kcarry_ref[idx] = kwin_ref[idx, last, :]
    for g in range(SW_KV_HEADS):
        vcarry_ref[g, 0:SW_HEAD_DIM, :] = vt_ref[g, 0:SW_HEAD_DIM, last]


def _rope_tables(seq_len):
    pos = jnp.arange(seq_len, dtype=F32)
    inv_freq = ROPE_THETA ** (-jnp.arange(0, ROPE_DIM, 2, dtype=F32) / ROPE_DIM)
    ang = pos[:, None] * inv_freq[None, :]
    cos = jnp.cos(ang)
    sin = jnp.sin(ang)
    ones = jnp.ones((seq_len, SW_HEAD_DIM - ROPE_DIM), F32)
    cos_head = jnp.concatenate([cos, cos, ones], axis=1)
    sin_head = jnp.concatenate([-sin, sin, jnp.zeros_like(ones)], axis=1)
    return (jnp.tile(cos_head, (1, HEADS_PER_TILE)), jnp.tile(sin_head, (1, HEADS_PER_TILE)))


def kernel(x, w_in, lb_logits, hg_norm_w, sinks, w_out, ln_g, ln_b):
    batch, seq_len, d_model = x.shape
    assert d_model == D_MODEL and w_in.shape == (DEPTH, D_MODEL, IN_WIDTH)
    step_rows = TILES_PER_STEP * TIME_TILE
    assert seq_len % step_rows == 0
    cos_tab, sin_tab = _rope_tables(seq_len)
    n_steps = seq_len // step_rows
    last_step = batch * n_steps - 1

    def next_tile_index(b, u, *_):
        nxt = jnp.minimum(b * n_steps + u + 1, last_step)
        return (nxt // n_steps, TILES_PER_STEP * (nxt % n_steps), 0)

    const = lambda b, u, *_: (0, 0)
    grid_spec = pltpu.PrefetchScalarGridSpec(
        num_scalar_prefetch=1,
        grid=(batch, n_steps),
        in_specs=[
            pl.BlockSpec((1, step_rows, D_MODEL), lambda b, u, *_: (b, u, 0)),
            pl.BlockSpec((1, TIME_TILE, D_MODEL), next_tile_index),
            pl.BlockSpec(memory_space=pl.ANY),
            pl.BlockSpec(memory_space=pl.ANY),
            pl.BlockSpec((DEPTH + 1, HG_WIDTH), const),
            pl.BlockSpec((1, HG_WIDTH), const),
            pl.BlockSpec((1, D_MODEL), const),
            pl.BlockSpec((1, D_MODEL), const),
            pl.BlockSpec((step_rows, LANES), lambda b, u, *_: (u, 0)),
            pl.BlockSpec((step_rows, LANES), lambda b, u, *_: (u, 0)),
        ],
        out_specs=pl.BlockSpec((1, step_rows, D_MODEL), lambda b, u, *_: (b, u, 0)),
        scratch_shapes=[
            pltpu.VMEM((TIME_TILE, IN_WIDTH), F32),
            pltpu.VMEM((TIME_TILE, IN_WIDTH), F32),
            pltpu.VMEM((2, TIME_TILE, D_MODEL), BF16),
            pltpu.VMEM((TILES_PER_STEP, TIME_TILE, D_MODEL), BF16),
            pltpu.VMEM((TIME_TILE, D_MODEL), F32),
            pltpu.VMEM((HG_HEADS, HG_HEAD_DIM, HG_HEAD_DIM), F32),
            pltpu.VMEM((2 * SW_KV_HEADS, STEP_BLOCKS * WINDOW, LANES), BF16),
            pltpu.VMEM((SW_KV_HEADS, VT_ROWS, STEP_BLOCKS * WINDOW), BF16),
            pltpu.VMEM((2 * SW_KV_HEADS, WINDOW, LANES), BF16),
            pltpu.VMEM((SW_KV_HEADS, VT_ROWS, WINDOW), BF16),
            pltpu.VMEM((D_MODEL, IN_WIDTH), BF16),
            pltpu.VMEM((D_MODEL, D_MODEL), BF16),
            pltpu.VMEM((WEIGHT_SLOTS, WEIGHT_CHUNK_ROWS, IN_WIDTH), F32),
            pltpu.VMEM((WEIGHT_SLOTS, WEIGHT_CHUNK_ROWS, D_MODEL), F32),
            pltpu.SemaphoreType.DMA((WEIGHT_SLOTS,)),
            pltpu.SemaphoreType.DMA((WEIGHT_SLOTS,)),
        ],
    )
    return pl.pallas_call(
        _layer_kernel,
        grid_spec=grid_spec,
        out_shape=jax.ShapeDtypeStruct(x.shape, x.dtype),
        compiler_params=pltpu.CompilerParams(
            dimension_semantics=("arbitrary", "arbitrary"),
            vmem_limit_bytes=VMEM_LIMIT_BYTES),
        name="hybrid_hgrn2_swa_layer",
    )(sinks[0], x, x, w_in, w_out, lb_logits, hg_norm_w, ln_g, ln_b, cos_tab, sin_tab)
```
